```python
import math
import jax, jax.numpy as jnp
from jax import lax
import numpy as np

D_MODEL = 1024
BATCH = 8
SEQ = 2048
DEPTH = 1
DEC_BATCH = 32
DEC_SEQ = 4
PAST_LEN = 8192
PAGE_SIZE = 128

ATTN_WIDTH = D_MODEL // 2
SSM_WIDTH = D_MODEL - ATTN_WIDTH
HEAD_DIM = 64
N_HEADS = ATTN_WIDTH // HEAD_DIM
DILATED_GROUPS = ((128, 1), (512, 4), (2048, 16))
MAX_WINDOW = max(w for w, _ in DILATED_GROUPS)
SSM_GROUP = 16
N_SSM_GROUPS = SSM_WIDTH // SSM_GROUP
SSM_STATE = 64
MIX_IN = 3 * ATTN_WIDTH + SSM_WIDTH
N_EXPERTS = 32
TOP_K = 4
D_FF = D_MODEL
SWIGLU_LIMIT = 7.0
SWIGLU_ALPHA = 1.702
MOE_BLOCK = 128
RMS_EPS = 1e-5
NEG_INF = -1e30

kernel_name = "hymba_dilated_s5_moe_step"


def rmsnorm(x, g):
    xf = x.astype(jnp.float32)
    y = xf * lax.rsqrt(jnp.mean(xf * xf, axis=-1, keepdims=True) + RMS_EPS)
    return (y * g.astype(jnp.float32)).astype(x.dtype)


def combine_groups(outs, lses):
    w = jax.nn.softmax(jnp.stack(lses, 0), axis=0)
    return jnp.sum(w[..., None] * jnp.stack(outs, 0), axis=0)


def dilated_attention_prompt(q, k, v):
    B, S, H, E = q.shape
    scale = HEAD_DIM ** -0.5
    outs, lses = [], []
    for w, d in DILATED_GROUPS:
        nb = w // d
        span = d * nb
        sp = -(-S // span) * span
        nblk = sp // span
        m_len = sp // d

        def to_blocks(t):
            t = jnp.pad(t, ((0, 0), (0, sp - S), (0, 0), (0, 0)))
            t = t.reshape(B, m_len, d, H, E).transpose(0, 2, 1, 3, 4)
            return t.reshape(B, d, nblk, nb, H, E)

        def with_prev(t):
            prev = jnp.pad(t, ((0, 0), (0, 0), (1, 0), (0, 0), (0, 0), (0, 0)))[:, :, :-1]
            return jnp.concatenate([prev, t], axis=3)

        qb = to_blocks(q)
        kk = with_prev(to_blocks(k))
        vv = with_prev(to_blocks(v))
        s = jnp.einsum('brnqhe,brnkhe->brnhqk', qb, kk,
                       preferred_element_type=jnp.float32) * scale
        qi = jnp.arange(nb)[:, None]
        kj = jnp.arange(2 * nb)[None, :]
        rel = nb + qi - kj
        band = (rel >= 0) & (rel <= nb)
        not_before_start = (jnp.arange(nblk)[:, None, None] > 0) | (kj[None] >= nb)
        valid = band[None] & not_before_start
        s = jnp.where(valid[:, None], s, NEG_INF)
        m = jnp.max(s, axis=-1, keepdims=True)
        p = jnp.exp(s - m)
        l = jnp.sum(p, axis=-1, keepdims=True)
        o = jnp.einsum('brnhqk,brnkhe->brnqhe', p, vv.astype(jnp.float32))
        o = o / jnp.swapaxes(l[..., 0], 3, 4)[..., None]
        lse = jnp.swapaxes((m + jnp.log(l))[..., 0], 3, 4)
        o = o.reshape(B, d, m_len, H, E).transpose(0, 2, 1, 3, 4).reshape(B, sp, H, E)[:, :S]
        lse = lse.reshape(B, d, m_len, H).transpose(0, 2, 1, 3).reshape(B, sp, H)[:, :S]
        outs.append(o)
        lses.append(lse)
    return combine_groups(outs, lses)


def dilated_attention_sample(q, k_all, v_all, n_buf):
    T = q.shape[1]
    scale = HEAD_DIM ** -0.5
    outs, lses = [], []
    for w, d in DILATED_GROUPS:
        nb = w // d
        idx = n_buf + jnp.arange(T)[:, None] - d * jnp.arange(nb + 1)[None, :]
        valid = idx >= 0
        idx = jnp.maximum(idx, 0)
        kg = k_all[:, idx]
        vg = v_all[:, idx]
        s = jnp.einsum('bthe,btjhe->bthj', q, kg,
                       preferred_element_type=jnp.float32) * scale
        s = jnp.where(valid[None, :, None, :], s, NEG_INF)
        m = jnp.max(s, axis=-1, keepdims=True)
        p = jnp.exp(s - m)
        l = jnp.sum(p, axis=-1, keepdims=True)
        o = jnp.einsum('bthj,btjhe->bthe', p, vg.astype(jnp.float32)) / l
        outs.append(o)
        lses.append((m + jnp.log(l))[..., 0])
    return combine_groups(outs, lses)


def s5_ssm(u, h0, a_re, a_im, log_dt, b_re, b_im, c_re, c_im, d_skip):
    f32 = jnp.float32
    uf = u.astype(f32)
    lam = lax.complex(a_re.astype(f32), a_im.astype(f32))
    dt = jnp.exp(log_dt.astype(f32))[:, None]
    a_bar = jnp.exp(lam * dt)
    b_bar = ((a_bar - 1.0) / lam)[..., None] * lax.complex(b_re.astype(f32), b_im.astype(f32))
    bu = jnp.einsum('gnc,bsgc->bsgn', b_bar, uf.astype(jnp.complex64))
    bu = bu.at[:, 0].add(a_bar * h0)
    a_seq = jnp.broadcast_to(a_bar, bu.shape)

    def combine(left, right):
        a1, b1 = left
        a2, b2 = right
        return a1 * a2, a2 * b1 + b2

    _, states = lax.associative_scan(combine, (a_seq, bu), axis=1)
    c = lax.complex(c_re.astype(f32), c_im.astype(f32))
    y = jnp.einsum('gcn,bsgn->bsgc', c, states).real + d_skip.astype(f32) * uf
    return y, states[:, -1]


def moe(h, w_router, b_router, w1, b1, w2, b2):
    N, D = h.shape
    logits = jnp.dot(h.astype(jnp.float32), w_router.astype(jnp.float32)) + b_router.astype(jnp.float32)
    top_v, top_i = lax.top_k(logits, TOP_K)
    gates = jax.nn.softmax(top_v, axis=-1)
    n_assign = N * TOP_K
    e_flat = top_i.reshape(-1)
    tok = jnp.repeat(jnp.arange(N, dtype=jnp.int32), TOP_K)
    g_flat = gates.reshape(-1)
    order = jnp.argsort(e_flat)
    e_sorted = e_flat[order]
    sizes = jnp.bincount(e_flat, length=N_EXPERTS)
    padded = (sizes + MOE_BLOCK - 1) // MOE_BLOCK * MOE_BLOCK
    start = jnp.cumsum(sizes) - sizes
    pend = jnp.cumsum(padded)
    pstart = pend - padded
    dest = pstart[e_sorted] + (jnp.arange(n_assign) - start[e_sorted])
    n_blocks = -(-n_assign // MOE_BLOCK) + N_EXPERTS
    rows = n_blocks * MOE_BLOCK
    row_tok = jnp.full((rows,), N, jnp.int32).at[dest].set(tok[order])
    row_gate = jnp.zeros((rows,), jnp.float32).at[dest].set(g_flat[order])
    block_exp = jnp.clip(jnp.searchsorted(pend, jnp.arange(n_blocks) * MOE_BLOCK, side='right'),
                         0, N_EXPERTS - 1)
    xs = jnp.concatenate([h, jnp.zeros((1, D), h.dtype)], axis=0)[row_tok]
    xs = xs.reshape(n_blocks, MOE_BLOCK, D)

    def expert_block(args):
        xb, e = args
        a = jnp.dot(xb, w1[e]) + b1[e]
        x_glu = jnp.minimum(a[:, :D_FF], SWIGLU_LIMIT)
        x_lin = jnp.clip(a[:, D_FF:], -SWIGLU_LIMIT, SWIGLU_LIMIT)
        mid = x_glu * jax.nn.sigmoid(SWIGLU_ALPHA * x_glu) * (x_lin + 1.0)
        return jnp.dot(mid, w2[e]) + b2[e]

    ys = lax.map(expert_block, (xs, block_exp)).reshape(rows, D)
    out = jnp.zeros((N + 1, D), jnp.float32).at[row_tok].add(ys.astype(jnp.float32) * row_gate[:, None])
    return out[:N].astype(h.dtype)


def project(x, norm_g, w_in):
    B, S, _ = x.shape
    z = jnp.dot(rmsnorm(x, norm_g), w_in)
    A = ATTN_WIDTH
    q = z[..., :A].reshape(B, S, N_HEADS, HEAD_DIM)
    k = z[..., A:2 * A].reshape(B, S, N_HEADS, HEAD_DIM)
    v = z[..., 2 * A:3 * A].reshape(B, S, N_HEADS, HEAD_DIM)
    u = z[..., 3 * A:].reshape(B, S, N_SSM_GROUPS, SSM_GROUP)
    return q, k, v, u


def finish_layer(x, attn, ssm_y, attn_out_norm_g, w_glu, b_glu, ssm_out_norm_g, w_out,
                 norm_ffn_g, w_router, b_router, w1, b1, w2, b2):
    B, S, D = x.shape
    attn = attn.reshape(B, S, ATTN_WIDTH).astype(x.dtype)
    ssm_y = ssm_y.reshape(B, S, SSM_WIDTH).astype(x.dtype)
    g = jax.nn.gelu(ssm_y)
    ssm_o = g * jax.nn.sigmoid(jnp.dot(g, w_glu) + b_glu)
    mixed = jnp.concatenate([rmsnorm(attn, attn_out_norm_g), rmsnorm(ssm_o, ssm_out_norm_g)], axis=-1)
    x = x + jnp.dot(mixed, w_out)
    f = moe(rmsnorm(x, norm_ffn_g).reshape(B * S, D), w_router, b_router, w1, b1, w2, b2)
    return x + f.reshape(B, S, D)


def setup_inputs(seed: int = 0) -> dict:
    key = jax.random.key(seed)
    ks = jax.random.split(key, 32)
    f32 = jnp.float32
    nrm = lambda k, shape, s: jax.random.normal(k, shape, f32) * s
    w_buf = min(MAX_WINDOW, PAST_LEN)
    L = DEPTH
    return {
        "x_prompt": nrm(ks[0], (BATCH, SEQ, D_MODEL), 1.0),
        "x_sample": nrm(ks[1], (DEC_BATCH, DEC_SEQ, D_MODEL), 1.0),
        "cache_k": nrm(ks[2], (L, DEC_BATCH, w_buf, N_HEADS, HEAD_DIM), 1.0),
        "cache_v": nrm(ks[3], (L, DEC_BATCH, w_buf, N_HEADS, HEAD_DIM), 1.0),
        "state_ssm_re": nrm(ks[4], (L, DEC_BATCH, N_SSM_GROUPS, SSM_STATE), 1.0),
        "state_ssm_im": nrm(ks[5], (L, DEC_BATCH, N_SSM_GROUPS, SSM_STATE), 1.0),
        "norm_mix_g": 1.0 + nrm(ks[6], (L, D_MODEL), 0.02),
        "w_in": nrm(ks[7], (L, D_MODEL, MIX_IN), D_MODEL ** -0.5),
        "attn_out_norm_g": 1.0 + nrm(ks[8], (L, ATTN_WIDTH), 0.02),
        "ssm_a_re": -0.5 + nrm(ks[9], (L, N_SSM_GROUPS, SSM_STATE), 0.01),
        "ssm_a_im": math.pi * jnp.arange(SSM_STATE, dtype=f32) + nrm(ks[10], (L, N_SSM_GROUPS, SSM_STATE), 0.01),
        "ssm_log_dt": jax.random.uniform(ks[11], (L, N_SSM_GROUPS), f32, math.log(1e-3), math.log(1e-1)),
        "ssm_b_re": nrm(ks[12], (L, N_SSM_GROUPS, SSM_STATE, SSM_GROUP), (2 * SSM_GROUP) ** -0.5),
        "ssm_b_im": nrm(ks[13], (L, N_SSM_GROUPS, SSM_STATE, SSM_GROUP), (2 * SSM_GROUP) ** -0.5),
        "ssm_c_re": nrm(ks[14], (L, N_SSM_GROUPS, SSM_GROUP, SSM_STATE), (2 * SSM_STATE) ** -0.5),
        "ssm_c_im": nrm(ks[15], (L, N_SSM_GROUPS, SSM_GROUP, SSM_STATE), (2 * SSM_STATE) ** -0.5),
        "ssm_d": nrm(ks[16], (L, N_SSM_GROUPS, SSM_GROUP), 1.0),
        "w_glu": nrm(ks[17], (L, SSM_WIDTH, SSM_WIDTH), SSM_WIDTH ** -0.5),
        "b_glu": nrm(ks[18], (L, SSM_WIDTH), 0.01),
        "ssm_out_norm_g": 1.0 + nrm(ks[19], (L, SSM_WIDTH), 0.02),
        "w_out": nrm(ks[20], (L, D_MODEL, D_MODEL), D_MODEL ** -0.5),
        "norm_ffn_g": 1.0 + nrm(ks[21], (L, D_MODEL), 0.02),
        "w_router": nrm(ks[22], (L, D_MODEL, N_EXPERTS), D_MODEL ** -0.5),
        "b_router": nrm(ks[23], (L, N_EXPERTS), 0.01),
        "w_moe1": nrm(ks[24], (L, N_EXPERTS, D_MODEL, 2 * D_FF), D_MODEL ** -0.5),
        "b_moe1": nrm(ks[25], (L, N_EXPERTS, 2 * D_FF), 0.01),
        "w_moe2": nrm(ks[26], (L, N_EXPERTS, D_FF, D_MODEL), D_FF ** -0.5),
        "b_moe2": nrm(ks[27], (L, N_EXPERTS, D_MODEL), 0.01),
        "norm_final_g": 1.0 + nrm(ks[28], (D_MODEL,), 0.02),
    }


def reference(x_prompt, x_sample, cache_k, cache_v, state_ssm_re, state_ssm_im,
              norm_mix_g, w_in, attn_out_norm_g, ssm_a_re, ssm_a_im, ssm_log_dt,
              ssm_b_re, ssm_b_im, ssm_c_re, ssm_c_im, ssm_d, w_glu, b_glu, ssm_out_norm_g,
              w_out, norm_ffn_g, w_router, b_router, w_moe1, b_moe1, w_moe2, b_moe2,
              norm_final_g):
    hp, hs = x_prompt, x_sample
    Bp, S, _ = x_prompt.shape
    n_buf = cache_k.shape[2]
    keep = min(MAX_WINDOW, S)
    kp_l, vp_l, rep_l, imp_l, ks_l, vs_l, res_l, ims_l = [], [], [], [], [], [], [], []
    for l in range(DEPTH):
        ssm_p = (ssm_a_re[l], ssm_a_im[l], ssm_log_dt[l], ssm_b_re[l], ssm_b_im[l],
                 ssm_c_re[l], ssm_c_im[l], ssm_d[l])
        tail = (attn_out_norm_g[l], w_glu[l], b_glu[l], ssm_out_norm_g[l], w_out[l],
                norm_ffn_g[l], w_router[l], b_router[l], w_moe1[l], b_moe1[l], w_moe2[l], b_moe2[l])
        q, k, v, u = project(hp, norm_mix_g[l], w_in[l])
        attn = dilated_attention_prompt(q, k, v)
        h0 = jnp.zeros((Bp, N_SSM_GROUPS, SSM_STATE), jnp.complex64)
        ssm_y, h_last = s5_ssm(u, h0, *ssm_p)
        hp = finish_layer(hp, attn, ssm_y, *tail)
        kp_l.append(k[:, S - keep:])
        vp_l.append(v[:, S - keep:])
        rep_l.append(jnp.real(h_last))
        imp_l.append(jnp.imag(h_last))
        q, k, v, u = project(hs, norm_mix_g[l], w_in[l])
        k_all = jnp.concatenate([cache_k[l].astype(k.dtype), k], axis=1)
        v_all = jnp.concatenate([cache_v[l].astype(v.dtype), v], axis=1)
        attn = dilated_attention_sample(q, k_all, v_all, n_buf)
        h0 = lax.complex(state_ssm_re[l].astype(jnp.float32), state_ssm_im[l].astype(jnp.float32))
        ssm_y, h_last = s5_ssm(u, h0, *ssm_p)
        hs = finish_layer(hs, attn, ssm_y, *tail)
        ks_l.append(k)
        vs_l.append(v)
        res_l.append(jnp.real(h_last))
        ims_l.append(jnp.imag(h_last))
    y_prompt = rmsnorm(hp, norm_final_g)
    y_sample = rmsnorm(hs, norm_final_g)
    return (y_prompt, y_sample,
            jnp.stack(kp_l, 0), jnp.stack(vp_l, 0), jnp.stack(rep_l, 0), jnp.stack(imp_l, 0),
            jnp.stack(ks_l, 0), jnp.stack(vs_l, 0), jnp.stack(res_l, 0), jnp.stack(ims_l, 0))
```

```python
import functools
import math

import numpy as np
import jax
import jax.numpy as jnp
from jax import lax
from jax.experimental import pallas as pl
from jax.experimental.pallas import tpu as pltpu

F32 = jnp.float32
I32 = jnp.int32

V7X_LANES = 128
V7X_SUBLANES = 8
V7X_VMEM_BYTES = 64 * 1024 * 1024

HEAD_DIM = 64
HEADS_PER_LANE_TILE = V7X_LANES // HEAD_DIM
DILATED_GROUPS = ((128, 1), (512, 4), (2048, 16))
KEYS_BACK = 128
SSM_GROUP = 16
SSM_STATE = 64
N_EXPERTS = 32
TOP_K = 4
SWIGLU_LIMIT = 7.0
SWIGLU_ALPHA = 1.702
RMS_EPS = 1e-5
NEG_INF = -1e30
PROMPT_CHUNK = 16
MOE_TILE = 256


def _vmem_limit(nbytes):
    return int(min(nbytes + (8 << 20), V7X_VMEM_BYTES - (8 << 20)))


def _rms(x, g):
    return x * lax.rsqrt(jnp.mean(x * x, axis=-1, keepdims=True) + RMS_EPS) * g


def _proj_kernel(x_ref, g_ref, w_ref, q_ref, k_ref, v_ref, u_ref, *, attn_width, q_scale):
    h = _rms(x_ref[...], g_ref[...])
    z = jnp.dot(h, w_ref[...], preferred_element_type=F32)
    a = attn_width
    q_ref[...] = z[:, :a] * q_scale
    k_ref[...] = z[:, a:2 * a]
    v_ref[...] = z[:, 2 * a:3 * a]
    u_ref[...] = z[:, 3 * a:]


def _proj(x2d, norm_g, w_in, attn_width, tm):
    n, d = x2d.shape
    mix = w_in.shape[1]
    ssm_width = mix - 3 * attn_width
    row = lambda i: (i, 0)
    const = lambda i: (0, 0)
    out_shape = (jax.ShapeDtypeStruct((n, attn_width), F32),) * 3 + (
        jax.ShapeDtypeStruct((n, ssm_width), F32),)
    vmem = 2 * 4 * (tm * d + d * mix + tm * mix) + 4 * tm * mix
    return pl.pallas_call(
        functools.partial(_proj_kernel, attn_width=attn_width, q_scale=HEAD_DIM ** -0.5),
        grid=(n // tm,),
        in_specs=[pl.BlockSpec((tm, d), row), pl.BlockSpec((1, d), const),
                  pl.BlockSpec((d, mix), const)],
        out_specs=(pl.BlockSpec((tm, attn_width), row),) * 3 + (pl.BlockSpec((tm, ssm_width), row),),
        out_shape=out_shape,
        compiler_params=pltpu.CompilerParams(dimension_semantics=("arbitrary",),
                                             vmem_limit_bytes=_vmem_limit(vmem)),
        name="proj",
    )(x2d, norm_g.reshape(1, d), w_in)


def _attn_prompt_kernel(q_ref, k_ref, v_ref, o_ref, m_sc, l_sc, acc_sc, *, seq):
    nb = KEYS_BACK
    lane = lax.broadcasted_iota(I32, (1, V7X_LANES), 1)
    head_a = lane < HEAD_DIM
    qi = lax.broadcasted_iota(I32, (nb, 2 * nb), 0)
    kj = lax.broadcasted_iota(I32, (nb, 2 * nb), 1)
    rel = nb + qi - kj
    band = (rel >= 0) & (rel <= nb)
    n_groups = len(DILATED_GROUPS)

    def rows(ref, start, d):
        if d == 1:
            return ref[pl.ds(pl.multiple_of(start, nb), nb), :]
        return ref[pl.ds(start, nb, stride=d), :]

    def put(ref, start, d, val):
        if d == 1:
            ref[pl.ds(pl.multiple_of(start, nb), nb), :] = val
        else:
            ref[pl.ds(start, nb, stride=d), :] = val

    for gi, (w, d) in enumerate(DILATED_GROUPS):
        assert w // d == nb
        nblk = seq // (d * nb)

        def block(t, carry, gi=gi, d=d, nblk=nblk):
            r = t // nblk
            n = t % nblk
            cur = r + d * nb * n
            prev = r + d * nb * jnp.maximum(n - 1, 0)
            q = rows(q_ref, cur, d)
            kk = jnp.concatenate([rows(k_ref, prev, d), rows(k_ref, cur, d)], axis=0)
            vv = jnp.concatenate([rows(v_ref, prev, d), rows(v_ref, cur, d)], axis=0)
            qs = jnp.concatenate([jnp.where(head_a, q, 0.0), jnp.where(head_a, 0.0, q)], axis=0)
            s = lax.dot_general(qs, kk, (((1,), (1,)), ((), ())), preferred_element_type=F32)
            valid = band & (kj >= jnp.where(n > 0, 0, nb))
            s = jnp.where(jnp.concatenate([valid, valid], axis=0), s, NEG_INF)
            m = jnp.max(s, axis=-1, keepdims=True)
            p = jnp.exp(s - m)
            l = jnp.sum(p, axis=-1, keepdims=True)
            o = jnp.where(head_a,
                          jnp.dot(p[:nb], vv, preferred_element_type=F32),
                          jnp.dot(p[nb:], vv, preferred_element_type=F32))
            m_g = jnp.where(head_a, m[:nb], m[nb:])
            l_g = jnp.where(head_a, l[:nb], l[nb:])
            if gi == 0:
                m_new, l_new, acc_new = m_g, l_g, o
            else:
                m_old = rows(m_sc, cur, d)
                m_new = jnp.maximum(m_old, m_g)
                c_old = jnp.exp(m_old - m_new)
                c_g = jnp.exp(m_g - m_new)
                l_new = c_old * rows(l_sc, cur, d) + c_g * l_g
                acc_new = c_old * rows(acc_sc, cur, d) + c_g * o
            if gi == n_groups - 1:
                put(o_ref, cur, d, acc_new / l_new)
            else:
                put(m_sc, cur, d, m_new)
                put(l_sc, cur, d, l_new)
                put(acc_sc, cur, d, acc_new)
            return carry

        lax.fori_loop(0, seq // nb, block, 0)


def _attn_prompt(q, k, v):
    b, s, a = q.shape
    spec = pl.BlockSpec((None, s, V7X_LANES), lambda i, j: (i, 0, j))
    vmem = 4 * s * V7X_LANES * (2 * 4 + 3)
    return pl.pallas_call(
        functools.partial(_attn_prompt_kernel, seq=s),
        grid=(b, a // V7X_LANES),
        in_specs=[spec, spec, spec],
        out_specs=spec,
        out_shape=jax.ShapeDtypeStruct((b, s, a), F32),
        scratch_shapes=[pltpu.VMEM((s, V7X_LANES), F32)] * 3,
        compiler_params=pltpu.CompilerParams(dimension_semantics=("arbitrary", "arbitrary"),
                                             vmem_limit_bytes=_vmem_limit(vmem)),
        name="attn_prompt",
    )(q, k, v)


def _sample_key_counts(n_buf, n_new, n_pad):
    cnt = np.zeros((n_new, n_pad), np.float32)
    for w, d in DILATED_GROUPS:
        for i in range(n_new):
            for j in range(w // d + 1):
                idx = n_buf + i - d * j
                if idx >= 0:
                    cnt[i, idx] += 1.0
    return cnt


def _attn_sample_kernel(q_ref, kn_ref, vn_ref, kc_ref, vc_ref, cnt_ref, o_ref, k_all, v_all,
                        *, n_buf, n_new, n_heads):
    a = q_ref.shape[-1]
    n_pad = k_all.shape[0]
    k_all[pl.ds(0, n_buf), :] = kc_ref[...]
    v_all[pl.ds(0, n_buf), :] = vc_ref[...]
    k_all[pl.ds(n_buf, n_pad - n_buf), :] = jnp.zeros((n_pad - n_buf, a), F32)
    v_all[pl.ds(n_buf, n_pad - n_buf), :] = jnp.zeros((n_pad - n_buf, a), F32)
    k_all[pl.ds(n_buf, n_new), :] = kn_ref[...]
    v_all[pl.ds(n_buf, n_new), :] = vn_ref[...]
    rows = n_new * n_heads
    lane_head = lax.broadcasted_iota(I32, (rows, a), 1) // HEAD_DIM
    row_head = lax.broadcasted_iota(I32, (rows, a), 0) % n_heads
    own = lane_head == row_head
    q = q_ref[...]
    q_rep = jnp.broadcast_to(q[:, None, :], (n_new, n_heads, a)).reshape(rows, a)
    qb = jnp.where(own, q_rep, 0.0)
    s = lax.dot_general(qb, k_all[...], (((1,), (1,)), ((), ())), preferred_element_type=F32)
    cnt = cnt_ref[...]
    cnt = jnp.broadcast_to(cnt[:, None, :], (n_new, n_heads, n_pad)).reshape(rows, n_pad)
    valid = cnt > 0.0
    m = jnp.max(jnp.where(valid, s, NEG_INF), axis=-1, keepdims=True)
    p = jnp.where(valid, cnt * jnp.exp(s - m), 0.0)
    den = jnp.sum(p, axis=-1, keepdims=True)
    o = jnp.dot(p, v_all[...], preferred_element_type=F32) / den
    o = jnp.where(own, o, 0.0)
    o_ref[...] = jnp.sum(o.reshape(n_new, n_heads, a), axis=1)


def _attn_sample(q, k_new, v_new, cache_k, cache_v):
    b, t, a = q.shape
    n_buf = cache_k.shape[1]
    n_heads = a // HEAD_DIM
    n_pad = n_buf + V7X_LANES
    cnt = jnp.asarray(_sample_key_counts(n_buf, t, n_pad))
    new = pl.BlockSpec((None, t, a), lambda i: (i, 0, 0))
    buf = pl.BlockSpec((None, n_buf, a), lambda i: (i, 0, 0))
    vmem = 4 * (2 * 2 * n_buf * a + 2 * n_pad * a + 6 * t * n_heads * n_pad)
    return pl.pallas_call(
        functools.partial(_attn_sample_kernel, n_buf=n_buf, n_new=t, n_heads=n_heads),
        grid=(b,),
        in_specs=[new, new, new, buf, buf, pl.BlockSpec((t, n_pad), lambda i: (0, 0))],
        out_specs=new,
        out_shape=jax.ShapeDtypeStruct((b, t, a), F32),
        scratch_shapes=[pltpu.VMEM((n_pad, a), F32)] * 2,
        compiler_params=pltpu.CompilerParams(dimension_semantics=("arbitrary",),
                                             vmem_limit_bytes=_vmem_limit(vmem)),
        name="attn_sample",
    )(q, k_new, v_new, cache_k, cache_v, cnt)


def _ssm_tables(a_re, a_im, log_dt, b_re, b_im, c_re, c_im, d_skip, chunk):
    hp = lax.Precision.HIGHEST
    g, n = a_re.shape
    c = b_re.shape[-1]
    dt = jnp.exp(log_dt)[:, None]
    x, y = a_re * dt, a_im * dt
    ex = jnp.exp(x)
    ar, ai = ex * jnp.cos(y), ex * jnp.sin(y)
    nr = jnp.expm1(x) * jnp.cos(y) - 2.0 * jnp.sin(0.5 * y) ** 2
    ni = ai
    den = a_re * a_re + a_im * a_im
    fr = (nr * a_re + ni * a_im) / den
    fi = (ni * a_re - nr * a_im) / den
    bbr = fr[..., None] * b_re - fi[..., None] * b_im
    bbi = fr[..., None] * b_im + fi[..., None] * b_re
    pr, pi = [jnp.ones_like(ar)], [jnp.zeros_like(ar)]
    for _ in range(chunk):
        pr.append(pr[-1] * ar - pi[-1] * ai)
        pi.append(pr[-2] * ai + pi[-1] * ar)
    pr, pi = jnp.stack(pr), jnp.stack(pi)
    cpr = c_re[None] * pr[:, :, None, :] - c_im[None] * pi[:, :, None, :]
    cpi = c_re[None] * pi[:, :, None, :] + c_im[None] * pr[:, :, None, :]
    kern = (jnp.einsum("tgcn,gnd->tgcd", cpr[:chunk], bbr, precision=hp)
            - jnp.einsum("tgcn,gnd->tgcd", cpi[:chunk], bbi, precision=hp))
    lag = np.arange(chunk)[None, :] - np.arange(chunk)[:, None]
    toep = kern[np.maximum(lag, 0)]
    toep = jnp.where((lag >= 0)[:, :, None, None, None], toep, 0.0)
    toep = toep.transpose(2, 0, 4, 1, 3).reshape(g, chunk * c, chunk * c)
    wr = pr[chunk - 1::-1][:, :, :, None] * bbr[None] - pi[chunk - 1::-1][:, :, :, None] * bbi[None]
    wi = pr[chunk - 1::-1][:, :, :, None] * bbi[None] + pi[chunk - 1::-1][:, :, :, None] * bbr[None]
    wr = wr.transpose(1, 0, 3, 2).reshape(g, chunk * c, n)
    wi = wi.transpose(1, 0, 3, 2).reshape(g, chunk * c, n)
    vr = cpr[1:].transpose(1, 3, 0, 2).reshape(g, n, chunk * c)
    vi = -cpi[1:].transpose(1, 3, 0, 2).reshape(g, n, chunk * c)
    half = (np.arange(g) % 2)[:, None, None]
    z_w = jnp.zeros_like(wr)
    pad_w = lambda m: jnp.where(half == 0, jnp.concatenate([m, z_w], -1), jnp.concatenate([z_w, m], -1))
    z_v = jnp.zeros_like(vr)
    pad_v = lambda m: jnp.where(half == 0, jnp.concatenate([m, z_v], 1), jnp.concatenate([z_v, m], 1))
    d_t = jnp.tile(d_skip, (1, chunk)).reshape(g, 1, chunk * c)
    return dict(toep=toep, wr=pad_w(wr), wi=pad_w(wi), vr=pad_v(vr), vi=pad_v(vi), d=d_t,
                al_re=pr[chunk].reshape(1, g * n), al_im=pi[chunk].reshape(1, g * n))


def _ssm_state_kernel(u_ref, wr_ref, wi_ref, er_ref, ei_ref):
    er = jnp.dot(u_ref[0], wr_ref[0], preferred_element_type=F32)
    ei = jnp.dot(u_ref[0], wi_ref[0], preferred_element_type=F32)
    er_ref[...] = er + jnp.dot(u_ref[1], wr_ref[1], preferred_element_type=F32)
    ei_ref[...] = ei + jnp.dot(u_ref[1], wi_ref[1], preferred_element_type=F32)


def _ssm_scan_kernel(er_ref, ei_ref, ar_ref, ai_ref, hr_ref, hi_ref, xr_ref, xi_ref, lr_ref, li_ref,
                     *, n_chunks, nb):
    shape = hr_ref.shape
    ar = jnp.broadcast_to(ar_ref[...], shape)
    ai = jnp.broadcast_to(ai_ref[...], shape)

    def step(j, carry):
        xr, xi = carry
        sl = pl.ds(pl.multiple_of(j * nb, nb), nb)
        xr_ref[sl, :] = xr
        xi_ref[sl, :] = xi
        return (ar * xr - ai * xi + er_ref[sl, :], ar * xi + ai * xr + ei_ref[sl, :])

    xr, xi = lax.fori_loop(0, n_chunks, step, (hr_ref[...], hi_ref[...]))
    lr_ref[...] = xr
    li_ref[...] = xi


def _ssm_out_kernel(u_ref, t_ref, d_ref, xr_ref, xi_ref, vr_ref, vi_ref, y_ref):
    xr = xr_ref[...]
    xi = xi_ref[...]
    for a in range(2):
        u = u_ref[a]
        y = jnp.dot(u, t_ref[a], preferred_element_type=F32) + d_ref[a] * u
        y = y + jnp.dot(xr, vr_ref[a], preferred_element_type=F32)
        y_ref[a] = y + jnp.dot(xi, vi_ref[a], preferred_element_type=F32)


def _ssm(u, h0_re, h0_im, tab, chunk):
    b, s, width = u.shape
    g = width // SSM_GROUP
    n_chunks = s // chunk
    lc = chunk * SSM_GROUP
    r = n_chunks * b
    gn = g * SSM_STATE
    ut = u.reshape(b, n_chunks, chunk, g, SSM_GROUP).transpose(3, 1, 0, 2, 4).reshape(g, r, lc)
    pair = lambda i: (i, 0, 0)
    cp = lambda vmem: pltpu.CompilerParams(dimension_semantics=("arbitrary",),
                                           vmem_limit_bytes=_vmem_limit(vmem))
    er, ei = pl.pallas_call(
        _ssm_state_kernel,
        grid=(g // 2,),
        in_specs=[pl.BlockSpec((2, r, lc), pair), pl.BlockSpec((2, lc, V7X_LANES), pair),
                  pl.BlockSpec((2, lc, V7X_LANES), pair)],
        out_specs=(pl.BlockSpec((r, V7X_LANES), lambda i: (0, i)),) * 2,
        out_shape=(jax.ShapeDtypeStruct((r, gn), F32),) * 2,
        compiler_params=cp(2 * 4 * (2 * r * lc + 4 * lc * V7X_LANES + 2 * r * V7X_LANES)),
        name="ssm_state",
    )(ut, tab["wr"], tab["wi"])
    lanes = 4 * V7X_LANES
    col = lambda i: (0, i)
    xr, xi, lr, li = pl.pallas_call(
        functools.partial(_ssm_scan_kernel, n_chunks=n_chunks, nb=b),
        grid=(gn // lanes,),
        in_specs=[pl.BlockSpec((r, lanes), col), pl.BlockSpec((r, lanes), col),
                  pl.BlockSpec((1, lanes), col), pl.BlockSpec((1, lanes), col),
                  pl.BlockSpec((b, lanes), col), pl.BlockSpec((b, lanes), col)],
        out_specs=(pl.BlockSpec((r, lanes), col),) * 2 + (pl.BlockSpec((b, lanes), col),) * 2,
        out_shape=(jax.ShapeDtypeStruct((r, gn), F32),) * 2 + (jax.ShapeDtypeStruct((b, gn), F32),) * 2,
        compiler_params=cp(2 * 4 * 4 * r * lanes),
        name="ssm_scan",
    )(er, ei, tab["al_re"], tab["al_im"], h0_re, h0_im)
    y = pl.pallas_call(
        _ssm_out_kernel,
        grid=(g // 2,),
        in_specs=[pl.BlockSpec((2, r, lc), pair), pl.BlockSpec((2, lc, lc), pair),
                  pl.BlockSpec((2, 1, lc), pair),
                  pl.BlockSpec((r, V7X_LANES), lambda i: (0, i)),
                  pl.BlockSpec((r, V7X_LANES), lambda i: (0, i)),
                  pl.BlockSpec((2, V7X_LANES, lc), pair), pl.BlockSpec((2, V7X_LANES, lc), pair)],
        out_specs=pl.BlockSpec((2, r, lc), pair),
        out_shape=jax.ShapeDtypeStruct((g, r, lc), F32),
        compiler_params=cp(2 * 4 * (4 * r * lc + 2 * lc * lc + 2 * r * V7X_LANES + 4 * V7X_LANES * lc)),
        name="ssm_out",
    )(ut, tab["toep"], tab["d"], xr, xi, tab["vr"], tab["vi"])
    y = y.reshape(g, n_chunks, b, chunk, SSM_GROUP).transpose(2, 1, 3, 0, 4).reshape(b, s, width)
    return y, lr, li


def _finish_kernel(x_ref, attn_ref, ssm_ref, ga_ref, wg_ref, bg_ref, gs_ref, wo_ref, gf_ref,
                   wr_ref, br_ref, x1_ref, hf_ref, ti_ref, tg_ref):
    a = attn_ref.shape[-1]
    g = jax.nn.gelu(ssm_ref[...])
    gate = jax.nn.sigmoid(jnp.dot(g, wg_ref[...], preferred_element_type=F32) + bg_ref[...])
    mixed_a = _rms(attn_ref[...], ga_ref[...])
    mixed_s = _rms(g * gate, gs_ref[...])
    x1 = (x_ref[...] + jnp.dot(mixed_a, wo_ref[pl.ds(0, a), :], preferred_element_type=F32)
          + jnp.dot(mixed_s, wo_ref[pl.ds(a, wo_ref.shape[0] - a), :], preferred_element_type=F32))
    x1_ref[...] = x1
    hf = _rms(x1, gf_ref[...])
    hf_ref[...] = hf
    logits = jnp.dot(hf, wr_ref[...], preferred_element_type=F32,
                     precision=lax.Precision.HIGHEST) + br_ref[...]
    lane = lax.broadcasted_iota(I32, logits.shape, 1)
    work = logits
    vals, idxs = [], []
    for _ in range(TOP_K):
        m = jnp.max(work, axis=-1, keepdims=True)
        idx = jnp.min(jnp.where(work == m, lane, V7X_LANES), axis=-1, keepdims=True)
        vals.append(m)
        idxs.append(idx)
        work = jnp.where(lane == idx, -jnp.inf, work)
    exps = [jnp.exp(v - vals[0]) for v in vals]
    den = exps[0]
    for e in exps[1:]:
        den = den + e
    ti = jnp.zeros(logits.shape, I32)
    tg = jnp.zeros(logits.shape, F32)
    for k in range(TOP_K):
        ti = jnp.where(lane == k, idxs[k], ti)
        tg = jnp.where(lane == k, exps[k] / den, tg)
    ti_ref[...] = ti
    tg_ref[...] = tg


def _finish(x2d, attn, ssm_y, p, tm):
    n, d = x2d.shape
    a = attn.shape[1]
    sw = ssm_y.shape[1]
    ne = p["w_router"].shape[1]
    w_router = jnp.pad(p["w_router"], ((0, 0), (0, V7X_LANES - ne)))
    b_router = jnp.pad(p["b_router"], (0, V7X_LANES - ne), constant_values=-jnp.inf).reshape(1, V7X_LANES)
    row = lambda i: (i, 0)
    const = lambda i: (0, 0)
    full = lambda arr: pl.BlockSpec(arr.shape, const)
    ins = [x2d, attn, ssm_y, p["attn_out_norm_g"].reshape(1, a), p["w_glu"], p["b_glu"].reshape(1, sw),
           p["ssm_out_norm_g"].reshape(1, sw), p["w_out"], p["norm_ffn_g"].reshape(1, d),
           w_router, b_router]
    in_specs = [pl.BlockSpec((tm, d), row), pl.BlockSpec((tm, a), row), pl.BlockSpec((tm, sw), row)]
    in_specs += [full(arr) for arr in ins[3:]]
    vmem = 2 * 4 * (tm * (3 * d + a + sw + 2 * V7X_LANES) + sw * sw + d * d + d * V7X_LANES) + 16 * tm * d
    return pl.pallas_call(
        _finish_kernel,
        grid=(n // tm,),
        in_specs=in_specs,
        out_specs=(pl.BlockSpec((tm, d), row), pl.BlockSpec((tm, d), row),
                   pl.BlockSpec((tm, V7X_LANES), row), pl.BlockSpec((tm, V7X_LANES), row)),
        out_shape=(jax.ShapeDtypeStruct((n, d), F32), jax.ShapeDtypeStruct((n, d), F32),
                   jax.ShapeDtypeStruct((n, V7X_LANES), I32), jax.ShapeDtypeStruct((n, V7X_LANES), F32)),
        compiler_params=pltpu.CompilerParams(dimension_semantics=("arbitrary",),
                                             vmem_limit_bytes=_vmem_limit(vmem)),
        name="finish",
    )(*ins)


def _moe_kernel(te_ref, nu_ref, x_ref, g_ref, w1_ref, b1_ref, w2_ref, b2_ref, y_ref, *, d_ff):
    @pl.when(pl.program_id(0) < nu_ref[0])
    def _():
        a = jnp.dot(x_ref[...], w1_ref[0], preferred_element_type=F32) + b1_ref[0]
        x_glu = jnp.minimum(a[:, :d_ff], SWIGLU_LIMIT)
        x_lin = jnp.clip(a[:, d_ff:], -SWIGLU_LIMIT, SWIGLU_LIMIT)
        mid = x_glu * jax.nn.sigmoid(SWIGLU_ALPHA * x_glu) * (x_lin + 1.0)
        y = jnp.dot(mid, w2_ref[0], preferred_element_type=F32) + b2_ref[0]
        y_ref[...] = y * g_ref[...]

    @pl.when(pl.program_id(0) >= nu_ref[0])
    def _():
        y_ref[...] = jnp.zeros(y_ref.shape, F32)


def _moe_experts(xs, row_gate, tile_expert, n_used, w1, b1, w2, b2):
    rows, d = xs.shape
    n_tiles = rows // MOE_TILE
    ne, _, ff2 = w1.shape
    d_ff = ff2 // 2
    row = lambda i, te, nu: (i, 0)
    exp = lambda i, te, nu: (te[i], 0, 0)
    vmem = 2 * 4 * (2 * MOE_TILE * d + d * ff2 + d_ff * d + ff2 + d) + 4 * 4 * MOE_TILE * ff2
    grid_spec = pltpu.PrefetchScalarGridSpec(
        num_scalar_prefetch=2,
        grid=(n_tiles,),
        in_specs=[pl.BlockSpec((MOE_TILE, d), row), pl.BlockSpec((MOE_TILE, 1), row),
                  pl.BlockSpec((1, d, ff2), exp), pl.BlockSpec((1, 1, ff2), exp),
                  pl.BlockSpec((1, d_ff, d), exp), pl.BlockSpec((1, 1, d), exp)],
        out_specs=pl.BlockSpec((MOE_TILE, d), row),
    )
    return pl.pallas_call(
        functools.partial(_moe_kernel, d_ff=d_ff),
        grid_spec=grid_spec,
        out_shape=jax.ShapeDtypeStruct((rows, d), F32),
        compiler_params=pltpu.CompilerParams(dimension_semantics=("arbitrary",),
                                             vmem_limit_bytes=_vmem_limit(vmem)),
        name="moe",
    )(tile_expert, n_used, xs, row_gate.reshape(rows, 1), w1, b1.reshape(ne, 1, ff2), w2,
      b2.reshape(ne, 1, d))


def _moe(hf, top_i, top_g, w1, b1, w2, b2):
    n, d = hf.shape
    n_assign = n * TOP_K
    e_flat = top_i.reshape(-1)
    onehot = (e_flat[:, None] == jnp.arange(N_EXPERTS, dtype=I32)[None, :]).astype(I32)
    rank = jnp.sum((jnp.cumsum(onehot, axis=0) - onehot) * onehot, axis=1)
    sizes = jnp.sum(onehot, axis=0)
    padded = (sizes + MOE_TILE - 1) // MOE_TILE * MOE_TILE
    pend = jnp.cumsum(padded)
    dest = (pend - padded)[e_flat] + rank
    n_tiles = -(-n_assign // MOE_TILE) + N_EXPERTS
    rows = n_tiles * MOE_TILE
    tok = jnp.arange(n_assign, dtype=I32) // TOP_K
    row_tok = jnp.zeros((rows,), I32).at[dest].set(tok)
    row_gate = jnp.zeros((rows,), F32).at[dest].set(top_g.reshape(-1))
    n_used = (pend[-1] // MOE_TILE).astype(I32)
    tile_expert = jnp.searchsorted(pend, jnp.arange(n_tiles, dtype=I32) * MOE_TILE, side="right").astype(I32)
    last = jnp.take(tile_expert, jnp.maximum(n_used - 1, 0))
    tile_expert = jnp.where(jnp.arange(n_tiles) < n_used, tile_expert, last)
    xs = jnp.take(hf, row_tok, axis=0)
    ys = _moe_experts(xs, row_gate, tile_expert, n_used.reshape(1), w1, b1, w2, b2)
    return jnp.take(ys, dest, axis=0).reshape(n, TOP_K, d).sum(axis=1)


def _final_kernel(x_ref, f_ref, g_ref, y_ref):
    y_ref[...] = _rms(x_ref[...] + f_ref[...], g_ref[...])


def _final(x1, f, g, tm):
    n, d = x1.shape
    row = lambda i: (i, 0)
    return pl.pallas_call(
        _final_kernel,
        grid=(n // tm,),
        in_specs=[pl.BlockSpec((tm, d), row), pl.BlockSpec((tm, d), row), pl.BlockSpec((1, d), lambda i: (0, 0))],
        out_specs=pl.BlockSpec((tm, d), row),
        out_shape=jax.ShapeDtypeStruct((n, d), F32),
        compiler_params=pltpu.CompilerParams(dimension_semantics=("arbitrary",)),
        name="final",
    )(x1, f, g.reshape(1, d))


def kernel(x_prompt, x_sample, cache_k, cache_v, state_ssm_re, state_ssm_im, norm_mix_g, w_in, attn_out_norm_g, ssm_a_re, ssm_a_im, ssm_log_dt, ssm_b_re, ssm_b_im, ssm_c_re, ssm_c_im, ssm_d, w_glu, b_glu, ssm_out_norm_g, w_out, norm_ffn_g, w_router, b_router, w_moe1, b_moe1, w_moe2, b_moe2, norm_final_g):
    depth = w_in.shape[0]
    assert depth == 1, "single-layer trunk"
    bp, s, d = x_prompt.shape
    bs, t, _ = x_sample.shape
    n_buf = cache_k.shape[2]
    n_heads, head_dim = cache_k.shape[3], cache_k.shape[4]
    assert head_dim == HEAD_DIM
    a = n_heads * head_dim
    g = ssm_a_re.shape[1]
    gn = g * SSM_STATE
    assert s % (max(dd for _, dd in DILATED_GROUPS) * KEYS_BACK) == 0
    np_, ns = bp * s, bs * t
    tm_p, tm_s = 256, ns
    l = 0
    ssm_p = (ssm_a_re[l], ssm_a_im[l], ssm_log_dt[l], ssm_b_re[l], ssm_b_im[l], ssm_c_re[l], ssm_c_im[l],
             ssm_d[l])
    tail = dict(attn_out_norm_g=attn_out_norm_g[l], w_glu=w_glu[l], b_glu=b_glu[l],
                ssm_out_norm_g=ssm_out_norm_g[l], w_out=w_out[l], norm_ffn_g=norm_ffn_g[l],
                w_router=w_router[l], b_router=b_router[l])

    xp = x_prompt.reshape(np_, d)
    qp, kp, vp, up = _proj(xp, norm_mix_g[l], w_in[l], a, tm_p)
    attn_p = _attn_prompt(qp.reshape(bp, s, a), kp.reshape(bp, s, a), vp.reshape(bp, s, a))
    zeros = jnp.zeros((bp, gn), F32)
    y_p, hp_re, hp_im = _ssm(up.reshape(bp, s, -1), zeros, zeros, _ssm_tables(*ssm_p, PROMPT_CHUNK),
                             PROMPT_CHUNK)
    x1_p, hf_p, ti_p, tg_p = _finish(xp, attn_p.reshape(np_, a), y_p.reshape(np_, -1), tail, tm_p)

    xs = x_sample.reshape(ns, d)
    qs, ks, vs, us = _proj(xs, norm_mix_g[l], w_in[l], a, tm_s)
    attn_s = _attn_sample(qs.reshape(bs, t, a), ks.reshape(bs, t, a), vs.reshape(bs, t, a),
                          cache_k[l].reshape(bs, n_buf, a), cache_v[l].reshape(bs, n_buf, a))
    y_s, hs_re, hs_im = _ssm(us.reshape(bs, t, -1), state_ssm_re[l].reshape(bs, gn),
                             state_ssm_im[l].reshape(bs, gn), _ssm_tables(*ssm_p, t), t)
    x1_s, hf_s, ti_s, tg_s = _finish(xs, attn_s.reshape(ns, a), y_s.reshape(ns, -1), tail, tm_s)

    hf = jnp.concatenate([hf_p, hf_s], axis=0)
    top_i = jnp.concatenate([ti_p, ti_s], axis=0)[:, :TOP_K]
    top_g = jnp.concatenate([tg_p, tg_s], axis=0)[:, :TOP_K]
    f = _moe(hf, top_i, top_g, w_moe1[l], b_moe1[l], w_moe2[l], b_moe2[l])
    y_prompt = _final(x1_p, f[:np_], norm_final_g, tm_p).reshape(bp, s, d)
    y_sample = _final(x1_s, f[np_:], norm_final_g, tm_s).reshape(bs, t, d)

    keep = min(max(w for w, _ in DILATED_GROUPS), s)
    k5 = lambda z, b_, s_: z.reshape(1, b_, s_, n_heads, head_dim)
    st = lambda z, b_: z.reshape(1, b_, g, SSM_STATE)
    return (y_prompt, y_sample,
            k5(kp, bp, s)[:, :, s - keep:], k5(vp, bp, s)[:, :, s - keep:], st(hp_re, bp), st(hp_im, bp),
            k5(ks, bs, t), k5(vs, bs, t), st(hs_re, bs), st(hs_im, bs))
```

```python
import functools
import math

import numpy as np
import jax
import jax.numpy as jnp
from jax import lax
from jax.experimental import pallas as pl
from jax.experimental.pallas import tpu as pltpu

F32 = jnp.float32
I32 = jnp.int32

V7X_LANES = 128
V7X_SUBLANES = 8
V7X_VMEM_BYTES = 64 * 1024 * 1024

HEAD_DIM = 64
HEADS_PER_LANE_TILE = V7X_LANES // HEAD_DIM
DILATED_GROUPS = ((128, 1), (512, 4), (2048, 16))
KEYS_BACK = 128
SSM_GROUP = 16
SSM_STATE = 64
N_EXPERTS = 32
TOP_K = 4
SWIGLU_LIMIT = 7.0
SWIGLU_ALPHA = 1.702
RMS_EPS = 1e-5
NEG_INF = -1e30
PROMPT_CHUNK = 16
MOE_TILE = 256


def _vmem_limit(nbytes):
    return int(min(nbytes + (8 << 20), V7X_VMEM_BYTES - (8 << 20)))


def _rms(x, g):
    return x * lax.rsqrt(jnp.mean(x * x, axis=-1, keepdims=True) + RMS_EPS) * g


def _proj_kernel(x_ref, g_ref, w_ref, q_ref, k_ref, v_ref, u_ref, *, attn_width, q_scale):
    h = _rms(x_ref[...], g_ref[...])
    z = jnp.dot(h, w_ref[...], preferred_element_type=F32)
    a = attn_width
    q_ref[...] = z[:, :a] * q_scale
    k_ref[...] = z[:, a:2 * a]
    v_ref[...] = z[:, 2 * a:3 * a]
    u_ref[...] = z[:, 3 * a:]


def _proj(x2d, norm_g, w_in, attn_width, tm):
    n, d = x2d.shape
    mix = w_in.shape[1]
    ssm_width = mix - 3 * attn_width
    row = lambda i: (i, 0)
    const = lambda i: (0, 0)
    out_shape = (jax.ShapeDtypeStruct((n, attn_width), F32),) * 3 + (
        jax.ShapeDtypeStruct((n, ssm_width), F32),)
    vmem = 2 * 4 * (tm * d + d * mix + tm * mix) + 4 * tm * mix
    return pl.pallas_call(
        functools.partial(_proj_kernel, attn_width=attn_width, q_scale=HEAD_DIM ** -0.5),
        grid=(n // tm,),
        in_specs=[pl.BlockSpec((tm, d), row), pl.BlockSpec((1, d), const),
                  pl.BlockSpec((d, mix), const)],
        out_specs=(pl.BlockSpec((tm, attn_width), row),) * 3 + (pl.BlockSpec((tm, ssm_width), row),),
        out_shape=out_shape,
        compiler_params=pltpu.CompilerParams(dimension_semantics=("arbitrary",),
                                             vmem_limit_bytes=_vmem_limit(vmem)),
        name="proj",
    )(x2d, norm_g.reshape(1, d), w_in)


def _attn_prompt_kernel(q_ref, k_ref, v_ref, o_ref, m_sc, l_sc, acc_sc, *, seq):
    nb = KEYS_BACK
    lane = lax.broadcasted_iota(I32, (1, V7X_LANES), 1)
    head_a = lane < HEAD_DIM
    qi = lax.broadcasted_iota(I32, (nb, 2 * nb), 0)
    kj = lax.broadcasted_iota(I32, (nb, 2 * nb), 1)
    rel = nb + qi - kj
    band = (rel >= 0) & (rel <= nb)
    n_groups = len(DILATED_GROUPS)

    def rows(ref, start, d):
        if d == 1:
            return ref[pl.ds(pl.multiple_of(start, nb), nb), :]
        return ref[pl.ds(start, nb, stride=d), :]

    def put(ref, start, d, val):
        if d == 1:
            ref[pl.ds(pl.multiple_of(start, nb), nb), :] = val
        else:
            ref[pl.ds(start, nb, stride=d), :] = val

    for gi, (w, d) in enumerate(DILATED_GROUPS):
        assert w // d == nb
        nblk = seq // (d * nb)

        def block(t, carry, gi=gi, d=d, nblk=nblk):
            r = t // nblk
            n = t % nblk
            cur = r + d * nb * n
            prev = r + d * nb * jnp.maximum(n - 1, 0)
            q = rows(q_ref, cur, d)
            kk = jnp.concatenate([rows(k_ref, prev, d), rows(k_ref, cur, d)], axis=0)
            vv = jnp.concatenate([rows(v_ref, prev, d), rows(v_ref, cur, d)], axis=0)
            qs = jnp.concatenate([jnp.where(head_a, q, 0.0), jnp.where(head_a, 0.0, q)], axis=0)
            s = lax.dot_general(qs, kk, (((1,), (1,)), ((), ())), preferred_element_type=F32)
            valid = band & (kj >= jnp.where(n > 0, 0, nb))
            s = jnp.where(jnp.concatenate([valid, valid], axis=0), s, NEG_INF)
            m = jnp.max(s, axis=-1, keepdims=True)
            p = jnp.exp(s - m)
            l = jnp.sum(p, axis=-1, keepdims=True)
            o = jnp.where(head_a,
                          jnp.dot(p[:nb], vv, preferred_element_type=F32),
                          jnp.dot(p[nb:], vv, preferred_element_type=F32))
            m_g = jnp.where(head_a, m[:nb], m[nb:])
            l_g = jnp.where(head_a, l[:nb], l[nb:])
            if gi == 0:
                m_new, l_new, acc_new = m_g, l_g, o
            else:
                m_old = rows(m_sc, cur, d)
                m_new = jnp.maximum(m_old, m_g)
                c_old = jnp.exp(m_old - m_new)
                c_g = jnp.exp(m_g - m_new)
                l_new = c_old * rows(l_sc, cur, d) + c_g * l_g
                acc_new = c_old * rows(acc_sc, cur, d) + c_g * o
            if gi == n_groups - 1:
                put(o_ref, cur, d, acc_new / l_new)
            else:
                put(m_sc, cur, d, m_new)
                put(l_sc, cur, d, l_new)
                put(acc_sc, cur, d, acc_new)
            return carry

        lax.fori_loop(0, seq // nb, block, 0)


def _attn_prompt(q, k, v):
    b, s, a = q.shape
    spec = pl.BlockSpec((None, s, V7X_LANES), lambda i, j: (i, 0, j))
    vmem = 4 * s * V7X_LANES * (2 * 4 + 3)
    return pl.pallas_call(
        functools.partial(_attn_prompt_kernel, seq=s),
        grid=(b, a // V7X_LANES),
        in_specs=[spec, spec, spec],
        out_specs=spec,
        out_shape=jax.ShapeDtypeStruct((b, s, a), F32),
        scratch_shapes=[pltpu.VMEM((s, V7X_LANES), F32)] * 3,
        compiler_params=pltpu.CompilerParams(dimension_semantics=("arbitrary", "arbitrary"),
                                             vmem_limit_bytes=_vmem_limit(vmem)),
        name="attn_prompt",
    )(q, k, v)


def _sample_key_counts(n_buf, n_new, n_pad):
    cnt = np.zeros((n_new, n_pad), np.float32)
    for w, d in DILATED_GROUPS:
        for i in range(n_new):
            for j in range(w // d + 1):
                idx = n_buf + i - d * j
                if idx >= 0:
                    cnt[i, idx] += 1.0
    return cnt


def _attn_sample_kernel(q_ref, kn_ref, vn_ref, kc_ref, vc_ref, cnt_ref, o_ref, k_all, v_all,
                        *, n_buf, n_new, n_heads):
    a = q_ref.shape[-1]
    n_pad = k_all.shape[0]
    k_all[pl.ds(0, n_buf), :] = kc_ref[...]
    v_all[pl.ds(0, n_buf), :] = vc_ref[...]
    k_all[pl.ds(n_buf, n_pad - n_buf), :] = jnp.zeros((n_pad - n_buf, a), F32)
    v_all[pl.ds(n_buf, n_pad - n_buf), :] = jnp.zeros((n_pad - n_buf, a), F32)
    k_all[pl.ds(n_buf, n_new), :] = kn_ref[...]
    v_all[pl.ds(n_buf, n_new), :] = vn_ref[...]
    rows = n_new * n_heads
    lane_head = lax.broadcasted_iota(I32, (rows, a), 1) // HEAD_DIM
    row_head = lax.broadcasted_iota(I32, (rows, a), 0) % n_heads
    own = lane_head == row_head
    q = q_ref[...]
    q_rep = jnp.broadcast_to(q[:, None, :], (n_new, n_heads, a)).reshape(rows, a)
    qb = jnp.where(own, q_rep, 0.0)
    s = lax.dot_general(qb, k_all[...], (((1,), (1,)), ((), ())), preferred_element_type=F32)
    cnt = cnt_ref[...]
    cnt = jnp.broadcast_to(cnt[:, None, :], (n_new, n_heads, n_pad)).reshape(rows, n_pad)
    valid = cnt > 0.0
    m = jnp.max(jnp.where(valid, s, NEG_INF), axis=-1, keepdims=True)
    p = jnp.where(valid, cnt * jnp.exp(s - m), 0.0)
    den = jnp.sum(p, axis=-1, keepdims=True)
    o = jnp.dot(p, v_all[...], preferred_element_type=F32) / den
    o = jnp.where(own, o, 0.0)
    o_ref[...] = jnp.sum(o.reshape(n_new, n_heads, a), axis=1)


def _attn_sample(q, k_new, v_new, cache_k, cache_v):
    b, t, a = q.shape
    n_buf = cache_k.shape[1]
    n_heads = a // HEAD_DIM
    n_pad = n_buf + V7X_LANES
    cnt = jnp.asarray(_sample_key_counts(n_buf, t, n_pad))
    new = pl.BlockSpec((None, t, a), lambda i: (i, 0, 0))
    buf = pl.BlockSpec((None, n_buf, a), lambda i: (i, 0, 0))
    vmem = 4 * (2 * 2 * n_buf * a + 2 * n_pad * a + 6 * t * n_heads * n_pad)
    return pl.pallas_call(
        functools.partial(_attn_sample_kernel, n_buf=n_buf, n_new=t, n_heads=n_heads),
        grid=(b,),
        in_specs=[new, new, new, buf, buf, pl.BlockSpec((t, n_pad), lambda i: (0, 0))],
        out_specs=new,
        out_shape=jax.ShapeDtypeStruct((b, t, a), F32),
        scratch_shapes=[pltpu.VMEM((n_pad, a), F32)] * 2,
        compiler_params=pltpu.CompilerParams(dimension_semantics=("arbitrary",),
                                             vmem_limit_bytes=_vmem_limit(vmem)),
        name="attn_sample",
    )(q, k_new, v_new, cache_k, cache_v, cnt)


def _ssm_tables(a_re, a_im, log_dt, b_re, b_im, c_re, c_im, d_skip, chunk):
    hp = lax.Precision.HIGHEST
    g, n = a_re.shape
    c = b_re.shape[-1]
    dt = jnp.exp(log_dt)[:, None]
    x, y = a_re * dt, a_im * dt
    ex = jnp.exp(x)
    ar, ai = ex * jnp.cos(y), ex * jnp.sin(y)
    nr = jnp.expm1(x) * jnp.cos(y) - 2.0 * jnp.sin(0.5 * y) ** 2
    ni = ai
    den = a_re * a_re + a_im * a_im
    fr = (nr * a_re + ni * a_im) / den
    fi = (ni * a_re - nr * a_im) / den
    bbr = fr[..., None] * b_re - fi[..., None] * b_im
    bbi = fr[..., None] * b_im + fi[..., None] * b_re
    pr, pi = [jnp.ones_like(ar)], [jnp.zeros_like(ar)]
    for _ in range(chunk):
        pr.append(pr[-1] * ar - pi[-1] * ai)
        pi.append(pr[-2] * ai + pi[-1] * ar)
    pr, pi = jnp.stack(pr), jnp.stack(pi)
    cpr = c_re[None] * pr[:, :, None, :] - c_im[None] * pi[:, :, None, :]
    cpi = c_re[None] * pi[:, :, None, :] + c_im[None] * pr[:, :, None, :]
    kern = (jnp.einsum("tgcn,gnd->tgcd", cpr[:chunk], bbr, precision=hp)
            - jnp.einsum("tgcn,gnd->tgcd", cpi[:chunk], bbi, precision=hp))
    lag = np.arange(chunk)[None, :] - np.arange(chunk)[:, None]
    toep = kern[np.maximum(lag, 0)]
    toep = jnp.where((lag >= 0)[:, :, None, None, None], toep, 0.0)
    toep = toep.transpose(2, 0, 4, 1, 3).reshape(g, chunk * c, chunk * c)
    wr = pr[chunk - 1::-1][:, :, :, None] * bbr[None] - pi[chunk - 1::-1][:, :, :, None] * bbi[None]
    wi = pr[chunk - 1::-1][:, :, :, None] * bbi[None] + pi[chunk - 1::-1][:, :, :, None] * bbr[None]
    wr = wr.transpose(1, 0, 3, 2).reshape(g, chunk * c, n)
    wi = wi.transpose(1, 0, 3, 2).reshape(g, chunk * c, n)
    vr = cpr[1:].transpose(1, 3, 0, 2).reshape(g, n, chunk * c)
    vi = -cpi[1:].transpose(1, 3, 0, 2).reshape(g, n, chunk * c)
    half = (np.arange(g) % 2)[:, None, None]
    z_w = jnp.zeros_like(wr)
    pad_w = lambda m: jnp.where(half == 0, jnp.concatenate([m, z_w], -1), jnp.concatenate([z_w, m], -1))
    z_v = jnp.zeros_like(vr)
    pad_v = lambda m: jnp.where(half == 0, jnp.concatenate([m, z_v], 1), jnp.concatenate([z_v, m], 1))
    d_t = jnp.tile(d_skip, (1, chunk)).reshape(g, 1, chunk * c)
    return dict(toep=toep, wr=pad_w(wr), wi=pad_w(wi), vr=pad_v(vr), vi=pad_v(vi), d=d_t,
                al_re=pr[chunk].reshape(1, g * n), al_im=pi[chunk].reshape(1, g * n))


def _ssm_state_kernel(u_ref, wr_ref, wi_ref, er_ref, ei_ref):
    er = jnp.dot(u_ref[0], wr_ref[0], preferred_element_type=F32)
    ei = jnp.dot(u_ref[0], wi_ref[0], preferred_element_type=F32)
    er_ref[...] = er + jnp.dot(u_ref[1], wr_ref[1], preferred_element_type=F32)
    ei_ref[...] = ei + jnp.dot(u_ref[1], wi_ref[1], preferred_element_type=F32)


def _ssm_scan_kernel(er_ref, ei_ref, ar_ref, ai_ref, hr_ref, hi_ref, xr_ref, xi_ref, lr_ref, li_ref,
                     *, n_chunks, nb):
    shape = hr_ref.shape
    ar = jnp.broadcast_to(ar_ref[...], shape)
    ai = jnp.broadcast_to(ai_ref[...], shape)

    def step(j, carry):
        xr, xi = carry
        sl = pl.ds(pl.multiple_of(j * nb, nb), nb)
        xr_ref[sl, :] = xr
        xi_ref[sl, :] = xi
        return (ar * xr - ai * xi + er_ref[sl, :], ar * xi + ai * xr + ei_ref[sl, :])

    xr, xi = lax.fori_loop(0, n_chunks, step, (hr_ref[...], hi_ref[...]))
    lr_ref[...] = xr
    li_ref[...] = xi


def _ssm_out_kernel(u_ref, t_ref, d_ref, xr_ref, xi_ref, vr_ref, vi_ref, y_ref):
    xr = xr_ref[...]
    xi = xi_ref[...]
    for a in range(2):
        u = u_ref[a]
        y = jnp.dot(u, t_ref[a], preferred_element_type=F32) + d_ref[a] * u
        y = y + jnp.dot(xr, vr_ref[a], preferred_element_type=F32)
        y_ref[a] = y + jnp.dot(xi, vi_ref[a], preferred_element_type=F32)


def _ssm(u, h0_re, h0_im, tab, chunk):
    b, s, width = u.shape
    g = width // SSM_GROUP
    n_chunks = s // chunk
    lc = chunk * SSM_GROUP
    r = n_chunks * b
    gn = g * SSM_STATE
    ut = u.reshape(b, n_chunks, chunk, g, SSM_GROUP).transpose(3, 1, 0, 2, 4).reshape(g, r, lc)
    pair = lambda i: (i, 0, 0)
    cp = lambda vmem: pltpu.CompilerParams(dimension_semantics=("arbitrary",),
                                           vmem_limit_bytes=_vmem_limit(vmem))
    er, ei = pl.pallas_call(
        _ssm_state_kernel,
        grid=(g // 2,),
        in_specs=[pl.BlockSpec((2, r, lc), pair), pl.BlockSpec((2, lc, V7X_LANES), pair),
                  pl.BlockSpec((2, lc, V7X_LANES), pair)],
        out_specs=(pl.BlockSpec((r, V7X_LANES), lambda i: (0, i)),) * 2,
        out_shape=(jax.ShapeDtypeStruct((r, gn), F32),) * 2,
        compiler_params=cp(2 * 4 * (2 * r * lc + 4 * lc * V7X_LANES + 2 * r * V7X_LANES)),
        name="ssm_state",
    )(ut, tab["wr"], tab["wi"])
    lanes = 4 * V7X_LANES
    col = lambda i: (0, i)
    xr, xi, lr, li = pl.pallas_call(
        functools.partial(_ssm_scan_kernel, n_chunks=n_chunks, nb=b),
        grid=(gn // lanes,),
        in_specs=[pl.BlockSpec((r, lanes), col), pl.BlockSpec((r, lanes), col),
                  pl.BlockSpec((1, lanes), col), pl.BlockSpec((1, lanes), col),
                  pl.BlockSpec((b, lanes), col), pl.BlockSpec((b, lanes), col)],
        out_specs=(pl.BlockSpec((r, lanes), col),) * 2 + (pl.BlockSpec((b, lanes), col),) * 2,
        out_shape=(jax.ShapeDtypeStruct((r, gn), F32),) * 2 + (jax.ShapeDtypeStruct((b, gn), F32),) * 2,
        compiler_params=cp(2 * 4 * 4 * r * lanes),
        name="ssm_scan",
    )(er, ei, tab["al_re"], tab["al_im"], h0_re, h0_im)
    y = pl.pallas_call(
        _ssm_out_kernel,
        grid=(g // 2,),
        in_specs=[pl.BlockSpec((2, r, lc), pair), pl.BlockSpec((2, lc, lc), pair),
                  pl.BlockSpec((2, 1, lc), pair),
                  pl.BlockSpec((r, V7X_LANES), lambda i: (0, i)),
                  pl.BlockSpec((r, V7X_LANES), lambda i: (0, i)),
                  pl.BlockSpec((2, V7X_LANES, lc), pair), pl.BlockSpec((2, V7X_LANES, lc), pair)],
        out_specs=pl.BlockSpec((2, r, lc), pair),
        out_shape=jax.ShapeDtypeStruct((g, r, lc), F32),
        compiler_params=cp(2 * 4 * (4 * r * lc + 2 * lc * lc + 2 * r * V7X_LANES + 4 * V7X_LANES * lc)),
        name="ssm_out",
    )(ut, tab["toep"], tab["d"], xr, xi, tab["vr"], tab["vi"])
    y = y.reshape(g, n_chunks, b, chunk, SSM_GROUP).transpose(2, 1, 3, 0, 4).reshape(b, s, width)
    return y, lr, li


def _finish_kernel(x_ref, attn_ref, ssm_ref, ga_ref, wg_ref, bg_ref, gs_ref, wo_ref, gf_ref,
                   wr_ref, br_ref, base_ref, x1_ref, hf_ref, ti_ref, tg_ref, cnt_ref, seen):
    a = attn_ref.shape[-1]
    tm = x_ref.shape[0]

    @pl.when(pl.program_id(0) == 0)
    def _():
        seen[...] = base_ref[...]

    g = jax.nn.gelu(ssm_ref[...])
    gate = jax.nn.sigmoid(jnp.dot(g, wg_ref[...], preferred_element_type=F32) + bg_ref[...])
    mixed_a = _rms(attn_ref[...], ga_ref[...])
    mixed_s = _rms(g * gate, gs_ref[...])
    x1 = (x_ref[...] + jnp.dot(mixed_a, wo_ref[pl.ds(0, a), :], preferred_element_type=F32)
          + jnp.dot(mixed_s, wo_ref[pl.ds(a, wo_ref.shape[0] - a), :], preferred_element_type=F32))
    x1_ref[...] = x1
    hf = _rms(x1, gf_ref[...])
    hf_ref[...] = hf
    logits = jnp.dot(hf, wr_ref[...], preferred_element_type=F32,
                     precision=lax.Precision.HIGHEST) + br_ref[...]
    lane = lax.broadcasted_iota(I32, logits.shape, 1)
    work = logits
    vals, idxs = [], []
    for _ in range(TOP_K):
        m = jnp.max(work, axis=-1, keepdims=True)
        idx = jnp.min(jnp.where(work == m, lane, V7X_LANES), axis=-1, keepdims=True)
        vals.append(m)
        idxs.append(idx)
        work = jnp.where(lane == idx, -jnp.inf, work)
    exps = [jnp.exp(v - vals[0]) for v in vals]
    den = exps[0]
    for e in exps[1:]:
        den = den + e
    picked = jnp.zeros(logits.shape, F32)
    for k in range(TOP_K):
        picked = picked + jnp.where(lane == idxs[k], 1.0, 0.0)
    tri = jnp.where(lax.broadcasted_iota(I32, (tm, tm), 0) > lax.broadcasted_iota(I32, (tm, tm), 1), 1.0, 0.0)
    before = jnp.dot(tri, picked, preferred_element_type=F32) + seen[...]
    ti = jnp.zeros(logits.shape, I32)
    tg = jnp.zeros(logits.shape, F32)
    for k in range(TOP_K):
        rank = jnp.sum(jnp.where(lane == idxs[k], before, 0.0), axis=-1, keepdims=True)
        ti = jnp.where(lane == k, idxs[k], ti)
        ti = jnp.where(lane == TOP_K + k, rank.astype(I32), ti)
        tg = jnp.where(lane == k, exps[k] / den, tg)
    ti_ref[...] = ti
    tg_ref[...] = tg
    seen[...] = seen[...] + jnp.sum(picked, axis=0, keepdims=True)
    cnt_ref[...] = seen[...]


def _finish(x2d, attn, ssm_y, p, seen, tm):
    n, d = x2d.shape
    a = attn.shape[1]
    sw = ssm_y.shape[1]
    ne = p["w_router"].shape[1]
    w_router = jnp.pad(p["w_router"], ((0, 0), (0, V7X_LANES - ne)))
    b_router = jnp.pad(p["b_router"], (0, V7X_LANES - ne), constant_values=-jnp.inf).reshape(1, V7X_LANES)
    row = lambda i: (i, 0)
    const = lambda i: (0, 0)
    full = lambda arr: pl.BlockSpec(arr.shape, const)
    ins = [x2d, attn, ssm_y, p["attn_out_norm_g"].reshape(1, a), p["w_glu"], p["b_glu"].reshape(1, sw),
           p["ssm_out_norm_g"].reshape(1, sw), p["w_out"], p["norm_ffn_g"].reshape(1, d),
           w_router, b_router, seen]
    in_specs = [pl.BlockSpec((tm, d), row), pl.BlockSpec((tm, a), row), pl.BlockSpec((tm, sw), row)]
    in_specs += [full(arr) for arr in ins[3:]]
    vmem = 2 * 4 * (tm * (3 * d + a + sw + 2 * V7X_LANES) + sw * sw + d * d + d * V7X_LANES) + 16 * tm * d
    return pl.pallas_call(
        _finish_kernel,
        grid=(n // tm,),
        in_specs=in_specs,
        out_specs=(pl.BlockSpec((tm, d), row), pl.BlockSpec((tm, d), row),
                   pl.BlockSpec((tm, V7X_LANES), row), pl.BlockSpec((tm, V7X_LANES), row),
                   pl.BlockSpec((1, V7X_LANES), const)),
        out_shape=(jax.ShapeDtypeStruct((n, d), F32), jax.ShapeDtypeStruct((n, d), F32),
                   jax.ShapeDtypeStruct((n, V7X_LANES), I32), jax.ShapeDtypeStruct((n, V7X_LANES), F32),
                   jax.ShapeDtypeStruct((1, V7X_LANES), F32)),
        scratch_shapes=[pltpu.VMEM((1, V7X_LANES), F32)],
        compiler_params=pltpu.CompilerParams(dimension_semantics=("arbitrary",),
                                             vmem_limit_bytes=_vmem_limit(vmem)),
        name="finish",
    )(*ins)


def _row_copy(src_ref, src_row, dst_ref, dst_row, sem):
    return pltpu.make_async_copy(src_ref.at[pl.ds(src_row, 1), :], dst_ref.at[pl.ds(dst_row, 1), :], sem)


def _dispatch_kernel(dest_ref, hp_ref, hs_ref, xs_ref, sem, *, n_tiles_p):
    def scatter(hf_ref):
        tm = hf_ref.shape[0]

        def token(r, carry):
            for k in range(TOP_K):
                _row_copy(hf_ref, r, xs_ref, dest_ref[0, 0, r * TOP_K + k], sem).start(priority=k % 2)
            return carry

        lax.fori_loop(0, tm, token, 0, unroll=8)
        rows = pl.ds(0, tm * TOP_K)
        pltpu.make_async_copy(xs_ref.at[rows, :], xs_ref.at[rows, :], sem).wait()

    @pl.when(pl.program_id(0) < n_tiles_p)
    def _():
        scatter(hp_ref)

    @pl.when(pl.program_id(0) == n_tiles_p)
    def _():
        scatter(hs_ref)


def _dispatch(hf_p, dest_p, hf_s, dest_s, tm):
    (n_p, d), n_s = hf_p.shape, hf_s.shape[0]
    assert n_p % tm == 0 and n_s <= tm
    n_tiles_p = n_p // tm
    dest3 = jnp.concatenate([dest_p.reshape(n_tiles_p, tm * TOP_K),
                             jnp.pad(dest_s.reshape(1, n_s * TOP_K), ((0, 0), (0, (tm - n_s) * TOP_K)))])
    dest3 = dest3.reshape(n_tiles_p + 1, 1, tm * TOP_K)
    return pl.pallas_call(
        functools.partial(_dispatch_kernel, n_tiles_p=n_tiles_p),
        grid=(n_tiles_p + 1,),
        in_specs=[pl.BlockSpec((1, 1, tm * TOP_K), lambda i: (i, 0, 0), memory_space=pltpu.SMEM),
                  pl.BlockSpec((tm, d), lambda i: (jnp.minimum(i, n_tiles_p - 1), 0)),
                  pl.BlockSpec((n_s, d), lambda i: (0, 0))],
        out_specs=pl.BlockSpec(memory_space=pl.ANY),
        out_shape=jax.ShapeDtypeStruct(((n_p + n_s) * TOP_K, d), F32),
        scratch_shapes=[pltpu.SemaphoreType.DMA(())],
        compiler_params=pltpu.CompilerParams(dimension_semantics=("arbitrary",)),
        name="dispatch",
    )(dest3, hf_p, hf_s)


def _moe_kernel(vt_ref, ve_ref, lo_ref, hi_ref, x_ref, w1_ref, b1_ref, w2_ref, b2_ref, y_ref, *, d_ff):
    v = pl.program_id(0)
    lo, hi = lo_ref[v], hi_ref[v]
    first = jnp.logical_or(v == 0, vt_ref[v] != vt_ref[jnp.maximum(v - 1, 0)])

    @pl.when(hi > lo)
    def _():
        a = jnp.dot(x_ref[...], w1_ref[0], preferred_element_type=F32) + b1_ref[0]
        x_glu = jnp.minimum(a[:, :d_ff], SWIGLU_LIMIT)
        x_lin = jnp.clip(a[:, d_ff:], -SWIGLU_LIMIT, SWIGLU_LIMIT)
        mid = x_glu * jax.nn.sigmoid(SWIGLU_ALPHA * x_glu) * (x_lin + 1.0)
        y = jnp.dot(mid, w2_ref[0], preferred_element_type=F32) + b2_ref[0]
        rows = lax.broadcasted_iota(I32, (y.shape[0], 1), 0)
        mine = (rows >= lo) & (rows < hi)

        @pl.when(first)
        def _():
            y_ref[...] = jnp.where(mine, y, 0.0)

        @pl.when(jnp.logical_not(first))
        def _():
            y_ref[...] = jnp.where(mine, y, y_ref[...])


def _moe_experts(xs, visits, w1, b1, w2, b2):
    rows, d = xs.shape
    ne, _, ff2 = w1.shape
    d_ff = ff2 // 2
    n_visits = visits[0].shape[0]
    row = lambda v, vt, ve, lo, hi: (vt[v], 0)
    exp = lambda v, vt, ve, lo, hi: (ve[v], 0, 0)
    vmem = 2 * 4 * (2 * MOE_TILE * d + d * ff2 + d_ff * d + ff2 + d) + 4 * 4 * MOE_TILE * ff2
    grid_spec = pltpu.PrefetchScalarGridSpec(
        num_scalar_prefetch=4,
        grid=(n_visits,),
        in_specs=[pl.BlockSpec((MOE_TILE, d), row),
                  pl.BlockSpec((1, d, ff2), exp), pl.BlockSpec((1, 1, ff2), exp),
                  pl.BlockSpec((1, d_ff, d), exp), pl.BlockSpec((1, 1, d), exp)],
        out_specs=pl.BlockSpec((MOE_TILE, d), row),
    )
    return pl.pallas_call(
        functools.partial(_moe_kernel, d_ff=d_ff),
        grid_spec=grid_spec,
        out_shape=jax.ShapeDtypeStruct((rows, d), F32),
        compiler_params=pltpu.CompilerParams(dimension_semantics=("arbitrary",),
                                             vmem_limit_bytes=_vmem_limit(vmem)),
        name="moe",
    )(*visits, xs, w1, b1.reshape(ne, 1, ff2), w2, b2.reshape(ne, 1, d))


def _moe_visits(sizes, n_rows):
    n_tiles = n_rows // MOE_TILE
    n_visits = n_tiles + N_EXPERTS - 1
    ends = jnp.cumsum(sizes)
    starts = ends - sizes
    first_tile = starts // MOE_TILE
    n_vis = jnp.where(sizes > 0, (ends - 1) // MOE_TILE - first_tile + 1, 0)
    vis_end = jnp.cumsum(n_vis)
    total = vis_end[-1]
    v = jnp.minimum(jnp.arange(n_visits, dtype=I32), total - 1)
    e = jnp.sum((vis_end[None, :] <= v[:, None]).astype(I32), axis=1)
    pick = lambda tab: jnp.sum(jnp.where(e[:, None] == jnp.arange(N_EXPERTS)[None, :], tab[None, :], 0), axis=1)
    tile = pick(first_tile) + v - pick(vis_end - n_vis)
    lo = jnp.maximum(pick(starts), tile * MOE_TILE) - tile * MOE_TILE
    hi = jnp.minimum(pick(ends), (tile + 1) * MOE_TILE) - tile * MOE_TILE
    live = jnp.arange(n_visits) < total
    return (tile.astype(I32), e.astype(I32), jnp.where(live, lo, 0).astype(I32),
            jnp.where(live, hi, 0).astype(I32))


def _final_kernel(dest_ref, nxt_ref, x_ref, tg_ref, g_ref, ys_ref, y_ref, buf, sem, *, tm):
    i = pl.program_id(0)
    n = pl.num_programs(0)

    def gather(d_ref, slot):
        def token(r, carry):
            for k in range(TOP_K):
                _row_copy(ys_ref, d_ref[0, 0, r * TOP_K + k], buf.at[slot, k], r,
                          sem.at[slot]).start(priority=k % 2)
            return carry
        lax.fori_loop(0, tm, token, 0, unroll=8)

    @pl.when(i == 0)
    def _():
        gather(dest_ref, 0)

    for slot in range(2):
        @pl.when(jnp.logical_and(i + 1 < n, (i + 1) % 2 == slot))
        def _(slot=slot):
            gather(nxt_ref, slot)

    slot = i % 2
    pltpu.make_async_copy(buf.at[slot], buf.at[slot], sem.at[slot]).wait()
    tg = tg_ref[...]
    f = tg[:, 0:1] * buf[slot, 0]
    for k in range(1, TOP_K):
        f = f + tg[:, k:k + 1] * buf[slot, k]
    y_ref[...] = _rms(x_ref[...] + f, g_ref[...])


def _final(x1, tg, dest, ys, g, tm):
    n, d = x1.shape
    n_tiles = n // tm
    dest3 = dest.reshape(n_tiles, 1, tm * TOP_K)
    row = lambda i: (i, 0)
    smem = lambda index_map: pl.BlockSpec((1, 1, tm * TOP_K), index_map, memory_space=pltpu.SMEM)
    vmem = 4 * (2 * TOP_K * tm * d + 2 * 2 * tm * d + 2 * tm * V7X_LANES + 4 * tm * d)
    return pl.pallas_call(
        functools.partial(_final_kernel, tm=tm),
        grid=(n_tiles,),
        in_specs=[smem(lambda i: (i, 0, 0)), smem(lambda i: (jnp.minimum(i + 1, n_tiles - 1), 0, 0)),
                  pl.BlockSpec((tm, d), row), pl.BlockSpec((tm, V7X_LANES), row),
                  pl.BlockSpec((1, d), lambda i: (0, 0)), pl.BlockSpec(memory_space=pl.ANY)],
        out_specs=pl.BlockSpec((tm, d), row),
        out_shape=jax.ShapeDtypeStruct((n, d), F32),
        scratch_shapes=[pltpu.VMEM((2, TOP_K, tm, d), F32), pltpu.SemaphoreType.DMA((2,))],
        compiler_params=pltpu.CompilerParams(dimension_semantics=("arbitrary",),
                                             vmem_limit_bytes=_vmem_limit(vmem)),
        name="final",
    )(dest3, dest3, x1, tg, g.reshape(1, d), ys)


def kernel(x_prompt, x_sample, cache_k, cache_v, state_ssm_re, state_ssm_im, norm_mix_g, w_in, attn_out_norm_g, ssm_a_re, ssm_a_im, ssm_log_dt, ssm_b_re, ssm_b_im, ssm_c_re, ssm_c_im, ssm_d, w_glu, b_glu, ssm_out_norm_g, w_out, norm_ffn_g, w_router, b_router, w_moe1, b_moe1, w_moe2, b_moe2, norm_final_g):
    depth = w_in.shape[0]
    assert depth == 1, "single-layer trunk"
    bp, s, d = x_prompt.shape
    bs, t, _ = x_sample.shape
    n_buf = cache_k.shape[2]
    n_heads, head_dim = cache_k.shape[3], cache_k.shape[4]
    assert head_dim == HEAD_DIM
    a = n_heads * head_dim
    g = ssm_a_re.shape[1]
    gn = g * SSM_STATE
    assert s % (max(dd for _, dd in DILATED_GROUPS) * KEYS_BACK) == 0
    np_, ns = bp * s, bs * t
    tm_p, tm_s = 256, ns
    l = 0
    ssm_p = (ssm_a_re[l], ssm_a_im[l], ssm_log_dt[l], ssm_b_re[l], ssm_b_im[l], ssm_c_re[l], ssm_c_im[l],
             ssm_d[l])
    tail = dict(attn_out_norm_g=attn_out_norm_g[l], w_glu=w_glu[l], b_glu=b_glu[l],
                ssm_out_norm_g=ssm_out_norm_g[l], w_out=w_out[l], norm_ffn_g=norm_ffn_g[l],
                w_router=w_router[l], b_router=b_router[l])

    xp = x_prompt.reshape(np_, d)
    qp, kp, vp, up = _proj(xp, norm_mix_g[l], w_in[l], a, tm_p)
    attn_p = _attn_prompt(qp.reshape(bp, s, a), kp.reshape(bp, s, a), vp.reshape(bp, s, a))
    zeros = jnp.zeros((bp, gn), F32)
    y_p, hp_re, hp_im = _ssm(up.reshape(bp, s, -1), zeros, zeros, _ssm_tables(*ssm_p, PROMPT_CHUNK),
                             PROMPT_CHUNK)
    seen = jnp.zeros((1, V7X_LANES), F32)
    x1_p, hf_p, ti_p, tg_p, seen = _finish(xp, attn_p.reshape(np_, a), y_p.reshape(np_, -1), tail, seen, tm_p)

    xs = x_sample.reshape(ns, d)
    qs, ks, vs, us = _proj(xs, norm_mix_g[l], w_in[l], a, tm_s)
    attn_s = _attn_sample(qs.reshape(bs, t, a), ks.reshape(bs, t, a), vs.reshape(bs, t, a),
                          cache_k[l].reshape(bs, n_buf, a), cache_v[l].reshape(bs, n_buf, a))
    y_s, hs_re, hs_im = _ssm(us.reshape(bs, t, -1), state_ssm_re[l].reshape(bs, gn),
                             state_ssm_im[l].reshape(bs, gn), _ssm_tables(*ssm_p, t), t)
    x1_s, hf_s, ti_s, tg_s, seen = _finish(xs, attn_s.reshape(ns, a), y_s.reshape(ns, -1), tail, seen, tm_s)

    sizes = seen[0, :N_EXPERTS].astype(I32)
    starts = jnp.cumsum(sizes) - sizes
    experts = jnp.arange(N_EXPERTS, dtype=I32)[None, None, :]

    def sorted_row(ti):
        hit = ti[:, :TOP_K, None] == experts
        return jnp.sum(jnp.where(hit, starts[None, None, :], 0), axis=-1) + ti[:, TOP_K:2 * TOP_K]

    dest_p, dest_s = sorted_row(ti_p), sorted_row(ti_s)
    n_rows = (np_ + ns) * TOP_K
    assert n_rows % MOE_TILE == 0
    xs_sorted = _dispatch(hf_p, dest_p, hf_s, dest_s, tm_p)
    ys = _moe_experts(xs_sorted, _moe_visits(sizes, n_rows), w_moe1[l], b_moe1[l], w_moe2[l], b_moe2[l])
    y_prompt = _final(x1_p, tg_p, dest_p, ys, norm_final_g, tm_p).reshape(bp, s, d)
    y_sample = _final(x1_s, tg_s, dest_s, ys, norm_final_g, tm_s).reshape(bs, t, d)

    keep = min(max(w for w, _ in DILATED_GROUPS), s)
    k5 = lambda z, b_, s_: z.reshape(1, b_, s_, n_heads, head_dim)
    st = lambda z, b_: z.reshape(1, b_, g, SSM_STATE)
    return (y_prompt, y_sample,
            k5(kp, bp, s)[:, :, s - keep:], k5(vp, bp, s)[:, :, s - keep:], st(hp_re, bp), st(hp_im, bp),
            k5(ks, bs, t), k5(vs, bs, t), st(hs_re, bs), st(hs_im, bs))
```

```python
import functools
import math

import numpy as np
import jax
import jax.numpy as jnp
from jax import lax
from jax.experimental import pallas as pl
from jax.experimental.pallas import tpu as pltpu

F32 = jnp.float32
I32 = jnp.int32

V7X_LANES = 128
V7X_SUBLANES = 8
V7X_VMEM_BYTES = 64 * 1024 * 1024

HEAD_DIM = 64
HEADS_PER_LANE_TILE = V7X_LANES // HEAD_DIM
DILATED_GROUPS = ((128, 1), (512, 4), (2048, 16))
KEYS_BACK = 128
SSM_GROUP = 16
SSM_STATE = 64
N_EXPERTS = 32
TOP_K = 4
SWIGLU_LIMIT = 7.0
SWIGLU_ALPHA = 1.702
RMS_EPS = 1e-5
NEG_INF = -1e30
PROMPT_CHUNK = 16
MOE_TILE = 256


def _vmem_limit(nbytes):
    return int(min(nbytes + (8 << 20), V7X_VMEM_BYTES - (8 << 20)))


def _rms(x, g):
    return x * lax.rsqrt(jnp.mean(x * x, axis=-1, keepdims=True) + RMS_EPS) * g


def _proj_kernel(x_ref, g_ref, w_ref, q_ref, k_ref, v_ref, u_ref, *, attn_width, q_scale):
    h = _rms(x_ref[...], g_ref[...])
    z = jnp.dot(h, w_ref[...], preferred_element_type=F32)
    a = attn_width
    q_ref[...] = z[:, :a] * q_scale
    k_ref[...] = z[:, a:2 * a]
    v_ref[...] = z[:, 2 * a:3 * a]
    u_ref[...] = z[:, 3 * a:]


def _proj(x2d, norm_g, w_in, attn_width, tm):
    n, d = x2d.shape
    mix = w_in.shape[1]
    ssm_width = mix - 3 * attn_width
    row = lambda i: (i, 0)
    const = lambda i: (0, 0)
    out_shape = (jax.ShapeDtypeStruct((n, attn_width), F32),) * 3 + (
        jax.ShapeDtypeStruct((n, ssm_width), F32),)
    vmem = 2 * 4 * (tm * d + d * mix + tm * mix) + 4 * tm * mix
    return pl.pallas_call(
        functools.partial(_proj_kernel, attn_width=attn_width, q_scale=HEAD_DIM ** -0.5),
        grid=(n // tm,),
        in_specs=[pl.BlockSpec((tm, d), row), pl.BlockSpec((1, d), const),
                  pl.BlockSpec((d, mix), const)],
        out_specs=(pl.BlockSpec((tm, attn_width), row),) * 3 + (pl.BlockSpec((tm, ssm_width), row),),
        out_shape=out_shape,
        compiler_params=pltpu.CompilerParams(dimension_semantics=("arbitrary",),
                                             vmem_limit_bytes=_vmem_limit(vmem)),
        name="proj",
    )(x2d, norm_g.reshape(1, d), w_in)


def _attn_prompt_kernel(q_ref, k_ref, v_ref, o_ref, *scratch, seq, unroll):
    nb = KEYS_BACK
    n_groups = len(DILATED_GROUPS)
    m_sc, l_sc, acc_sc = scratch[:n_groups], scratch[n_groups:2 * n_groups], scratch[2 * n_groups:]
    lane = lax.broadcasted_iota(I32, (1, V7X_LANES), 1)
    head_a = lane < HEAD_DIM
    qi = lax.broadcasted_iota(I32, (nb, 2 * nb), 0)
    kj = lax.broadcasted_iota(I32, (nb, 2 * nb), 1)
    rel = nb + qi - kj
    band = (rel >= 0) & (rel <= nb)

    def rows(ref, start, d):
        if d == 1:
            return ref[pl.ds(pl.multiple_of(start, nb), nb), :]
        return ref[pl.ds(start, nb, stride=d), :]

    def put(ref, start, d, val):
        if d == 1:
            ref[pl.ds(pl.multiple_of(start, nb), nb), :] = val
        else:
            ref[pl.ds(start, nb, stride=d), :] = val

    for gi, (w, d) in enumerate(DILATED_GROUPS):
        assert w // d == nb
        nblk = seq // (d * nb)

        def block(t, gi=gi, d=d, nblk=nblk):
            r = t // nblk
            n = t % nblk
            cur = r + d * nb * n
            prev = r + d * nb * jnp.maximum(n - 1, 0)
            q = rows(q_ref, cur, d)
            kk = jnp.concatenate([rows(k_ref, prev, d), rows(k_ref, cur, d)], axis=0)
            vv = jnp.concatenate([rows(v_ref, prev, d), rows(v_ref, cur, d)], axis=0)
            qs = jnp.concatenate([jnp.where(head_a, q, 0.0), jnp.where(head_a, 0.0, q)], axis=0)
            s = lax.dot_general(qs, kk, (((1,), (1,)), ((), ())), preferred_element_type=F32)
            valid = band & (kj >= jnp.where(n > 0, 0, nb))
            s = jnp.where(jnp.concatenate([valid, valid], axis=0), s, NEG_INF)
            m = jnp.max(s, axis=-1, keepdims=True)
            p = jnp.exp(s - m)
            l = jnp.sum(p, axis=-1, keepdims=True)
            o = jnp.where(head_a,
                          jnp.dot(p[:nb], vv, preferred_element_type=F32),
                          jnp.dot(p[nb:], vv, preferred_element_type=F32))
            put(m_sc[gi], cur, d, jnp.where(head_a, m[:nb], m[nb:]))
            put(l_sc[gi], cur, d, jnp.where(head_a, l[:nb], l[nb:]))
            put(acc_sc[gi], cur, d, o)

        def blocks(tt, carry, block=block):
            for u in range(unroll):
                block(tt * unroll + u)
            return carry

        lax.fori_loop(0, seq // nb // unroll, blocks, 0)

    def merge(n, carry):
        sl = pl.ds(pl.multiple_of(n * nb, nb), nb)
        ms = [m[sl, :] for m in m_sc]
        m_all = functools.reduce(jnp.maximum, ms)
        cs = [jnp.exp(m - m_all) for m in ms]
        den = functools.reduce(jnp.add, [c * l[sl, :] for c, l in zip(cs, l_sc)])
        num = functools.reduce(jnp.add, [c * acc[sl, :] for c, acc in zip(cs, acc_sc)])
        o_ref[sl, :] = num / den
        return carry

    lax.fori_loop(0, seq // nb, merge, 0, unroll=2)


def _attn_prompt(q, k, v, unroll=8):
    b, s, a = q.shape
    n_scratch = 3 * len(DILATED_GROUPS)
    spec = pl.BlockSpec((None, s, V7X_LANES), lambda i, j: (i, 0, j))
    vmem = 4 * s * V7X_LANES * (2 * 4 + n_scratch)
    return pl.pallas_call(
        functools.partial(_attn_prompt_kernel, seq=s, unroll=unroll),
        grid=(b, a // V7X_LANES),
        in_specs=[spec, spec, spec],
        out_specs=spec,
        out_shape=jax.ShapeDtypeStruct((b, s, a), F32),
        scratch_shapes=[pltpu.VMEM((s, V7X_LANES), F32)] * n_scratch,
        compiler_params=pltpu.CompilerParams(dimension_semantics=("arbitrary", "arbitrary"),
                                             vmem_limit_bytes=_vmem_limit(vmem)),
        name="attn_prompt",
    )(q, k, v)


def _sample_key_counts(n_buf, n_new, n_pad):
    cnt = np.zeros((n_new, n_pad), np.float32)
    for w, d in DILATED_GROUPS:
        for i in range(n_new):
            for j in range(w // d + 1):
                idx = n_buf + i - d * j
                if idx >= 0:
                    cnt[i, idx] += 1.0
    return cnt


def _attn_sample_kernel(q_ref, kn_ref, vn_ref, kc_ref, vc_ref, cnt_ref, o_ref, k_all, v_all,
                        *, n_buf, n_new, n_heads):
    a = q_ref.shape[-1]
    n_pad = k_all.shape[0]
    k_all[pl.ds(0, n_buf), :] = kc_ref[...]
    v_all[pl.ds(0, n_buf), :] = vc_ref[...]
    k_all[pl.ds(n_buf, n_pad - n_buf), :] = jnp.zeros((n_pad - n_buf, a), F32)
    v_all[pl.ds(n_buf, n_pad - n_buf), :] = jnp.zeros((n_pad - n_buf, a), F32)
    k_all[pl.ds(n_buf, n_new), :] = kn_ref[...]
    v_all[pl.ds(n_buf, n_new), :] = vn_ref[...]
    rows = n_new * n_heads
    lane_head = lax.broadcasted_iota(I32, (rows, a), 1) // HEAD_DIM
    row_head = lax.broadcasted_iota(I32, (rows, a), 0) % n_heads
    own = lane_head == row_head
    q = q_ref[...]
    q_rep = jnp.broadcast_to(q[:, None, :], (n_new, n_heads, a)).reshape(rows, a)
    qb = jnp.where(own, q_rep, 0.0)
    s = lax.dot_general(qb, k_all[...], (((1,), (1,)), ((), ())), preferred_element_type=F32)
    cnt = cnt_ref[...]
    cnt = jnp.broadcast_to(cnt[:, None, :], (n_new, n_heads, n_pad)).reshape(rows, n_pad)
    valid = cnt > 0.0
    m = jnp.max(jnp.where(valid, s, NEG_INF), axis=-1, keepdims=True)
    p = jnp.where(valid, cnt * jnp.exp(s - m), 0.0)
    den = jnp.sum(p, axis=-1, keepdims=True)
    o = jnp.dot(p, v_all[...], preferred_element_type=F32) / den
    o = jnp.where(own, o, 0.0)
    o_ref[...] = jnp.sum(o.reshape(n_new, n_heads, a), axis=1)


def _attn_sample(q, k_new, v_new, cache_k, cache_v):
    b, t, a = q.shape
    n_buf = cache_k.shape[1]
    n_heads = a // HEAD_DIM
    n_pad = n_buf + V7X_LANES
    cnt = jnp.asarray(_sample_key_counts(n_buf, t, n_pad))
    new = pl.BlockSpec((None, t, a), lambda i: (i, 0, 0))
    buf = pl.BlockSpec((None, n_buf, a), lambda i: (i, 0, 0))
    vmem = 4 * (2 * 2 * n_buf * a + 2 * n_pad * a + 6 * t * n_heads * n_pad)
    return pl.pallas_call(
        functools.partial(_attn_sample_kernel, n_buf=n_buf, n_new=t, n_heads=n_heads),
        grid=(b,),
        in_specs=[new, new, new, buf, buf, pl.BlockSpec((t, n_pad), lambda i: (0, 0))],
        out_specs=new,
        out_shape=jax.ShapeDtypeStruct((b, t, a), F32),
        scratch_shapes=[pltpu.VMEM((n_pad, a), F32)] * 2,
        compiler_params=pltpu.CompilerParams(dimension_semantics=("arbitrary",),
                                             vmem_limit_bytes=_vmem_limit(vmem)),
        name="attn_sample",
    )(q, k_new, v_new, cache_k, cache_v, cnt)


def _ssm_tables(a_re, a_im, log_dt, b_re, b_im, c_re, c_im, d_skip, chunk):
    hp = lax.Precision.HIGHEST
    g, n = a_re.shape
    c = b_re.shape[-1]
    dt = jnp.exp(log_dt)[:, None]
    x, y = a_re * dt, a_im * dt
    ex = jnp.exp(x)
    ar, ai = ex * jnp.cos(y), ex * jnp.sin(y)
    nr = jnp.expm1(x) * jnp.cos(y) - 2.0 * jnp.sin(0.5 * y) ** 2
    ni = ai
    den = a_re * a_re + a_im * a_im
    fr = (nr * a_re + ni * a_im) / den
    fi = (ni * a_re - nr * a_im) / den
    bbr = fr[..., None] * b_re - fi[..., None] * b_im
    bbi = fr[..., None] * b_im + fi[..., None] * b_re
    pr, pi = [jnp.ones_like(ar)], [jnp.zeros_like(ar)]
    for _ in range(chunk):
        pr.append(pr[-1] * ar - pi[-1] * ai)
        pi.append(pr[-2] * ai + pi[-1] * ar)
    pr, pi = jnp.stack(pr), jnp.stack(pi)
    cpr = c_re[None] * pr[:, :, None, :] - c_im[None] * pi[:, :, None, :]
    cpi = c_re[None] * pi[:, :, None, :] + c_im[None] * pr[:, :, None, :]
    kern = (jnp.einsum("tgcn,gnd->tgcd", cpr[:chunk], bbr, precision=hp)
            - jnp.einsum("tgcn,gnd->tgcd", cpi[:chunk], bbi, precision=hp))
    lag = np.arange(chunk)[None, :] - np.arange(chunk)[:, None]
    toep = kern[np.maximum(lag, 0)]
    toep = jnp.where((lag >= 0)[:, :, None, None, None], toep, 0.0)
    toep = toep.transpose(2, 0, 4, 1, 3).reshape(g, chunk * c, chunk * c)
    wr = pr[chunk - 1::-1][:, :, :, None] * bbr[None] - pi[chunk - 1::-1][:, :, :, None] * bbi[None]
    wi = pr[chunk - 1::-1][:, :, :, None] * bbi[None] + pi[chunk - 1::-1][:, :, :, None] * bbr[None]
    wr = wr.transpose(1, 0, 3, 2).reshape(g, chunk * c, n)
    wi = wi.transpose(1, 0, 3, 2).reshape(g, chunk * c, n)
    vr = cpr[1:].transpose(1, 3, 0, 2).reshape(g, n, chunk * c)
    vi = -cpi[1:].transpose(1, 3, 0, 2).reshape(g, n, chunk * c)
    half = (np.arange(g) % 2)[:, None, None]
    z_w = jnp.zeros_like(wr)
    pad_w = lambda m: jnp.where(half == 0, jnp.concatenate([m, z_w], -1), jnp.concatenate([z_w, m], -1))
    z_v = jnp.zeros_like(vr)
    pad_v = lambda m: jnp.where(half == 0, jnp.concatenate([m, z_v], 1), jnp.concatenate([z_v, m], 1))
    d_t = jnp.tile(d_skip, (1, chunk)).reshape(g, 1, chunk * c)
    return dict(toep=toep, wr=pad_w(wr), wi=pad_w(wi), vr=pad_v(vr), vi=pad_v(vi), d=d_t,
                al_re=pr[chunk].reshape(1, g * n), al_im=pi[chunk].reshape(1, g * n))


def _ssm_state_kernel(u_ref, wr_ref, wi_ref, er_ref, ei_ref):
    er = jnp.dot(u_ref[0], wr_ref[0], preferred_element_type=F32)
    ei = jnp.dot(u_ref[0], wi_ref[0], preferred_element_type=F32)
    er_ref[...] = er + jnp.dot(u_ref[1], wr_ref[1], preferred_element_type=F32)
    ei_ref[...] = ei + jnp.dot(u_ref[1], wi_ref[1], preferred_element_type=F32)


def _ssm_scan_kernel(er_ref, ei_ref, ar_ref, ai_ref, hr_ref, hi_ref, xr_ref, xi_ref, lr_ref, li_ref,
                     *, n_chunks, nb):
    shape = hr_ref.shape
    ar = jnp.broadcast_to(ar_ref[...], shape)
    ai = jnp.broadcast_to(ai_ref[...], shape)

    def step(j, carry):
        xr, xi = carry
        sl = pl.ds(pl.multiple_of(j * nb, nb), nb)
        xr_ref[sl, :] = xr
        xi_ref[sl, :] = xi
        return (ar * xr - ai * xi + er_ref[sl, :], ar * xi + ai * xr + ei_ref[sl, :])

    xr, xi = lax.fori_loop(0, n_chunks, step, (hr_ref[...], hi_ref[...]))
    lr_ref[...] = xr
    li_ref[...] = xi


def _ssm_out_kernel(u_ref, t_ref, d_ref, xr_ref, xi_ref, vr_ref, vi_ref, y_ref):
    xr = xr_ref[...]
    xi = xi_ref[...]
    for a in range(2):
        u = u_ref[a]
        y = jnp.dot(u, t_ref[a], preferred_element_type=F32) + d_ref[a] * u
        y = y + jnp.dot(xr, vr_ref[a], preferred_element_type=F32)
        y_ref[a] = y + jnp.dot(xi, vi_ref[a], preferred_element_type=F32)


def _ssm_scan(er, ei, tab, h0_re, h0_im, n_chunks):
    r, gn = er.shape
    b = r // n_chunks
    lanes = 4 * V7X_LANES
    col = lambda i: (0, i)
    return pl.pallas_call(
        functools.partial(_ssm_scan_kernel, n_chunks=n_chunks, nb=b),
        grid=(gn // lanes,),
        in_specs=[pl.BlockSpec((r, lanes), col), pl.BlockSpec((r, lanes), col),
                  pl.BlockSpec((1, lanes), col), pl.BlockSpec((1, lanes), col),
                  pl.BlockSpec((b, lanes), col), pl.BlockSpec((b, lanes), col)],
        out_specs=(pl.BlockSpec((r, lanes), col),) * 2 + (pl.BlockSpec((b, lanes), col),) * 2,
        out_shape=(jax.ShapeDtypeStruct((r, gn), F32),) * 2 + (jax.ShapeDtypeStruct((b, gn), F32),) * 2,
        compiler_params=pltpu.CompilerParams(dimension_semantics=("arbitrary",),
                                             vmem_limit_bytes=_vmem_limit(2 * 4 * 4 * r * lanes)),
        name="ssm_scan",
    )(er, ei, tab["al_re"], tab["al_im"], h0_re, h0_im)


def _ssm_state_t_kernel(a_ref, wr_ref, wi_ref, er_ref, ei_ref):
    def seq(b, carry):
        a0, a1 = a_ref[0, b], a_ref[1, b]
        er_ref[b] = (jnp.dot(wr_ref[0], a0, preferred_element_type=F32)
                     + jnp.dot(wr_ref[1], a1, preferred_element_type=F32))
        ei_ref[b] = (jnp.dot(wi_ref[0], a0, preferred_element_type=F32)
                     + jnp.dot(wi_ref[1], a1, preferred_element_type=F32))
        return carry

    lax.fori_loop(0, a_ref.shape[1], seq, 0)


def _ssm_out_t_kernel(a_ref, t_ref, d_ref, xr_ref, xi_ref, vr_ref, vi_ref, y_ref):
    def seq(b, carry):
        xr, xi = xr_ref[b], xi_ref[b]
        for a in range(2):
            at = a_ref[a, b]
            y = jnp.dot(t_ref[a], at, preferred_element_type=F32) + d_ref[a] * at
            y = y + jnp.dot(vr_ref[a], xr, preferred_element_type=F32)
            y_ref[a, b] = y + jnp.dot(vi_ref[a], xi, preferred_element_type=F32)
        return carry

    lax.fori_loop(0, a_ref.shape[1], seq, 0)


def _ssm_chunks_on_lanes(u, h0_re, h0_im, tab, chunk):
    b, s, width = u.shape
    g = width // SSM_GROUP
    n_chunks = s // chunk
    lc = chunk * SSM_GROUP
    assert n_chunks == V7X_LANES
    at = (u.reshape(b, n_chunks, chunk * width).transpose(0, 2, 1)
          .reshape(b, chunk, g, SSM_GROUP, n_chunks).transpose(2, 0, 1, 3, 4).reshape(g, b, lc, n_chunks))
    tr = lambda m: m.transpose(0, 2, 1)
    pair4 = lambda i: (i, 0, 0, 0)
    pair3 = lambda i: (i, 0, 0)
    cp = lambda vmem: pltpu.CompilerParams(dimension_semantics=("arbitrary",),
                                           vmem_limit_bytes=_vmem_limit(vmem))
    a_spec = pl.BlockSpec((2, b, lc, n_chunks), pair4)
    st_spec = pl.BlockSpec((None, b, V7X_LANES, n_chunks), pair4)
    w_spec = pl.BlockSpec((2, V7X_LANES, lc), pair3)
    a_bytes = 2 * b * lc * n_chunks
    st_bytes = b * V7X_LANES * n_chunks
    et_re, et_im = pl.pallas_call(
        _ssm_state_t_kernel,
        grid=(g // 2,),
        in_specs=[a_spec, w_spec, w_spec],
        out_specs=(st_spec,) * 2,
        out_shape=(jax.ShapeDtypeStruct((g // 2, b, V7X_LANES, n_chunks), F32),) * 2,
        compiler_params=cp(2 * 4 * (a_bytes + 2 * st_bytes + 4 * V7X_LANES * lc)),
        name="ssm_state_t",
    )(at, tr(tab["wr"]), tr(tab["wi"]))
    rows = lambda m: m.transpose(3, 1, 0, 2).reshape(n_chunks * b, g * SSM_STATE)
    xr, xi, lr, li = _ssm_scan(rows(et_re), rows(et_im), tab, h0_re, h0_im, n_chunks)
    cols = lambda m: m.reshape(n_chunks, b, g // 2, V7X_LANES).transpose(2, 1, 3, 0)
    yt = pl.pallas_call(
        _ssm_out_t_kernel,
        grid=(g // 2,),
        in_specs=[a_spec, pl.BlockSpec((2, lc, lc), pair3), pl.BlockSpec((2, lc, 1), pair3),
                  st_spec, st_spec, pl.BlockSpec((2, lc, V7X_LANES), pair3),
                  pl.BlockSpec((2, lc, V7X_LANES), pair3)],
        out_specs=a_spec,
        out_shape=jax.ShapeDtypeStruct((g, b, lc, n_chunks), F32),
        compiler_params=cp(2 * 4 * (2 * a_bytes + 2 * st_bytes + 2 * lc * lc + 6 * V7X_LANES * lc)),
        name="ssm_out_t",
    )(at, tr(tab["toep"]), tr(tab["d"]), cols(xr), cols(xi), tr(tab["vr"]), tr(tab["vi"]))
    y = (yt.reshape(g, b, chunk, SSM_GROUP, n_chunks).transpose(1, 2, 0, 3, 4)
         .reshape(b, chunk * width, n_chunks).transpose(0, 2, 1).reshape(b, s, width))
    return y, lr, li


def _ssm(u, h0_re, h0_im, tab, chunk):
    b, s, width = u.shape
    g = width // SSM_GROUP
    n_chunks = s // chunk
    lc = chunk * SSM_GROUP
    r = n_chunks * b
    gn = g * SSM_STATE
    ut = u.reshape(b, n_chunks, chunk, g, SSM_GROUP).transpose(3, 1, 0, 2, 4).reshape(g, r, lc)
    pair = lambda i: (i, 0, 0)
    cp = lambda vmem: pltpu.CompilerParams(dimension_semantics=("arbitrary",),
                                           vmem_limit_bytes=_vmem_limit(vmem))
    er, ei = pl.pallas_call(
        _ssm_state_kernel,
        grid=(g // 2,),
        in_specs=[pl.BlockSpec((2, r, lc), pair), pl.BlockSpec((2, lc, V7X_LANES), pair),
                  pl.BlockSpec((2, lc, V7X_LANES), pair)],
        out_specs=(pl.BlockSpec((r, V7X_LANES), lambda i: (0, i)),) * 2,
        out_shape=(jax.ShapeDtypeStruct((r, gn), F32),) * 2,
        compiler_params=cp(2 * 4 * (2 * r * lc + 4 * lc * V7X_LANES + 2 * r * V7X_LANES)),
        name="ssm_state",
    )(ut, tab["wr"], tab["wi"])
    xr, xi, lr, li = _ssm_scan(er, ei, tab, h0_re, h0_im, n_chunks)
    y = pl.pallas_call(
        _ssm_out_kernel,
        grid=(g // 2,),
        in_specs=[pl.BlockSpec((2, r, lc), pair), pl.BlockSpec((2, lc, lc), pair),
                  pl.BlockSpec((2, 1, lc), pair),
                  pl.BlockSpec((r, V7X_LANES), lambda i: (0, i)),
                  pl.BlockSpec((r, V7X_LANES), lambda i: (0, i)),
                  pl.BlockSpec((2, V7X_LANES, lc), pair), pl.BlockSpec((2, V7X_LANES, lc), pair)],
        out_specs=pl.BlockSpec((2, r, lc), pair),
        out_shape=jax.ShapeDtypeStruct((g, r, lc), F32),
        compiler_params=cp(2 * 4 * (4 * r * lc + 2 * lc * lc + 2 * r * V7X_LANES + 4 * V7X_LANES * lc)),
        name="ssm_out",
    )(ut, tab["toep"], tab["d"], xr, xi, tab["vr"], tab["vi"])
    y = y.reshape(g, n_chunks, b, chunk, SSM_GROUP).transpose(2, 1, 3, 0, 4).reshape(b, s, width)
    return y, lr, li


def _finish_kernel(x_ref, attn_ref, ssm_ref, ga_ref, wg_ref, bg_ref, gs_ref, wo_ref, gf_ref,
                   wr_ref, br_ref, base_ref, x1_ref, hf_ref, ti_ref, tg_ref, cnt_ref, seen):
    a = attn_ref.shape[-1]
    tm = x_ref.shape[0]

    @pl.when(pl.program_id(0) == 0)
    def _():
        seen[...] = base_ref[...]

    g = jax.nn.gelu(ssm_ref[...])
    gate = jax.nn.sigmoid(jnp.dot(g, wg_ref[...], preferred_element_type=F32) + bg_ref[...])
    mixed_a = _rms(attn_ref[...], ga_ref[...])
    mixed_s = _rms(g * gate, gs_ref[...])
    x1 = (x_ref[...] + jnp.dot(mixed_a, wo_ref[pl.ds(0, a), :], preferred_element_type=F32)
          + jnp.dot(mixed_s, wo_ref[pl.ds(a, wo_ref.shape[0] - a), :], preferred_element_type=F32))
    x1_ref[...] = x1
    hf = _rms(x1, gf_ref[...])
    hf_ref[...] = hf
    logits = jnp.dot(hf, wr_ref[...], preferred_element_type=F32,
                     precision=lax.Precision.HIGHEST) + br_ref[...]
    lane = lax.broadcasted_iota(I32, logits.shape, 1)
    work = logits
    vals, idxs = [], []
    for _ in range(TOP_K):
        m = jnp.max(work, axis=-1, keepdims=True)
        idx = jnp.min(jnp.where(work == m, lane, V7X_LANES), axis=-1, keepdims=True)
        vals.append(m)
        idxs.append(idx)
        work = jnp.where(lane == idx, -jnp.inf, work)
    exps = [jnp.exp(v - vals[0]) for v in vals]
    den = exps[0]
    for e in exps[1:]:
        den = den + e
    picked = jnp.zeros(logits.shape, F32)
    for k in range(TOP_K):
        picked = picked + jnp.where(lane == idxs[k], 1.0, 0.0)
    tri = jnp.where(lax.broadcasted_iota(I32, (tm, tm), 0) > lax.broadcasted_iota(I32, (tm, tm), 1), 1.0, 0.0)
    before = jnp.dot(tri, picked, preferred_element_type=F32) + seen[...]
    ti = jnp.zeros(logits.shape, I32)
    tg = jnp.zeros(logits.shape, F32)
    for k in range(TOP_K):
        rank = jnp.sum(jnp.where(lane == idxs[k], before, 0.0), axis=-1, keepdims=True)
        ti = jnp.where(lane == k, idxs[k], ti)
        ti = jnp.where(lane == TOP_K + k, rank.astype(I32), ti)
        tg = jnp.where(lane == k, exps[k] / den, tg)
    ti_ref[...] = ti
    tg_ref[...] = tg
    seen[...] = seen[...] + jnp.sum(picked, axis=0, keepdims=True)
    cnt_ref[...] = seen[...]


def _finish(x2d, attn, ssm_y, p, seen, tm):
    n, d = x2d.shape
    a = attn.shape[1]
    sw = ssm_y.shape[1]
    ne = p["w_router"].shape[1]
    w_router = jnp.pad(p["w_router"], ((0, 0), (0, V7X_LANES - ne)))
    b_router = jnp.pad(p["b_router"], (0, V7X_LANES - ne), constant_values=-jnp.inf).reshape(1, V7X_LANES)
    row = lambda i: (i, 0)
    const = lambda i: (0, 0)
    full = lambda arr: pl.BlockSpec(arr.shape, const)
    ins = [x2d, attn, ssm_y, p["attn_out_norm_g"].reshape(1, a), p["w_glu"], p["b_glu"].reshape(1, sw),
           p["ssm_out_norm_g"].reshape(1, sw), p["w_out"], p["norm_ffn_g"].reshape(1, d),
           w_router, b_router, seen]
    in_specs = [pl.BlockSpec((tm, d), row), pl.BlockSpec((tm, a), row), pl.BlockSpec((tm, sw), row)]
    in_specs += [full(arr) for arr in ins[3:]]
    vmem = 2 * 4 * (tm * (3 * d + a + sw + 2 * V7X_LANES) + sw * sw + d * d + d * V7X_LANES) + 16 * tm * d
    return pl.pallas_call(
        _finish_kernel,
        grid=(n // tm,),
        in_specs=in_specs,
        out_specs=(pl.BlockSpec((tm, d), row), pl.BlockSpec((tm, d), row),
                   pl.BlockSpec((tm, V7X_LANES), row), pl.BlockSpec((tm, V7X_LANES), row),
                   pl.BlockSpec((1, V7X_LANES), const)),
        out_shape=(jax.ShapeDtypeStruct((n, d), F32), jax.ShapeDtypeStruct((n, d), F32),
                   jax.ShapeDtypeStruct((n, V7X_LANES), I32), jax.ShapeDtypeStruct((n, V7X_LANES), F32),
                   jax.ShapeDtypeStruct((1, V7X_LANES), F32)),
        scratch_shapes=[pltpu.VMEM((1, V7X_LANES), F32)],
        compiler_params=pltpu.CompilerParams(dimension_semantics=("arbitrary",),
                                             vmem_limit_bytes=_vmem_limit(vmem)),
        name="finish",
    )(*ins)


def _row_copy(src_ref, src_row, dst_ref, dst_row, sem):
    return pltpu.make_async_copy(src_ref.at[pl.ds(src_row, 1), :], dst_ref.at[pl.ds(dst_row, 1), :], sem)


def _dispatch_kernel(dest_ref, hp_ref, hs_ref, xs_ref, sem, *, n_tiles_p):
    def scatter(hf_ref):
        tm = hf_ref.shape[0]

        def token(r, carry):
            for k in range(TOP_K):
                _row_copy(hf_ref, r, xs_ref, dest_ref[0, 0, r * TOP_K + k], sem).start(priority=k % 2)
            return carry

        lax.fori_loop(0, tm, token, 0, unroll=8)
        rows = pl.ds(0, tm * TOP_K)
        pltpu.make_async_copy(xs_ref.at[rows, :], xs_ref.at[rows, :], sem).wait()

    @pl.when(pl.program_id(0) < n_tiles_p)
    def _():
        scatter(hp_ref)

    @pl.when(pl.program_id(0) == n_tiles_p)
    def _():
        scatter(hs_ref)


def _dispatch(hf_p, dest_p, hf_s, dest_s, tm):
    (n_p, d), n_s = hf_p.shape, hf_s.shape[0]
    assert n_p % tm == 0 and n_s <= tm
    n_tiles_p = n_p // tm
    dest3 = jnp.concatenate([dest_p.reshape(n_tiles_p, tm * TOP_K),
                             jnp.pad(dest_s.reshape(1, n_s * TOP_K), ((0, 0), (0, (tm - n_s) * TOP_K)))])
    dest3 = dest3.reshape(n_tiles_p + 1, 1, tm * TOP_K)
    return pl.pallas_call(
        functools.partial(_dispatch_kernel, n_tiles_p=n_tiles_p),
        grid=(n_tiles_p + 1,),
        in_specs=[pl.BlockSpec((1, 1, tm * TOP_K), lambda i: (i, 0, 0), memory_space=pltpu.SMEM),
                  pl.BlockSpec((tm, d), lambda i: (jnp.minimum(i, n_tiles_p - 1), 0)),
                  pl.BlockSpec((n_s, d), lambda i: (0, 0))],
        out_specs=pl.BlockSpec(memory_space=pl.ANY),
        out_shape=jax.ShapeDtypeStruct(((n_p + n_s) * TOP_K, d), F32),
        scratch_shapes=[pltpu.SemaphoreType.DMA(())],
        compiler_params=pltpu.CompilerParams(dimension_semantics=("arbitrary",)),
        name="dispatch",
    )(dest3, hf_p, hf_s)


def _moe_kernel(vt_ref, ve_ref, lo_ref, hi_ref, x_ref, w1_ref, b1_ref, w2_ref, b2_ref, y_ref, w1b, w2b,
                *, d_ff):
    v = pl.program_id(0)
    lo, hi = lo_ref[v], hi_ref[v]
    prev = jnp.maximum(v - 1, 0)
    first = jnp.logical_or(v == 0, vt_ref[v] != vt_ref[prev])

    @pl.when(jnp.logical_or(v == 0, ve_ref[v] != ve_ref[prev]))
    def _():
        w1b[...] = w1_ref[0].astype(jnp.bfloat16)
        w2b[...] = w2_ref[0].astype(jnp.bfloat16)

    @pl.when(hi > lo)
    def _():
        a = jnp.dot(x_ref[...].astype(jnp.bfloat16), w1b[...], preferred_element_type=F32) + b1_ref[0]
        x_glu = jnp.minimum(a[:, :d_ff], SWIGLU_LIMIT)
        x_lin = jnp.clip(a[:, d_ff:], -SWIGLU_LIMIT, SWIGLU_LIMIT)
        mid = x_glu * jax.nn.sigmoid(SWIGLU_ALPHA * x_glu) * (x_lin + 1.0)
        y = jnp.dot(mid.astype(jnp.bfloat16), w2b[...], preferred_element_type=F32) + b2_ref[0]
        rows = lax.broadcasted_iota(I32, (y.shape[0], 1), 0)
        mine = (rows >= lo) & (rows < hi)

        @pl.when(first)
        def _():
            y_ref[...] = jnp.where(mine, y, 0.0)

        @pl.when(jnp.logical_not(first))
        def _():
            y_ref[...] = jnp.where(mine, y, y_ref[...])


def _moe_experts(xs, visits, w1, b1, w2, b2):
    rows, d = xs.shape
    ne, _, ff2 = w1.shape
    d_ff = ff2 // 2
    n_visits = visits[0].shape[0]
    row = lambda v, vt, ve, lo, hi: (vt[v], 0)
    exp = lambda v, vt, ve, lo, hi: (ve[v], 0, 0)
    vmem = (2 * 4 * (2 * MOE_TILE * d + d * ff2 + d_ff * d + ff2 + d) + 2 * (d * ff2 + d_ff * d)
            + 4 * 4 * MOE_TILE * ff2)
    grid_spec = pltpu.PrefetchScalarGridSpec(
        num_scalar_prefetch=4,
        grid=(n_visits,),
        in_specs=[pl.BlockSpec((MOE_TILE, d), row),
                  pl.BlockSpec((1, d, ff2), exp), pl.BlockSpec((1, 1, ff2), exp),
                  pl.BlockSpec((1, d_ff, d), exp), pl.BlockSpec((1, 1, d), exp)],
        out_specs=pl.BlockSpec((MOE_TILE, d), row),
        scratch_shapes=[pltpu.VMEM((d, ff2), jnp.bfloat16), pltpu.VMEM((d_ff, d), jnp.bfloat16)],
    )
    return pl.pallas_call(
        functools.partial(_moe_kernel, d_ff=d_ff),
        grid_spec=grid_spec,
        out_shape=jax.ShapeDtypeStruct((rows, d), F32),
        compiler_params=pltpu.CompilerParams(dimension_semantics=("arbitrary",),
                                             vmem_limit_bytes=_vmem_limit(vmem)),
        name="moe",
    )(*visits, xs, w1, b1.reshape(ne, 1, ff2), w2, b2.reshape(ne, 1, d))


def _moe_visits(sizes, n_rows):
    n_tiles = n_rows // MOE_TILE
    n_visits = n_tiles + N_EXPERTS - 1
    ends = jnp.cumsum(sizes)
    starts = ends - sizes
    first_tile = starts // MOE_TILE
    n_vis = jnp.where(sizes > 0, (ends - 1) // MOE_TILE - first_tile + 1, 0)
    vis_end = jnp.cumsum(n_vis)
    total = vis_end[-1]
    v = jnp.clip(jnp.arange(n_visits, dtype=I32), 0, jnp.maximum(total - 1, 0))
    e = jnp.sum((vis_end[None, :] <= v[:, None]).astype(I32), axis=1)
    pick = lambda tab: jnp.sum(jnp.where(e[:, None] == jnp.arange(N_EXPERTS)[None, :], tab[None, :], 0), axis=1)
    tile = pick(first_tile) + v - pick(vis_end - n_vis)
    lo = jnp.maximum(pick(starts), tile * MOE_TILE) - tile * MOE_TILE
    hi = jnp.minimum(pick(ends), (tile + 1) * MOE_TILE) - tile * MOE_TILE
    live = jnp.arange(n_visits) < total
    return (tile.astype(I32), e.astype(I32), jnp.where(live, lo, 0).astype(I32),
            jnp.where(live, hi, 0).astype(I32))


def _final_kernel(dest_ref, nxt_ref, x_ref, tg_ref, g_ref, ys_ref, y_ref, buf, sem, *, tm):
    i = pl.program_id(0)
    n = pl.num_programs(0)

    def gather(d_ref, slot):
        def token(r, carry):
            for k in range(TOP_K):
                _row_copy(ys_ref, d_ref[0, 0, r * TOP_K + k], buf.at[slot, k], r,
                          sem.at[slot]).start(priority=k % 2)
            return carry
        lax.fori_loop(0, tm, token, 0, unroll=8)

    @pl.when(i == 0)
    def _():
        gather(dest_ref, 0)

    for slot in range(2):
        @pl.when(jnp.logical_and(i + 1 < n, (i + 1) % 2 == slot))
        def _(slot=slot):
            gather(nxt_ref, slot)

    slot = i % 2
    pltpu.make_async_copy(buf.at[slot], buf.at[slot], sem.at[slot]).wait()
    tg = tg_ref[...]
    f = tg[:, 0:1] * buf[slot, 0]
    for k in range(1, TOP_K):
        f = f + tg[:, k:k + 1] * buf[slot, k]
    y_ref[...] = _rms(x_ref[...] + f, g_ref[...])


def _final(x1, tg, dest, ys, g, tm):
    n, d = x1.shape
    n_tiles = n // tm
    dest3 = dest.reshape(n_tiles, 1, tm * TOP_K)
    row = lambda i: (i, 0)
    smem = lambda index_map: pl.BlockSpec((1, 1, tm * TOP_K), index_map, memory_space=pltpu.SMEM)
    vmem = 4 * (2 * TOP_K * tm * d + 2 * 2 * tm * d + 2 * tm * V7X_LANES + 4 * tm * d)
    return pl.pallas_call(
        functools.partial(_final_kernel, tm=tm),
        grid=(n_tiles,),
        in_specs=[smem(lambda i: (i, 0, 0)), smem(lambda i: (jnp.minimum(i + 1, n_tiles - 1), 0, 0)),
                  pl.BlockSpec((tm, d), row), pl.BlockSpec((tm, V7X_LANES), row),
                  pl.BlockSpec((1, d), lambda i: (0, 0)), pl.BlockSpec(memory_space=pl.ANY)],
        out_specs=pl.BlockSpec((tm, d), row),
        out_shape=jax.ShapeDtypeStruct((n, d), F32),
        scratch_shapes=[pltpu.VMEM((2, TOP_K, tm, d), F32), pltpu.SemaphoreType.DMA((2,))],
        compiler_params=pltpu.CompilerParams(dimension_semantics=("arbitrary",),
                                             vmem_limit_bytes=_vmem_limit(vmem)),
        name="final",
    )(dest3, dest3, x1, tg, g.reshape(1, d), ys)


def kernel(x_prompt, x_sample, cache_k, cache_v, state_ssm_re, state_ssm_im, norm_mix_g, w_in, attn_out_norm_g, ssm_a_re, ssm_a_im, ssm_log_dt, ssm_b_re, ssm_b_im, ssm_c_re, ssm_c_im, ssm_d, w_glu, b_glu, ssm_out_norm_g, w_out, norm_ffn_g, w_router, b_router, w_moe1, b_moe1, w_moe2, b_moe2, norm_final_g):
    depth = w_in.shape[0]
    assert depth == 1, "single-layer trunk"
    bp, s, d = x_prompt.shape
    bs, t, _ = x_sample.shape
    n_buf = cache_k.shape[2]
    n_heads, head_dim = cache_k.shape[3], cache_k.shape[4]
    assert head_dim == HEAD_DIM
    a = n_heads * head_dim
    g = ssm_a_re.shape[1]
    gn = g * SSM_STATE
    assert s % (max(dd for _, dd in DILATED_GROUPS) * KEYS_BACK) == 0
    np_, ns = bp * s, bs * t
    tm_p, tm_s = 256, ns
    l = 0
    ssm_p = (ssm_a_re[l], ssm_a_im[l], ssm_log_dt[l], ssm_b_re[l], ssm_b_im[l], ssm_c_re[l], ssm_c_im[l],
             ssm_d[l])
    tail = dict(attn_out_norm_g=attn_out_norm_g[l], w_glu=w_glu[l], b_glu=b_glu[l],
                ssm_out_norm_g=ssm_out_norm_g[l], w_out=w_out[l], norm_ffn_g=norm_ffn_g[l],
                w_router=w_router[l], b_router=b_router[l])

    xp = x_prompt.reshape(np_, d)
    qp, kp, vp, up = _proj(xp, norm_mix_g[l], w_in[l], a, tm_p)
    attn_p = _attn_prompt(qp.reshape(bp, s, a), kp.reshape(bp, s, a), vp.reshape(bp, s, a))
    zeros = jnp.zeros((bp, gn), F32)
    y_p, hp_re, hp_im = _ssm_chunks_on_lanes(up.reshape(bp, s, -1), zeros, zeros,
                                             _ssm_tables(*ssm_p, PROMPT_CHUNK), PROMPT_CHUNK)
    seen = jnp.zeros((1, V7X_LANES), F32)
    x1_p, hf_p, ti_p, tg_p, seen = _finish(xp, attn_p.reshape(np_, a), y_p.reshape(np_, -1), tail, seen, tm_p)

    xs = x_sample.reshape(ns, d)
    qs, ks, vs, us = _proj(xs, norm_mix_g[l], w_in[l], a, tm_s)
    attn_s = _attn_sample(qs.reshape(bs, t, a), ks.reshape(bs, t, a), vs.reshape(bs, t, a),
                          cache_k[l].reshape(bs, n_buf, a), cache_v[l].reshape(bs, n_buf, a))
    y_s, hs_re, hs_im = _ssm(us.reshape(bs, t, -1), state_ssm_re[l].reshape(bs, gn),
                             state_ssm_im[l].reshape(bs, gn), _ssm_tables(*ssm_p, t), t)
    x1_s, hf_s, ti_s, tg_s, seen = _finish(xs, attn_s.reshape(ns, a), y_s.reshape(ns, -1), tail, seen, tm_s)

    sizes = seen[0, :N_EXPERTS].astype(I32)
    starts = jnp.cumsum(sizes) - sizes
    experts = jnp.arange(N_EXPERTS, dtype=I32)[None, None, :]

    def sorted_row(ti):
        hit = ti[:, :TOP_K, None] == experts
        return jnp.sum(jnp.where(hit, starts[None, None, :], 0), axis=-1) + ti[:, TOP_K:2 * TOP_K]

    dest_p, dest_s = sorted_row(ti_p), sorted_row(ti_s)
    n_rows = (np_ + ns) * TOP_K
    assert n_rows % MOE_TILE == 0
    xs_sorted = _dispatch(hf_p, dest_p, hf_s, dest_s, tm_p)
    ys = _moe_experts(xs_sorted, _moe_visits(sizes, n_rows), w_moe1[l], b_moe1[l], w_moe2[l], b_moe2[l])
    y_prompt = _final(x1_p, tg_p, dest_p, ys, norm_final_g, tm_p).reshape(bp, s, d)
    y_sample = _final(x1_s, tg_s, dest_s, ys, norm_final_g, tm_s).reshape(bs, t, d)

    keep = min(max(w for w, _ in DILATED_GROUPS), s)
    k5 = lambda z, b_, s_: z.reshape(1, b_, s_, n_heads, head_dim)
    st = lambda z, b_: z.reshape(1, b_, g, SSM_STATE)
    return (y_prompt, y_sample,
            k5(kp, bp, s)[:, :, s - keep:], k5(vp, bp, s)[:, :, s - keep:], st(hp_re, bp), st(hp_im, bp),
            k5(ks, bs, t), k5(vs, bs, t), st(hs_re, bs), st(hs_im, bs))
```

```python
import functools
import math

import numpy as np
import jax
import jax.numpy as jnp
from jax import lax
from jax.experimental import pallas as pl
from jax.experimental.pallas import tpu as pltpu

F32 = jnp.float32
I32 = jnp.int32

V7X_LANES = 128
V7X_SUBLANES = 8
V7X_VMEM_BYTES = 64 * 1024 * 1024

HEAD_DIM = 64
HEADS_PER_LANE_TILE = V7X_LANES // HEAD_DIM
DILATED_GROUPS = ((128, 1), (512, 4), (2048, 16))
KEYS_BACK = 128
SSM_GROUP = 16
SSM_STATE = 64
N_EXPERTS = 32
TOP_K = 4
SWIGLU_LIMIT = 7.0
SWIGLU_ALPHA = 1.702
RMS_EPS = 1e-5
NEG_INF = -1e30
PROMPT_CHUNK = 16
MOE_TILE = 256


def _vmem_limit(nbytes):
    return int(min(nbytes + (8 << 20), V7X_VMEM_BYTES - (8 << 20)))


def _rms(x, g):
    return x * lax.rsqrt(jnp.mean(x * x, axis=-1, keepdims=True) + RMS_EPS) * g


def _to_token_tiles(ref, val):
    rows, width = val.shape
    assert width == V7X_SUBLANES * V7X_LANES and ref.shape == (rows * V7X_SUBLANES, V7X_LANES)
    for j in range(V7X_SUBLANES):
        ref[pl.ds(j, rows, stride=V7X_SUBLANES), :] = val[:, j * V7X_LANES:(j + 1) * V7X_LANES]


def _from_token_tiles(ref):
    rows = ref.shape[0] // V7X_SUBLANES
    return jnp.concatenate([ref[pl.ds(j, rows, stride=V7X_SUBLANES), :] for j in range(V7X_SUBLANES)], axis=1)


def _proj_kernel(x_ref, g_ref, w_ref, q_ref, k_ref, v_ref, u_ref, *, attn_width, q_scale):
    h = _rms(x_ref[...], g_ref[...])
    z = jnp.dot(h, w_ref[...], preferred_element_type=F32)
    a = attn_width
    q_ref[...] = z[:, :a] * q_scale
    k_ref[...] = z[:, a:2 * a]
    v_ref[...] = z[:, 2 * a:3 * a]
    u_ref[...] = z[:, 3 * a:]


def _proj(x2d, norm_g, w_in, attn_width, tm):
    n, d = x2d.shape
    mix = w_in.shape[1]
    ssm_width = mix - 3 * attn_width
    row = lambda i: (i, 0)
    const = lambda i: (0, 0)
    out_shape = (jax.ShapeDtypeStruct((n, attn_width), F32),) * 3 + (
        jax.ShapeDtypeStruct((n, ssm_width), F32),)
    vmem = 2 * 4 * (tm * d + d * mix + tm * mix) + 4 * tm * mix
    return pl.pallas_call(
        functools.partial(_proj_kernel, attn_width=attn_width, q_scale=HEAD_DIM ** -0.5),
        grid=(n // tm,),
        in_specs=[pl.BlockSpec((tm, d), row), pl.BlockSpec((1, d), const),
                  pl.BlockSpec((d, mix), const)],
        out_specs=(pl.BlockSpec((tm, attn_width), row),) * 3 + (pl.BlockSpec((tm, ssm_width), row),),
        out_shape=out_shape,
        compiler_params=pltpu.CompilerParams(dimension_semantics=("arbitrary",),
                                             vmem_limit_bytes=_vmem_limit(vmem)),
        name="proj",
    )(x2d, norm_g.reshape(1, d), w_in)


def _attn_prompt_kernel(q_ref, k_ref, v_ref, o_ref, *scratch, seq, unroll):
    nb = KEYS_BACK
    n_groups = len(DILATED_GROUPS)
    m_sc, l_sc, acc_sc = scratch[:n_groups], scratch[n_groups:2 * n_groups], scratch[2 * n_groups:]
    lane = lax.broadcasted_iota(I32, (1, V7X_LANES), 1)
    head_a = lane < HEAD_DIM
    qi = lax.broadcasted_iota(I32, (nb, 2 * nb), 0)
    kj = lax.broadcasted_iota(I32, (nb, 2 * nb), 1)
    rel = nb + qi - kj
    band = (rel >= 0) & (rel <= nb)

    def rows(ref, start, d):
        if d == 1:
            return ref[pl.ds(pl.multiple_of(start, nb), nb), :]
        return ref[pl.ds(start, nb, stride=d), :]

    def put(ref, start, d, val):
        if d == 1:
            ref[pl.ds(pl.multiple_of(start, nb), nb), :] = val
        else:
            ref[pl.ds(start, nb, stride=d), :] = val

    for gi, (w, d) in enumerate(DILATED_GROUPS):
        assert w // d == nb
        nblk = seq // (d * nb)

        def block(t, gi=gi, d=d, nblk=nblk):
            r = t // nblk
            n = t % nblk
            cur = r + d * nb * n
            prev = r + d * nb * jnp.maximum(n - 1, 0)
            q = rows(q_ref, cur, d)
            kk = jnp.concatenate([rows(k_ref, prev, d), rows(k_ref, cur, d)], axis=0)
            vv = jnp.concatenate([rows(v_ref, prev, d), rows(v_ref, cur, d)], axis=0)
            qs = jnp.concatenate([jnp.where(head_a, q, 0.0), jnp.where(head_a, 0.0, q)], axis=0)
            s = lax.dot_general(qs, kk, (((1,), (1,)), ((), ())), preferred_element_type=F32)
            valid = band & (kj >= jnp.where(n > 0, 0, nb))
            s = jnp.where(jnp.concatenate([valid, valid], axis=0), s, NEG_INF)
            m = jnp.max(s, axis=-1, keepdims=True)
            p = jnp.exp(s - m)
            l = jnp.sum(p, axis=-1, keepdims=True)
            o = jnp.where(head_a,
                          jnp.dot(p[:nb], vv, preferred_element_type=F32),
                          jnp.dot(p[nb:], vv, preferred_element_type=F32))
            put(m_sc[gi], cur, d, jnp.where(head_a, m[:nb], m[nb:]))
            put(l_sc[gi], cur, d, jnp.where(head_a, l[:nb], l[nb:]))
            put(acc_sc[gi], cur, d, o)

        def blocks(tt, carry, block=block):
            for u in range(unroll):
                block(tt * unroll + u)
            return carry

        lax.fori_loop(0, seq // nb // unroll, blocks, 0)

    def merge(n, carry):
        sl = pl.ds(pl.multiple_of(n * nb, nb), nb)
        ms = [m[sl, :] for m in m_sc]
        m_all = functools.reduce(jnp.maximum, ms)
        cs = [jnp.exp(m - m_all) for m in ms]
        den = functools.reduce(jnp.add, [c * l[sl, :] for c, l in zip(cs, l_sc)])
        num = functools.reduce(jnp.add, [c * acc[sl, :] for c, acc in zip(cs, acc_sc)])
        o_ref[sl, :] = num / den
        return carry

    lax.fori_loop(0, seq // nb, merge, 0, unroll=2)


def _attn_prompt(q, k, v, unroll=8):
    b, s, a = q.shape
    n_scratch = 3 * len(DILATED_GROUPS)
    spec = pl.BlockSpec((None, s, V7X_LANES), lambda i, j: (i, 0, j))
    vmem = 4 * s * V7X_LANES * (2 * 4 + n_scratch)
    return pl.pallas_call(
        functools.partial(_attn_prompt_kernel, seq=s, unroll=unroll),
        grid=(b, a // V7X_LANES),
        in_specs=[spec, spec, spec],
        out_specs=spec,
        out_shape=jax.ShapeDtypeStruct((b, s, a), F32),
        scratch_shapes=[pltpu.VMEM((s, V7X_LANES), F32)] * n_scratch,
        compiler_params=pltpu.CompilerParams(dimension_semantics=("arbitrary", "arbitrary"),
                                             vmem_limit_bytes=_vmem_limit(vmem)),
        name="attn_prompt",
    )(q, k, v)


def _sample_key_counts(n_buf, n_new, n_pad):
    cnt = np.zeros((n_new, n_pad), np.float32)
    for w, d in DILATED_GROUPS:
        for i in range(n_new):
            for j in range(w // d + 1):
                idx = n_buf + i - d * j
                if idx >= 0:
                    cnt[i, idx] += 1.0
    return cnt


def _attn_sample_kernel(q_ref, kn_ref, vn_ref, kc_ref, vc_ref, cnt_ref, o_ref, k_all, v_all,
                        *, n_buf, n_new, n_heads):
    a = q_ref.shape[-1]
    n_pad = k_all.shape[0]
    k_all[pl.ds(0, n_buf), :] = kc_ref[...]
    v_all[pl.ds(0, n_buf), :] = vc_ref[...]
    k_all[pl.ds(n_buf, n_pad - n_buf), :] = jnp.zeros((n_pad - n_buf, a), F32)
    v_all[pl.ds(n_buf, n_pad - n_buf), :] = jnp.zeros((n_pad - n_buf, a), F32)
    k_all[pl.ds(n_buf, n_new), :] = kn_ref[...]
    v_all[pl.ds(n_buf, n_new), :] = vn_ref[...]
    rows = n_new * n_heads
    lane_head = lax.broadcasted_iota(I32, (rows, a), 1) // HEAD_DIM
    row_head = lax.broadcasted_iota(I32, (rows, a), 0) % n_heads
    own = lane_head == row_head
    q = q_ref[...]
    q_rep = jnp.broadcast_to(q[:, None, :], (n_new, n_heads, a)).reshape(rows, a)
    qb = jnp.where(own, q_rep, 0.0)
    s = lax.dot_general(qb, k_all[...], (((1,), (1,)), ((), ())), preferred_element_type=F32)
    cnt = cnt_ref[...]
    cnt = jnp.broadcast_to(cnt[:, None, :], (n_new, n_heads, n_pad)).reshape(rows, n_pad)
    valid = cnt > 0.0
    m = jnp.max(jnp.where(valid, s, NEG_INF), axis=-1, keepdims=True)
    p = jnp.where(valid, cnt * jnp.exp(s - m), 0.0)
    den = jnp.sum(p, axis=-1, keepdims=True)
    o = jnp.dot(p, v_all[...], preferred_element_type=F32) / den
    o = jnp.where(own, o, 0.0)
    o_ref[...] = jnp.sum(o.reshape(n_new, n_heads, a), axis=1)


def _attn_sample(q, k_new, v_new, cache_k, cache_v):
    b, t, a = q.shape
    n_buf = cache_k.shape[1]
    n_heads = a // HEAD_DIM
    n_pad = n_buf + V7X_LANES
    cnt = jnp.asarray(_sample_key_counts(n_buf, t, n_pad))
    new = pl.BlockSpec((None, t, a), lambda i: (i, 0, 0))
    buf = pl.BlockSpec((None, n_buf, a), lambda i: (i, 0, 0))
    vmem = 4 * (2 * 2 * n_buf * a + 2 * n_pad * a + 6 * t * n_heads * n_pad)
    return pl.pallas_call(
        functools.partial(_attn_sample_kernel, n_buf=n_buf, n_new=t, n_heads=n_heads),
        grid=(b,),
        in_specs=[new, new, new, buf, buf, pl.BlockSpec((t, n_pad), lambda i: (0, 0))],
        out_specs=new,
        out_shape=jax.ShapeDtypeStruct((b, t, a), F32),
        scratch_shapes=[pltpu.VMEM((n_pad, a), F32)] * 2,
        compiler_params=pltpu.CompilerParams(dimension_semantics=("arbitrary",),
                                             vmem_limit_bytes=_vmem_limit(vmem)),
        name="attn_sample",
    )(q, k_new, v_new, cache_k, cache_v, cnt)


def _ssm_tables(a_re, a_im, log_dt, b_re, b_im, c_re, c_im, d_skip, chunk):
    hp = lax.Precision.HIGHEST
    g, n = a_re.shape
    c = b_re.shape[-1]
    dt = jnp.exp(log_dt)[:, None]
    x, y = a_re * dt, a_im * dt
    ex = jnp.exp(x)
    ar, ai = ex * jnp.cos(y), ex * jnp.sin(y)
    nr = jnp.expm1(x) * jnp.cos(y) - 2.0 * jnp.sin(0.5 * y) ** 2
    ni = ai
    den = a_re * a_re + a_im * a_im
    fr = (nr * a_re + ni * a_im) / den
    fi = (ni * a_re - nr * a_im) / den
    bbr = fr[..., None] * b_re - fi[..., None] * b_im
    bbi = fr[..., None] * b_im + fi[..., None] * b_re
    pr, pi = [jnp.ones_like(ar)], [jnp.zeros_like(ar)]
    for _ in range(chunk):
        pr.append(pr[-1] * ar - pi[-1] * ai)
        pi.append(pr[-2] * ai + pi[-1] * ar)
    pr, pi = jnp.stack(pr), jnp.stack(pi)
    cpr = c_re[None] * pr[:, :, None, :] - c_im[None] * pi[:, :, None, :]
    cpi = c_re[None] * pi[:, :, None, :] + c_im[None] * pr[:, :, None, :]
    kern = (jnp.einsum("tgcn,gnd->tgcd", cpr[:chunk], bbr, precision=hp)
            - jnp.einsum("tgcn,gnd->tgcd", cpi[:chunk], bbi, precision=hp))
    lag = np.arange(chunk)[None, :] - np.arange(chunk)[:, None]
    toep = kern[np.maximum(lag, 0)]
    toep = jnp.where((lag >= 0)[:, :, None, None, None], toep, 0.0)
    toep = toep.transpose(2, 0, 4, 1, 3).reshape(g, chunk * c, chunk * c)
    wr = pr[chunk - 1::-1][:, :, :, None] * bbr[None] - pi[chunk - 1::-1][:, :, :, None] * bbi[None]
    wi = pr[chunk - 1::-1][:, :, :, None] * bbi[None] + pi[chunk - 1::-1][:, :, :, None] * bbr[None]
    wr = wr.transpose(1, 0, 3, 2).reshape(g, chunk * c, n)
    wi = wi.transpose(1, 0, 3, 2).reshape(g, chunk * c, n)
    vr = cpr[1:].transpose(1, 3, 0, 2).reshape(g, n, chunk * c)
    vi = -cpi[1:].transpose(1, 3, 0, 2).reshape(g, n, chunk * c)
    half = (np.arange(g) % 2)[:, None, None]
    z_w = jnp.zeros_like(wr)
    pad_w = lambda m: jnp.where(half == 0, jnp.concatenate([m, z_w], -1), jnp.concatenate([z_w, m], -1))
    z_v = jnp.zeros_like(vr)
    pad_v = lambda m: jnp.where(half == 0, jnp.concatenate([m, z_v], 1), jnp.concatenate([z_v, m], 1))
    d_t = jnp.tile(d_skip, (1, chunk)).reshape(g, 1, chunk * c)
    return dict(toep=toep, wr=pad_w(wr), wi=pad_w(wi), vr=pad_v(vr), vi=pad_v(vi), d=d_t,
                al_re=pr[chunk].reshape(1, g * n), al_im=pi[chunk].reshape(1, g * n))


def _ssm_state_kernel(u_ref, wr_ref, wi_ref, er_ref, ei_ref):
    er = jnp.dot(u_ref[0], wr_ref[0], preferred_element_type=F32)
    ei = jnp.dot(u_ref[0], wi_ref[0], preferred_element_type=F32)
    er_ref[...] = er + jnp.dot(u_ref[1], wr_ref[1], preferred_element_type=F32)
    ei_ref[...] = ei + jnp.dot(u_ref[1], wi_ref[1], preferred_element_type=F32)


def _ssm_scan_kernel(er_ref, ei_ref, ar_ref, ai_ref, hr_ref, hi_ref, xr_ref, xi_ref, lr_ref, li_ref,
                     *, n_chunks, nb):
    shape = hr_ref.shape
    ar = jnp.broadcast_to(ar_ref[...], shape)
    ai = jnp.broadcast_to(ai_ref[...], shape)

    def step(j, carry):
        xr, xi = carry
        sl = pl.ds(pl.multiple_of(j * nb, nb), nb)
        xr_ref[sl, :] = xr
        xi_ref[sl, :] = xi
        return (ar * xr - ai * xi + er_ref[sl, :], ar * xi + ai * xr + ei_ref[sl, :])

    xr, xi = lax.fori_loop(0, n_chunks, step, (hr_ref[...], hi_ref[...]))
    lr_ref[...] = xr
    li_ref[...] = xi


def _ssm_out_kernel(u_ref, t_ref, d_ref, xr_ref, xi_ref, vr_ref, vi_ref, y_ref):
    xr = xr_ref[...]
    xi = xi_ref[...]
    for a in range(2):
        u = u_ref[a]
        y = jnp.dot(u, t_ref[a], preferred_element_type=F32) + d_ref[a] * u
        y = y + jnp.dot(xr, vr_ref[a], preferred_element_type=F32)
        y_ref[a] = y + jnp.dot(xi, vi_ref[a], preferred_element_type=F32)


def _ssm_scan(er, ei, tab, h0_re, h0_im, n_chunks):
    r, gn = er.shape
    b = r // n_chunks
    lanes = 4 * V7X_LANES
    col = lambda i: (0, i)
    return pl.pallas_call(
        functools.partial(_ssm_scan_kernel, n_chunks=n_chunks, nb=b),
        grid=(gn // lanes,),
        in_specs=[pl.BlockSpec((r, lanes), col), pl.BlockSpec((r, lanes), col),
                  pl.BlockSpec((1, lanes), col), pl.BlockSpec((1, lanes), col),
                  pl.BlockSpec((b, lanes), col), pl.BlockSpec((b, lanes), col)],
        out_specs=(pl.BlockSpec((r, lanes), col),) * 2 + (pl.BlockSpec((b, lanes), col),) * 2,
        out_shape=(jax.ShapeDtypeStruct((r, gn), F32),) * 2 + (jax.ShapeDtypeStruct((b, gn), F32),) * 2,
        compiler_params=pltpu.CompilerParams(dimension_semantics=("arbitrary",),
                                             vmem_limit_bytes=_vmem_limit(2 * 4 * 4 * r * lanes)),
        name="ssm_scan",
    )(er, ei, tab["al_re"], tab["al_im"], h0_re, h0_im)


def _ssm_state_t_kernel(a_ref, wr_ref, wi_ref, er_ref, ei_ref):
    def seq(b, carry):
        a0, a1 = a_ref[0, b], a_ref[1, b]
        er_ref[b] = (jnp.dot(wr_ref[0], a0, preferred_element_type=F32)
                     + jnp.dot(wr_ref[1], a1, preferred_element_type=F32))
        ei_ref[b] = (jnp.dot(wi_ref[0], a0, preferred_element_type=F32)
                     + jnp.dot(wi_ref[1], a1, preferred_element_type=F32))
        return carry

    lax.fori_loop(0, a_ref.shape[1], seq, 0)


def _ssm_out_t_kernel(a_ref, t_ref, d_ref, xr_ref, xi_ref, vr_ref, vi_ref, y_ref):
    def seq(b, carry):
        xr, xi = xr_ref[b], xi_ref[b]
        for a in range(2):
            at = a_ref[a, b]
            y = jnp.dot(t_ref[a], at, preferred_element_type=F32) + d_ref[a] * at
            y = y + jnp.dot(vr_ref[a], xr, preferred_element_type=F32)
            y_ref[a, b] = y + jnp.dot(vi_ref[a], xi, preferred_element_type=F32)
        return carry

    lax.fori_loop(0, a_ref.shape[1], seq, 0)


def _ssm_chunks_on_lanes(u, h0_re, h0_im, tab, chunk):
    b, s, width = u.shape
    g = width // SSM_GROUP
    n_chunks = s // chunk
    lc = chunk * SSM_GROUP
    assert n_chunks == V7X_LANES
    at = (u.reshape(b, n_chunks, chunk * width).transpose(0, 2, 1)
          .reshape(b, chunk, g, SSM_GROUP, n_chunks).transpose(2, 0, 1, 3, 4).reshape(g, b, lc, n_chunks))
    tr = lambda m: m.transpose(0, 2, 1)
    pair4 = lambda i: (i, 0, 0, 0)
    pair3 = lambda i: (i, 0, 0)
    cp = lambda vmem: pltpu.CompilerParams(dimension_semantics=("arbitrary",),
                                           vmem_limit_bytes=_vmem_limit(vmem))
    a_spec = pl.BlockSpec((2, b, lc, n_chunks), pair4)
    st_spec = pl.BlockSpec((None, b, V7X_LANES, n_chunks), pair4)
    w_spec = pl.BlockSpec((2, V7X_LANES, lc), pair3)
    a_bytes = 2 * b * lc * n_chunks
    st_bytes = b * V7X_LANES * n_chunks
    et_re, et_im = pl.pallas_call(
        _ssm_state_t_kernel,
        grid=(g // 2,),
        in_specs=[a_spec, w_spec, w_spec],
        out_specs=(st_spec,) * 2,
        out_shape=(jax.ShapeDtypeStruct((g // 2, b, V7X_LANES, n_chunks), F32),) * 2,
        compiler_params=cp(2 * 4 * (a_bytes + 2 * st_bytes + 4 * V7X_LANES * lc)),
        name="ssm_state_t",
    )(at, tr(tab["wr"]), tr(tab["wi"]))
    rows = lambda m: m.transpose(3, 1, 0, 2).reshape(n_chunks * b, g * SSM_STATE)
    xr, xi, lr, li = _ssm_scan(rows(et_re), rows(et_im), tab, h0_re, h0_im, n_chunks)
    cols = lambda m: m.reshape(n_chunks, b, g // 2, V7X_LANES).transpose(2, 1, 3, 0)
    yt = pl.pallas_call(
        _ssm_out_t_kernel,
        grid=(g // 2,),
        in_specs=[a_spec, pl.BlockSpec((2, lc, lc), pair3), pl.BlockSpec((2, lc, 1), pair3),
                  st_spec, st_spec, pl.BlockSpec((2, lc, V7X_LANES), pair3),
                  pl.BlockSpec((2, lc, V7X_LANES), pair3)],
        out_specs=a_spec,
        out_shape=jax.ShapeDtypeStruct((g, b, lc, n_chunks), F32),
        compiler_params=cp(2 * 4 * (2 * a_bytes + 2 * st_bytes + 2 * lc * lc + 6 * V7X_LANES * lc)),
        name="ssm_out_t",
    )(at, tr(tab["toep"]), tr(tab["d"]), cols(xr), cols(xi), tr(tab["vr"]), tr(tab["vi"]))
    y = (yt.reshape(g, b, chunk, SSM_GROUP, n_chunks).transpose(1, 2, 0, 3, 4)
         .reshape(b, chunk * width, n_chunks).transpose(0, 2, 1).reshape(b, s, width))
    return y, lr, li


def _ssm(u, h0_re, h0_im, tab, chunk):
    b, s, width = u.shape
    g = width // SSM_GROUP
    n_chunks = s // chunk
    lc = chunk * SSM_GROUP
    r = n_chunks * b
    gn = g * SSM_STATE
    ut = u.reshape(b, n_chunks, chunk, g, SSM_GROUP).transpose(3, 1, 0, 2, 4).reshape(g, r, lc)
    pair = lambda i: (i, 0, 0)
    cp = lambda vmem: pltpu.CompilerParams(dimension_semantics=("arbitrary",),
                                           vmem_limit_bytes=_vmem_limit(vmem))
    er, ei = pl.pallas_call(
        _ssm_state_kernel,
        grid=(g // 2,),
        in_specs=[pl.BlockSpec((2, r, lc), pair), pl.BlockSpec((2, lc, V7X_LANES), pair),
                  pl.BlockSpec((2, lc, V7X_LANES), pair)],
        out_specs=(pl.BlockSpec((r, V7X_LANES), lambda i: (0, i)),) * 2,
        out_shape=(jax.ShapeDtypeStruct((r, gn), F32),) * 2,
        compiler_params=cp(2 * 4 * (2 * r * lc + 4 * lc * V7X_LANES + 2 * r * V7X_LANES)),
        name="ssm_state",
    )(ut, tab["wr"], tab["wi"])
    xr, xi, lr, li = _ssm_scan(er, ei, tab, h0_re, h0_im, n_chunks)
    y = pl.pallas_call(
        _ssm_out_kernel,
        grid=(g // 2,),
        in_specs=[pl.BlockSpec((2, r, lc), pair), pl.BlockSpec((2, lc, lc), pair),
                  pl.BlockSpec((2, 1, lc), pair),
                  pl.BlockSpec((r, V7X_LANES), lambda i: (0, i)),
                  pl.BlockSpec((r, V7X_LANES), lambda i: (0, i)),
                  pl.BlockSpec((2, V7X_LANES, lc), pair), pl.BlockSpec((2, V7X_LANES, lc), pair)],
        out_specs=pl.BlockSpec((2, r, lc), pair),
        out_shape=jax.ShapeDtypeStruct((g, r, lc), F32),
        compiler_params=cp(2 * 4 * (4 * r * lc + 2 * lc * lc + 2 * r * V7X_LANES + 4 * V7X_LANES * lc)),
        name="ssm_out",
    )(ut, tab["toep"], tab["d"], xr, xi, tab["vr"], tab["vi"])
    y = y.reshape(g, n_chunks, b, chunk, SSM_GROUP).transpose(2, 1, 3, 0, 4).reshape(b, s, width)
    return y, lr, li


def _finish_kernel(x_ref, attn_ref, ssm_ref, ga_ref, wg_ref, bg_ref, gs_ref, wo_ref, gf_ref,
                   wr_ref, br_ref, base_ref, x1_ref, hf_ref, ti_ref, tg_ref, cnt_ref, seen):
    a = attn_ref.shape[-1]
    tm = x_ref.shape[0]

    @pl.when(pl.program_id(0) == 0)
    def _():
        seen[...] = base_ref[...]

    g = jax.nn.gelu(ssm_ref[...])
    gate = jax.nn.sigmoid(jnp.dot(g, wg_ref[...], preferred_element_type=F32) + bg_ref[...])
    mixed_a = _rms(attn_ref[...], ga_ref[...])
    mixed_s = _rms(g * gate, gs_ref[...])
    x1 = (x_ref[...] + jnp.dot(mixed_a, wo_ref[pl.ds(0, a), :], preferred_element_type=F32)
          + jnp.dot(mixed_s, wo_ref[pl.ds(a, wo_ref.shape[0] - a), :], preferred_element_type=F32))
    x1_ref[...] = x1
    hf = _rms(x1, gf_ref[...])
    _to_token_tiles(hf_ref, hf)
    hf_hi = hf.astype(jnp.bfloat16).astype(F32)
    hf_lo = hf - hf_hi
    small = (jnp.dot(hf_lo, wr_ref[1], preferred_element_type=F32)
             + jnp.dot(hf_hi, wr_ref[1], preferred_element_type=F32)
             + jnp.dot(hf_lo, wr_ref[0], preferred_element_type=F32))
    logits = jnp.dot(hf_hi, wr_ref[0], preferred_element_type=F32) + small + br_ref[...]
    lane = lax.broadcasted_iota(I32, logits.shape, 1)
    work = logits
    vals, idxs = [], []
    for _ in range(TOP_K):
        m = jnp.max(work, axis=-1, keepdims=True)
        idx = jnp.min(jnp.where(work == m, lane, V7X_LANES), axis=-1, keepdims=True)
        vals.append(m)
        idxs.append(idx)
        work = jnp.where(lane == idx, -jnp.inf, work)
    exps = [jnp.exp(v - vals[0]) for v in vals]
    den = exps[0]
    for e in exps[1:]:
        den = den + e
    picked = jnp.zeros(logits.shape, F32)
    for k in range(TOP_K):
        picked = picked + jnp.where(lane == idxs[k], 1.0, 0.0)
    tri = jnp.where(lax.broadcasted_iota(I32, (tm, tm), 0) > lax.broadcasted_iota(I32, (tm, tm), 1), 1.0, 0.0)
    before = jnp.dot(tri, picked, preferred_element_type=F32) + seen[...]
    ti = jnp.zeros(logits.shape, I32)
    tg = jnp.zeros(logits.shape, F32)
    for k in range(TOP_K):
        rank = jnp.sum(jnp.where(lane == idxs[k], before, 0.0), axis=-1, keepdims=True)
        ti = jnp.where(lane == k, idxs[k], ti)
        ti = jnp.where(lane == TOP_K + k, rank.astype(I32), ti)
        tg = jnp.where(lane == k, exps[k] / den, tg)
    ti_ref[...] = ti
    tg_ref[...] = tg
    seen[...] = seen[...] + jnp.sum(picked, axis=0, keepdims=True)
    cnt_ref[...] = seen[...]


def _finish(x2d, attn, ssm_y, p, seen, tm):
    n, d = x2d.shape
    a = attn.shape[1]
    sw = ssm_y.shape[1]
    ne = p["w_router"].shape[1]
    w_router = jnp.pad(p["w_router"], ((0, 0), (0, V7X_LANES - ne)))
    w_hi = w_router.astype(jnp.bfloat16).astype(F32)
    w_router = jnp.stack([w_hi, w_router - w_hi])
    b_router = jnp.pad(p["b_router"], (0, V7X_LANES - ne), constant_values=-jnp.inf).reshape(1, V7X_LANES)
    row = lambda i: (i, 0)
    full = lambda arr: pl.BlockSpec(arr.shape, lambda i: (0,) * arr.ndim)
    ins = [x2d, attn, ssm_y, p["attn_out_norm_g"].reshape(1, a), p["w_glu"], p["b_glu"].reshape(1, sw),
           p["ssm_out_norm_g"].reshape(1, sw), p["w_out"], p["norm_ffn_g"].reshape(1, d),
           w_router, b_router, seen]
    in_specs = [pl.BlockSpec((tm, d), row), pl.BlockSpec((tm, a), row), pl.BlockSpec((tm, sw), row)]
    in_specs += [full(arr) for arr in ins[3:]]
    vmem = 2 * 4 * (tm * (3 * d + a + sw + 2 * V7X_LANES) + sw * sw + d * d + 2 * d * V7X_LANES) + 16 * tm * d
    return pl.pallas_call(
        _finish_kernel,
        grid=(n // tm,),
        in_specs=in_specs,
        out_specs=(pl.BlockSpec((tm, d), row), pl.BlockSpec((tm * V7X_SUBLANES, V7X_LANES), row),
                   pl.BlockSpec((tm, V7X_LANES), row), pl.BlockSpec((tm, V7X_LANES), row),
                   pl.BlockSpec((1, V7X_LANES), lambda i: (0, 0))),
        out_shape=(jax.ShapeDtypeStruct((n, d), F32), jax.ShapeDtypeStruct((n * V7X_SUBLANES, V7X_LANES), F32),
                   jax.ShapeDtypeStruct((n, V7X_LANES), I32), jax.ShapeDtypeStruct((n, V7X_LANES), F32),
                   jax.ShapeDtypeStruct((1, V7X_LANES), F32)),
        scratch_shapes=[pltpu.VMEM((1, V7X_LANES), F32)],
        compiler_params=pltpu.CompilerParams(dimension_semantics=("arbitrary",),
                                             vmem_limit_bytes=_vmem_limit(vmem)),
        name="finish",
    )(*ins)


def _tile_copy(src_ref, src_row, dst_ref, dst_row, sem):
    return pltpu.make_async_copy(src_ref.at[src_row], dst_ref.at[dst_row], sem)


def _dispatch_kernel(dest_ref, hp_ref, hs_ref, xs_ref, sem, *, n_tiles_p):
    def scatter(hf_ref):
        tm = hf_ref.shape[0]

        def tokens(i, carry):
            for u in range(V7X_SUBLANES):
                r = i * V7X_SUBLANES + u
                for k in range(TOP_K):
                    _tile_copy(hf_ref, r, xs_ref, dest_ref[0, 0, r * TOP_K + k], sem).start(priority=k % 2)
            return carry

        lax.fori_loop(0, tm // V7X_SUBLANES, tokens, 0)
        rows = pl.ds(0, tm * TOP_K)
        pltpu.make_async_copy(xs_ref.at[rows], xs_ref.at[rows], sem).wait()

    @pl.when(pl.program_id(0) < n_tiles_p)
    def _():
        scatter(hp_ref)

    @pl.when(pl.program_id(0) == n_tiles_p)
    def _():
        scatter(hs_ref)


def _dispatch(hf_p, dest_p, hf_s, dest_s, tm):
    tile = (V7X_SUBLANES, V7X_LANES)
    hf_p, hf_s = hf_p.reshape(-1, *tile), hf_s.reshape(-1, *tile)
    n_p, n_s = hf_p.shape[0], hf_s.shape[0]
    assert n_p % tm == 0 and n_s <= tm
    n_tiles_p = n_p // tm
    dest3 = jnp.concatenate([dest_p.reshape(n_tiles_p, tm * TOP_K),
                             jnp.pad(dest_s.reshape(1, n_s * TOP_K), ((0, 0), (0, (tm - n_s) * TOP_K)))])
    dest3 = dest3.reshape(n_tiles_p + 1, 1, tm * TOP_K)
    return pl.pallas_call(
        functools.partial(_dispatch_kernel, n_tiles_p=n_tiles_p),
        grid=(n_tiles_p + 1,),
        in_specs=[pl.BlockSpec((1, 1, tm * TOP_K), lambda i: (i, 0, 0), memory_space=pltpu.SMEM),
                  pl.BlockSpec((tm, *tile), lambda i: (jnp.minimum(i, n_tiles_p - 1), 0, 0)),
                  pl.BlockSpec((n_s, *tile), lambda i: (0, 0, 0))],
        out_specs=pl.BlockSpec(memory_space=pl.ANY),
        out_shape=jax.ShapeDtypeStruct(((n_p + n_s) * TOP_K, *tile), F32),
        scratch_shapes=[pltpu.SemaphoreType.DMA(())],
        compiler_params=pltpu.CompilerParams(dimension_semantics=("arbitrary",)),
        name="dispatch",
    )(dest3, hf_p, hf_s)


def _moe_kernel(vt_ref, ve_ref, lo_ref, hi_ref, x_ref, w1_ref, b1_ref, w2_ref, b2_ref, y_ref, w1b, w2b,
                *, d_ff):
    v = pl.program_id(0)
    lo, hi = lo_ref[v], hi_ref[v]
    prev = jnp.maximum(v - 1, 0)
    first = jnp.logical_or(v == 0, vt_ref[v] != vt_ref[prev])

    @pl.when(jnp.logical_or(v == 0, ve_ref[v] != ve_ref[prev]))
    def _():
        w1b[...] = w1_ref[0].astype(jnp.bfloat16)
        w2b[...] = w2_ref[0].astype(jnp.bfloat16)

    @pl.when(hi > lo)
    def _():
        x = _from_token_tiles(x_ref).astype(jnp.bfloat16)
        a = jnp.dot(x, w1b[...], preferred_element_type=F32) + b1_ref[0]
        x_glu = jnp.minimum(a[:, :d_ff], SWIGLU_LIMIT)
        x_lin = jnp.clip(a[:, d_ff:], -SWIGLU_LIMIT, SWIGLU_LIMIT)
        mid = x_glu * jax.nn.sigmoid(SWIGLU_ALPHA * x_glu) * (x_lin + 1.0)
        y = jnp.dot(mid.astype(jnp.bfloat16), w2b[...], preferred_element_type=F32) + b2_ref[0]
        rows = lax.broadcasted_iota(I32, (y.shape[0], 1), 0)
        mine = (rows >= lo) & (rows < hi)

        @pl.when(first)
        def _():
            _to_token_tiles(y_ref, jnp.where(mine, y, 0.0))

        @pl.when(jnp.logical_not(first))
        def _():
            _to_token_tiles(y_ref, jnp.where(mine, y, _from_token_tiles(y_ref)))


def _moe_experts(xs, visits, w1, b1, w2, b2):
    ne, d, ff2 = w1.shape
    d_ff = ff2 // 2
    n_visits = visits[0].shape[0]
    row = lambda v, vt, ve, lo, hi: (vt[v], 0)
    exp = lambda v, vt, ve, lo, hi: (ve[v], 0, 0)
    vmem = (2 * 4 * (2 * MOE_TILE * d + d * ff2 + d_ff * d + ff2 + d) + 2 * (d * ff2 + d_ff * d)
            + 4 * 4 * MOE_TILE * ff2)
    tiles = pl.BlockSpec((MOE_TILE * V7X_SUBLANES, V7X_LANES), row)
    grid_spec = pltpu.PrefetchScalarGridSpec(
        num_scalar_prefetch=4,
        grid=(n_visits,),
        in_specs=[tiles,
                  pl.BlockSpec((1, d, ff2), exp), pl.BlockSpec((1, 1, ff2), exp),
                  pl.BlockSpec((1, d_ff, d), exp), pl.BlockSpec((1, 1, d), exp)],
        out_specs=tiles,
        scratch_shapes=[pltpu.VMEM((d, ff2), jnp.bfloat16), pltpu.VMEM((d_ff, d), jnp.bfloat16)],
    )
    return pl.pallas_call(
        functools.partial(_moe_kernel, d_ff=d_ff),
        grid_spec=grid_spec,
        out_shape=jax.ShapeDtypeStruct(xs.shape, F32),
        compiler_params=pltpu.CompilerParams(dimension_semantics=("arbitrary",),
                                             vmem_limit_bytes=_vmem_limit(vmem)),
        name="moe",
    )(*visits, xs, w1, b1.reshape(ne, 1, ff2), w2, b2.reshape(ne, 1, d))


def _moe_visits(sizes, n_rows):
    n_tiles = n_rows // MOE_TILE
    n_visits = n_tiles + N_EXPERTS - 1
    ends = jnp.cumsum(sizes)
    starts = ends - sizes
    first_tile = starts // MOE_TILE
    n_vis = jnp.where(sizes > 0, (ends - 1) // MOE_TILE - first_tile + 1, 0)
    vis_end = jnp.cumsum(n_vis)
    total = vis_end[-1]
    v = jnp.clip(jnp.arange(n_visits, dtype=I32), 0, jnp.maximum(total - 1, 0))
    e = jnp.sum((vis_end[None, :] <= v[:, None]).astype(I32), axis=1)
    pick = lambda tab: jnp.sum(jnp.where(e[:, None] == jnp.arange(N_EXPERTS)[None, :], tab[None, :], 0), axis=1)
    tile = pick(first_tile) + v - pick(vis_end - n_vis)
    lo = jnp.maximum(pick(starts), tile * MOE_TILE) - tile * MOE_TILE
    hi = jnp.minimum(pick(ends), (tile + 1) * MOE_TILE) - tile * MOE_TILE
    live = jnp.arange(n_visits) < total
    return (tile.astype(I32), e.astype(I32), jnp.where(live, lo, 0).astype(I32),
            jnp.where(live, hi, 0).astype(I32))


def _final_kernel(dest_ref, nxt_ref, x_ref, tg_ref, g_ref, ys_ref, y_ref, buf, sem, *, tm):
    i = pl.program_id(0)
    n = pl.num_programs(0)

    def gather(d_ref, slot):
        def tokens(i8, carry):
            for u in range(V7X_SUBLANES):
                r = i8 * V7X_SUBLANES + u
                tile = pl.ds(pl.multiple_of(r * V7X_SUBLANES, V7X_SUBLANES), V7X_SUBLANES)
                for k in range(TOP_K):
                    pltpu.make_async_copy(ys_ref.at[d_ref[0, 0, r * TOP_K + k]], buf.at[slot, k, tile, :],
                                          sem.at[slot]).start(priority=k % 2)
            return carry
        lax.fori_loop(0, tm // V7X_SUBLANES, tokens, 0)

    @pl.when(i == 0)
    def _():
        gather(dest_ref, 0)

    for slot in range(2):
        @pl.when(jnp.logical_and(i + 1 < n, (i + 1) % 2 == slot))
        def _(slot=slot):
            gather(nxt_ref, slot)

    for slot in range(2):
        @pl.when(i % 2 == slot)
        def _(slot=slot):
            pltpu.make_async_copy(buf.at[slot], buf.at[slot], sem.at[slot]).wait()
            tg = tg_ref[...]
            parts = []
            for j in range(V7X_SUBLANES):
                f = None
                for k in range(TOP_K):
                    rows = buf[slot, k, pl.ds(j, tm, stride=V7X_SUBLANES), :]
                    f = tg[:, k:k + 1] * rows if f is None else f + tg[:, k:k + 1] * rows
                parts.append(f)
            y_ref[...] = _rms(x_ref[...] + jnp.concatenate(parts, axis=1), g_ref[...])


def _final(x1, tg, dest, ys, g, tm):
    n, d = x1.shape
    n_tiles = n // tm
    tile = (V7X_SUBLANES, V7X_LANES)
    dest3 = dest.reshape(n_tiles, 1, tm * TOP_K)
    row = lambda i: (i, 0)
    smem = lambda index_map: pl.BlockSpec((1, 1, tm * TOP_K), index_map, memory_space=pltpu.SMEM)
    vmem = 4 * (2 * TOP_K * tm * d + 2 * 2 * tm * d + 2 * tm * V7X_LANES + 4 * tm * d)
    return pl.pallas_call(
        functools.partial(_final_kernel, tm=tm),
        grid=(n_tiles,),
        in_specs=[smem(lambda i: (i, 0, 0)), smem(lambda i: (jnp.minimum(i + 1, n_tiles - 1), 0, 0)),
                  pl.BlockSpec((tm, d), row), pl.BlockSpec((tm, V7X_LANES), row),
                  pl.BlockSpec((1, d), lambda i: (0, 0)), pl.BlockSpec(memory_space=pl.ANY)],
        out_specs=pl.BlockSpec((tm, d), row),
        out_shape=jax.ShapeDtypeStruct((n, d), F32),
        scratch_shapes=[pltpu.VMEM((2, TOP_K, tm * V7X_SUBLANES, V7X_LANES), F32), pltpu.SemaphoreType.DMA((2,))],
        compiler_params=pltpu.CompilerParams(dimension_semantics=("arbitrary",),
                                             vmem_limit_bytes=_vmem_limit(vmem)),
        name="final",
    )(dest3, dest3, x1, tg, g.reshape(1, d), ys.reshape(-1, *tile))


def kernel(x_prompt, x_sample, cache_k, cache_v, state_ssm_re, state_ssm_im, norm_mix_g, w_in, attn_out_norm_g, ssm_a_re, ssm_a_im, ssm_log_dt, ssm_b_re, ssm_b_im, ssm_c_re, ssm_c_im, ssm_d, w_glu, b_glu, ssm_out_norm_g, w_out, norm_ffn_g, w_router, b_router, w_moe1, b_moe1, w_moe2, b_moe2, norm_final_g):
    depth = w_in.shape[0]
    assert depth == 1, "single-layer trunk"
    bp, s, d = x_prompt.shape
    bs, t, _ = x_sample.shape
    n_buf = cache_k.shape[2]
    n_heads, head_dim = cache_k.shape[3], cache_k.shape[4]
    assert head_dim == HEAD_DIM
    a = n_heads * head_dim
    g = ssm_a_re.shape[1]
    gn = g * SSM_STATE
    assert s % (max(dd for _, dd in DILATED_GROUPS) * KEYS_BACK) == 0
    np_, ns = bp * s, bs * t
    tm_p, tm_s = 256, ns
    l = 0
    ssm_p = (ssm_a_re[l], ssm_a_im[l], ssm_log_dt[l], ssm_b_re[l], ssm_b_im[l], ssm_c_re[l], ssm_c_im[l],
             ssm_d[l])
    tail = dict(attn_out_norm_g=attn_out_norm_g[l], w_glu=w_glu[l], b_glu=b_glu[l],
                ssm_out_norm_g=ssm_out_norm_g[l], w_out=w_out[l], norm_ffn_g=norm_ffn_g[l],
                w_router=w_router[l], b_router=b_router[l])

    xp = x_prompt.reshape(np_, d)
    qp, kp, vp, up = _proj(xp, norm_mix_g[l], w_in[l], a, tm_p)
    attn_p = _attn_prompt(qp.reshape(bp, s, a), kp.reshape(bp, s, a), vp.reshape(bp, s, a))
    zeros = jnp.zeros((bp, gn), F32)
    y_p, hp_re, hp_im = _ssm_chunks_on_lanes(up.reshape(bp, s, -1), zeros, zeros,
                                             _ssm_tables(*ssm_p, PROMPT_CHUNK), PROMPT_CHUNK)
    seen = jnp.zeros((1, V7X_LANES), F32)
    x1_p, hf_p, ti_p, tg_p, seen = _finish(xp, attn_p.reshape(np_, a), y_p.reshape(np_, -1), tail, seen, tm_p)

    xs = x_sample.reshape(ns, d)
    qs, ks, vs, us = _proj(xs, norm_mix_g[l], w_in[l], a, tm_s)
    attn_s = _attn_sample(qs.reshape(bs, t, a), ks.reshape(bs, t, a), vs.reshape(bs, t, a),
                          cache_k[l].reshape(bs, n_buf, a), cache_v[l].reshape(bs, n_buf, a))
    y_s, hs_re, hs_im = _ssm(us.reshape(bs, t, -1), state_ssm_re[l].reshape(bs, gn),
                             state_ssm_im[l].reshape(bs, gn), _ssm_tables(*ssm_p, t), t)
    x1_s, hf_s, ti_s, tg_s, seen = _finish(xs, attn_s.reshape(ns, a), y_s.reshape(ns, -1), tail, seen, tm_s)

    sizes = seen[0, :N_EXPERTS].astype(I32)
    starts = jnp.cumsum(sizes) - sizes
    experts = jnp.arange(N_EXPERTS, dtype=I32)[None, None, :]

    def sorted_row(ti):
        hit = ti[:, :TOP_K, None] == experts
        return jnp.sum(jnp.where(hit, starts[None, None, :], 0), axis=-1) + ti[:, TOP_K:2 * TOP_K]

    dest_p, dest_s = sorted_row(ti_p), sorted_row(ti_s)
    n_rows = (np_ + ns) * TOP_K
    assert n_rows % MOE_TILE == 0
    xs_sorted = _dispatch(hf_p, dest_p, hf_s, dest_s, tm_p).reshape(-1, V7X_LANES)
    ys = _moe_experts(xs_sorted, _moe_visits(sizes, n_rows), w_moe1[l], b_moe1[l], w_moe2[l], b_moe2[l])
    y_prompt = _final(x1_p, tg_p, dest_p, ys, norm_final_g, tm_p).reshape(bp, s, d)
    y_sample = _final(x1_s, tg_s, dest_s, ys, norm_final_g, tm_s).reshape(bs, t, d)

    keep = min(max(w for w, _ in DILATED_GROUPS), s)
    k5 = lambda z, b_, s_: z.reshape(1, b_, s_, n_heads, head_dim)
    st = lambda z, b_: z.reshape(1, b_, g, SSM_STATE)
    return (y_prompt, y_sample,
            k5(kp, bp, s)[:, :, s - keep:], k5(vp, bp, s)[:, :, s - keep:], st(hp_re, bp), st(hp_im, bp),
            k5(ks, bs, t), k5(vs, bs, t), st(hs_re, bs), st(hs_im, bs))
```

```python
import functools
import math

import numpy as np
import jax
import jax.numpy as jnp
from jax import lax
from jax.experimental import pallas as pl
from jax.experimental.pallas import tpu as pltpu

F32 = jnp.float32
I32 = jnp.int32

V7X_LANES = 128
V7X_SUBLANES = 8
V7X_VMEM_BYTES = 64 * 1024 * 1024

HEAD_DIM = 64
HEADS_PER_LANE_TILE = V7X_LANES // HEAD_DIM
DILATED_GROUPS = ((128, 1), (512, 4), (2048, 16))
KEYS_BACK = 128
SSM_GROUP = 16
SSM_STATE = 64
N_EXPERTS = 32
TOP_K = 4
SWIGLU_LIMIT = 7.0
SWIGLU_ALPHA = 1.702
RMS_EPS = 1e-5
NEG_INF = -1e30
PROMPT_CHUNK = 16
MOE_TILE = 256


def _vmem_limit(nbytes):
    return int(min(nbytes + (8 << 20), V7X_VMEM_BYTES - (8 << 20)))


def _rms(x, g):
    return x * lax.rsqrt(jnp.mean(x * x, axis=-1, keepdims=True) + RMS_EPS) * g


def _to_token_tiles(ref, val):
    rows, width = val.shape
    assert width == V7X_SUBLANES * V7X_LANES and ref.shape == (rows * V7X_SUBLANES, V7X_LANES)
    for j in range(V7X_SUBLANES):
        ref[pl.ds(j, rows, stride=V7X_SUBLANES), :] = val[:, j * V7X_LANES:(j + 1) * V7X_LANES]


def _from_token_tiles(ref):
    rows = ref.shape[0] // V7X_SUBLANES
    return jnp.concatenate([ref[pl.ds(j, rows, stride=V7X_SUBLANES), :] for j in range(V7X_SUBLANES)], axis=1)


def _proj_kernel(x_ref, g_ref, w_ref, q_ref, k_ref, v_ref, u_ref, *, attn_width, q_scale):
    h = _rms(x_ref[...], g_ref[...])
    z = jnp.dot(h, w_ref[...], preferred_element_type=F32)
    a = attn_width
    q_ref[...] = z[:, :a] * q_scale
    k_ref[...] = z[:, a:2 * a]
    v_ref[...] = z[:, 2 * a:3 * a]
    u_ref[...] = z[:, 3 * a:]


def _proj(x2d, norm_g, w_in, attn_width, tm):
    n, d = x2d.shape
    mix = w_in.shape[1]
    ssm_width = mix - 3 * attn_width
    row = lambda i: (i, 0)
    const = lambda i: (0, 0)
    out_shape = (jax.ShapeDtypeStruct((n, attn_width), F32),) * 3 + (
        jax.ShapeDtypeStruct((n, ssm_width), F32),)
    vmem = 2 * 4 * (tm * d + d * mix + tm * mix) + 4 * tm * mix
    return pl.pallas_call(
        functools.partial(_proj_kernel, attn_width=attn_width, q_scale=HEAD_DIM ** -0.5),
        grid=(n // tm,),
        in_specs=[pl.BlockSpec((tm, d), row), pl.BlockSpec((1, d), const),
                  pl.BlockSpec((d, mix), const)],
        out_specs=(pl.BlockSpec((tm, attn_width), row),) * 3 + (pl.BlockSpec((tm, ssm_width), row),),
        out_shape=out_shape,
        compiler_params=pltpu.CompilerParams(dimension_semantics=("arbitrary",),
                                             vmem_limit_bytes=_vmem_limit(vmem)),
        name="proj",
    )(x2d, norm_g.reshape(1, d), w_in)


def _attn_prompt_kernel(q_ref, k_ref, v_ref, o_ref, *scratch, seq, unroll):
    nb = KEYS_BACK
    n_groups = len(DILATED_GROUPS)
    m_sc, l_sc, acc_sc = scratch[:n_groups], scratch[n_groups:2 * n_groups], scratch[2 * n_groups:]
    lane = lax.broadcasted_iota(I32, (1, V7X_LANES), 1)
    head_a = lane < HEAD_DIM
    qi = lax.broadcasted_iota(I32, (nb, 2 * nb), 0)
    kj = lax.broadcasted_iota(I32, (nb, 2 * nb), 1)
    rel = nb + qi - kj
    band = (rel >= 0) & (rel <= nb)

    def rows(ref, start, d):
        if d == 1:
            return ref[pl.ds(pl.multiple_of(start, nb), nb), :]
        return ref[pl.ds(start, nb, stride=d), :]

    def put(ref, start, d, val):
        if d == 1:
            ref[pl.ds(pl.multiple_of(start, nb), nb), :] = val
        else:
            ref[pl.ds(start, nb, stride=d), :] = val

    for gi, (w, d) in enumerate(DILATED_GROUPS):
        assert w // d == nb
        nblk = seq // (d * nb)

        def block(t, gi=gi, d=d, nblk=nblk):
            r = t // nblk
            n = t % nblk
            cur = r + d * nb * n
            prev = r + d * nb * jnp.maximum(n - 1, 0)
            q = rows(q_ref, cur, d)
            kk = jnp.concatenate([rows(k_ref, prev, d), rows(k_ref, cur, d)], axis=0)
            vv = jnp.concatenate([rows(v_ref, prev, d), rows(v_ref, cur, d)], axis=0)
            qs = jnp.concatenate([jnp.where(head_a, q, 0.0), jnp.where(head_a, 0.0, q)], axis=0)
            s = lax.dot_general(qs, kk, (((1,), (1,)), ((), ())), preferred_element_type=F32)
            valid = band & (kj >= jnp.where(n > 0, 0, nb))
            s = jnp.where(jnp.concatenate([valid, valid], axis=0), s, NEG_INF)
            m = jnp.max(s, axis=-1, keepdims=True)
            p = jnp.exp(s - m)
            l = jnp.sum(p, axis=-1, keepdims=True)
            o = jnp.where(head_a,
                          jnp.dot(p[:nb], vv, preferred_element_type=F32),
                          jnp.dot(p[nb:], vv, preferred_element_type=F32))
            put(m_sc[gi], cur, d, jnp.where(head_a, m[:nb], m[nb:]))
            put(l_sc[gi], cur, d, jnp.where(head_a, l[:nb], l[nb:]))
            put(acc_sc[gi], cur, d, o)

        def blocks(tt, carry, block=block):
            for u in range(unroll):
                block(tt * unroll + u)
            return carry

        lax.fori_loop(0, seq // nb // unroll, blocks, 0)

    def merge(n, carry):
        sl = pl.ds(pl.multiple_of(n * nb, nb), nb)
        ms = [m[sl, :] for m in m_sc]
        m_all = functools.reduce(jnp.maximum, ms)
        cs = [jnp.exp(m - m_all) for m in ms]
        den = functools.reduce(jnp.add, [c * l[sl, :] for c, l in zip(cs, l_sc)])
        num = functools.reduce(jnp.add, [c * acc[sl, :] for c, acc in zip(cs, acc_sc)])
        o_ref[sl, :] = num / den
        return carry

    lax.fori_loop(0, seq // nb, merge, 0, unroll=2)


def _attn_prompt(q, k, v, unroll=8):
    b, s, a = q.shape
    n_scratch = 3 * len(DILATED_GROUPS)
    spec = pl.BlockSpec((None, s, V7X_LANES), lambda i, j: (i, 0, j))
    vmem = 4 * s * V7X_LANES * (2 * 4 + n_scratch)
    return pl.pallas_call(
        functools.partial(_attn_prompt_kernel, seq=s, unroll=unroll),
        grid=(b, a // V7X_LANES),
        in_specs=[spec, spec, spec],
        out_specs=spec,
        out_shape=jax.ShapeDtypeStruct((b, s, a), F32),
        scratch_shapes=[pltpu.VMEM((s, V7X_LANES), F32)] * n_scratch,
        compiler_params=pltpu.CompilerParams(dimension_semantics=("arbitrary", "arbitrary"),
                                             vmem_limit_bytes=_vmem_limit(vmem)),
        name="attn_prompt",
    )(q, k, v)


def _sample_key_counts(n_buf, n_new, n_heads):
    (w1, d1), (w4, d4), (w16, d16) = DILATED_GROUPS
    assert (d1, d4) == (1, 4) and n_new <= d4 and n_buf % d16 == 0 and w16 <= n_buf
    tail = n_buf // 4
    assert w4 <= tail and w1 <= tail
    n_m = n_buf // d16
    pos_a = (np.arange(n_m)[:, None] * d16 + np.arange(n_new)[None, :]).reshape(-1)
    pos_b = n_buf - tail + np.arange(tail)
    pos_n = n_buf + np.arange(n_new)

    def reads(i, pos, groups):
        back = n_buf + i - pos
        return sum(((back >= 0) & (back % d == 0) & (back // d <= w // d)).astype(np.float32) for w, d in groups)

    def expand(per_pos):
        eye = np.eye(n_heads, dtype=np.float32)
        return np.einsum("ip,hg->ihpg", per_pos, eye).reshape(n_new * n_heads, -1)

    cnt_a = expand(np.stack([reads(i, pos_a, DILATED_GROUPS[2:]) for i in range(n_new)]))
    cnt_b = expand(np.stack([reads(i, pos_b, DILATED_GROUPS[:2]) for i in range(n_new)]))
    cnt_n = expand(np.stack([reads(i, pos_n, DILATED_GROUPS) for i in range(n_new)]))
    total = sum(w // d + 1 for w, d in DILATED_GROUPS) * n_heads
    assert all(abs(cnt_a[r].sum() + cnt_b[r].sum() + cnt_n[r].sum() - total / n_heads) < 1e-3
               for r in range(n_new * n_heads))
    return cnt_a, cnt_b, cnt_n


def _attn_sample_kernel(q_ref, kn_ref, vn_ref, ka_ref, va_ref, kb_ref, vb_ref, ca_ref, cb_ref, cn_ref, o_ref):
    q = q_ref[...]
    rows = lambda ref: ref[...].reshape(-1, HEAD_DIM)
    segs = [(rows(ka_ref), rows(va_ref), ca_ref[...]), (rows(kb_ref), rows(vb_ref), cb_ref[...]),
            (kn_ref[...], vn_ref[...], cn_ref[...])]
    scores = [lax.dot_general(q, k, (((1,), (1,)), ((), ())), preferred_element_type=F32) for k, _, _ in segs]
    m = functools.reduce(jnp.maximum, [jnp.max(jnp.where(c > 0.0, s, NEG_INF), axis=-1, keepdims=True)
                                       for s, (_, _, c) in zip(scores, segs)])
    num, den = 0.0, 0.0
    for s, (_, v, c) in zip(scores, segs):
        p = jnp.where(c > 0.0, c * jnp.exp(s - m), 0.0)
        den = den + jnp.sum(p, axis=-1, keepdims=True)
        num = num + jnp.dot(p, v, preferred_element_type=F32)
    o_ref[...] = num / den


def _attn_sample(q, k_new, v_new, cache_k, cache_v):
    b, t, h, e = q.shape
    n_buf = cache_k.shape[1]
    d16 = DILATED_GROUPS[2][1]
    tail = n_buf // 4
    cnt_a, cnt_b, cnt_n = (jnp.asarray(c) for c in _sample_key_counts(n_buf, t, h))
    new = pl.BlockSpec((None, t * h, e), lambda i: (i, 0, 0))
    seg_a = pl.BlockSpec((None, n_buf // d16, t, h, e), lambda i: (i, 0, 0, 0, 0))
    seg_b = pl.BlockSpec((None, tail, h, e), lambda i: (i, n_buf // tail - 1, 0, 0))
    const = lambda c: pl.BlockSpec(c.shape, lambda i: (0, 0))
    by_class = lambda c: c.reshape(b, n_buf // d16, d16, h, e)
    seg_rows = (n_buf // d16) * t * h + tail * h
    vmem = 4 * (2 * 2 * seg_rows * V7X_LANES + 2 * t * h * seg_rows + 8 * t * h * seg_rows)
    o = pl.pallas_call(
        _attn_sample_kernel,
        grid=(b,),
        in_specs=[new, new, new, seg_a, seg_a, seg_b, seg_b, const(cnt_a), const(cnt_b), const(cnt_n)],
        out_specs=new,
        out_shape=jax.ShapeDtypeStruct((b, t * h, e), F32),
        compiler_params=pltpu.CompilerParams(dimension_semantics=("arbitrary",),
                                             vmem_limit_bytes=_vmem_limit(vmem)),
        name="attn_sample",
    )(q.reshape(b, t * h, e), k_new.reshape(b, t * h, e), v_new.reshape(b, t * h, e),
      by_class(cache_k), by_class(cache_v), cache_k, cache_v, cnt_a, cnt_b, cnt_n)
    return o.reshape(b, t, h * e)


def _pad_states(m, g):
    half = (np.arange(g) % 2)[:, None, None]
    z = jnp.zeros_like(m)
    return jnp.where(half == 0, jnp.concatenate([m, z], -1), jnp.concatenate([z, m], -1))


def _ssm_tables(a_re, a_im, log_dt, b_re, b_im, c_re, c_im, d_skip, chunk):
    g, n = a_re.shape
    c = b_re.shape[-1]
    lc = chunk * c
    dt = jnp.exp(log_dt)[:, None]
    x, y = a_re * dt, a_im * dt
    ex = jnp.exp(x)
    ar, ai = ex * jnp.cos(y), ex * jnp.sin(y)
    nr = jnp.expm1(x) * jnp.cos(y) - 2.0 * jnp.sin(0.5 * y) ** 2
    ni = ai
    den = a_re * a_re + a_im * a_im
    fr = (nr * a_re + ni * a_im) / den
    fi = (ni * a_re - nr * a_im) / den
    b_re_t, b_im_t = b_re.transpose(0, 2, 1), b_im.transpose(0, 2, 1)
    bbr = fr[:, None, :] * b_re_t - fi[:, None, :] * b_im_t
    bbi = fr[:, None, :] * b_im_t + fi[:, None, :] * b_re_t
    pr, pi = [jnp.ones_like(ar)], [jnp.zeros_like(ar)]
    for _ in range(chunk):
        pr.append(pr[-1] * ar - pi[-1] * ai)
        pi.append(pr[-2] * ai + pi[-1] * ar)
    pr, pi = jnp.stack(pr, 1), jnp.stack(pi, 1)
    cpr = c_re[:, None] * pr[:, :, None, :] - c_im[:, None] * pi[:, :, None, :]
    cpi = c_re[:, None] * pi[:, :, None, :] + c_im[:, None] * pr[:, :, None, :]
    cq_r = cpr[:, :chunk].transpose(0, 2, 1, 3)[:, :, :, None, :]
    cq_i = cpi[:, :chunk].transpose(0, 2, 1, 3)[:, :, :, None, :]
    kern = jnp.sum(cq_r * bbr[:, None, None] - cq_i * bbi[:, None, None], axis=-1)
    ribbon = jnp.concatenate([kern[:, :, ::-1].reshape(g, c, lc), jnp.zeros((g, c, lc - c), F32)], axis=-1)
    tt = jnp.stack([ribbon[:, :, (chunk - 1 - t) * c:(chunk - 1 - t) * c + lc] for t in range(chunk)], axis=1)
    tt = tt.reshape(g, lc, lc)
    qr, qi = pr[:, chunk - 1::-1][:, :, None, :], pi[:, chunk - 1::-1][:, :, None, :]
    wr = (qr * bbr[:, None] - qi * bbi[:, None]).reshape(g, lc, n)
    wi = (qr * bbi[:, None] + qi * bbr[:, None]).reshape(g, lc, n)
    vr = cpr[:, 1:].reshape(g, lc, n)
    vi = -cpi[:, 1:].reshape(g, lc, n)
    return dict(chunk=chunk, tt=tt, wr=wr, wi=wi, vr=vr, vi=vi, d=d_skip, pr=pr, pi=pi)


def _ssm_tables_lanes(tab):
    g, chunk = tab["d"].shape[0], tab["chunk"]
    tr = lambda m: m.transpose(0, 2, 1)
    return dict(tt=tab["tt"], wt_re=tr(_pad_states(tab["wr"], g)), wt_im=tr(_pad_states(tab["wi"], g)),
                v_re=_pad_states(tab["vr"], g), v_im=_pad_states(tab["vi"], g),
                dcol=jnp.tile(tab["d"], (1, chunk))[:, :, None],
                al_re=tab["pr"][:, chunk].reshape(1, -1), al_im=tab["pi"][:, chunk].reshape(1, -1))


def _ssm_tables_rows(tab, chunk):
    g, c = tab["d"].shape
    big, lc = tab["chunk"], chunk * c
    tr = lambda m: m.transpose(0, 2, 1)
    tail = slice((big - chunk) * c, big * c)
    return dict(toep=tr(tab["tt"][:, :lc, :lc]),
                wr=_pad_states(tab["wr"][:, tail], g), wi=_pad_states(tab["wi"][:, tail], g),
                vr=tr(_pad_states(tab["vr"][:, :lc], g)), vi=tr(_pad_states(tab["vi"][:, :lc], g)),
                d=jnp.tile(tab["d"], (1, chunk))[:, None, :],
                al_re=tab["pr"][:, chunk].reshape(1, -1), al_im=tab["pi"][:, chunk].reshape(1, -1))


def _ssm_state_kernel(u_ref, wr_ref, wi_ref, er_ref, ei_ref):
    er = jnp.dot(u_ref[0], wr_ref[0], preferred_element_type=F32)
    ei = jnp.dot(u_ref[0], wi_ref[0], preferred_element_type=F32)
    er_ref[...] = er + jnp.dot(u_ref[1], wr_ref[1], preferred_element_type=F32)
    ei_ref[...] = ei + jnp.dot(u_ref[1], wi_ref[1], preferred_element_type=F32)


def _ssm_scan_kernel(er_ref, ei_ref, ar_ref, ai_ref, hr_ref, hi_ref, xr_ref, xi_ref, lr_ref, li_ref,
                     *, n_chunks, nb):
    shape = hr_ref.shape
    ar = jnp.broadcast_to(ar_ref[...], shape)
    ai = jnp.broadcast_to(ai_ref[...], shape)

    def step(j, carry):
        xr, xi = carry
        sl = pl.ds(pl.multiple_of(j * nb, nb), nb)
        xr_ref[sl, :] = xr
        xi_ref[sl, :] = xi
        return (ar * xr - ai * xi + er_ref[sl, :], ar * xi + ai * xr + ei_ref[sl, :])

    xr, xi = lax.fori_loop(0, n_chunks, step, (hr_ref[...], hi_ref[...]))
    lr_ref[...] = xr
    li_ref[...] = xi


def _ssm_out_kernel(u_ref, t_ref, d_ref, xr_ref, xi_ref, vr_ref, vi_ref, y_ref):
    xr = xr_ref[...]
    xi = xi_ref[...]
    for a in range(2):
        u = u_ref[a]
        y = jnp.dot(u, t_ref[a], preferred_element_type=F32) + d_ref[a] * u
        y = y + jnp.dot(xr, vr_ref[a], preferred_element_type=F32)
        y_ref[a] = y + jnp.dot(xi, vi_ref[a], preferred_element_type=F32)


def _ssm_scan(er, ei, tab, h0_re, h0_im, n_chunks):
    r, gn = er.shape
    b = r // n_chunks
    lanes = 4 * V7X_LANES
    col = lambda i: (0, i)
    return pl.pallas_call(
        functools.partial(_ssm_scan_kernel, n_chunks=n_chunks, nb=b),
        grid=(gn // lanes,),
        in_specs=[pl.BlockSpec((r, lanes), col), pl.BlockSpec((r, lanes), col),
                  pl.BlockSpec((1, lanes), col), pl.BlockSpec((1, lanes), col),
                  pl.BlockSpec((b, lanes), col), pl.BlockSpec((b, lanes), col)],
        out_specs=(pl.BlockSpec((r, lanes), col),) * 2 + (pl.BlockSpec((b, lanes), col),) * 2,
        out_shape=(jax.ShapeDtypeStruct((r, gn), F32),) * 2 + (jax.ShapeDtypeStruct((b, gn), F32),) * 2,
        compiler_params=pltpu.CompilerParams(dimension_semantics=("arbitrary",),
                                             vmem_limit_bytes=_vmem_limit(2 * 4 * 4 * r * lanes)),
        name="ssm_scan",
    )(er, ei, tab["al_re"], tab["al_im"], h0_re, h0_im)


def _ssm_state_t_kernel(a_ref, wr_ref, wi_ref, er_ref, ei_ref):
    def seq(b, carry):
        a0, a1 = a_ref[0, b], a_ref[1, b]
        er_ref[b] = (jnp.dot(wr_ref[0], a0, preferred_element_type=F32)
                     + jnp.dot(wr_ref[1], a1, preferred_element_type=F32))
        ei_ref[b] = (jnp.dot(wi_ref[0], a0, preferred_element_type=F32)
                     + jnp.dot(wi_ref[1], a1, preferred_element_type=F32))
        return carry

    lax.fori_loop(0, a_ref.shape[1], seq, 0)


def _ssm_out_t_kernel(a_ref, t_ref, d_ref, xr_ref, xi_ref, vr_ref, vi_ref, y_ref):
    def seq(b, carry):
        xr, xi = xr_ref[b], xi_ref[b]
        for a in range(2):
            at = a_ref[a, b]
            y = jnp.dot(t_ref[a], at, preferred_element_type=F32) + d_ref[a] * at
            y = y + jnp.dot(vr_ref[a], xr, preferred_element_type=F32)
            y_ref[a, b] = y + jnp.dot(vi_ref[a], xi, preferred_element_type=F32)
        return carry

    lax.fori_loop(0, a_ref.shape[1], seq, 0)


def _ssm_chunks_on_lanes(u, h0_re, h0_im, tab, chunk):
    b, s, width = u.shape
    g = width // SSM_GROUP
    n_chunks = s // chunk
    lc = chunk * SSM_GROUP
    assert n_chunks == V7X_LANES
    at = (u.reshape(b, n_chunks, chunk * width).transpose(0, 2, 1)
          .reshape(b, chunk, g, SSM_GROUP, n_chunks).transpose(2, 0, 1, 3, 4).reshape(g, b, lc, n_chunks))
    pair4 = lambda i: (i, 0, 0, 0)
    pair3 = lambda i: (i, 0, 0)
    cp = lambda vmem: pltpu.CompilerParams(dimension_semantics=("arbitrary",),
                                           vmem_limit_bytes=_vmem_limit(vmem))
    a_spec = pl.BlockSpec((2, b, lc, n_chunks), pair4)
    st_spec = pl.BlockSpec((None, b, V7X_LANES, n_chunks), pair4)
    w_spec = pl.BlockSpec((2, V7X_LANES, lc), pair3)
    a_bytes = 2 * b * lc * n_chunks
    st_bytes = b * V7X_LANES * n_chunks
    et_re, et_im = pl.pallas_call(
        _ssm_state_t_kernel,
        grid=(g // 2,),
        in_specs=[a_spec, w_spec, w_spec],
        out_specs=(st_spec,) * 2,
        out_shape=(jax.ShapeDtypeStruct((g // 2, b, V7X_LANES, n_chunks), F32),) * 2,
        compiler_params=cp(2 * 4 * (a_bytes + 2 * st_bytes + 4 * V7X_LANES * lc)),
        name="ssm_state_t",
    )(at, tab["wt_re"], tab["wt_im"])
    rows = lambda m: m.transpose(3, 1, 0, 2).reshape(n_chunks * b, g * SSM_STATE)
    xr, xi, lr, li = _ssm_scan(rows(et_re), rows(et_im), tab, h0_re, h0_im, n_chunks)
    cols = lambda m: m.reshape(n_chunks, b, g // 2, V7X_LANES).transpose(2, 1, 3, 0)
    yt = pl.pallas_call(
        _ssm_out_t_kernel,
        grid=(g // 2,),
        in_specs=[a_spec, pl.BlockSpec((2, lc, lc), pair3), pl.BlockSpec((2, lc, 1), pair3),
                  st_spec, st_spec, pl.BlockSpec((2, lc, V7X_LANES), pair3),
                  pl.BlockSpec((2, lc, V7X_LANES), pair3)],
        out_specs=a_spec,
        out_shape=jax.ShapeDtypeStruct((g, b, lc, n_chunks), F32),
        compiler_params=cp(2 * 4 * (2 * a_bytes + 2 * st_bytes + 2 * lc * lc + 6 * V7X_LANES * lc)),
        name="ssm_out_t",
    )(at, tab["tt"], tab["dcol"], cols(xr), cols(xi), tab["v_re"], tab["v_im"])
    y = (yt.reshape(g, b, chunk, SSM_GROUP, n_chunks).transpose(1, 2, 0, 3, 4)
         .reshape(b, chunk * width, n_chunks).transpose(0, 2, 1).reshape(b, s, width))
    return y, lr, li


def _ssm(u, h0_re, h0_im, tab, chunk):
    b, s, width = u.shape
    g = width // SSM_GROUP
    n_chunks = s // chunk
    lc = chunk * SSM_GROUP
    r = n_chunks * b
    gn = g * SSM_STATE
    ut = u.reshape(b, n_chunks, chunk, g, SSM_GROUP).transpose(3, 1, 0, 2, 4).reshape(g, r, lc)
    pair = lambda i: (i, 0, 0)
    cp = lambda vmem: pltpu.CompilerParams(dimension_semantics=("arbitrary",),
                                           vmem_limit_bytes=_vmem_limit(vmem))
    er, ei = pl.pallas_call(
        _ssm_state_kernel,
        grid=(g // 2,),
        in_specs=[pl.BlockSpec((2, r, lc), pair), pl.BlockSpec((2, lc, V7X_LANES), pair),
                  pl.BlockSpec((2, lc, V7X_LANES), pair)],
        out_specs=(pl.BlockSpec((r, V7X_LANES), lambda i: (0, i)),) * 2,
        out_shape=(jax.ShapeDtypeStruct((r, gn), F32),) * 2,
        compiler_params=cp(2 * 4 * (2 * r * lc + 4 * lc * V7X_LANES + 2 * r * V7X_LANES)),
        name="ssm_state",
    )(ut, tab["wr"], tab["wi"])
    xr, xi, lr, li = _ssm_scan(er, ei, tab, h0_re, h0_im, n_chunks)
    y = pl.pallas_call(
        _ssm_out_kernel,
        grid=(g // 2,),
        in_specs=[pl.BlockSpec((2, r, lc), pair), pl.BlockSpec((2, lc, lc), pair),
                  pl.BlockSpec((2, 1, lc), pair),
                  pl.BlockSpec((r, V7X_LANES), lambda i: (0, i)),
                  pl.BlockSpec((r, V7X_LANES), lambda i: (0, i)),
                  pl.BlockSpec((2, V7X_LANES, lc), pair), pl.BlockSpec((2, V7X_LANES, lc), pair)],
        out_specs=pl.BlockSpec((2, r, lc), pair),
        out_shape=jax.ShapeDtypeStruct((g, r, lc), F32),
        compiler_params=cp(2 * 4 * (4 * r * lc + 2 * lc * lc + 2 * r * V7X_LANES + 4 * V7X_LANES * lc)),
        name="ssm_out",
    )(ut, tab["toep"], tab["d"], xr, xi, tab["vr"], tab["vi"])
    y = y.reshape(g, n_chunks, b, chunk, SSM_GROUP).transpose(2, 1, 3, 0, 4).reshape(b, s, width)
    return y, lr, li


def _finish_kernel(x_ref, attn_ref, ssm_ref, ga_ref, wg_ref, bg_ref, gs_ref, wo_ref, gf_ref,
                   wr_ref, br_ref, base_ref, x1_ref, hf_ref, ti_ref, tg_ref, cnt_ref, seen):
    a = attn_ref.shape[-1]
    tm = x_ref.shape[0]

    @pl.when(pl.program_id(0) == 0)
    def _():
        seen[...] = base_ref[...]

    g = jax.nn.gelu(ssm_ref[...])
    gate = jax.nn.sigmoid(jnp.dot(g, wg_ref[...], preferred_element_type=F32) + bg_ref[...])
    mixed_a = _rms(attn_ref[...], ga_ref[...])
    mixed_s = _rms(g * gate, gs_ref[...])
    x1 = (x_ref[...] + jnp.dot(mixed_a, wo_ref[pl.ds(0, a), :], preferred_element_type=F32)
          + jnp.dot(mixed_s, wo_ref[pl.ds(a, wo_ref.shape[0] - a), :], preferred_element_type=F32))
    x1_ref[...] = x1
    hf = _rms(x1, gf_ref[...])
    _to_token_tiles(hf_ref, hf)
    hf_hi = hf.astype(jnp.bfloat16).astype(F32)
    hf_lo = hf - hf_hi
    small = (jnp.dot(hf_lo, wr_ref[1], preferred_element_type=F32)
             + jnp.dot(hf_hi, wr_ref[1], preferred_element_type=F32)
             + jnp.dot(hf_lo, wr_ref[0], preferred_element_type=F32))
    logits = jnp.dot(hf_hi, wr_ref[0], preferred_element_type=F32) + small + br_ref[...]
    lane = lax.broadcasted_iota(I32, logits.shape, 1)
    work = logits
    vals, idxs = [], []
    for _ in range(TOP_K):
        m = jnp.max(work, axis=-1, keepdims=True)
        idx = jnp.min(jnp.where(work == m, lane, V7X_LANES), axis=-1, keepdims=True)
        vals.append(m)
        idxs.append(idx)
        work = jnp.where(lane == idx, -jnp.inf, work)
    exps = [jnp.exp(v - vals[0]) for v in vals]
    den = exps[0]
    for e in exps[1:]:
        den = den + e
    picked = jnp.zeros(logits.shape, F32)
    for k in range(TOP_K):
        picked = picked + jnp.where(lane == idxs[k], 1.0, 0.0)
    tri = jnp.where(lax.broadcasted_iota(I32, (tm, tm), 0) > lax.broadcasted_iota(I32, (tm, tm), 1), 1.0, 0.0)
    before = jnp.dot(tri, picked, preferred_element_type=F32) + seen[...]
    ti = jnp.zeros(logits.shape, I32)
    tg = jnp.zeros(logits.shape, F32)
    for k in range(TOP_K):
        rank = jnp.sum(jnp.where(lane == idxs[k], before, 0.0), axis=-1, keepdims=True)
        ti = jnp.where(lane == k, idxs[k], ti)
        ti = jnp.where(lane == TOP_K + k, rank.astype(I32), ti)
        tg = jnp.where(lane == k, exps[k] / den, tg)
    ti_ref[...] = ti
    tg_ref[...] = tg
    seen[...] = seen[...] + jnp.sum(picked, axis=0, keepdims=True)
    cnt_ref[...] = seen[...]


def _finish(x2d, attn, ssm_y, p, seen, tm):
    n, d = x2d.shape
    a = attn.shape[1]
    sw = ssm_y.shape[1]
    ne = p["w_router"].shape[1]
    w_router = jnp.pad(p["w_router"], ((0, 0), (0, V7X_LANES - ne)))
    w_hi = w_router.astype(jnp.bfloat16).astype(F32)
    w_router = jnp.stack([w_hi, w_router - w_hi])
    b_router = jnp.pad(p["b_router"], (0, V7X_LANES - ne), constant_values=-jnp.inf).reshape(1, V7X_LANES)
    row = lambda i: (i, 0)
    full = lambda arr: pl.BlockSpec(arr.shape, lambda i: (0,) * arr.ndim)
    ins = [x2d, attn, ssm_y, p["attn_out_norm_g"].reshape(1, a), p["w_glu"], p["b_glu"].reshape(1, sw),
           p["ssm_out_norm_g"].reshape(1, sw), p["w_out"], p["norm_ffn_g"].reshape(1, d),
           w_router, b_router, seen]
    in_specs = [pl.BlockSpec((tm, d), row), pl.BlockSpec((tm, a), row), pl.BlockSpec((tm, sw), row)]
    in_specs += [full(arr) for arr in ins[3:]]
    vmem = 2 * 4 * (tm * (3 * d + a + sw + 2 * V7X_LANES) + sw * sw + d * d + 2 * d * V7X_LANES) + 16 * tm * d
    return pl.pallas_call(
        _finish_kernel,
        grid=(n // tm,),
        in_specs=in_specs,
        out_specs=(pl.BlockSpec((tm, d), row), pl.BlockSpec((tm * V7X_SUBLANES, V7X_LANES), row),
                   pl.BlockSpec((tm, V7X_LANES), row), pl.BlockSpec((tm, V7X_LANES), row),
                   pl.BlockSpec((1, V7X_LANES), lambda i: (0, 0))),
        out_shape=(jax.ShapeDtypeStruct((n, d), F32), jax.ShapeDtypeStruct((n * V7X_SUBLANES, V7X_LANES), F32),
                   jax.ShapeDtypeStruct((n, V7X_LANES), I32), jax.ShapeDtypeStruct((n, V7X_LANES), F32),
                   jax.ShapeDtypeStruct((1, V7X_LANES), F32)),
        scratch_shapes=[pltpu.VMEM((1, V7X_LANES), F32)],
        compiler_params=pltpu.CompilerParams(dimension_semantics=("arbitrary",),
                                             vmem_limit_bytes=_vmem_limit(vmem)),
        name="finish",
    )(*ins)


def _tile_copy(src_ref, src_row, dst_ref, dst_row, sem):
    return pltpu.make_async_copy(src_ref.at[src_row], dst_ref.at[dst_row], sem)


def _dispatch_kernel(dest_ref, hp_ref, hs_ref, xs_ref, sem, *, n_tiles_p):
    def scatter(hf_ref):
        tm = hf_ref.shape[0]

        def tokens(i, carry):
            for u in range(V7X_SUBLANES):
                r = i * V7X_SUBLANES + u
                for k in range(TOP_K):
                    _tile_copy(hf_ref, r, xs_ref, dest_ref[0, 0, r * TOP_K + k], sem).start(priority=k % 2)
            return carry

        lax.fori_loop(0, tm // V7X_SUBLANES, tokens, 0)
        rows = pl.ds(0, tm * TOP_K)
        pltpu.make_async_copy(xs_ref.at[rows], xs_ref.at[rows], sem).wait()

    @pl.when(pl.program_id(0) < n_tiles_p)
    def _():
        scatter(hp_ref)

    @pl.when(pl.program_id(0) == n_tiles_p)
    def _():
        scatter(hs_ref)


def _dispatch(hf_p, dest_p, hf_s, dest_s, tm):
    tile = (V7X_SUBLANES, V7X_LANES)
    hf_p, hf_s = hf_p.reshape(-1, *tile), hf_s.reshape(-1, *tile)
    n_p, n_s = hf_p.shape[0], hf_s.shape[0]
    assert n_p % tm == 0 and n_s <= tm
    n_tiles_p = n_p // tm
    dest3 = jnp.concatenate([dest_p.reshape(n_tiles_p, tm * TOP_K),
                             jnp.pad(dest_s.reshape(1, n_s * TOP_K), ((0, 0), (0, (tm - n_s) * TOP_K)))])
    dest3 = dest3.reshape(n_tiles_p + 1, 1, tm * TOP_K)
    return pl.pallas_call(
        functools.partial(_dispatch_kernel, n_tiles_p=n_tiles_p),
        grid=(n_tiles_p + 1,),
        in_specs=[pl.BlockSpec((1, 1, tm * TOP_K), lambda i: (i, 0, 0), memory_space=pltpu.SMEM),
                  pl.BlockSpec((tm, *tile), lambda i: (jnp.minimum(i, n_tiles_p - 1), 0, 0)),
                  pl.BlockSpec((n_s, *tile), lambda i: (0, 0, 0))],
        out_specs=pl.BlockSpec(memory_space=pl.ANY),
        out_shape=jax.ShapeDtypeStruct(((n_p + n_s) * TOP_K, *tile), F32),
        scratch_shapes=[pltpu.SemaphoreType.DMA(())],
        compiler_params=pltpu.CompilerParams(dimension_semantics=("arbitrary",)),
        name="dispatch",
    )(dest3, hf_p, hf_s)


def _moe_kernel(vt_ref, ve_ref, nx_ref, lo_ref, hi_ref, x_ref, w1_ref, b1_ref, w2_ref, b2_ref, y_ref,
                w1s, w2s, w1b, w2b, sem, *, d_ff):
    v = pl.program_id(0)
    lo, hi = lo_ref[v], hi_ref[v]
    prev = jnp.maximum(v - 1, 0)
    first = jnp.logical_or(v == 0, vt_ref[v] != vt_ref[prev])

    def fetch(e):
        return (pltpu.make_async_copy(w1_ref.at[e], w1s, sem.at[0]),
                pltpu.make_async_copy(w2_ref.at[e], w2s, sem.at[1]))

    @pl.when(v == 0)
    def _():
        for cp in fetch(ve_ref[0]):
            cp.start()

    @pl.when(jnp.logical_or(v == 0, ve_ref[v] != ve_ref[prev]))
    def _():
        for cp in fetch(ve_ref[v]):
            cp.wait()
        w1b[...] = w1s[...].astype(jnp.bfloat16)
        w2b[...] = w2s[...].astype(jnp.bfloat16)

        @pl.when(nx_ref[v] >= 0)
        def _():
            for cp in fetch(nx_ref[v]):
                cp.start()

    @pl.when(hi > lo)
    def _():
        x = _from_token_tiles(x_ref).astype(jnp.bfloat16)
        a = jnp.dot(x, w1b[...], preferred_element_type=F32) + b1_ref[0]
        x_glu = jnp.minimum(a[:, :d_ff], SWIGLU_LIMIT)
        x_lin = jnp.clip(a[:, d_ff:], -SWIGLU_LIMIT, SWIGLU_LIMIT)
        mid = x_glu * jax.nn.sigmoid(SWIGLU_ALPHA * x_glu) * (x_lin + 1.0)
        y = jnp.dot(mid.astype(jnp.bfloat16), w2b[...], preferred_element_type=F32) + b2_ref[0]
        rows = lax.broadcasted_iota(I32, (y.shape[0], 1), 0)
        mine = (rows >= lo) & (rows < hi)

        @pl.when(first)
        def _():
            _to_token_tiles(y_ref, jnp.where(mine, y, 0.0))

        @pl.when(jnp.logical_not(first))
        def _():
            _to_token_tiles(y_ref, jnp.where(mine, y, _from_token_tiles(y_ref)))


def _moe_experts(xs, visits, w1, b1, w2, b2):
    ne, d, ff2 = w1.shape
    d_ff = ff2 // 2
    n_visits = visits[0].shape[0]
    row = lambda v, vt, ve, nx, lo, hi: (vt[v], 0)
    exp = lambda v, vt, ve, nx, lo, hi: (ve[v], 0, 0)
    vmem = (4 * 2 * 2 * MOE_TILE * d + (4 + 2) * (d * ff2 + d_ff * d) + 2 * 4 * (ff2 + d)
            + 4 * 4 * MOE_TILE * ff2)
    tiles = pl.BlockSpec((MOE_TILE * V7X_SUBLANES, V7X_LANES), row)
    grid_spec = pltpu.PrefetchScalarGridSpec(
        num_scalar_prefetch=5,
        grid=(n_visits,),
        in_specs=[tiles,
                  pl.BlockSpec(memory_space=pl.ANY), pl.BlockSpec((1, 1, ff2), exp),
                  pl.BlockSpec(memory_space=pl.ANY), pl.BlockSpec((1, 1, d), exp)],
        out_specs=tiles,
        scratch_shapes=[pltpu.VMEM((d, ff2), F32), pltpu.VMEM((d_ff, d), F32),
                        pltpu.VMEM((d, ff2), jnp.bfloat16), pltpu.VMEM((d_ff, d), jnp.bfloat16),
                        pltpu.SemaphoreType.DMA((2,))],
    )
    return pl.pallas_call(
        functools.partial(_moe_kernel, d_ff=d_ff),
        grid_spec=grid_spec,
        out_shape=jax.ShapeDtypeStruct(xs.shape, F32),
        compiler_params=pltpu.CompilerParams(dimension_semantics=("arbitrary",),
                                             vmem_limit_bytes=_vmem_limit(vmem)),
        name="moe",
    )(*visits, xs, w1, b1.reshape(ne, 1, ff2), w2, b2.reshape(ne, 1, d))


def _moe_visits(sizes, n_rows):
    n_tiles = n_rows // MOE_TILE
    n_visits = n_tiles + N_EXPERTS - 1
    ends = jnp.cumsum(sizes)
    starts = ends - sizes
    first_tile = starts // MOE_TILE
    n_vis = jnp.where(sizes > 0, (ends - 1) // MOE_TILE - first_tile + 1, 0)
    vis_end = jnp.cumsum(n_vis)
    total = vis_end[-1]
    v = jnp.clip(jnp.arange(n_visits, dtype=I32), 0, jnp.maximum(total - 1, 0))
    e = jnp.sum((vis_end[None, :] <= v[:, None]).astype(I32), axis=1)
    pick = lambda tab: jnp.sum(jnp.where(e[:, None] == jnp.arange(N_EXPERTS)[None, :], tab[None, :], 0), axis=1)
    tile = pick(first_tile) + v - pick(vis_end - n_vis)
    lo = jnp.maximum(pick(starts), tile * MOE_TILE) - tile * MOE_TILE
    hi = jnp.minimum(pick(ends), (tile + 1) * MOE_TILE) - tile * MOE_TILE
    live = jnp.arange(n_visits) < total
    ids = jnp.arange(N_EXPERTS, dtype=I32)
    later = jnp.where((ids[None, :] > ids[:, None]) & (sizes[None, :] > 0), ids[None, :], N_EXPERTS)
    nxt = jnp.min(later, axis=1)
    nxt = pick(jnp.where(nxt < N_EXPERTS, nxt, -1))
    return (tile.astype(I32), e.astype(I32), nxt.astype(I32), jnp.where(live, lo, 0).astype(I32),
            jnp.where(live, hi, 0).astype(I32))


def _final_kernel(dest_ref, nxt_ref, x_ref, tg_ref, g_ref, ys_ref, y_ref, buf, sem, *, tm):
    i = pl.program_id(0)
    n = pl.num_programs(0)

    def gather(d_ref, slot):
        def tokens(i8, carry):
            for u in range(V7X_SUBLANES):
                r = i8 * V7X_SUBLANES + u
                tile = pl.ds(pl.multiple_of(r * V7X_SUBLANES, V7X_SUBLANES), V7X_SUBLANES)
                for k in range(TOP_K):
                    pltpu.make_async_copy(ys_ref.at[d_ref[0, 0, r * TOP_K + k]], buf.at[slot, k, tile, :],
                                          sem.at[slot]).start(priority=k % 2)
            return carry
        lax.fori_loop(0, tm // V7X_SUBLANES, tokens, 0)

    @pl.when(i == 0)
    def _():
        gather(dest_ref, 0)

    for slot in range(2):
        @pl.when(jnp.logical_and(i + 1 < n, (i + 1) % 2 == slot))
        def _(slot=slot):
            gather(nxt_ref, slot)

    for slot in range(2):
        @pl.when(i % 2 == slot)
        def _(slot=slot):
            pltpu.make_async_copy(buf.at[slot], buf.at[slot], sem.at[slot]).wait()
            tg = tg_ref[...]
            parts = []
            for j in range(V7X_SUBLANES):
                f = None
                for k in range(TOP_K):
                    rows = buf[slot, k, pl.ds(j, tm, stride=V7X_SUBLANES), :]
                    f = tg[:, k:k + 1] * rows if f is None else f + tg[:, k:k + 1] * rows
                parts.append(f)
            y_ref[...] = _rms(x_ref[...] + jnp.concatenate(parts, axis=1), g_ref[...])


def _final(x1, tg, dest, ys, g, tm):
    n, d = x1.shape
    n_tiles = n // tm
    tile = (V7X_SUBLANES, V7X_LANES)
    dest3 = dest.reshape(n_tiles, 1, tm * TOP_K)
    row = lambda i: (i, 0)
    smem = lambda index_map: pl.BlockSpec((1, 1, tm * TOP_K), index_map, memory_space=pltpu.SMEM)
    vmem = 4 * (2 * TOP_K * tm * d + 2 * 2 * tm * d + 2 * tm * V7X_LANES + 4 * tm * d)
    return pl.pallas_call(
        functools.partial(_final_kernel, tm=tm),
        grid=(n_tiles,),
        in_specs=[smem(lambda i: (i, 0, 0)), smem(lambda i: (jnp.minimum(i + 1, n_tiles - 1), 0, 0)),
                  pl.BlockSpec((tm, d), row), pl.BlockSpec((tm, V7X_LANES), row),
                  pl.BlockSpec((1, d), lambda i: (0, 0)), pl.BlockSpec(memory_space=pl.ANY)],
        out_specs=pl.BlockSpec((tm, d), row),
        out_shape=jax.ShapeDtypeStruct((n, d), F32),
        scratch_shapes=[pltpu.VMEM((2, TOP_K, tm * V7X_SUBLANES, V7X_LANES), F32), pltpu.SemaphoreType.DMA((2,))],
        compiler_params=pltpu.CompilerParams(dimension_semantics=("arbitrary",),
                                             vmem_limit_bytes=_vmem_limit(vmem)),
        name="final",
    )(dest3, dest3, x1, tg, g.reshape(1, d), ys.reshape(-1, *tile))


def kernel(x_prompt, x_sample, cache_k, cache_v, state_ssm_re, state_ssm_im, norm_mix_g, w_in, attn_out_norm_g, ssm_a_re, ssm_a_im, ssm_log_dt, ssm_b_re, ssm_b_im, ssm_c_re, ssm_c_im, ssm_d, w_glu, b_glu, ssm_out_norm_g, w_out, norm_ffn_g, w_router, b_router, w_moe1, b_moe1, w_moe2, b_moe2, norm_final_g):
    depth = w_in.shape[0]
    assert depth == 1, "single-layer trunk"
    bp, s, d = x_prompt.shape
    bs, t, _ = x_sample.shape
    n_buf = cache_k.shape[2]
    n_heads, head_dim = cache_k.shape[3], cache_k.shape[4]
    assert head_dim == HEAD_DIM
    a = n_heads * head_dim
    g = ssm_a_re.shape[1]
    gn = g * SSM_STATE
    assert s % (max(dd for _, dd in DILATED_GROUPS) * KEYS_BACK) == 0
    np_, ns = bp * s, bs * t
    tm_p, tm_s = 256, ns
    l = 0
    ssm_p = (ssm_a_re[l], ssm_a_im[l], ssm_log_dt[l], ssm_b_re[l], ssm_b_im[l], ssm_c_re[l], ssm_c_im[l],
             ssm_d[l])
    tail = dict(attn_out_norm_g=attn_out_norm_g[l], w_glu=w_glu[l], b_glu=b_glu[l],
                ssm_out_norm_g=ssm_out_norm_g[l], w_out=w_out[l], norm_ffn_g=norm_ffn_g[l],
                w_router=w_router[l], b_router=b_router[l])

    xp = x_prompt.reshape(np_, d)
    qp, kp, vp, up = _proj(xp, norm_mix_g[l], w_in[l], a, tm_p)
    attn_p = _attn_prompt(qp.reshape(bp, s, a), kp.reshape(bp, s, a), vp.reshape(bp, s, a))
    zeros = jnp.zeros((bp, gn), F32)
    tables = _ssm_tables(*ssm_p, PROMPT_CHUNK)
    y_p, hp_re, hp_im = _ssm_chunks_on_lanes(up.reshape(bp, s, -1), zeros, zeros, _ssm_tables_lanes(tables),
                                             PROMPT_CHUNK)
    seen = jnp.zeros((1, V7X_LANES), F32)
    x1_p, hf_p, ti_p, tg_p, seen = _finish(xp, attn_p.reshape(np_, a), y_p.reshape(np_, -1), tail, seen, tm_p)

    xs = x_sample.reshape(ns, d)
    qs, ks, vs, us = _proj(xs, norm_mix_g[l], w_in[l], a, tm_s)
    heads = lambda z: z.reshape(bs, t, n_heads, head_dim)
    attn_s = _attn_sample(heads(qs), heads(ks), heads(vs), cache_k[l], cache_v[l])
    y_s, hs_re, hs_im = _ssm(us.reshape(bs, t, -1), state_ssm_re[l].reshape(bs, gn),
                             state_ssm_im[l].reshape(bs, gn), _ssm_tables_rows(tables, t), t)
    x1_s, hf_s, ti_s, tg_s, seen = _finish(xs, attn_s.reshape(ns, a), y_s.reshape(ns, -1), tail, seen, tm_s)

    sizes = seen[0, :N_EXPERTS].astype(I32)
    starts = jnp.cumsum(sizes) - sizes
    experts = jnp.arange(N_EXPERTS, dtype=I32)[None, None, :]

    def sorted_row(ti):
        hit = ti[:, :TOP_K, None] == experts
        return jnp.sum(jnp.where(hit, starts[None, None, :], 0), axis=-1) + ti[:, TOP_K:2 * TOP_K]

    dest_p, dest_s = sorted_row(ti_p), sorted_row(ti_s)
    n_rows = (np_ + ns) * TOP_K
    assert n_rows % MOE_TILE == 0
    xs_sorted = _dispatch(hf_p, dest_p, hf_s, dest_s, tm_p).reshape(-1, V7X_LANES)
    ys = _moe_experts(xs_sorted, _moe_visits(sizes, n_rows), w_moe1[l], b_moe1[l], w_moe2[l], b_moe2[l])
    y_prompt = _final(x1_p, tg_p, dest_p, ys, norm_final_g, tm_p).reshape(bp, s, d)
    y_sample = _final(x1_s, tg_s, dest_s, ys, norm_final_g, tm_s).reshape(bs, t, d)

    keep = min(max(w for w, _ in DILATED_GROUPS), s)
    k5 = lambda z, b_, s_: z.reshape(1, b_, s_, n_heads, head_dim)
    st = lambda z, b_: z.reshape(1, b_, g, SSM_STATE)
    return (y_prompt, y_sample,
            k5(kp, bp, s)[:, :, s - keep:], k5(vp, bp, s)[:, :, s - keep:], st(hp_re, bp), st(hp_im, bp),
            k5(ks, bs, t), k5(vs, bs, t), st(hs_re, bs), st(hs_im, bs))
```

```python
import functools
import math

import numpy as np
import jax
import jax.numpy as jnp
from jax import lax
from jax.experimental import pallas as pl
from jax.experimental.pallas import tpu as pltpu

F32 = jnp.float32
I32 = jnp.int32

V7X_LANES = 128
V7X_SUBLANES = 8
V7X_VMEM_BYTES = 64 * 1024 * 1024

HEAD_DIM = 64
HEADS_PER_LANE_TILE = V7X_LANES // HEAD_DIM
DILATED_GROUPS = ((128, 1), (512, 4), (2048, 16))
KEYS_BACK = 128
SSM_GROUP = 16
SSM_STATE = 64
N_EXPERTS = 32
TOP_K = 4
SWIGLU_LIMIT = 7.0
SWIGLU_ALPHA = 1.702
RMS_EPS = 1e-5
NEG_INF = -1e30
PROMPT_CHUNK = 16
MOE_TILE = 256


def _vmem_limit(nbytes):
    return int(min(nbytes + (8 << 20), V7X_VMEM_BYTES - (8 << 20)))


def _rms(x, g):
    return x * lax.rsqrt(jnp.mean(x * x, axis=-1, keepdims=True) + RMS_EPS) * g


def _to_token_tiles(ref, val):
    rows, width = val.shape
    assert width == V7X_SUBLANES * V7X_LANES and ref.shape == (rows * V7X_SUBLANES, V7X_LANES)
    for j in range(V7X_SUBLANES):
        ref[pl.ds(j, rows, stride=V7X_SUBLANES), :] = val[:, j * V7X_LANES:(j + 1) * V7X_LANES]


def _from_token_tiles(ref):
    rows = ref.shape[0] // V7X_SUBLANES
    return jnp.concatenate([ref[pl.ds(j, rows, stride=V7X_SUBLANES), :] for j in range(V7X_SUBLANES)], axis=1)


def _proj_kernel(x_ref, g_ref, w_ref, q_ref, k_ref, v_ref, u_ref, *, attn_width, q_scale):
    h = _rms(x_ref[...], g_ref[...])
    z = jnp.dot(h, w_ref[...], preferred_element_type=F32)
    a = attn_width
    q_ref[...] = z[:, :a] * q_scale
    k_ref[...] = z[:, a:2 * a]
    v_ref[...] = z[:, 2 * a:3 * a]
    u_ref[...] = z[:, 3 * a:]


def _proj(x2d, norm_g, w_in, attn_width, tm):
    n, d = x2d.shape
    mix = w_in.shape[1]
    ssm_width = mix - 3 * attn_width
    row = lambda i: (i, 0)
    const = lambda i: (0, 0)
    out_shape = (jax.ShapeDtypeStruct((n, attn_width), F32),) * 3 + (
        jax.ShapeDtypeStruct((n, ssm_width), F32),)
    vmem = 2 * 4 * (tm * d + d * mix + tm * mix) + 4 * tm * mix
    return pl.pallas_call(
        functools.partial(_proj_kernel, attn_width=attn_width, q_scale=HEAD_DIM ** -0.5),
        grid=(n // tm,),
        in_specs=[pl.BlockSpec((tm, d), row), pl.BlockSpec((1, d), const),
                  pl.BlockSpec((d, mix), const)],
        out_specs=(pl.BlockSpec((tm, attn_width), row),) * 3 + (pl.BlockSpec((tm, ssm_width), row),),
        out_shape=out_shape,
        compiler_params=pltpu.CompilerParams(dimension_semantics=("arbitrary",),
                                             vmem_limit_bytes=_vmem_limit(vmem)),
        name="proj",
    )(x2d, norm_g.reshape(1, d), w_in)


def _attn_prompt_kernel(q_ref, k_ref, v_ref, o_ref, *scratch, seq, unroll):
    nb = KEYS_BACK
    n_groups = len(DILATED_GROUPS)
    m_sc, l_sc, acc_sc = scratch[:n_groups], scratch[n_groups:2 * n_groups], scratch[2 * n_groups:]
    lane = lax.broadcasted_iota(I32, (1, V7X_LANES), 1)
    head_a = lane < HEAD_DIM
    qi = lax.broadcasted_iota(I32, (nb, 2 * nb), 0)
    kj = lax.broadcasted_iota(I32, (nb, 2 * nb), 1)
    rel = nb + qi - kj
    band = (rel >= 0) & (rel <= nb)

    def rows(ref, start, d):
        if d == 1:
            return ref[pl.ds(pl.multiple_of(start, nb), nb), :]
        return ref[pl.ds(start, nb, stride=d), :]

    def put(ref, start, d, val):
        if d == 1:
            ref[pl.ds(pl.multiple_of(start, nb), nb), :] = val
        else:
            ref[pl.ds(start, nb, stride=d), :] = val

    for gi, (w, d) in enumerate(DILATED_GROUPS):
        assert w // d == nb
        nblk = seq // (d * nb)

        def block(t, gi=gi, d=d, nblk=nblk):
            r = t // nblk
            n = t % nblk
            cur = r + d * nb * n
            prev = r + d * nb * jnp.maximum(n - 1, 0)
            q = rows(q_ref, cur, d)
            kk = jnp.concatenate([rows(k_ref, prev, d), rows(k_ref, cur, d)], axis=0)
            vv = jnp.concatenate([rows(v_ref, prev, d), rows(v_ref, cur, d)], axis=0)
            qs = jnp.concatenate([jnp.where(head_a, q, 0.0), jnp.where(head_a, 0.0, q)], axis=0)
            s = lax.dot_general(qs, kk, (((1,), (1,)), ((), ())), preferred_element_type=F32)
            valid = band & (kj >= jnp.where(n > 0, 0, nb))
            s = jnp.where(jnp.concatenate([valid, valid], axis=0), s, NEG_INF)
            m = jnp.max(s, axis=-1, keepdims=True)
            p = jnp.exp(s - m)
            l = jnp.sum(p, axis=-1, keepdims=True)
            o = jnp.where(head_a,
                          jnp.dot(p[:nb], vv, preferred_element_type=F32),
                          jnp.dot(p[nb:], vv, preferred_element_type=F32))
            put(m_sc[gi], cur, d, jnp.where(head_a, m[:nb], m[nb:]))
            put(l_sc[gi], cur, d, jnp.where(head_a, l[:nb], l[nb:]))
            put(acc_sc[gi], cur, d, o)

        def blocks(tt, carry, block=block):
            for u in range(unroll):
                block(tt * unroll + u)
            return carry

        lax.fori_loop(0, seq // nb // unroll, blocks, 0)

    def merge(n, carry):
        sl = pl.ds(pl.multiple_of(n * nb, nb), nb)
        ms = [m[sl, :] for m in m_sc]
        m_all = functools.reduce(jnp.maximum, ms)
        cs = [jnp.exp(m - m_all) for m in ms]
        den = functools.reduce(jnp.add, [c * l[sl, :] for c, l in zip(cs, l_sc)])
        num = functools.reduce(jnp.add, [c * acc[sl, :] for c, acc in zip(cs, acc_sc)])
        o_ref[sl, :] = num / den
        return carry

    lax.fori_loop(0, seq // nb, merge, 0, unroll=2)


def _attn_prompt(q, k, v, unroll=8):
    b, s, a = q.shape
    n_scratch = 3 * len(DILATED_GROUPS)
    spec = pl.BlockSpec((None, s, V7X_LANES), lambda i, j: (i, 0, j))
    vmem = 4 * s * V7X_LANES * (2 * 4 + n_scratch)
    return pl.pallas_call(
        functools.partial(_attn_prompt_kernel, seq=s, unroll=unroll),
        grid=(b, a // V7X_LANES),
        in_specs=[spec, spec, spec],
        out_specs=spec,
        out_shape=jax.ShapeDtypeStruct((b, s, a), F32),
        scratch_shapes=[pltpu.VMEM((s, V7X_LANES), F32)] * n_scratch,
        compiler_params=pltpu.CompilerParams(dimension_semantics=("arbitrary", "arbitrary"),
                                             vmem_limit_bytes=_vmem_limit(vmem)),
        name="attn_prompt",
    )(q, k, v)


def _sample_key_counts(n_buf, n_new):
    cnt = np.zeros((n_new, n_buf + n_new), np.float32)
    for w, d in DILATED_GROUPS:
        for i in range(n_new):
            for j in range(w // d + 1):
                idx = n_buf + i - d * j
                if idx >= 0:
                    cnt[i, idx] += 1.0
    return cnt[:, :n_buf], cnt[:, n_buf:]


def _attn_sample_kernel(q_ref, kn_ref, vn_ref, kt_ref, vt_ref, cnt_ref, o_ref, *, n_heads, cnt_new):
    n_new, a = q_ref.shape
    rows = n_new * n_heads
    lane_head = lax.broadcasted_iota(I32, (rows, a), 1) // HEAD_DIM
    row_head = lax.broadcasted_iota(I32, (rows, a), 0) % n_heads
    row_query = lax.broadcasted_iota(I32, (rows, 1), 0) // n_heads
    own = lane_head == row_head
    q = q_ref[...]
    qb = jnp.where(own, jnp.broadcast_to(q[:, None, :], (n_new, n_heads, a)).reshape(rows, a), 0.0)
    s_buf = jnp.dot(qb, kt_ref[...], preferred_element_type=F32)
    cnt = cnt_ref[...]
    cnt = jnp.broadcast_to(cnt[:, None, :], (n_new, n_heads, cnt.shape[-1])).reshape(rows, -1)
    valid = cnt > 0.0
    m = jnp.max(jnp.where(valid, s_buf, NEG_INF), axis=-1, keepdims=True)
    s_new, c_new = [], []
    for j in range(n_new):
        c = functools.reduce(lambda acc, i: jnp.where(row_query == i, float(cnt_new[i, j]), acc),
                             range(n_new), jnp.zeros((rows, 1), F32))
        s = jnp.sum(qb * kn_ref[pl.ds(j, 1), :], axis=-1, keepdims=True)
        m = jnp.maximum(m, jnp.where(c > 0.0, s, NEG_INF))
        s_new.append(s)
        c_new.append(c)
    p = jnp.where(valid, cnt * jnp.exp(s_buf - m), 0.0)
    den = jnp.sum(p, axis=-1, keepdims=True)
    o = lax.dot_general(p, vt_ref[...], (((1,), (1,)), ((), ())), preferred_element_type=F32)
    for j in range(n_new):
        pj = jnp.where(c_new[j] > 0.0, c_new[j] * jnp.exp(s_new[j] - m), 0.0)
        den = den + pj
        o = o + pj * vn_ref[pl.ds(j, 1), :]
    o = jnp.where(own, o / den, 0.0)
    o_ref[...] = jnp.sum(o.reshape(n_new, n_heads, a), axis=1)


def _attn_sample(q, k_new, v_new, cache_kt, cache_vt):
    b, t, a = q.shape
    n_buf = cache_kt.shape[-1]
    n_heads = a // HEAD_DIM
    cnt_buf, cnt_new = _sample_key_counts(n_buf, t)
    new = pl.BlockSpec((None, t, a), lambda i: (i, 0, 0))
    buf = pl.BlockSpec((None, a, n_buf), lambda i: (i, 0, 0))
    vmem = 4 * (2 * 2 * n_buf * a + 8 * t * n_heads * n_buf)
    return pl.pallas_call(
        functools.partial(_attn_sample_kernel, n_heads=n_heads, cnt_new=cnt_new),
        grid=(b,),
        in_specs=[new, new, new, buf, buf, pl.BlockSpec((t, n_buf), lambda i: (0, 0))],
        out_specs=new,
        out_shape=jax.ShapeDtypeStruct((b, t, a), F32),
        compiler_params=pltpu.CompilerParams(dimension_semantics=("arbitrary",),
                                             vmem_limit_bytes=_vmem_limit(vmem)),
        name="attn_sample",
    )(q, k_new, v_new, cache_kt, cache_vt, jnp.asarray(cnt_buf))


def _pad_states(m, g):
    half = (np.arange(g) % 2)[:, None, None]
    z = jnp.zeros_like(m)
    return jnp.where(half == 0, jnp.concatenate([m, z], -1), jnp.concatenate([z, m], -1))


def _ssm_tables(a_re, a_im, log_dt, b_re, b_im, c_re, c_im, d_skip, chunk):
    g, n = a_re.shape
    c = b_re.shape[-1]
    lc = chunk * c
    dt = jnp.exp(log_dt)[:, None]
    x, y = a_re * dt, a_im * dt
    ex = jnp.exp(x)
    ar, ai = ex * jnp.cos(y), ex * jnp.sin(y)
    nr = jnp.expm1(x) * jnp.cos(y) - 2.0 * jnp.sin(0.5 * y) ** 2
    ni = ai
    den = a_re * a_re + a_im * a_im
    fr = (nr * a_re + ni * a_im) / den
    fi = (ni * a_re - nr * a_im) / den
    b_re_t, b_im_t = b_re.transpose(0, 2, 1), b_im.transpose(0, 2, 1)
    bbr = fr[:, None, :] * b_re_t - fi[:, None, :] * b_im_t
    bbi = fr[:, None, :] * b_im_t + fi[:, None, :] * b_re_t
    pr, pi = [jnp.ones_like(ar)], [jnp.zeros_like(ar)]
    for _ in range(chunk):
        pr.append(pr[-1] * ar - pi[-1] * ai)
        pi.append(pr[-2] * ai + pi[-1] * ar)
    pr, pi = jnp.stack(pr, 1), jnp.stack(pi, 1)
    cpr = c_re[:, None] * pr[:, :, None, :] - c_im[:, None] * pi[:, :, None, :]
    cpi = c_re[:, None] * pi[:, :, None, :] + c_im[:, None] * pr[:, :, None, :]
    cq_r = cpr[:, :chunk].transpose(0, 2, 1, 3)[:, :, :, None, :]
    cq_i = cpi[:, :chunk].transpose(0, 2, 1, 3)[:, :, :, None, :]
    kern = jnp.sum(cq_r * bbr[:, None, None] - cq_i * bbi[:, None, None], axis=-1)
    ribbon = jnp.concatenate([kern[:, :, ::-1].reshape(g, c, lc), jnp.zeros((g, c, lc - c), F32)], axis=-1)
    tt = jnp.stack([ribbon[:, :, (chunk - 1 - t) * c:(chunk - 1 - t) * c + lc] for t in range(chunk)], axis=1)
    tt = tt.reshape(g, lc, lc)
    qr, qi = pr[:, chunk - 1::-1][:, :, None, :], pi[:, chunk - 1::-1][:, :, None, :]
    wr = (qr * bbr[:, None] - qi * bbi[:, None]).reshape(g, lc, n)
    wi = (qr * bbi[:, None] + qi * bbr[:, None]).reshape(g, lc, n)
    vr = cpr[:, 1:].reshape(g, lc, n)
    vi = -cpi[:, 1:].reshape(g, lc, n)
    return dict(chunk=chunk, tt=tt, wr=wr, wi=wi, vr=vr, vi=vi, d=d_skip, pr=pr, pi=pi)


def _ssm_tables_lanes(tab):
    g, chunk = tab["d"].shape[0], tab["chunk"]
    tr = lambda m: m.transpose(0, 2, 1)
    return dict(tt=tab["tt"], wt_re=tr(_pad_states(tab["wr"], g)), wt_im=tr(_pad_states(tab["wi"], g)),
                v_re=_pad_states(tab["vr"], g), v_im=_pad_states(tab["vi"], g),
                dcol=jnp.tile(tab["d"], (1, chunk))[:, :, None],
                al_re=tab["pr"][:, chunk].reshape(1, -1), al_im=tab["pi"][:, chunk].reshape(1, -1))


def _ssm_tables_rows(tab, chunk):
    g, c = tab["d"].shape
    big, lc = tab["chunk"], chunk * c
    tr = lambda m: m.transpose(0, 2, 1)
    tail = slice((big - chunk) * c, big * c)
    return dict(toep=tr(tab["tt"][:, :lc, :lc]),
                wr=_pad_states(tab["wr"][:, tail], g), wi=_pad_states(tab["wi"][:, tail], g),
                vr=tr(_pad_states(tab["vr"][:, :lc], g)), vi=tr(_pad_states(tab["vi"][:, :lc], g)),
                d=jnp.tile(tab["d"], (1, chunk))[:, None, :],
                al_re=tab["pr"][:, chunk].reshape(1, -1), al_im=tab["pi"][:, chunk].reshape(1, -1))


def _ssm_state_kernel(u_ref, wr_ref, wi_ref, er_ref, ei_ref):
    er = jnp.dot(u_ref[0], wr_ref[0], preferred_element_type=F32)
    ei = jnp.dot(u_ref[0], wi_ref[0], preferred_element_type=F32)
    er_ref[...] = er + jnp.dot(u_ref[1], wr_ref[1], preferred_element_type=F32)
    ei_ref[...] = ei + jnp.dot(u_ref[1], wi_ref[1], preferred_element_type=F32)


def _ssm_scan_kernel(er_ref, ei_ref, ar_ref, ai_ref, hr_ref, hi_ref, xr_ref, xi_ref, lr_ref, li_ref,
                     *, n_chunks, nb):
    shape = hr_ref.shape
    ar = jnp.broadcast_to(ar_ref[...], shape)
    ai = jnp.broadcast_to(ai_ref[...], shape)

    def step(j, carry):
        xr, xi = carry
        sl = pl.ds(pl.multiple_of(j * nb, nb), nb)
        xr_ref[sl, :] = xr
        xi_ref[sl, :] = xi
        return (ar * xr - ai * xi + er_ref[sl, :], ar * xi + ai * xr + ei_ref[sl, :])

    xr, xi = lax.fori_loop(0, n_chunks, step, (hr_ref[...], hi_ref[...]))
    lr_ref[...] = xr
    li_ref[...] = xi


def _ssm_out_kernel(u_ref, t_ref, d_ref, xr_ref, xi_ref, vr_ref, vi_ref, y_ref):
    xr = xr_ref[...]
    xi = xi_ref[...]
    for a in range(2):
        u = u_ref[a]
        y = jnp.dot(u, t_ref[a], preferred_element_type=F32) + d_ref[a] * u
        y = y + jnp.dot(xr, vr_ref[a], preferred_element_type=F32)
        y_ref[a] = y + jnp.dot(xi, vi_ref[a], preferred_element_type=F32)


def _ssm_scan(er, ei, tab, h0_re, h0_im, n_chunks):
    r, gn = er.shape
    b = r // n_chunks
    lanes = 4 * V7X_LANES
    col = lambda i: (0, i)
    return pl.pallas_call(
        functools.partial(_ssm_scan_kernel, n_chunks=n_chunks, nb=b),
        grid=(gn // lanes,),
        in_specs=[pl.BlockSpec((r, lanes), col), pl.BlockSpec((r, lanes), col),
                  pl.BlockSpec((1, lanes), col), pl.BlockSpec((1, lanes), col),
                  pl.BlockSpec((b, lanes), col), pl.BlockSpec((b, lanes), col)],
        out_specs=(pl.BlockSpec((r, lanes), col),) * 2 + (pl.BlockSpec((b, lanes), col),) * 2,
        out_shape=(jax.ShapeDtypeStruct((r, gn), F32),) * 2 + (jax.ShapeDtypeStruct((b, gn), F32),) * 2,
        compiler_params=pltpu.CompilerParams(dimension_semantics=("arbitrary",),
                                             vmem_limit_bytes=_vmem_limit(2 * 4 * 4 * r * lanes)),
        name="ssm_scan",
    )(er, ei, tab["al_re"], tab["al_im"], h0_re, h0_im)


def _ssm_state_t_kernel(a_ref, wr_ref, wi_ref, er_ref, ei_ref):
    def seq(b, carry):
        a0, a1 = a_ref[0, b], a_ref[1, b]
        er_ref[b] = (jnp.dot(wr_ref[0], a0, preferred_element_type=F32)
                     + jnp.dot(wr_ref[1], a1, preferred_element_type=F32))
        ei_ref[b] = (jnp.dot(wi_ref[0], a0, preferred_element_type=F32)
                     + jnp.dot(wi_ref[1], a1, preferred_element_type=F32))
        return carry

    lax.fori_loop(0, a_ref.shape[1], seq, 0)


def _ssm_out_t_kernel(a_ref, t_ref, d_ref, xr_ref, xi_ref, vr_ref, vi_ref, y_ref):
    def seq(b, carry):
        xr, xi = xr_ref[b], xi_ref[b]
        for a in range(2):
            at = a_ref[a, b]
            y = jnp.dot(t_ref[a], at, preferred_element_type=F32) + d_ref[a] * at
            y = y + jnp.dot(vr_ref[a], xr, preferred_element_type=F32)
            y_ref[a, b] = y + jnp.dot(vi_ref[a], xi, preferred_element_type=F32)
        return carry

    lax.fori_loop(0, a_ref.shape[1], seq, 0)


def _ssm_chunks_on_lanes(u, h0_re, h0_im, tab, chunk):
    b, s, width = u.shape
    g = width // SSM_GROUP
    n_chunks = s // chunk
    lc = chunk * SSM_GROUP
    assert n_chunks == V7X_LANES
    at = (u.reshape(b, n_chunks, chunk * width).transpose(0, 2, 1)
          .reshape(b, chunk, g, SSM_GROUP, n_chunks).transpose(2, 0, 1, 3, 4).reshape(g, b, lc, n_chunks))
    pair4 = lambda i: (i, 0, 0, 0)
    pair3 = lambda i: (i, 0, 0)
    cp = lambda vmem: pltpu.CompilerParams(dimension_semantics=("arbitrary",),
                                           vmem_limit_bytes=_vmem_limit(vmem))
    a_spec = pl.BlockSpec((2, b, lc, n_chunks), pair4)
    st_spec = pl.BlockSpec((None, b, V7X_LANES, n_chunks), pair4)
    w_spec = pl.BlockSpec((2, V7X_LANES, lc), pair3)
    a_bytes = 2 * b * lc * n_chunks
    st_bytes = b * V7X_LANES * n_chunks
    et_re, et_im = pl.pallas_call(
        _ssm_state_t_kernel,
        grid=(g // 2,),
        in_specs=[a_spec, w_spec, w_spec],
        out_specs=(st_spec,) * 2,
        out_shape=(jax.ShapeDtypeStruct((g // 2, b, V7X_LANES, n_chunks), F32),) * 2,
        compiler_params=cp(2 * 4 * (a_bytes + 2 * st_bytes + 4 * V7X_LANES * lc)),
        name="ssm_state_t",
    )(at, tab["wt_re"], tab["wt_im"])
    rows = lambda m: m.transpose(3, 1, 0, 2).reshape(n_chunks * b, g * SSM_STATE)
    xr, xi, lr, li = _ssm_scan(rows(et_re), rows(et_im), tab, h0_re, h0_im, n_chunks)
    cols = lambda m: m.reshape(n_chunks, b, g // 2, V7X_LANES).transpose(2, 1, 3, 0)
    yt = pl.pallas_call(
        _ssm_out_t_kernel,
        grid=(g // 2,),
        in_specs=[a_spec, pl.BlockSpec((2, lc, lc), pair3), pl.BlockSpec((2, lc, 1), pair3),
                  st_spec, st_spec, pl.BlockSpec((2, lc, V7X_LANES), pair3),
                  pl.BlockSpec((2, lc, V7X_LANES), pair3)],
        out_specs=a_spec,
        out_shape=jax.ShapeDtypeStruct((g, b, lc, n_chunks), F32),
        compiler_params=cp(2 * 4 * (2 * a_bytes + 2 * st_bytes + 2 * lc * lc + 6 * V7X_LANES * lc)),
        name="ssm_out_t",
    )(at, tab["tt"], tab["dcol"], cols(xr), cols(xi), tab["v_re"], tab["v_im"])
    y = (yt.reshape(g, b, chunk, SSM_GROUP, n_chunks).transpose(1, 2, 0, 3, 4)
         .reshape(b, chunk * width, n_chunks).transpose(0, 2, 1).reshape(b, s, width))
    return y, lr, li


def _ssm(u, h0_re, h0_im, tab, chunk):
    b, s, width = u.shape
    g = width // SSM_GROUP
    n_chunks = s // chunk
    lc = chunk * SSM_GROUP
    r = n_chunks * b
    gn = g * SSM_STATE
    ut = u.reshape(b, n_chunks, chunk, g, SSM_GROUP).transpose(3, 1, 0, 2, 4).reshape(g, r, lc)
    pair = lambda i: (i, 0, 0)
    cp = lambda vmem: pltpu.CompilerParams(dimension_semantics=("arbitrary",),
                                           vmem_limit_bytes=_vmem_limit(vmem))
    er, ei = pl.pallas_call(
        _ssm_state_kernel,
        grid=(g // 2,),
        in_specs=[pl.BlockSpec((2, r, lc), pair), pl.BlockSpec((2, lc, V7X_LANES), pair),
                  pl.BlockSpec((2, lc, V7X_LANES), pair)],
        out_specs=(pl.BlockSpec((r, V7X_LANES), lambda i: (0, i)),) * 2,
        out_shape=(jax.ShapeDtypeStruct((r, gn), F32),) * 2,
        compiler_params=cp(2 * 4 * (2 * r * lc + 4 * lc * V7X_LANES + 2 * r * V7X_LANES)),
        name="ssm_state",
    )(ut, tab["wr"], tab["wi"])
    xr, xi, lr, li = _ssm_scan(er, ei, tab, h0_re, h0_im, n_chunks)
    y = pl.pallas_call(
        _ssm_out_kernel,
        grid=(g // 2,),
        in_specs=[pl.BlockSpec((2, r, lc), pair), pl.BlockSpec((2, lc, lc), pair),
                  pl.BlockSpec((2, 1, lc), pair),
                  pl.BlockSpec((r, V7X_LANES), lambda i: (0, i)),
                  pl.BlockSpec((r, V7X_LANES), lambda i: (0, i)),
                  pl.BlockSpec((2, V7X_LANES, lc), pair), pl.BlockSpec((2, V7X_LANES, lc), pair)],
        out_specs=pl.BlockSpec((2, r, lc), pair),
        out_shape=jax.ShapeDtypeStruct((g, r, lc), F32),
        compiler_params=cp(2 * 4 * (4 * r * lc + 2 * lc * lc + 2 * r * V7X_LANES + 4 * V7X_LANES * lc)),
        name="ssm_out",
    )(ut, tab["toep"], tab["d"], xr, xi, tab["vr"], tab["vi"])
    y = y.reshape(g, n_chunks, b, chunk, SSM_GROUP).transpose(2, 1, 3, 0, 4).reshape(b, s, width)
    return y, lr, li


def _finish_kernel(x_ref, attn_ref, ssm_ref, ga_ref, wg_ref, bg_ref, gs_ref, wo_ref, gf_ref,
                   wr_ref, br_ref, base_ref, x1_ref, hf_ref, ti_ref, tg_ref, cnt_ref, seen):
    a = attn_ref.shape[-1]
    tm = x_ref.shape[0]

    @pl.when(pl.program_id(0) == 0)
    def _():
        seen[...] = base_ref[...]

    g = jax.nn.gelu(ssm_ref[...])
    gate = jax.nn.sigmoid(jnp.dot(g, wg_ref[...], preferred_element_type=F32) + bg_ref[...])
    mixed_a = _rms(attn_ref[...], ga_ref[...])
    mixed_s = _rms(g * gate, gs_ref[...])
    x1 = (x_ref[...] + jnp.dot(mixed_a, wo_ref[pl.ds(0, a), :], preferred_element_type=F32)
          + jnp.dot(mixed_s, wo_ref[pl.ds(a, wo_ref.shape[0] - a), :], preferred_element_type=F32))
    x1_ref[...] = x1
    hf = _rms(x1, gf_ref[...])
    _to_token_tiles(hf_ref, hf)
    hf_hi = hf.astype(jnp.bfloat16).astype(F32)
    hf_lo = hf - hf_hi
    small = (jnp.dot(hf_lo, wr_ref[1], preferred_element_type=F32)
             + jnp.dot(hf_hi, wr_ref[1], preferred_element_type=F32)
             + jnp.dot(hf_lo, wr_ref[0], preferred_element_type=F32))
    logits = jnp.dot(hf_hi, wr_ref[0], preferred_element_type=F32) + small + br_ref[...]
    lane = lax.broadcasted_iota(I32, logits.shape, 1)
    work = logits
    vals, idxs = [], []
    for _ in range(TOP_K):
        m = jnp.max(work, axis=-1, keepdims=True)
        idx = jnp.min(jnp.where(work == m, lane, V7X_LANES), axis=-1, keepdims=True)
        vals.append(m)
        idxs.append(idx)
        work = jnp.where(lane == idx, -jnp.inf, work)
    exps = [jnp.exp(v - vals[0]) for v in vals]
    den = exps[0]
    for e in exps[1:]:
        den = den + e
    picked = jnp.zeros(logits.shape, F32)
    for k in range(TOP_K):
        picked = picked + jnp.where(lane == idxs[k], 1.0, 0.0)
    tri = jnp.where(lax.broadcasted_iota(I32, (tm, tm), 0) > lax.broadcasted_iota(I32, (tm, tm), 1), 1.0, 0.0)
    before = jnp.dot(tri, picked, preferred_element_type=F32) + seen[...]
    ti = jnp.zeros(logits.shape, I32)
    tg = jnp.zeros(logits.shape, F32)
    for k in range(TOP_K):
        rank = jnp.sum(jnp.where(lane == idxs[k], before, 0.0), axis=-1, keepdims=True)
        ti = jnp.where(lane == k, idxs[k], ti)
        ti = jnp.where(lane == TOP_K + k, rank.astype(I32), ti)
        tg = jnp.where(lane == k, exps[k] / den, tg)
    ti_ref[...] = ti
    tg_ref[...] = tg
    seen[...] = seen[...] + jnp.sum(picked, axis=0, keepdims=True)
    cnt_ref[...] = seen[...]


def _finish(x2d, attn, ssm_y, p, seen, tm):
    n, d = x2d.shape
    a = attn.shape[1]
    sw = ssm_y.shape[1]
    ne = p["w_router"].shape[1]
    w_router = jnp.pad(p["w_router"], ((0, 0), (0, V7X_LANES - ne)))
    w_hi = w_router.astype(jnp.bfloat16).astype(F32)
    w_router = jnp.stack([w_hi, w_router - w_hi])
    b_router = jnp.pad(p["b_router"], (0, V7X_LANES - ne), constant_values=-jnp.inf).reshape(1, V7X_LANES)
    row = lambda i: (i, 0)
    full = lambda arr: pl.BlockSpec(arr.shape, lambda i: (0,) * arr.ndim)
    ins = [x2d, attn, ssm_y, p["attn_out_norm_g"].reshape(1, a), p["w_glu"], p["b_glu"].reshape(1, sw),
           p["ssm_out_norm_g"].reshape(1, sw), p["w_out"], p["norm_ffn_g"].reshape(1, d),
           w_router, b_router, seen]
    in_specs = [pl.BlockSpec((tm, d), row), pl.BlockSpec((tm, a), row), pl.BlockSpec((tm, sw), row)]
    in_specs += [full(arr) for arr in ins[3:]]
    vmem = 2 * 4 * (tm * (3 * d + a + sw + 2 * V7X_LANES) + sw * sw + d * d + 2 * d * V7X_LANES) + 16 * tm * d
    return pl.pallas_call(
        _finish_kernel,
        grid=(n // tm,),
        in_specs=in_specs,
        out_specs=(pl.BlockSpec((tm, d), row), pl.BlockSpec((tm * V7X_SUBLANES, V7X_LANES), row),
                   pl.BlockSpec((tm, V7X_LANES), row), pl.BlockSpec((tm, V7X_LANES), row),
                   pl.BlockSpec((1, V7X_LANES), lambda i: (0, 0))),
        out_shape=(jax.ShapeDtypeStruct((n, d), F32), jax.ShapeDtypeStruct((n * V7X_SUBLANES, V7X_LANES), F32),
                   jax.ShapeDtypeStruct((n, V7X_LANES), I32), jax.ShapeDtypeStruct((n, V7X_LANES), F32),
                   jax.ShapeDtypeStruct((1, V7X_LANES), F32)),
        scratch_shapes=[pltpu.VMEM((1, V7X_LANES), F32)],
        compiler_params=pltpu.CompilerParams(dimension_semantics=("arbitrary",),
                                             vmem_limit_bytes=_vmem_limit(vmem)),
        name="finish",
    )(*ins)


def _tile_copy(src_ref, src_row, dst_ref, dst_row, sem):
    return pltpu.make_async_copy(src_ref.at[src_row], dst_ref.at[dst_row], sem)


def _dispatch_kernel(dest_ref, hp_ref, hs_ref, xs_ref, sem, *, n_tiles_p):
    def scatter(hf_ref):
        tm = hf_ref.shape[0]

        def tokens(i, carry):
            for u in range(V7X_SUBLANES):
                r = i * V7X_SUBLANES + u
                for k in range(TOP_K):
                    _tile_copy(hf_ref, r, xs_ref, dest_ref[0, 0, r * TOP_K + k], sem).start(priority=k % 2)
            return carry

        lax.fori_loop(0, tm // V7X_SUBLANES, tokens, 0)
        rows = pl.ds(0, tm * TOP_K)
        pltpu.make_async_copy(xs_ref.at[rows], xs_ref.at[rows], sem).wait()

    @pl.when(pl.program_id(0) < n_tiles_p)
    def _():
        scatter(hp_ref)

    @pl.when(pl.program_id(0) == n_tiles_p)
    def _():
        scatter(hs_ref)


def _dispatch(hf_p, dest_p, hf_s, dest_s, tm):
    tile = (V7X_SUBLANES, V7X_LANES)
    hf_p, hf_s = hf_p.reshape(-1, *tile), hf_s.reshape(-1, *tile)
    n_p, n_s = hf_p.shape[0], hf_s.shape[0]
    assert n_p % tm == 0 and n_s <= tm
    n_tiles_p = n_p // tm
    dest3 = jnp.concatenate([dest_p.reshape(n_tiles_p, tm * TOP_K),
                             jnp.pad(dest_s.reshape(1, n_s * TOP_K), ((0, 0), (0, (tm - n_s) * TOP_K)))])
    dest3 = dest3.reshape(n_tiles_p + 1, 1, tm * TOP_K)
    return pl.pallas_call(
        functools.partial(_dispatch_kernel, n_tiles_p=n_tiles_p),
        grid=(n_tiles_p + 1,),
        in_specs=[pl.BlockSpec((1, 1, tm * TOP_K), lambda i: (i, 0, 0), memory_space=pltpu.SMEM),
                  pl.BlockSpec((tm, *tile), lambda i: (jnp.minimum(i, n_tiles_p - 1), 0, 0)),
                  pl.BlockSpec((n_s, *tile), lambda i: (0, 0, 0))],
        out_specs=pl.BlockSpec(memory_space=pl.ANY),
        out_shape=jax.ShapeDtypeStruct(((n_p + n_s) * TOP_K, *tile), F32),
        scratch_shapes=[pltpu.SemaphoreType.DMA(())],
        compiler_params=pltpu.CompilerParams(dimension_semantics=("arbitrary",)),
        name="dispatch",
    )(dest3, hf_p, hf_s)


def _moe_kernel(vt_ref, ve_ref, nx_ref, lo_ref, hi_ref, x_ref, w1_ref, b1_ref, w2_ref, b2_ref, y_ref,
                w1s, w2s, w1b, w2b, sem, *, d_ff):
    v = pl.program_id(0)
    lo, hi = lo_ref[v], hi_ref[v]
    prev = jnp.maximum(v - 1, 0)
    first = jnp.logical_or(v == 0, vt_ref[v] != vt_ref[prev])

    def fetch(e):
        return (pltpu.make_async_copy(w1_ref.at[e], w1s, sem.at[0]),
                pltpu.make_async_copy(w2_ref.at[e], w2s, sem.at[1]))

    @pl.when(v == 0)
    def _():
        for cp in fetch(ve_ref[0]):
            cp.start()

    @pl.when(jnp.logical_or(v == 0, ve_ref[v] != ve_ref[prev]))
    def _():
        for cp in fetch(ve_ref[v]):
            cp.wait()
        w1b[...] = w1s[...].astype(jnp.bfloat16)
        w2b[...] = w2s[...].astype(jnp.bfloat16)

        @pl.when(nx_ref[v] >= 0)
        def _():
            for cp in fetch(nx_ref[v]):
                cp.start()

    @pl.when(hi > lo)
    def _():
        x = _from_token_tiles(x_ref).astype(jnp.bfloat16)
        a = jnp.dot(x, w1b[...], preferred_element_type=F32) + b1_ref[0]
        x_glu = jnp.minimum(a[:, :d_ff], SWIGLU_LIMIT)
        x_lin = jnp.clip(a[:, d_ff:], -SWIGLU_LIMIT, SWIGLU_LIMIT)
        mid = x_glu * jax.nn.sigmoid(SWIGLU_ALPHA * x_glu) * (x_lin + 1.0)
        y = jnp.dot(mid.astype(jnp.bfloat16), w2b[...], preferred_element_type=F32) + b2_ref[0]
        rows = lax.broadcasted_iota(I32, (y.shape[0], 1), 0)
        mine = (rows >= lo) & (rows < hi)

        @pl.when(first)
        def _():
            _to_token_tiles(y_ref, jnp.where(mine, y, 0.0))

        @pl.when(jnp.logical_not(first))
        def _():
            _to_token_tiles(y_ref, jnp.where(mine, y, _from_token_tiles(y_ref)))


def _moe_experts(xs, visits, w1, b1, w2, b2):
    ne, d, ff2 = w1.shape
    d_ff = ff2 // 2
    n_visits = visits[0].shape[0]
    row = lambda v, vt, ve, nx, lo, hi: (vt[v], 0)
    exp = lambda v, vt, ve, nx, lo, hi: (ve[v], 0, 0)
    vmem = (4 * 2 * 2 * MOE_TILE * d + (4 + 2) * (d * ff2 + d_ff * d) + 2 * 4 * (ff2 + d)
            + 4 * 4 * MOE_TILE * ff2)
    tiles = pl.BlockSpec((MOE_TILE * V7X_SUBLANES, V7X_LANES), row)
    grid_spec = pltpu.PrefetchScalarGridSpec(
        num_scalar_prefetch=5,
        grid=(n_visits,),
        in_specs=[tiles,
                  pl.BlockSpec(memory_space=pl.ANY), pl.BlockSpec((1, 1, ff2), exp),
                  pl.BlockSpec(memory_space=pl.ANY), pl.BlockSpec((1, 1, d), exp)],
        out_specs=tiles,
        scratch_shapes=[pltpu.VMEM((d, ff2), F32), pltpu.VMEM((d_ff, d), F32),
                        pltpu.VMEM((d, ff2), jnp.bfloat16), pltpu.VMEM((d_ff, d), jnp.bfloat16),
                        pltpu.SemaphoreType.DMA((2,))],
    )
    return pl.pallas_call(
        functools.partial(_moe_kernel, d_ff=d_ff),
        grid_spec=grid_spec,
        out_shape=jax.ShapeDtypeStruct(xs.shape, F32),
        compiler_params=pltpu.CompilerParams(dimension_semantics=("arbitrary",),
                                             vmem_limit_bytes=_vmem_limit(vmem)),
        name="moe",
    )(*visits, xs, w1, b1.reshape(ne, 1, ff2), w2, b2.reshape(ne, 1, d))


def _moe_visits(sizes, n_rows):
    n_tiles = n_rows // MOE_TILE
    n_visits = n_tiles + N_EXPERTS - 1
    ends = jnp.cumsum(sizes)
    starts = ends - sizes
    first_tile = starts // MOE_TILE
    n_vis = jnp.where(sizes > 0, (ends - 1) // MOE_TILE - first_tile + 1, 0)
    vis_end = jnp.cumsum(n_vis)
    total = vis_end[-1]
    v = jnp.clip(jnp.arange(n_visits, dtype=I32), 0, jnp.maximum(total - 1, 0))
    e = jnp.sum((vis_end[None, :] <= v[:, None]).astype(I32), axis=1)
    pick = lambda tab: jnp.sum(jnp.where(e[:, None] == jnp.arange(N_EXPERTS)[None, :], tab[None, :], 0), axis=1)
    tile = pick(first_tile) + v - pick(vis_end - n_vis)
    lo = jnp.maximum(pick(starts), tile * MOE_TILE) - tile * MOE_TILE
    hi = jnp.minimum(pick(ends), (tile + 1) * MOE_TILE) - tile * MOE_TILE
    live = jnp.arange(n_visits) < total
    ids = jnp.arange(N_EXPERTS, dtype=I32)
    later = jnp.where((ids[None, :] > ids[:, None]) & (sizes[None, :] > 0), ids[None, :], N_EXPERTS)
    nxt = jnp.min(later, axis=1)
    nxt = pick(jnp.where(nxt < N_EXPERTS, nxt, -1))
    return (tile.astype(I32), e.astype(I32), nxt.astype(I32), jnp.where(live, lo, 0).astype(I32),
            jnp.where(live, hi, 0).astype(I32))


def _final_kernel(dest_ref, nxt_ref, x_ref, tg_ref, g_ref, ys_ref, y_ref, buf, sem, *, tm):
    i = pl.program_id(0)
    n = pl.num_programs(0)

    def gather(d_ref, slot):
        def tokens(i8, carry):
            for u in range(V7X_SUBLANES):
                r = i8 * V7X_SUBLANES + u
                tile = pl.ds(pl.multiple_of(r * V7X_SUBLANES, V7X_SUBLANES), V7X_SUBLANES)
                for k in range(TOP_K):
                    pltpu.make_async_copy(ys_ref.at[d_ref[0, 0, r * TOP_K + k]], buf.at[slot, k, tile, :],
                                          sem.at[slot]).start(priority=k % 2)
            return carry
        lax.fori_loop(0, tm // V7X_SUBLANES, tokens, 0)

    @pl.when(i == 0)
    def _():
        gather(dest_ref, 0)

    for slot in range(2):
        @pl.when(jnp.logical_and(i + 1 < n, (i + 1) % 2 == slot))
        def _(slot=slot):
            gather(nxt_ref, slot)

    for slot in range(2):
        @pl.when(i % 2 == slot)
        def _(slot=slot):
            pltpu.make_async_copy(buf.at[slot], buf.at[slot], sem.at[slot]).wait()
            tg = tg_ref[...]
            parts = []
            for j in range(V7X_SUBLANES):
                f = None
                for k in range(TOP_K):
                    rows = buf[slot, k, pl.ds(j, tm, stride=V7X_SUBLANES), :]
                    f = tg[:, k:k + 1] * rows if f is None else f + tg[:, k:k + 1] * rows
                parts.append(f)
            y_ref[...] = _rms(x_ref[...] + jnp.concatenate(parts, axis=1), g_ref[...])


def _final(x1, tg, dest, ys, g, tm):
    n, d = x1.shape
    n_tiles = n // tm
    tile = (V7X_SUBLANES, V7X_LANES)
    dest3 = dest.reshape(n_tiles, 1, tm * TOP_K)
    row = lambda i: (i, 0)
    smem = lambda index_map: pl.BlockSpec((1, 1, tm * TOP_K), index_map, memory_space=pltpu.SMEM)
    vmem = 4 * (2 * TOP_K * tm * d + 2 * 2 * tm * d + 2 * tm * V7X_LANES + 4 * tm * d)
    return pl.pallas_call(
        functools.partial(_final_kernel, tm=tm),
        grid=(n_tiles,),
        in_specs=[smem(lambda i: (i, 0, 0)), smem(lambda i: (jnp.minimum(i + 1, n_tiles - 1), 0, 0)),
                  pl.BlockSpec((tm, d), row), pl.BlockSpec((tm, V7X_LANES), row),
                  pl.BlockSpec((1, d), lambda i: (0, 0)), pl.BlockSpec(memory_space=pl.ANY)],
        out_specs=pl.BlockSpec((tm, d), row),
        out_shape=jax.ShapeDtypeStruct((n, d), F32),
        scratch_shapes=[pltpu.VMEM((2, TOP_K, tm * V7X_SUBLANES, V7X_LANES), F32), pltpu.SemaphoreType.DMA((2,))],
        compiler_params=pltpu.CompilerParams(dimension_semantics=("arbitrary",),
                                             vmem_limit_bytes=_vmem_limit(vmem)),
        name="final",
    )(dest3, dest3, x1, tg, g.reshape(1, d), ys.reshape(-1, *tile))


def kernel(x_prompt, x_sample, cache_k, cache_v, state_ssm_re, state_ssm_im, norm_mix_g, w_in, attn_out_norm_g, ssm_a_re, ssm_a_im, ssm_log_dt, ssm_b_re, ssm_b_im, ssm_c_re, ssm_c_im, ssm_d, w_glu, b_glu, ssm_out_norm_g, w_out, norm_ffn_g, w_router, b_router, w_moe1, b_moe1, w_moe2, b_moe2, norm_final_g):
    depth = w_in.shape[0]
    assert depth == 1, "single-layer trunk"
    bp, s, d = x_prompt.shape
    bs, t, _ = x_sample.shape
    n_buf = cache_k.shape[2]
    n_heads, head_dim = cache_k.shape[3], cache_k.shape[4]
    assert head_dim == HEAD_DIM
    a = n_heads * head_dim
    g = ssm_a_re.shape[1]
    gn = g * SSM_STATE
    assert s % (max(dd for _, dd in DILATED_GROUPS) * KEYS_BACK) == 0
    np_, ns = bp * s, bs * t
    tm_p, tm_s = 256, ns
    l = 0
    ssm_p = (ssm_a_re[l], ssm_a_im[l], ssm_log_dt[l], ssm_b_re[l], ssm_b_im[l], ssm_c_re[l], ssm_c_im[l],
             ssm_d[l])
    tail = dict(attn_out_norm_g=attn_out_norm_g[l], w_glu=w_glu[l], b_glu=b_glu[l],
                ssm_out_norm_g=ssm_out_norm_g[l], w_out=w_out[l], norm_ffn_g=norm_ffn_g[l],
                w_router=w_router[l], b_router=b_router[l])

    xp = x_prompt.reshape(np_, d)
    qp, kp, vp, up = _proj(xp, norm_mix_g[l], w_in[l], a, tm_p)
    attn_p = _attn_prompt(qp.reshape(bp, s, a), kp.reshape(bp, s, a), vp.reshape(bp, s, a))
    zeros = jnp.zeros((bp, gn), F32)
    tables = _ssm_tables(*ssm_p, PROMPT_CHUNK)
    y_p, hp_re, hp_im = _ssm_chunks_on_lanes(up.reshape(bp, s, -1), zeros, zeros, _ssm_tables_lanes(tables),
                                             PROMPT_CHUNK)
    seen = jnp.zeros((1, V7X_LANES), F32)
    x1_p, hf_p, ti_p, tg_p, seen = _finish(xp, attn_p.reshape(np_, a), y_p.reshape(np_, -1), tail, seen, tm_p)

    xs = x_sample.reshape(ns, d)
    qs, ks, vs, us = _proj(xs, norm_mix_g[l], w_in[l], a, tm_s)
    by_dim = lambda c: c.transpose(0, 2, 3, 1).reshape(bs, a, n_buf)
    attn_s = _attn_sample(qs.reshape(bs, t, a), ks.reshape(bs, t, a), vs.reshape(bs, t, a),
                          by_dim(cache_k[l]), by_dim(cache_v[l]))
    y_s, hs_re, hs_im = _ssm(us.reshape(bs, t, -1), state_ssm_re[l].reshape(bs, gn),
                             state_ssm_im[l].reshape(bs, gn), _ssm_tables_rows(tables, t), t)
    x1_s, hf_s, ti_s, tg_s, seen = _finish(xs, attn_s.reshape(ns, a), y_s.reshape(ns, -1), tail, seen, tm_s)

    sizes = seen[0, :N_EXPERTS].astype(I32)
    starts = jnp.cumsum(sizes) - sizes
    experts = jnp.arange(N_EXPERTS, dtype=I32)[None, None, :]

    def sorted_row(ti):
        hit = ti[:, :TOP_K, None] == experts
        return jnp.sum(jnp.where(hit, starts[None, None, :], 0), axis=-1) + ti[:, TOP_K:2 * TOP_K]

    dest_p, dest_s = sorted_row(ti_p), sorted_row(ti_s)
    n_rows = (np_ + ns) * TOP_K
    assert n_rows % MOE_TILE == 0
    xs_sorted = _dispatch(hf_p, dest_p, hf_s, dest_s, tm_p).reshape(-1, V7X_LANES)
    ys = _moe_experts(xs_sorted, _moe_visits(sizes, n_rows), w_moe1[l], b_moe1[l], w_moe2[l], b_moe2[l])
    y_prompt = _final(x1_p, tg_p, dest_p, ys, norm_final_g, tm_p).reshape(bp, s, d)
    y_sample = _final(x1_s, tg_s, dest_s, ys, norm_final_g, tm_s).reshape(bs, t, d)

    keep = min(max(w for w, _ in DILATED_GROUPS), s)
    k5 = lambda z, b_, s_: z.reshape(1, b_, s_, n_heads, head_dim)
    st = lambda z, b_: z.reshape(1, b_, g, SSM_STATE)
    return (y_prompt, y_sample,
            k5(kp, bp, s)[:, :, s - keep:], k5(vp, bp, s)[:, :, s - keep:], st(hp_re, bp), st(hp_im, bp),
            k5(ks, bs, t), k5(vs, bs, t), st(hs_re, bs), st(hs_im, bs))
```

```python
import functools
import math

import numpy as np
import jax
import jax.numpy as jnp
from jax import lax
from jax.experimental import pallas as pl
from jax.experimental.pallas import tpu as pltpu

F32 = jnp.float32
I32 = jnp.int32

V7X_LANES = 128
V7X_SUBLANES = 8
V7X_VMEM_BYTES = 64 * 1024 * 1024

HEAD_DIM = 64
HEADS_PER_LANE_TILE = V7X_LANES // HEAD_DIM
DILATED_GROUPS = ((128, 1), (512, 4), (2048, 16))
KEYS_BACK = 128
SSM_GROUP = 16
SSM_STATE = 64
N_EXPERTS = 32
TOP_K = 4
SWIGLU_LIMIT = 7.0
SWIGLU_ALPHA = 1.702
RMS_EPS = 1e-5
NEG_INF = -1e30
PROMPT_CHUNK = 16
MOE_TILE = 512
MOE_SUB = 256


def _vmem_limit(nbytes):
    return int(min(nbytes + (8 << 20), V7X_VMEM_BYTES - (8 << 20)))


def _rms(x, g):
    return x * lax.rsqrt(jnp.mean(x * x, axis=-1, keepdims=True) + RMS_EPS) * g


def _to_token_tiles(ref, val):
    rows, width = val.shape
    assert width == V7X_SUBLANES * V7X_LANES and ref.shape == (rows * V7X_SUBLANES, V7X_LANES)
    for j in range(V7X_SUBLANES):
        ref[pl.ds(j, rows, stride=V7X_SUBLANES), :] = val[:, j * V7X_LANES:(j + 1) * V7X_LANES]


def _from_token_tiles(ref):
    rows = ref.shape[0] // V7X_SUBLANES
    return jnp.concatenate([ref[pl.ds(j, rows, stride=V7X_SUBLANES), :] for j in range(V7X_SUBLANES)], axis=1)


def _proj_kernel(x_ref, g_ref, w_ref, q_ref, k_ref, v_ref, u_ref, *, attn_width, q_scale):
    h = _rms(x_ref[...], g_ref[...])
    z = jnp.dot(h, w_ref[...], preferred_element_type=F32)
    a = attn_width
    q_ref[...] = z[:, :a] * q_scale
    k_ref[...] = z[:, a:2 * a]
    v_ref[...] = z[:, 2 * a:3 * a]
    u_ref[...] = z[:, 3 * a:]


def _proj(x2d, norm_g, w_in, attn_width, tm):
    n, d = x2d.shape
    mix = w_in.shape[1]
    ssm_width = mix - 3 * attn_width
    row = lambda i: (i, 0)
    const = lambda i: (0, 0)
    out_shape = (jax.ShapeDtypeStruct((n, attn_width), F32),) * 3 + (
        jax.ShapeDtypeStruct((n, ssm_width), F32),)
    vmem = 2 * 4 * (tm * d + d * mix + tm * mix) + 4 * tm * mix
    return pl.pallas_call(
        functools.partial(_proj_kernel, attn_width=attn_width, q_scale=HEAD_DIM ** -0.5),
        grid=(n // tm,),
        in_specs=[pl.BlockSpec((tm, d), row), pl.BlockSpec((1, d), const),
                  pl.BlockSpec((d, mix), const)],
        out_specs=(pl.BlockSpec((tm, attn_width), row),) * 3 + (pl.BlockSpec((tm, ssm_width), row),),
        out_shape=out_shape,
        compiler_params=pltpu.CompilerParams(dimension_semantics=("arbitrary",),
                                             vmem_limit_bytes=_vmem_limit(vmem)),
        name="proj",
    )(x2d, norm_g.reshape(1, d), w_in)


def _attn_prompt_kernel(q_ref, k_ref, v_ref, o_ref, *scratch, seq, unroll):
    nb = KEYS_BACK
    n_groups = len(DILATED_GROUPS)
    m_sc, l_sc, acc_sc = scratch[:n_groups], scratch[n_groups:2 * n_groups], scratch[2 * n_groups:]
    lane = lax.broadcasted_iota(I32, (1, V7X_LANES), 1)
    head_a = lane < HEAD_DIM
    qi = lax.broadcasted_iota(I32, (nb, 2 * nb), 0)
    kj = lax.broadcasted_iota(I32, (nb, 2 * nb), 1)
    rel = nb + qi - kj
    band = (rel >= 0) & (rel <= nb)

    def rows(ref, start, d):
        if d == 1:
            return ref[pl.ds(pl.multiple_of(start, nb), nb), :]
        return ref[pl.ds(start, nb, stride=d), :]

    def put(ref, start, d, val):
        if d == 1:
            ref[pl.ds(pl.multiple_of(start, nb), nb), :] = val
        else:
            ref[pl.ds(start, nb, stride=d), :] = val

    for gi, (w, d) in enumerate(DILATED_GROUPS):
        assert w // d == nb
        nblk = seq // (d * nb)

        def block(t, gi=gi, d=d, nblk=nblk):
            r = t // nblk
            n = t % nblk
            cur = r + d * nb * n
            prev = r + d * nb * jnp.maximum(n - 1, 0)
            q = rows(q_ref, cur, d)
            kk = jnp.concatenate([rows(k_ref, prev, d), rows(k_ref, cur, d)], axis=0)
            vv = jnp.concatenate([rows(v_ref, prev, d), rows(v_ref, cur, d)], axis=0)
            qs = jnp.concatenate([jnp.where(head_a, q, 0.0), jnp.where(head_a, 0.0, q)], axis=0)
            s = lax.dot_general(qs, kk, (((1,), (1,)), ((), ())), preferred_element_type=F32)
            valid = band & (kj >= jnp.where(n > 0, 0, nb))
            s = jnp.where(jnp.concatenate([valid, valid], axis=0), s, NEG_INF)
            m = jnp.max(s, axis=-1, keepdims=True)
            p = jnp.exp(s - m)
            l = jnp.sum(p, axis=-1, keepdims=True)
            o = jnp.where(head_a,
                          jnp.dot(p[:nb], vv, preferred_element_type=F32),
                          jnp.dot(p[nb:], vv, preferred_element_type=F32))
            put(m_sc[gi], cur, d, jnp.where(head_a, m[:nb], m[nb:]))
            put(l_sc[gi], cur, d, jnp.where(head_a, l[:nb], l[nb:]))
            put(acc_sc[gi], cur, d, o)

        def blocks(tt, carry, block=block):
            for u in range(unroll):
                block(tt * unroll + u)
            return carry

        lax.fori_loop(0, seq // nb // unroll, blocks, 0)

    def merge(n, carry):
        sl = pl.ds(pl.multiple_of(n * nb, nb), nb)
        ms = [m[sl, :] for m in m_sc]
        m_all = functools.reduce(jnp.maximum, ms)
        cs = [jnp.exp(m - m_all) for m in ms]
        den = functools.reduce(jnp.add, [c * l[sl, :] for c, l in zip(cs, l_sc)])
        num = functools.reduce(jnp.add, [c * acc[sl, :] for c, acc in zip(cs, acc_sc)])
        o_ref[sl, :] = num / den
        return carry

    lax.fori_loop(0, seq // nb, merge, 0, unroll=2)


def _attn_prompt(q, k, v, unroll=8):
    b, s, a = q.shape
    n_scratch = 3 * len(DILATED_GROUPS)
    spec = pl.BlockSpec((None, s, V7X_LANES), lambda i, j: (i, 0, j))
    vmem = 4 * s * V7X_LANES * (2 * 4 + n_scratch)
    return pl.pallas_call(
        functools.partial(_attn_prompt_kernel, seq=s, unroll=unroll),
        grid=(b, a // V7X_LANES),
        in_specs=[spec, spec, spec],
        out_specs=spec,
        out_shape=jax.ShapeDtypeStruct((b, s, a), F32),
        scratch_shapes=[pltpu.VMEM((s, V7X_LANES), F32)] * n_scratch,
        compiler_params=pltpu.CompilerParams(dimension_semantics=("arbitrary", "arbitrary"),
                                             vmem_limit_bytes=_vmem_limit(vmem)),
        name="attn_prompt",
    )(q, k, v)


def _sample_key_counts(n_buf, n_new):
    cnt = np.zeros((n_new, n_buf + n_new), np.float32)
    for w, d in DILATED_GROUPS:
        for i in range(n_new):
            for j in range(w // d + 1):
                idx = n_buf + i - d * j
                if idx >= 0:
                    cnt[i, idx] += 1.0
    return cnt[:, :n_buf], cnt[:, n_buf:]


def _attn_sample_kernel(q_ref, kn_ref, vn_ref, kt_ref, vt_ref, cnt_ref, o_ref, *, n_heads, cnt_new):
    n_new, a = q_ref.shape
    rows = n_new * n_heads
    lane_head = lax.broadcasted_iota(I32, (rows, a), 1) // HEAD_DIM
    row_head = lax.broadcasted_iota(I32, (rows, a), 0) % n_heads
    row_query = lax.broadcasted_iota(I32, (rows, 1), 0) // n_heads
    own = lane_head == row_head
    q = q_ref[...]
    qb = jnp.where(own, jnp.broadcast_to(q[:, None, :], (n_new, n_heads, a)).reshape(rows, a), 0.0)
    s_buf = jnp.dot(qb, kt_ref[...], preferred_element_type=F32)
    cnt = cnt_ref[...]
    cnt = jnp.broadcast_to(cnt[:, None, :], (n_new, n_heads, cnt.shape[-1])).reshape(rows, -1)
    valid = cnt > 0.0
    m = jnp.max(jnp.where(valid, s_buf, NEG_INF), axis=-1, keepdims=True)
    s_new, c_new = [], []
    for j in range(n_new):
        c = functools.reduce(lambda acc, i: jnp.where(row_query == i, float(cnt_new[i, j]), acc),
                             range(n_new), jnp.zeros((rows, 1), F32))
        s = jnp.sum(qb * kn_ref[pl.ds(j, 1), :], axis=-1, keepdims=True)
        m = jnp.maximum(m, jnp.where(c > 0.0, s, NEG_INF))
        s_new.append(s)
        c_new.append(c)
    p = jnp.where(valid, cnt * jnp.exp(s_buf - m), 0.0)
    den = jnp.sum(p, axis=-1, keepdims=True)
    o = lax.dot_general(p, vt_ref[...], (((1,), (1,)), ((), ())), preferred_element_type=F32)
    for j in range(n_new):
        pj = jnp.where(c_new[j] > 0.0, c_new[j] * jnp.exp(s_new[j] - m), 0.0)
        den = den + pj
        o = o + pj * vn_ref[pl.ds(j, 1), :]
    o = jnp.where(own, o / den, 0.0)
    o_ref[...] = jnp.sum(o.reshape(n_new, n_heads, a), axis=1)


def _attn_sample(q, k_new, v_new, cache_kt, cache_vt):
    b, t, a = q.shape
    n_buf = cache_kt.shape[-1]
    n_heads = a // HEAD_DIM
    cnt_buf, cnt_new = _sample_key_counts(n_buf, t)
    new = pl.BlockSpec((None, t, a), lambda i: (i, 0, 0))
    buf = pl.BlockSpec((None, a, n_buf), lambda i: (i, 0, 0))
    vmem = 4 * (2 * 2 * n_buf * a + 8 * t * n_heads * n_buf)
    return pl.pallas_call(
        functools.partial(_attn_sample_kernel, n_heads=n_heads, cnt_new=cnt_new),
        grid=(b,),
        in_specs=[new, new, new, buf, buf, pl.BlockSpec((t, n_buf), lambda i: (0, 0))],
        out_specs=new,
        out_shape=jax.ShapeDtypeStruct((b, t, a), F32),
        compiler_params=pltpu.CompilerParams(dimension_semantics=("arbitrary",),
                                             vmem_limit_bytes=_vmem_limit(vmem)),
        name="attn_sample",
    )(q, k_new, v_new, cache_kt, cache_vt, jnp.asarray(cnt_buf))


def _block_toeplitz_kernel(k_ref, toep_ref, tt_ref, *, chunk):
    c = k_ref.shape[1]
    for i in range(k_ref.shape[0]):
        k = k_ref[i]
        for s in range(chunk):
            shifted = k if s == 0 else jnp.concatenate(
                [jnp.zeros((c, s * c), F32), k[:, :(chunk - s) * c]], axis=1)
            toep_ref[i, pl.ds(s * c, c), :] = shifted
        tt_ref[i] = toep_ref[i].T


def _block_toeplitz(kern, chunk):
    g, c, lc = kern.shape
    per_step = 4
    spec = pl.BlockSpec((per_step, lc, lc), lambda i: (i, 0, 0))
    return pl.pallas_call(
        functools.partial(_block_toeplitz_kernel, chunk=chunk),
        grid=(g // per_step,),
        in_specs=[pl.BlockSpec((per_step, c, lc), lambda i: (i, 0, 0))],
        out_specs=(spec, spec),
        out_shape=(jax.ShapeDtypeStruct((g, lc, lc), F32),) * 2,
        compiler_params=pltpu.CompilerParams(dimension_semantics=("arbitrary",)),
        name="ssm_toeplitz",
    )(kern)


def _pad_states(m, g):
    half = (np.arange(g) % 2)[:, None, None]
    z = jnp.zeros_like(m)
    return jnp.where(half == 0, jnp.concatenate([m, z], -1), jnp.concatenate([z, m], -1))


def _ssm_tables(a_re, a_im, log_dt, b_re, b_im, c_re, c_im, d_skip, chunk):
    g, n = a_re.shape
    c = b_re.shape[-1]
    lc = chunk * c
    dt = jnp.exp(log_dt)[:, None]
    x, y = a_re * dt, a_im * dt
    ex = jnp.exp(x)
    ar, ai = ex * jnp.cos(y), ex * jnp.sin(y)
    nr = jnp.expm1(x) * jnp.cos(y) - 2.0 * jnp.sin(0.5 * y) ** 2
    ni = ai
    den = a_re * a_re + a_im * a_im
    fr = (nr * a_re + ni * a_im) / den
    fi = (ni * a_re - nr * a_im) / den
    b_re_t, b_im_t = b_re.transpose(0, 2, 1), b_im.transpose(0, 2, 1)
    bbr = fr[:, None, :] * b_re_t - fi[:, None, :] * b_im_t
    bbi = fr[:, None, :] * b_im_t + fi[:, None, :] * b_re_t
    pr, pi = [jnp.ones_like(ar)], [jnp.zeros_like(ar)]
    for _ in range(chunk):
        pr.append(pr[-1] * ar - pi[-1] * ai)
        pi.append(pr[-2] * ai + pi[-1] * ar)
    pr, pi = jnp.stack(pr, 1), jnp.stack(pi, 1)
    cpr = c_re[:, None] * pr[:, :, None, :] - c_im[:, None] * pi[:, :, None, :]
    cpi = c_re[:, None] * pi[:, :, None, :] + c_im[:, None] * pr[:, :, None, :]
    kern = jnp.sum(cpr[:, None, :chunk] * bbr[:, :, None, None] - cpi[:, None, :chunk] * bbi[:, :, None, None],
                   axis=-1).reshape(g, c, lc)
    toep, tt = _block_toeplitz(kern, chunk)
    qr, qi = pr[:, chunk - 1::-1][:, :, None, :], pi[:, chunk - 1::-1][:, :, None, :]
    wr = (qr * bbr[:, None] - qi * bbi[:, None]).reshape(g, lc, n)
    wi = (qr * bbi[:, None] + qi * bbr[:, None]).reshape(g, lc, n)
    vr = cpr[:, 1:].reshape(g, lc, n)
    vi = -cpi[:, 1:].reshape(g, lc, n)
    return dict(chunk=chunk, toep=toep, tt=tt, wr=wr, wi=wi, vr=vr, vi=vi, d=d_skip, pr=pr, pi=pi)


def _ssm_tables_lanes(tab):
    g, chunk = tab["d"].shape[0], tab["chunk"]
    tr = lambda m: m.transpose(0, 2, 1)
    return dict(tt=tab["tt"], wt_re=tr(_pad_states(tab["wr"], g)), wt_im=tr(_pad_states(tab["wi"], g)),
                v_re=_pad_states(tab["vr"], g), v_im=_pad_states(tab["vi"], g),
                dcol=jnp.tile(tab["d"], (1, chunk))[:, :, None],
                al_re=tab["pr"][:, chunk].reshape(1, -1), al_im=tab["pi"][:, chunk].reshape(1, -1))


def _ssm_tables_rows(tab, chunk):
    g, c = tab["d"].shape
    big, lc = tab["chunk"], chunk * c
    tr = lambda m: m.transpose(0, 2, 1)
    tail = slice((big - chunk) * c, big * c)
    return dict(toep=tab["toep"][:, :lc, :lc],
                wr=_pad_states(tab["wr"][:, tail], g), wi=_pad_states(tab["wi"][:, tail], g),
                vr=tr(_pad_states(tab["vr"][:, :lc], g)), vi=tr(_pad_states(tab["vi"][:, :lc], g)),
                d=jnp.tile(tab["d"], (1, chunk))[:, None, :],
                al_re=tab["pr"][:, chunk].reshape(1, -1), al_im=tab["pi"][:, chunk].reshape(1, -1))


def _ssm_state_kernel(u_ref, wr_ref, wi_ref, er_ref, ei_ref):
    er = jnp.dot(u_ref[0], wr_ref[0], preferred_element_type=F32)
    ei = jnp.dot(u_ref[0], wi_ref[0], preferred_element_type=F32)
    er_ref[...] = er + jnp.dot(u_ref[1], wr_ref[1], preferred_element_type=F32)
    ei_ref[...] = ei + jnp.dot(u_ref[1], wi_ref[1], preferred_element_type=F32)


def _ssm_scan_kernel(er_ref, ei_ref, ar_ref, ai_ref, hr_ref, hi_ref, xr_ref, xi_ref, lr_ref, li_ref,
                     *, n_chunks, nb):
    shape = hr_ref.shape
    ar = jnp.broadcast_to(ar_ref[...], shape)
    ai = jnp.broadcast_to(ai_ref[...], shape)

    def step(j, carry):
        xr, xi = carry
        sl = pl.ds(pl.multiple_of(j * nb, nb), nb)
        xr_ref[sl, :] = xr
        xi_ref[sl, :] = xi
        return (ar * xr - ai * xi + er_ref[sl, :], ar * xi + ai * xr + ei_ref[sl, :])

    xr, xi = lax.fori_loop(0, n_chunks, step, (hr_ref[...], hi_ref[...]))
    lr_ref[...] = xr
    li_ref[...] = xi


def _ssm_out_kernel(u_ref, t_ref, d_ref, xr_ref, xi_ref, vr_ref, vi_ref, y_ref):
    xr = xr_ref[...]
    xi = xi_ref[...]
    for a in range(2):
        u = u_ref[a]
        y = jnp.dot(u, t_ref[a], preferred_element_type=F32) + d_ref[a] * u
        y = y + jnp.dot(xr, vr_ref[a], preferred_element_type=F32)
        y_ref[a] = y + jnp.dot(xi, vi_ref[a], preferred_element_type=F32)


def _ssm_scan(er, ei, tab, h0_re, h0_im, n_chunks):
    r, gn = er.shape
    b = r // n_chunks
    lanes = 4 * V7X_LANES
    col = lambda i: (0, i)
    return pl.pallas_call(
        functools.partial(_ssm_scan_kernel, n_chunks=n_chunks, nb=b),
        grid=(gn // lanes,),
        in_specs=[pl.BlockSpec((r, lanes), col), pl.BlockSpec((r, lanes), col),
                  pl.BlockSpec((1, lanes), col), pl.BlockSpec((1, lanes), col),
                  pl.BlockSpec((b, lanes), col), pl.BlockSpec((b, lanes), col)],
        out_specs=(pl.BlockSpec((r, lanes), col),) * 2 + (pl.BlockSpec((b, lanes), col),) * 2,
        out_shape=(jax.ShapeDtypeStruct((r, gn), F32),) * 2 + (jax.ShapeDtypeStruct((b, gn), F32),) * 2,
        compiler_params=pltpu.CompilerParams(dimension_semantics=("arbitrary",),
                                             vmem_limit_bytes=_vmem_limit(2 * 4 * 4 * r * lanes)),
        name="ssm_scan",
    )(er, ei, tab["al_re"], tab["al_im"], h0_re, h0_im)


def _ssm_state_t_kernel(a_ref, wr_ref, wi_ref, er_ref, ei_ref):
    def seq(b, carry):
        a0, a1 = a_ref[0, b], a_ref[1, b]
        er_ref[b] = (jnp.dot(wr_ref[0], a0, preferred_element_type=F32)
                     + jnp.dot(wr_ref[1], a1, preferred_element_type=F32))
        ei_ref[b] = (jnp.dot(wi_ref[0], a0, preferred_element_type=F32)
                     + jnp.dot(wi_ref[1], a1, preferred_element_type=F32))
        return carry

    lax.fori_loop(0, a_ref.shape[1], seq, 0)


def _ssm_out_t_kernel(a_ref, t_ref, d_ref, xr_ref, xi_ref, vr_ref, vi_ref, y_ref):
    def seq(b, carry):
        xr, xi = xr_ref[b], xi_ref[b]
        for a in range(2):
            at = a_ref[a, b]
            y = jnp.dot(t_ref[a], at, preferred_element_type=F32) + d_ref[a] * at
            y = y + jnp.dot(vr_ref[a], xr, preferred_element_type=F32)
            y_ref[a, b] = y + jnp.dot(vi_ref[a], xi, preferred_element_type=F32)
        return carry

    lax.fori_loop(0, a_ref.shape[1], seq, 0)


def _ssm_chunks_on_lanes(u, h0_re, h0_im, tab, chunk):
    b, s, width = u.shape
    g = width // SSM_GROUP
    n_chunks = s // chunk
    lc = chunk * SSM_GROUP
    assert n_chunks == V7X_LANES
    at = (u.reshape(b, n_chunks, chunk * width).transpose(0, 2, 1)
          .reshape(b, chunk, g, SSM_GROUP, n_chunks).transpose(2, 0, 1, 3, 4).reshape(g, b, lc, n_chunks))
    pair4 = lambda i: (i, 0, 0, 0)
    pair3 = lambda i: (i, 0, 0)
    cp = lambda vmem: pltpu.CompilerParams(dimension_semantics=("arbitrary",),
                                           vmem_limit_bytes=_vmem_limit(vmem))
    a_spec = pl.BlockSpec((2, b, lc, n_chunks), pair4)
    st_spec = pl.BlockSpec((None, b, V7X_LANES, n_chunks), pair4)
    w_spec = pl.BlockSpec((2, V7X_LANES, lc), pair3)
    a_bytes = 2 * b * lc * n_chunks
    st_bytes = b * V7X_LANES * n_chunks
    et_re, et_im = pl.pallas_call(
        _ssm_state_t_kernel,
        grid=(g // 2,),
        in_specs=[a_spec, w_spec, w_spec],
        out_specs=(st_spec,) * 2,
        out_shape=(jax.ShapeDtypeStruct((g // 2, b, V7X_LANES, n_chunks), F32),) * 2,
        compiler_params=cp(2 * 4 * (a_bytes + 2 * st_bytes + 4 * V7X_LANES * lc)),
        name="ssm_state_t",
    )(at, tab["wt_re"], tab["wt_im"])
    rows = lambda m: m.transpose(3, 1, 0, 2).reshape(n_chunks * b, g * SSM_STATE)
    xr, xi, lr, li = _ssm_scan(rows(et_re), rows(et_im), tab, h0_re, h0_im, n_chunks)
    cols = lambda m: m.reshape(n_chunks, b, g // 2, V7X_LANES).transpose(2, 1, 3, 0)
    yt = pl.pallas_call(
        _ssm_out_t_kernel,
        grid=(g // 2,),
        in_specs=[a_spec, pl.BlockSpec((2, lc, lc), pair3), pl.BlockSpec((2, lc, 1), pair3),
                  st_spec, st_spec, pl.BlockSpec((2, lc, V7X_LANES), pair3),
                  pl.BlockSpec((2, lc, V7X_LANES), pair3)],
        out_specs=a_spec,
        out_shape=jax.ShapeDtypeStruct((g, b, lc, n_chunks), F32),
        compiler_params=cp(2 * 4 * (2 * a_bytes + 2 * st_bytes + 2 * lc * lc + 6 * V7X_LANES * lc)),
        name="ssm_out_t",
    )(at, tab["tt"], tab["dcol"], cols(xr), cols(xi), tab["v_re"], tab["v_im"])
    y = (yt.reshape(g, b, chunk, SSM_GROUP, n_chunks).transpose(1, 2, 0, 3, 4)
         .reshape(b, chunk * width, n_chunks).transpose(0, 2, 1).reshape(b, s, width))
    return y, lr, li


def _ssm(u, h0_re, h0_im, tab, chunk):
    b, s, width = u.shape
    g = width // SSM_GROUP
    n_chunks = s // chunk
    lc = chunk * SSM_GROUP
    r = n_chunks * b
    gn = g * SSM_STATE
    ut = u.reshape(b, n_chunks, chunk, g, SSM_GROUP).transpose(3, 1, 0, 2, 4).reshape(g, r, lc)
    pair = lambda i: (i, 0, 0)
    cp = lambda vmem: pltpu.CompilerParams(dimension_semantics=("arbitrary",),
                                           vmem_limit_bytes=_vmem_limit(vmem))
    er, ei = pl.pallas_call(
        _ssm_state_kernel,
        grid=(g // 2,),
        in_specs=[pl.BlockSpec((2, r, lc), pair), pl.BlockSpec((2, lc, V7X_LANES), pair),
                  pl.BlockSpec((2, lc, V7X_LANES), pair)],
        out_specs=(pl.BlockSpec((r, V7X_LANES), lambda i: (0, i)),) * 2,
        out_shape=(jax.ShapeDtypeStruct((r, gn), F32),) * 2,
        compiler_params=cp(2 * 4 * (2 * r * lc + 4 * lc * V7X_LANES + 2 * r * V7X_LANES)),
        name="ssm_state",
    )(ut, tab["wr"], tab["wi"])
    xr, xi, lr, li = _ssm_scan(er, ei, tab, h0_re, h0_im, n_chunks)
    y = pl.pallas_call(
        _ssm_out_kernel,
        grid=(g // 2,),
        in_specs=[pl.BlockSpec((2, r, lc), pair), pl.BlockSpec((2, lc, lc), pair),
                  pl.BlockSpec((2, 1, lc), pair),
                  pl.BlockSpec((r, V7X_LANES), lambda i: (0, i)),
                  pl.BlockSpec((r, V7X_LANES), lambda i: (0, i)),
                  pl.BlockSpec((2, V7X_LANES, lc), pair), pl.BlockSpec((2, V7X_LANES, lc), pair)],
        out_specs=pl.BlockSpec((2, r, lc), pair),
        out_shape=jax.ShapeDtypeStruct((g, r, lc), F32),
        compiler_params=cp(2 * 4 * (4 * r * lc + 2 * lc * lc + 2 * r * V7X_LANES + 4 * V7X_LANES * lc)),
        name="ssm_out",
    )(ut, tab["toep"], tab["d"], xr, xi, tab["vr"], tab["vi"])
    y = y.reshape(g, n_chunks, b, chunk, SSM_GROUP).transpose(2, 1, 3, 0, 4).reshape(b, s, width)
    return y, lr, li


def _finish_kernel(x_ref, attn_ref, ssm_ref, ga_ref, wg_ref, bg_ref, gs_ref, wo_ref, gf_ref,
                   wr_ref, br_ref, base_ref, x1_ref, hf_ref, ti_ref, tg_ref, cnt_ref, seen):
    a = attn_ref.shape[-1]
    tm = x_ref.shape[0]

    @pl.when(pl.program_id(0) == 0)
    def _():
        seen[...] = base_ref[...]

    g = jax.nn.gelu(ssm_ref[...])
    gate = jax.nn.sigmoid(jnp.dot(g, wg_ref[...], preferred_element_type=F32) + bg_ref[...])
    mixed_a = _rms(attn_ref[...], ga_ref[...])
    mixed_s = _rms(g * gate, gs_ref[...])
    x1 = (x_ref[...] + jnp.dot(mixed_a, wo_ref[pl.ds(0, a), :], preferred_element_type=F32)
          + jnp.dot(mixed_s, wo_ref[pl.ds(a, wo_ref.shape[0] - a), :], preferred_element_type=F32))
    x1_ref[...] = x1
    hf = _rms(x1, gf_ref[...])
    _to_token_tiles(hf_ref, hf)
    hf_hi = hf.astype(jnp.bfloat16).astype(F32)
    hf_lo = hf - hf_hi
    small = (jnp.dot(hf_lo, wr_ref[1], preferred_element_type=F32)
             + jnp.dot(hf_hi, wr_ref[1], preferred_element_type=F32)
             + jnp.dot(hf_lo, wr_ref[0], preferred_element_type=F32))
    logits = jnp.dot(hf_hi, wr_ref[0], preferred_element_type=F32) + small + br_ref[...]
    lane = lax.broadcasted_iota(I32, logits.shape, 1)
    work = logits
    vals, idxs = [], []
    for _ in range(TOP_K):
        m = jnp.max(work, axis=-1, keepdims=True)
        idx = jnp.min(jnp.where(work == m, lane, V7X_LANES), axis=-1, keepdims=True)
        vals.append(m)
        idxs.append(idx)
        work = jnp.where(lane == idx, -jnp.inf, work)
    exps = [jnp.exp(v - vals[0]) for v in vals]
    den = exps[0]
    for e in exps[1:]:
        den = den + e
    picked = jnp.zeros(logits.shape, F32)
    for k in range(TOP_K):
        picked = picked + jnp.where(lane == idxs[k], 1.0, 0.0)
    tri = jnp.where(lax.broadcasted_iota(I32, (tm, tm), 0) > lax.broadcasted_iota(I32, (tm, tm), 1), 1.0, 0.0)
    before = jnp.dot(tri, picked, preferred_element_type=F32) + seen[...]
    ti = jnp.zeros(logits.shape, I32)
    tg = jnp.zeros(logits.shape, F32)
    for k in range(TOP_K):
        rank = jnp.sum(jnp.where(lane == idxs[k], before, 0.0), axis=-1, keepdims=True)
        ti = jnp.where(lane == k, idxs[k], ti)
        ti = jnp.where(lane == TOP_K + k, rank.astype(I32), ti)
        tg = jnp.where(lane == k, exps[k] / den, tg)
    ti_ref[...] = ti
    tg_ref[...] = tg
    seen[...] = seen[...] + jnp.sum(picked, axis=0, keepdims=True)
    cnt_ref[...] = seen[...]


def _finish(x2d, attn, ssm_y, p, seen, tm):
    n, d = x2d.shape
    a = attn.shape[1]
    sw = ssm_y.shape[1]
    ne = p["w_router"].shape[1]
    w_router = jnp.pad(p["w_router"], ((0, 0), (0, V7X_LANES - ne)))
    w_hi = w_router.astype(jnp.bfloat16).astype(F32)
    w_router = jnp.stack([w_hi, w_router - w_hi])
    b_router = jnp.pad(p["b_router"], (0, V7X_LANES - ne), constant_values=-jnp.inf).reshape(1, V7X_LANES)
    row = lambda i: (i, 0)
    full = lambda arr: pl.BlockSpec(arr.shape, lambda i: (0,) * arr.ndim)
    ins = [x2d, attn, ssm_y, p["attn_out_norm_g"].reshape(1, a), p["w_glu"], p["b_glu"].reshape(1, sw),
           p["ssm_out_norm_g"].reshape(1, sw), p["w_out"], p["norm_ffn_g"].reshape(1, d),
           w_router, b_router, seen]
    in_specs = [pl.BlockSpec((tm, d), row), pl.BlockSpec((tm, a), row), pl.BlockSpec((tm, sw), row)]
    in_specs += [full(arr) for arr in ins[3:]]
    vmem = 2 * 4 * (tm * (3 * d + a + sw + 2 * V7X_LANES) + sw * sw + d * d + 2 * d * V7X_LANES) + 16 * tm * d
    return pl.pallas_call(
        _finish_kernel,
        grid=(n // tm,),
        in_specs=in_specs,
        out_specs=(pl.BlockSpec((tm, d), row), pl.BlockSpec((tm * V7X_SUBLANES, V7X_LANES), row),
                   pl.BlockSpec((tm, V7X_LANES), row), pl.BlockSpec((tm, V7X_LANES), row),
                   pl.BlockSpec((1, V7X_LANES), lambda i: (0, 0))),
        out_shape=(jax.ShapeDtypeStruct((n, d), F32), jax.ShapeDtypeStruct((n * V7X_SUBLANES, V7X_LANES), F32),
                   jax.ShapeDtypeStruct((n, V7X_LANES), I32), jax.ShapeDtypeStruct((n, V7X_LANES), F32),
                   jax.ShapeDtypeStruct((1, V7X_LANES), F32)),
        scratch_shapes=[pltpu.VMEM((1, V7X_LANES), F32)],
        compiler_params=pltpu.CompilerParams(dimension_semantics=("arbitrary",),
                                             vmem_limit_bytes=_vmem_limit(vmem)),
        name="finish",
    )(*ins)


def _tile_copy(src_ref, src_row, dst_ref, dst_row, sem):
    return pltpu.make_async_copy(src_ref.at[src_row], dst_ref.at[dst_row], sem)


def _dispatch_kernel(dest_ref, hp_ref, hs_ref, xs_ref, sem, *, n_tiles_p):
    def scatter(hf_ref):
        tm = hf_ref.shape[0]

        def tokens(i, carry):
            for u in range(V7X_SUBLANES):
                r = i * V7X_SUBLANES + u
                for k in range(TOP_K):
                    _tile_copy(hf_ref, r, xs_ref, dest_ref[0, 0, r * TOP_K + k], sem).start(priority=k % 2)
            return carry

        lax.fori_loop(0, tm // V7X_SUBLANES, tokens, 0)
        rows = pl.ds(0, tm * TOP_K)
        pltpu.make_async_copy(xs_ref.at[rows], xs_ref.at[rows], sem).wait()

    @pl.when(pl.program_id(0) < n_tiles_p)
    def _():
        scatter(hp_ref)

    @pl.when(pl.program_id(0) == n_tiles_p)
    def _():
        scatter(hs_ref)


def _dispatch(hf_p, dest_p, hf_s, dest_s, tm):
    tile = (V7X_SUBLANES, V7X_LANES)
    hf_p, hf_s = hf_p.reshape(-1, *tile), hf_s.reshape(-1, *tile)
    n_p, n_s = hf_p.shape[0], hf_s.shape[0]
    assert n_p % tm == 0 and n_s <= tm
    n_tiles_p = n_p // tm
    dest3 = jnp.concatenate([dest_p.reshape(n_tiles_p, tm * TOP_K),
                             jnp.pad(dest_s.reshape(1, n_s * TOP_K), ((0, 0), (0, (tm - n_s) * TOP_K)))])
    dest3 = dest3.reshape(n_tiles_p + 1, 1, tm * TOP_K)
    return pl.pallas_call(
        functools.partial(_dispatch_kernel, n_tiles_p=n_tiles_p),
        grid=(n_tiles_p + 1,),
        in_specs=[pl.BlockSpec((1, 1, tm * TOP_K), lambda i: (i, 0, 0), memory_space=pltpu.SMEM),
                  pl.BlockSpec((tm, *tile), lambda i: (jnp.minimum(i, n_tiles_p - 1), 0, 0)),
                  pl.BlockSpec((n_s, *tile), lambda i: (0, 0, 0))],
        out_specs=pl.BlockSpec(memory_space=pl.ANY),
        out_shape=jax.ShapeDtypeStruct(((n_p + n_s) * TOP_K, *tile), F32),
        scratch_shapes=[pltpu.SemaphoreType.DMA(())],
        compiler_params=pltpu.CompilerParams(dimension_semantics=("arbitrary",)),
        name="dispatch",
    )(dest3, hf_p, hf_s)


def _moe_kernel(vt_ref, ve_ref, nx_ref, lo_ref, hi_ref, x_ref, w1_ref, b1_ref, w2_ref, b2_ref, y_ref,
                w1s, w2s, w1b, w2b, sem, *, d_ff):
    v = pl.program_id(0)
    lo, hi = lo_ref[v], hi_ref[v]
    prev = jnp.maximum(v - 1, 0)
    first = jnp.logical_or(v == 0, vt_ref[v] != vt_ref[prev])

    def fetch(e):
        return (pltpu.make_async_copy(w1_ref.at[e], w1s, sem.at[0]),
                pltpu.make_async_copy(w2_ref.at[e], w2s, sem.at[1]))

    @pl.when(v == 0)
    def _():
        for cp in fetch(ve_ref[0]):
            cp.start()

    @pl.when(jnp.logical_or(v == 0, ve_ref[v] != ve_ref[prev]))
    def _():
        for cp in fetch(ve_ref[v]):
            cp.wait()
        w1b[...] = w1s[...].astype(jnp.bfloat16)
        w2b[...] = w2s[...].astype(jnp.bfloat16)

        @pl.when(nx_ref[v] >= 0)
        def _():
            for cp in fetch(nx_ref[v]):
                cp.start()

    for r0 in range(0, MOE_TILE, MOE_SUB):
        sub = pl.ds(r0 * V7X_SUBLANES, MOE_SUB * V7X_SUBLANES)
        x_sub, y_sub = x_ref.at[sub, :], y_ref.at[sub, :]
        touched = jnp.logical_and(lo < r0 + MOE_SUB, hi > r0)

        @pl.when(touched)
        def _(r0=r0, x_sub=x_sub, y_sub=y_sub):
            x = _from_token_tiles(x_sub).astype(jnp.bfloat16)
            a = jnp.dot(x, w1b[...], preferred_element_type=F32) + b1_ref[0]
            x_glu = jnp.minimum(a[:, :d_ff], SWIGLU_LIMIT)
            x_lin = jnp.clip(a[:, d_ff:], -SWIGLU_LIMIT, SWIGLU_LIMIT)
            mid = x_glu * jax.nn.sigmoid(SWIGLU_ALPHA * x_glu) * (x_lin + 1.0)
            y = jnp.dot(mid.astype(jnp.bfloat16), w2b[...], preferred_element_type=F32) + b2_ref[0]
            rows = r0 + lax.broadcasted_iota(I32, (MOE_SUB, 1), 0)
            mine = (rows >= lo) & (rows < hi)

            @pl.when(first)
            def _():
                _to_token_tiles(y_sub, jnp.where(mine, y, 0.0))

            @pl.when(jnp.logical_not(first))
            def _():
                _to_token_tiles(y_sub, jnp.where(mine, y, _from_token_tiles(y_sub)))

        @pl.when(jnp.logical_and(first, jnp.logical_not(touched)))
        def _(y_sub=y_sub):
            y_sub[...] = jnp.zeros(y_sub.shape, F32)


def _moe_experts(xs, visits, w1, b1, w2, b2):
    ne, d, ff2 = w1.shape
    d_ff = ff2 // 2
    n_visits = visits[0].shape[0]
    row = lambda v, vt, ve, nx, lo, hi: (vt[v], 0)
    exp = lambda v, vt, ve, nx, lo, hi: (ve[v], 0, 0)
    vmem = (4 * 2 * 2 * MOE_TILE * d + (4 + 2) * (d * ff2 + d_ff * d) + 2 * 4 * (ff2 + d)
            + 4 * 4 * MOE_SUB * ff2)
    tiles = pl.BlockSpec((MOE_TILE * V7X_SUBLANES, V7X_LANES), row)
    grid_spec = pltpu.PrefetchScalarGridSpec(
        num_scalar_prefetch=5,
        grid=(n_visits,),
        in_specs=[tiles,
                  pl.BlockSpec(memory_space=pl.ANY), pl.BlockSpec((1, 1, ff2), exp),
                  pl.BlockSpec(memory_space=pl.ANY), pl.BlockSpec((1, 1, d), exp)],
        out_specs=tiles,
        scratch_shapes=[pltpu.VMEM((d, ff2), F32), pltpu.VMEM((d_ff, d), F32),
                        pltpu.VMEM((d, ff2), jnp.bfloat16), pltpu.VMEM((d_ff, d), jnp.bfloat16),
                        pltpu.SemaphoreType.DMA((2,))],
    )
    return pl.pallas_call(
        functools.partial(_moe_kernel, d_ff=d_ff),
        grid_spec=grid_spec,
        out_shape=jax.ShapeDtypeStruct(xs.shape, F32),
        compiler_params=pltpu.CompilerParams(dimension_semantics=("arbitrary",),
                                             vmem_limit_bytes=_vmem_limit(vmem)),
        name="moe",
    )(*visits, xs, w1, b1.reshape(ne, 1, ff2), w2, b2.reshape(ne, 1, d))


def _moe_visits(sizes, n_rows):
    n_tiles = n_rows // MOE_TILE
    n_visits = n_tiles + N_EXPERTS - 1
    ends = jnp.cumsum(sizes)
    starts = ends - sizes
    first_tile = starts // MOE_TILE
    n_vis = jnp.where(sizes > 0, (ends - 1) // MOE_TILE - first_tile + 1, 0)
    vis_end = jnp.cumsum(n_vis)
    total = vis_end[-1]
    v = jnp.clip(jnp.arange(n_visits, dtype=I32), 0, jnp.maximum(total - 1, 0))
    e = jnp.sum((vis_end[None, :] <= v[:, None]).astype(I32), axis=1)
    pick = lambda tab: jnp.sum(jnp.where(e[:, None] == jnp.arange(N_EXPERTS)[None, :], tab[None, :], 0), axis=1)
    tile = pick(first_tile) + v - pick(vis_end - n_vis)
    lo = jnp.maximum(pick(starts), tile * MOE_TILE) - tile * MOE_TILE
    hi = jnp.minimum(pick(ends), (tile + 1) * MOE_TILE) - tile * MOE_TILE
    live = jnp.arange(n_visits) < total
    ids = jnp.arange(N_EXPERTS, dtype=I32)
    later = jnp.where((ids[None, :] > ids[:, None]) & (sizes[None, :] > 0), ids[None, :], N_EXPERTS)
    nxt = jnp.min(later, axis=1)
    nxt = pick(jnp.where(nxt < N_EXPERTS, nxt, -1))
    return (tile.astype(I32), e.astype(I32), nxt.astype(I32), jnp.where(live, lo, 0).astype(I32),
            jnp.where(live, hi, 0).astype(I32))


def _final_kernel(dest_ref, nxt_ref, x_ref, tg_ref, g_ref, ys_ref, y_ref, buf, sem, *, tm):
    i = pl.program_id(0)
    n = pl.num_programs(0)

    def gather(d_ref, slot):
        def tokens(i8, carry):
            for u in range(V7X_SUBLANES):
                r = i8 * V7X_SUBLANES + u
                tile = pl.ds(pl.multiple_of(r * V7X_SUBLANES, V7X_SUBLANES), V7X_SUBLANES)
                for k in range(TOP_K):
                    pltpu.make_async_copy(ys_ref.at[d_ref[0, 0, r * TOP_K + k]], buf.at[slot, k, tile, :],
                                          sem.at[slot]).start(priority=k % 2)
            return carry
        lax.fori_loop(0, tm // V7X_SUBLANES, tokens, 0)

    @pl.when(i == 0)
    def _():
        gather(dest_ref, 0)

    for slot in range(2):
        @pl.when(jnp.logical_and(i + 1 < n, (i + 1) % 2 == slot))
        def _(slot=slot):
            gather(nxt_ref, slot)

    for slot in range(2):
        @pl.when(i % 2 == slot)
        def _(slot=slot):
            pltpu.make_async_copy(buf.at[slot], buf.at[slot], sem.at[slot]).wait()
            tg = tg_ref[...]
            parts = []
            for j in range(V7X_SUBLANES):
                f = None
                for k in range(TOP_K):
                    rows = buf[slot, k, pl.ds(j, tm, stride=V7X_SUBLANES), :]
                    f = tg[:, k:k + 1] * rows if f is None else f + tg[:, k:k + 1] * rows
                parts.append(f)
            y_ref[...] = _rms(x_ref[...] + jnp.concatenate(parts, axis=1), g_ref[...])


def _final(x1, tg, dest, ys, g, tm):
    n, d = x1.shape
    n_tiles = n // tm
    tile = (V7X_SUBLANES, V7X_LANES)
    dest3 = dest.reshape(n_tiles, 1, tm * TOP_K)
    row = lambda i: (i, 0)
    smem = lambda index_map: pl.BlockSpec((1, 1, tm * TOP_K), index_map, memory_space=pltpu.SMEM)
    vmem = 4 * (2 * TOP_K * tm * d + 2 * 2 * tm * d + 2 * tm * V7X_LANES + 4 * tm * d)
    return pl.pallas_call(
        functools.partial(_final_kernel, tm=tm),
        grid=(n_tiles,),
        in_specs=[smem(lambda i: (i, 0, 0)), smem(lambda i: (jnp.minimum(i + 1, n_tiles - 1), 0, 0)),
                  pl.BlockSpec((tm, d), row), pl.BlockSpec((tm, V7X_LANES), row),
                  pl.BlockSpec((1, d), lambda i: (0, 0)), pl.BlockSpec(memory_space=pl.ANY)],
        out_specs=pl.BlockSpec((tm, d), row),
        out_shape=jax.ShapeDtypeStruct((n, d), F32),
        scratch_shapes=[pltpu.VMEM((2, TOP_K, tm * V7X_SUBLANES, V7X_LANES), F32), pltpu.SemaphoreType.DMA((2,))],
        compiler_params=pltpu.CompilerParams(dimension_semantics=("arbitrary",),
                                             vmem_limit_bytes=_vmem_limit(vmem)),
        name="final",
    )(dest3, dest3, x1, tg, g.reshape(1, d), ys.reshape(-1, *tile))


def kernel(x_prompt, x_sample, cache_k, cache_v, state_ssm_re, state_ssm_im, norm_mix_g, w_in, attn_out_norm_g, ssm_a_re, ssm_a_im, ssm_log_dt, ssm_b_re, ssm_b_im, ssm_c_re, ssm_c_im, ssm_d, w_glu, b_glu, ssm_out_norm_g, w_out, norm_ffn_g, w_router, b_router, w_moe1, b_moe1, w_moe2, b_moe2, norm_final_g):
    depth = w_in.shape[0]
    assert depth == 1, "single-layer trunk"
    bp, s, d = x_prompt.shape
    bs, t, _ = x_sample.shape
    n_buf = cache_k.shape[2]
    n_heads, head_dim = cache_k.shape[3], cache_k.shape[4]
    assert head_dim == HEAD_DIM
    a = n_heads * head_dim
    g = ssm_a_re.shape[1]
    gn = g * SSM_STATE
    assert s % (max(dd for _, dd in DILATED_GROUPS) * KEYS_BACK) == 0
    np_, ns = bp * s, bs * t
    tm_p, tm_s = 256, ns
    l = 0
    ssm_p = (ssm_a_re[l], ssm_a_im[l], ssm_log_dt[l], ssm_b_re[l], ssm_b_im[l], ssm_c_re[l], ssm_c_im[l],
             ssm_d[l])
    tail = dict(attn_out_norm_g=attn_out_norm_g[l], w_glu=w_glu[l], b_glu=b_glu[l],
                ssm_out_norm_g=ssm_out_norm_g[l], w_out=w_out[l], norm_ffn_g=norm_ffn_g[l],
                w_router=w_router[l], b_router=b_router[l])

    xp = x_prompt.reshape(np_, d)
    qp, kp, vp, up = _proj(xp, norm_mix_g[l], w_in[l], a, tm_p)
    attn_p = _attn_prompt(qp.reshape(bp, s, a), kp.reshape(bp, s, a), vp.reshape(bp, s, a))
    zeros = jnp.zeros((bp, gn), F32)
    tables = _ssm_tables(*ssm_p, PROMPT_CHUNK)
    y_p, hp_re, hp_im = _ssm_chunks_on_lanes(up.reshape(bp, s, -1), zeros, zeros, _ssm_tables_lanes(tables),
                                             PROMPT_CHUNK)
    seen = jnp.zeros((1, V7X_LANES), F32)
    x1_p, hf_p, ti_p, tg_p, seen = _finish(xp, attn_p.reshape(np_, a), y_p.reshape(np_, -1), tail, seen, tm_p)

    xs = x_sample.reshape(ns, d)
    qs, ks, vs, us = _proj(xs, norm_mix_g[l], w_in[l], a, tm_s)
    by_dim = lambda c: c.transpose(0, 2, 3, 1).reshape(bs, a, n_buf)
    attn_s = _attn_sample(qs.reshape(bs, t, a), ks.reshape(bs, t, a), vs.reshape(bs, t, a),
                          by_dim(cache_k[l]), by_dim(cache_v[l]))
    y_s, hs_re, hs_im = _ssm(us.reshape(bs, t, -1), state_ssm_re[l].reshape(bs, gn),
                             state_ssm_im[l].reshape(bs, gn), _ssm_tables_rows(tables, t), t)
    x1_s, hf_s, ti_s, tg_s, seen = _finish(xs, attn_s.reshape(ns, a), y_s.reshape(ns, -1), tail, seen, tm_s)

    sizes = seen[0, :N_EXPERTS].astype(I32)
    starts = jnp.cumsum(sizes) - sizes
    experts = jnp.arange(N_EXPERTS, dtype=I32)[None, None, :]

    def sorted_row(ti):
        hit = ti[:, :TOP_K, None] == experts
        return jnp.sum(jnp.where(hit, starts[None, None, :], 0), axis=-1) + ti[:, TOP_K:2 * TOP_K]

    dest_p, dest_s = sorted_row(ti_p), sorted_row(ti_s)
    n_rows = (np_ + ns) * TOP_K
    assert n_rows % MOE_TILE == 0
    xs_sorted = _dispatch(hf_p, dest_p, hf_s, dest_s, tm_p).reshape(-1, V7X_LANES)
    ys = _moe_experts(xs_sorted, _moe_visits(sizes, n_rows), w_moe1[l], b_moe1[l], w_moe2[l], b_moe2[l])
    y_prompt = _final(x1_p, tg_p, dest_p, ys, norm_final_g, tm_p).reshape(bp, s, d)
    y_sample = _final(x1_s, tg_s, dest_s, ys, norm_final_g, tm_s).reshape(bs, t, d)

    keep = min(max(w for w, _ in DILATED_GROUPS), s)
    k5 = lambda z, b_, s_: z.reshape(1, b_, s_, n_heads, head_dim)
    st = lambda z, b_: z.reshape(1, b_, g, SSM_STATE)
    return (y_prompt, y_sample,
            k5(kp, bp, s)[:, :, s - keep:], k5(vp, bp, s)[:, :, s - keep:], st(hp_re, bp), st(hp_im, bp),
            k5(ks, bs, t), k5(vs, bs, t), st(hs_re, bs), st(hs_im, bs))
```

```python
import functools
import math

import numpy as np
import jax
import jax.numpy as jnp
from jax import lax
from jax.experimental import pallas as pl
from jax.experimental.pallas import tpu as pltpu

F32 = jnp.float32
I32 = jnp.int32

V7X_LANES = 128
V7X_SUBLANES = 8
V7X_VMEM_BYTES = 64 * 1024 * 1024

HEAD_DIM = 64
HEADS_PER_LANE_TILE = V7X_LANES // HEAD_DIM
DILATED_GROUPS = ((128, 1), (512, 4), (2048, 16))
KEYS_BACK = 128
SSM_GROUP = 16
SSM_STATE = 64
N_EXPERTS = 32
TOP_K = 4
SWIGLU_LIMIT = 7.0
SWIGLU_ALPHA = 1.702
RMS_EPS = 1e-5
NEG_INF = -1e30
PROMPT_CHUNK = 16
MOE_TILE = 512
MOE_SUB = 256


def _vmem_limit(nbytes):
    return int(min(nbytes + (8 << 20), V7X_VMEM_BYTES - (8 << 20)))


def _rms(x, g):
    return x * lax.rsqrt(jnp.mean(x * x, axis=-1, keepdims=True) + RMS_EPS) * g


def _to_token_tiles(ref, val):
    rows, width = val.shape
    assert width == V7X_SUBLANES * V7X_LANES and ref.shape == (rows * V7X_SUBLANES, V7X_LANES)
    for j in range(V7X_SUBLANES):
        ref[pl.ds(j, rows, stride=V7X_SUBLANES), :] = val[:, j * V7X_LANES:(j + 1) * V7X_LANES]


def _from_token_tiles(ref):
    rows = ref.shape[0] // V7X_SUBLANES
    return jnp.concatenate([ref[pl.ds(j, rows, stride=V7X_SUBLANES), :] for j in range(V7X_SUBLANES)], axis=1)


def _proj_kernel(x_ref, g_ref, w_ref, q_ref, k_ref, v_ref, u_ref, *, attn_width, q_scale):
    h = _rms(x_ref[...], g_ref[...])
    z = jnp.dot(h, w_ref[...], preferred_element_type=F32)
    a = attn_width
    q_ref[...] = z[:, :a] * q_scale
    k_ref[...] = z[:, a:2 * a]
    v_ref[...] = z[:, 2 * a:3 * a]
    u_ref[...] = z[:, 3 * a:]


def _proj(x2d, norm_g, w_in, attn_width, tm):
    n, d = x2d.shape
    mix = w_in.shape[1]
    ssm_width = mix - 3 * attn_width
    row = lambda i: (i, 0)
    const = lambda i: (0, 0)
    out_shape = (jax.ShapeDtypeStruct((n, attn_width), F32),) * 3 + (
        jax.ShapeDtypeStruct((n, ssm_width), F32),)
    vmem = 2 * 4 * (tm * d + d * mix + tm * mix) + 4 * tm * mix
    return pl.pallas_call(
        functools.partial(_proj_kernel, attn_width=attn_width, q_scale=HEAD_DIM ** -0.5),
        grid=(n // tm,),
        in_specs=[pl.BlockSpec((tm, d), row), pl.BlockSpec((1, d), const),
                  pl.BlockSpec((d, mix), const)],
        out_specs=(pl.BlockSpec((tm, attn_width), row),) * 3 + (pl.BlockSpec((tm, ssm_width), row),),
        out_shape=out_shape,
        compiler_params=pltpu.CompilerParams(dimension_semantics=("arbitrary",),
                                             vmem_limit_bytes=_vmem_limit(vmem)),
        name="proj",
    )(x2d, norm_g.reshape(1, d), w_in)


def _attn_prompt_kernel(q_ref, k_ref, v_ref, o_ref, *scratch, seq, unroll):
    nb = KEYS_BACK
    n_groups = len(DILATED_GROUPS)
    m_sc, l_sc, acc_sc = scratch[:n_groups], scratch[n_groups:2 * n_groups], scratch[2 * n_groups:]
    lane = lax.broadcasted_iota(I32, (1, V7X_LANES), 1)
    head_a = lane < HEAD_DIM
    qi = lax.broadcasted_iota(I32, (nb, 2 * nb), 0)
    kj = lax.broadcasted_iota(I32, (nb, 2 * nb), 1)
    rel = nb + qi - kj
    band = (rel >= 0) & (rel <= nb)

    def rows(ref, start, d):
        if d == 1:
            return ref[pl.ds(pl.multiple_of(start, nb), nb), :]
        return ref[pl.ds(start, nb, stride=d), :]

    def put(ref, start, d, val):
        if d == 1:
            ref[pl.ds(pl.multiple_of(start, nb), nb), :] = val
        else:
            ref[pl.ds(start, nb, stride=d), :] = val

    for gi, (w, d) in enumerate(DILATED_GROUPS):
        assert w // d == nb
        nblk = seq // (d * nb)

        def block(t, gi=gi, d=d, nblk=nblk):
            r = t // nblk
            n = t % nblk
            cur = r + d * nb * n
            prev = r + d * nb * jnp.maximum(n - 1, 0)
            q = rows(q_ref, cur, d)
            kk = jnp.concatenate([rows(k_ref, prev, d), rows(k_ref, cur, d)], axis=0)
            vv = jnp.concatenate([rows(v_ref, prev, d), rows(v_ref, cur, d)], axis=0)
            qs = jnp.concatenate([jnp.where(head_a, q, 0.0), jnp.where(head_a, 0.0, q)], axis=0)
            s = lax.dot_general(qs, kk, (((1,), (1,)), ((), ())), preferred_element_type=F32)
            valid = band & (kj >= jnp.where(n > 0, 0, nb))
            s = jnp.where(jnp.concatenate([valid, valid], axis=0), s, NEG_INF)
            m = jnp.max(s, axis=-1, keepdims=True)
            p = jnp.exp(s - m)
            l = jnp.sum(p, axis=-1, keepdims=True)
            o = jnp.dot(p, vv, preferred_element_type=F32)
            o = jnp.where(head_a, o[:nb], o[nb:])
            put(m_sc[gi], cur, d, jnp.where(head_a, m[:nb], m[nb:]))
            put(l_sc[gi], cur, d, jnp.where(head_a, l[:nb], l[nb:]))
            put(acc_sc[gi], cur, d, o)

        def blocks(tt, carry, block=block):
            for u in range(unroll):
                block(tt * unroll + u)
            return carry

        lax.fori_loop(0, seq // nb // unroll, blocks, 0)

    def merge(n, carry):
        sl = pl.ds(pl.multiple_of(n * nb, nb), nb)
        ms = [m[sl, :] for m in m_sc]
        m_all = functools.reduce(jnp.maximum, ms)
        cs = [jnp.exp(m - m_all) for m in ms]
        den = functools.reduce(jnp.add, [c * l[sl, :] for c, l in zip(cs, l_sc)])
        num = functools.reduce(jnp.add, [c * acc[sl, :] for c, acc in zip(cs, acc_sc)])
        o_ref[sl, :] = num / den
        return carry

    lax.fori_loop(0, seq // nb, merge, 0, unroll=2)


def _attn_prompt(q, k, v, unroll=8):
    b, s, a = q.shape
    n_scratch = 3 * len(DILATED_GROUPS)
    spec = pl.BlockSpec((None, s, V7X_LANES), lambda i, j: (i, 0, j))
    vmem = 4 * s * V7X_LANES * (2 * 4 + n_scratch)
    return pl.pallas_call(
        functools.partial(_attn_prompt_kernel, seq=s, unroll=unroll),
        grid=(b, a // V7X_LANES),
        in_specs=[spec, spec, spec],
        out_specs=spec,
        out_shape=jax.ShapeDtypeStruct((b, s, a), F32),
        scratch_shapes=[pltpu.VMEM((s, V7X_LANES), F32)] * n_scratch,
        compiler_params=pltpu.CompilerParams(dimension_semantics=("arbitrary", "arbitrary"),
                                             vmem_limit_bytes=_vmem_limit(vmem)),
        name="attn_prompt",
    )(q, k, v)


def _sample_key_counts(n_buf, n_new):
    cnt = np.zeros((n_new, n_buf + n_new), np.float32)
    for w, d in DILATED_GROUPS:
        for i in range(n_new):
            for j in range(w // d + 1):
                idx = n_buf + i - d * j
                if idx >= 0:
                    cnt[i, idx] += 1.0
    return cnt[:, :n_buf], cnt[:, n_buf:]


def _attn_sample_kernel(q_ref, kn_ref, vn_ref, kt_ref, vt_ref, cnt_ref, o_ref, *, n_heads, cnt_new):
    n_new, a = q_ref.shape
    rows = n_new * n_heads
    lane_head = lax.broadcasted_iota(I32, (rows, a), 1) // HEAD_DIM
    row_head = lax.broadcasted_iota(I32, (rows, a), 0) % n_heads
    row_query = lax.broadcasted_iota(I32, (rows, 1), 0) // n_heads
    own = lane_head == row_head
    q = q_ref[...]
    qb = jnp.where(own, jnp.broadcast_to(q[:, None, :], (n_new, n_heads, a)).reshape(rows, a), 0.0)
    s_buf = jnp.dot(qb, kt_ref[...], preferred_element_type=F32)
    cnt = cnt_ref[...]
    cnt = jnp.broadcast_to(cnt[:, None, :], (n_new, n_heads, cnt.shape[-1])).reshape(rows, -1)
    valid = cnt > 0.0
    m = jnp.max(jnp.where(valid, s_buf, NEG_INF), axis=-1, keepdims=True)
    s_new, c_new = [], []
    for j in range(n_new):
        c = functools.reduce(lambda acc, i: jnp.where(row_query == i, float(cnt_new[i, j]), acc),
                             range(n_new), jnp.zeros((rows, 1), F32))
        s = jnp.sum(qb * kn_ref[pl.ds(j, 1), :], axis=-1, keepdims=True)
        m = jnp.maximum(m, jnp.where(c > 0.0, s, NEG_INF))
        s_new.append(s)
        c_new.append(c)
    p = jnp.where(valid, cnt * jnp.exp(s_buf - m), 0.0)
    den = jnp.sum(p, axis=-1, keepdims=True)
    o = lax.dot_general(p, vt_ref[...], (((1,), (1,)), ((), ())), preferred_element_type=F32)
    for j in range(n_new):
        pj = jnp.where(c_new[j] > 0.0, c_new[j] * jnp.exp(s_new[j] - m), 0.0)
        den = den + pj
        o = o + pj * vn_ref[pl.ds(j, 1), :]
    o = jnp.where(own, o / den, 0.0)
    o_ref[...] = jnp.sum(o.reshape(n_new, n_heads, a), axis=1)


def _attn_sample(q, k_new, v_new, cache_kt, cache_vt):
    b, t, a = q.shape
    n_buf = cache_kt.shape[-1]
    n_heads = a // HEAD_DIM
    cnt_buf, cnt_new = _sample_key_counts(n_buf, t)
    new = pl.BlockSpec((None, t, a), lambda i: (i, 0, 0))
    buf = pl.BlockSpec((None, a, n_buf), lambda i: (i, 0, 0))
    vmem = 4 * (2 * 2 * n_buf * a + 8 * t * n_heads * n_buf)
    return pl.pallas_call(
        functools.partial(_attn_sample_kernel, n_heads=n_heads, cnt_new=cnt_new),
        grid=(b,),
        in_specs=[new, new, new, buf, buf, pl.BlockSpec((t, n_buf), lambda i: (0, 0))],
        out_specs=new,
        out_shape=jax.ShapeDtypeStruct((b, t, a), F32),
        compiler_params=pltpu.CompilerParams(dimension_semantics=("arbitrary",),
                                             vmem_limit_bytes=_vmem_limit(vmem)),
        name="attn_sample",
    )(q, k_new, v_new, cache_kt, cache_vt, jnp.asarray(cnt_buf))


def _block_toeplitz_kernel(k_ref, toep_ref, tt_ref, *, chunk):
    c = k_ref.shape[1]
    for i in range(k_ref.shape[0]):
        k = k_ref[i]
        for s in range(chunk):
            shifted = k if s == 0 else jnp.concatenate(
                [jnp.zeros((c, s * c), F32), k[:, :(chunk - s) * c]], axis=1)
            toep_ref[i, pl.ds(s * c, c), :] = shifted
        tt_ref[i] = toep_ref[i].T


def _block_toeplitz(kern, chunk):
    g, c, lc = kern.shape
    per_step = 4
    spec = pl.BlockSpec((per_step, lc, lc), lambda i: (i, 0, 0))
    return pl.pallas_call(
        functools.partial(_block_toeplitz_kernel, chunk=chunk),
        grid=(g // per_step,),
        in_specs=[pl.BlockSpec((per_step, c, lc), lambda i: (i, 0, 0))],
        out_specs=(spec, spec),
        out_shape=(jax.ShapeDtypeStruct((g, lc, lc), F32),) * 2,
        compiler_params=pltpu.CompilerParams(dimension_semantics=("arbitrary",)),
        name="ssm_toeplitz",
    )(kern)


def _pad_states(m, g):
    half = (np.arange(g) % 2)[:, None, None]
    z = jnp.zeros_like(m)
    return jnp.where(half == 0, jnp.concatenate([m, z], -1), jnp.concatenate([z, m], -1))


def _ssm_tables(a_re, a_im, log_dt, b_re, b_im, c_re, c_im, d_skip, chunk):
    g, n = a_re.shape
    c = b_re.shape[-1]
    lc = chunk * c
    dt = jnp.exp(log_dt)[:, None]
    x, y = a_re * dt, a_im * dt
    ex = jnp.exp(x)
    ar, ai = ex * jnp.cos(y), ex * jnp.sin(y)
    nr = jnp.expm1(x) * jnp.cos(y) - 2.0 * jnp.sin(0.5 * y) ** 2
    ni = ai
    den = a_re * a_re + a_im * a_im
    fr = (nr * a_re + ni * a_im) / den
    fi = (ni * a_re - nr * a_im) / den
    b_re_t, b_im_t = b_re.transpose(0, 2, 1), b_im.transpose(0, 2, 1)
    bbr = fr[:, None, :] * b_re_t - fi[:, None, :] * b_im_t
    bbi = fr[:, None, :] * b_im_t + fi[:, None, :] * b_re_t
    pr, pi = [jnp.ones_like(ar)], [jnp.zeros_like(ar)]
    for _ in range(chunk):
        pr.append(pr[-1] * ar - pi[-1] * ai)
        pi.append(pr[-2] * ai + pi[-1] * ar)
    pr, pi = jnp.stack(pr, 1), jnp.stack(pi, 1)
    cpr = c_re[:, None] * pr[:, :, None, :] - c_im[:, None] * pi[:, :, None, :]
    cpi = c_re[:, None] * pi[:, :, None, :] + c_im[:, None] * pr[:, :, None, :]
    kern = jnp.sum(cpr[:, None, :chunk] * bbr[:, :, None, None] - cpi[:, None, :chunk] * bbi[:, :, None, None],
                   axis=-1).reshape(g, c, lc)
    toep, tt = _block_toeplitz(kern, chunk)
    qr, qi = pr[:, chunk - 1::-1][:, :, None, :], pi[:, chunk - 1::-1][:, :, None, :]
    wr = (qr * bbr[:, None] - qi * bbi[:, None]).reshape(g, lc, n)
    wi = (qr * bbi[:, None] + qi * bbr[:, None]).reshape(g, lc, n)
    vr = cpr[:, 1:].reshape(g, lc, n)
    vi = -cpi[:, 1:].reshape(g, lc, n)
    return dict(chunk=chunk, toep=toep, tt=tt, wr=wr, wi=wi, vr=vr, vi=vi, d=d_skip, pr=pr, pi=pi)


def _ssm_tables_lanes(tab):
    g, chunk = tab["d"].shape[0], tab["chunk"]
    tr = lambda m: m.transpose(0, 2, 1)
    return dict(tt=tab["tt"], wt_re=tr(_pad_states(tab["wr"], g)), wt_im=tr(_pad_states(tab["wi"], g)),
                v_re=_pad_states(tab["vr"], g), v_im=_pad_states(tab["vi"], g),
                dcol=jnp.tile(tab["d"], (1, chunk))[:, :, None],
                al_re=tab["pr"][:, chunk].reshape(1, -1), al_im=tab["pi"][:, chunk].reshape(1, -1))


def _ssm_tables_rows(tab, chunk):
    g, c = tab["d"].shape
    big, lc = tab["chunk"], chunk * c
    tr = lambda m: m.transpose(0, 2, 1)
    tail = slice((big - chunk) * c, big * c)
    return dict(toep=tab["toep"][:, :lc, :lc],
                wr=_pad_states(tab["wr"][:, tail], g), wi=_pad_states(tab["wi"][:, tail], g),
                vr=tr(_pad_states(tab["vr"][:, :lc], g)), vi=tr(_pad_states(tab["vi"][:, :lc], g)),
                d=jnp.tile(tab["d"], (1, chunk))[:, None, :],
                al_re=tab["pr"][:, chunk].reshape(1, -1), al_im=tab["pi"][:, chunk].reshape(1, -1))


def _ssm_state_kernel(u_ref, wr_ref, wi_ref, er_ref, ei_ref):
    er = jnp.dot(u_ref[0], wr_ref[0], preferred_element_type=F32)
    ei = jnp.dot(u_ref[0], wi_ref[0], preferred_element_type=F32)
    er_ref[...] = er + jnp.dot(u_ref[1], wr_ref[1], preferred_element_type=F32)
    ei_ref[...] = ei + jnp.dot(u_ref[1], wi_ref[1], preferred_element_type=F32)


def _ssm_scan_kernel(er_ref, ei_ref, ar_ref, ai_ref, hr_ref, hi_ref, xr_ref, xi_ref, lr_ref, li_ref,
                     *, n_chunks, nb):
    shape = hr_ref.shape
    ar = jnp.broadcast_to(ar_ref[...], shape)
    ai = jnp.broadcast_to(ai_ref[...], shape)

    def step(j, carry):
        xr, xi = carry
        sl = pl.ds(pl.multiple_of(j * nb, nb), nb)
        xr_ref[sl, :] = xr
        xi_ref[sl, :] = xi
        return (ar * xr - ai * xi + er_ref[sl, :], ar * xi + ai * xr + ei_ref[sl, :])

    xr, xi = lax.fori_loop(0, n_chunks, step, (hr_ref[...], hi_ref[...]))
    lr_ref[...] = xr
    li_ref[...] = xi


def _ssm_out_kernel(u_ref, t_ref, d_ref, xr_ref, xi_ref, vr_ref, vi_ref, y_ref):
    xr = xr_ref[...]
    xi = xi_ref[...]
    for a in range(2):
        u = u_ref[a]
        y = jnp.dot(u, t_ref[a], preferred_element_type=F32) + d_ref[a] * u
        y = y + jnp.dot(xr, vr_ref[a], preferred_element_type=F32)
        y_ref[a] = y + jnp.dot(xi, vi_ref[a], preferred_element_type=F32)


def _ssm_scan(er, ei, tab, h0_re, h0_im, n_chunks):
    r, gn = er.shape
    b = r // n_chunks
    lanes = 4 * V7X_LANES
    col = lambda i: (0, i)
    return pl.pallas_call(
        functools.partial(_ssm_scan_kernel, n_chunks=n_chunks, nb=b),
        grid=(gn // lanes,),
        in_specs=[pl.BlockSpec((r, lanes), col), pl.BlockSpec((r, lanes), col),
                  pl.BlockSpec((1, lanes), col), pl.BlockSpec((1, lanes), col),
                  pl.BlockSpec((b, lanes), col), pl.BlockSpec((b, lanes), col)],
        out_specs=(pl.BlockSpec((r, lanes), col),) * 2 + (pl.BlockSpec((b, lanes), col),) * 2,
        out_shape=(jax.ShapeDtypeStruct((r, gn), F32),) * 2 + (jax.ShapeDtypeStruct((b, gn), F32),) * 2,
        compiler_params=pltpu.CompilerParams(dimension_semantics=("arbitrary",),
                                             vmem_limit_bytes=_vmem_limit(2 * 4 * 4 * r * lanes)),
        name="ssm_scan",
    )(er, ei, tab["al_re"], tab["al_im"], h0_re, h0_im)


def _ssm_state_t_kernel(a_ref, wr_ref, wi_ref, er_ref, ei_ref):
    def seq(b, carry):
        a0, a1 = a_ref[0, b], a_ref[1, b]
        er_ref[b] = (jnp.dot(wr_ref[0], a0, preferred_element_type=F32)
                     + jnp.dot(wr_ref[1], a1, preferred_element_type=F32))
        ei_ref[b] = (jnp.dot(wi_ref[0], a0, preferred_element_type=F32)
                     + jnp.dot(wi_ref[1], a1, preferred_element_type=F32))
        return carry

    lax.fori_loop(0, a_ref.shape[1], seq, 0)


def _ssm_out_t_kernel(a_ref, t_ref, d_ref, xr_ref, xi_ref, vr_ref, vi_ref, y_ref):
    def seq(b, carry):
        xr, xi = xr_ref[b], xi_ref[b]
        for a in range(2):
            at = a_ref[a, b]
            y = jnp.dot(t_ref[a], at, preferred_element_type=F32) + d_ref[a] * at
            y = y + jnp.dot(vr_ref[a], xr, preferred_element_type=F32)
            y_ref[a, b] = y + jnp.dot(vi_ref[a], xi, preferred_element_type=F32)
        return carry

    lax.fori_loop(0, a_ref.shape[1], seq, 0)


def _ssm_chunks_on_lanes(u, h0_re, h0_im, tab, chunk):
    b, s, width = u.shape
    g = width // SSM_GROUP
    n_chunks = s // chunk
    lc = chunk * SSM_GROUP
    assert n_chunks == V7X_LANES
    at = (u.reshape(b, n_chunks, chunk * width).transpose(0, 2, 1)
          .reshape(b, chunk, g, SSM_GROUP, n_chunks).transpose(2, 0, 1, 3, 4).reshape(g, b, lc, n_chunks))
    pair4 = lambda i: (i, 0, 0, 0)
    pair3 = lambda i: (i, 0, 0)
    cp = lambda vmem: pltpu.CompilerParams(dimension_semantics=("arbitrary",),
                                           vmem_limit_bytes=_vmem_limit(vmem))
    a_spec = pl.BlockSpec((2, b, lc, n_chunks), pair4)
    st_spec = pl.BlockSpec((None, b, V7X_LANES, n_chunks), pair4)
    w_spec = pl.BlockSpec((2, V7X_LANES, lc), pair3)
    a_bytes = 2 * b * lc * n_chunks
    st_bytes = b * V7X_LANES * n_chunks
    et_re, et_im = pl.pallas_call(
        _ssm_state_t_kernel,
        grid=(g // 2,),
        in_specs=[a_spec, w_spec, w_spec],
        out_specs=(st_spec,) * 2,
        out_shape=(jax.ShapeDtypeStruct((g // 2, b, V7X_LANES, n_chunks), F32),) * 2,
        compiler_params=cp(2 * 4 * (a_bytes + 2 * st_bytes + 4 * V7X_LANES * lc)),
        name="ssm_state_t",
    )(at, tab["wt_re"], tab["wt_im"])
    rows = lambda m: m.transpose(3, 1, 0, 2).reshape(n_chunks * b, g * SSM_STATE)
    xr, xi, lr, li = _ssm_scan(rows(et_re), rows(et_im), tab, h0_re, h0_im, n_chunks)
    cols = lambda m: m.reshape(n_chunks, b, g // 2, V7X_LANES).transpose(2, 1, 3, 0)
    yt = pl.pallas_call(
        _ssm_out_t_kernel,
        grid=(g // 2,),
        in_specs=[a_spec, pl.BlockSpec((2, lc, lc), pair3), pl.BlockSpec((2, lc, 1), pair3),
                  st_spec, st_spec, pl.BlockSpec((2, lc, V7X_LANES), pair3),
                  pl.BlockSpec((2, lc, V7X_LANES), pair3)],
        out_specs=a_spec,
        out_shape=jax.ShapeDtypeStruct((g, b, lc, n_chunks), F32),
        compiler_params=cp(2 * 4 * (2 * a_bytes + 2 * st_bytes + 2 * lc * lc + 6 * V7X_LANES * lc)),
        name="ssm_out_t",
    )(at, tab["tt"], tab["dcol"], cols(xr), cols(xi), tab["v_re"], tab["v_im"])
    y = (yt.reshape(g, b, chunk, SSM_GROUP, n_chunks).transpose(1, 2, 0, 3, 4)
         .reshape(b, chunk * width, n_chunks).transpose(0, 2, 1).reshape(b, s, width))
    return y, lr, li


def _ssm(u, h0_re, h0_im, tab, chunk):
    b, s, width = u.shape
    g = width // SSM_GROUP
    n_chunks = s // chunk
    lc = chunk * SSM_GROUP
    r = n_chunks * b
    gn = g * SSM_STATE
    ut = u.reshape(b, n_chunks, chunk, g, SSM_GROUP).transpose(3, 1, 0, 2, 4).reshape(g, r, lc)
    pair = lambda i: (i, 0, 0)
    cp = lambda vmem: pltpu.CompilerParams(dimension_semantics=("arbitrary",),
                                           vmem_limit_bytes=_vmem_limit(vmem))
    er, ei = pl.pallas_call(
        _ssm_state_kernel,
        grid=(g // 2,),
        in_specs=[pl.BlockSpec((2, r, lc), pair), pl.BlockSpec((2, lc, V7X_LANES), pair),
                  pl.BlockSpec((2, lc, V7X_LANES), pair)],
        out_specs=(pl.BlockSpec((r, V7X_LANES), lambda i: (0, i)),) * 2,
        out_shape=(jax.ShapeDtypeStruct((r, gn), F32),) * 2,
        compiler_params=cp(2 * 4 * (2 * r * lc + 4 * lc * V7X_LANES + 2 * r * V7X_LANES)),
        name="ssm_state",
    )(ut, tab["wr"], tab["wi"])
    xr, xi, lr, li = _ssm_scan(er, ei, tab, h0_re, h0_im, n_chunks)
    y = pl.pallas_call(
        _ssm_out_kernel,
        grid=(g // 2,),
        in_specs=[pl.BlockSpec((2, r, lc), pair), pl.BlockSpec((2, lc, lc), pair),
                  pl.BlockSpec((2, 1, lc), pair),
                  pl.BlockSpec((r, V7X_LANES), lambda i: (0, i)),
                  pl.BlockSpec((r, V7X_LANES), lambda i: (0, i)),
                  pl.BlockSpec((2, V7X_LANES, lc), pair), pl.BlockSpec((2, V7X_LANES, lc), pair)],
        out_specs=pl.BlockSpec((2, r, lc), pair),
        out_shape=jax.ShapeDtypeStruct((g, r, lc), F32),
        compiler_params=cp(2 * 4 * (4 * r * lc + 2 * lc * lc + 2 * r * V7X_LANES + 4 * V7X_LANES * lc)),
        name="ssm_out",
    )(ut, tab["toep"], tab["d"], xr, xi, tab["vr"], tab["vi"])
    y = y.reshape(g, n_chunks, b, chunk, SSM_GROUP).transpose(2, 1, 3, 0, 4).reshape(b, s, width)
    return y, lr, li


def _finish_kernel(x_ref, attn_ref, ssm_ref, ga_ref, wg_ref, bg_ref, gs_ref, wo_ref, gf_ref,
                   wr_ref, br_ref, base_ref, x1_ref, hf_ref, ti_ref, tg_ref, cnt_ref, seen):
    a = attn_ref.shape[-1]
    tm = x_ref.shape[0]

    @pl.when(pl.program_id(0) == 0)
    def _():
        seen[...] = base_ref[...]

    g = jax.nn.gelu(ssm_ref[...])
    gate = jax.nn.sigmoid(jnp.dot(g, wg_ref[...], preferred_element_type=F32) + bg_ref[...])
    mixed_a = _rms(attn_ref[...], ga_ref[...])
    mixed_s = _rms(g * gate, gs_ref[...])
    x1 = (x_ref[...] + jnp.dot(mixed_a, wo_ref[pl.ds(0, a), :], preferred_element_type=F32)
          + jnp.dot(mixed_s, wo_ref[pl.ds(a, wo_ref.shape[0] - a), :], preferred_element_type=F32))
    x1_ref[...] = x1
    hf = _rms(x1, gf_ref[...])
    _to_token_tiles(hf_ref, hf)
    hf_hi = hf.astype(jnp.bfloat16).astype(F32)
    hf_lo = hf - hf_hi
    small = (jnp.dot(hf_lo, wr_ref[1], preferred_element_type=F32)
             + jnp.dot(hf_hi, wr_ref[1], preferred_element_type=F32)
             + jnp.dot(hf_lo, wr_ref[0], preferred_element_type=F32))
    logits = jnp.dot(hf_hi, wr_ref[0], preferred_element_type=F32) + small + br_ref[...]
    lane = lax.broadcasted_iota(I32, logits.shape, 1)
    work = logits
    vals, idxs = [], []
    for _ in range(TOP_K):
        m = jnp.max(work, axis=-1, keepdims=True)
        idx = jnp.min(jnp.where(work == m, lane, V7X_LANES), axis=-1, keepdims=True)
        vals.append(m)
        idxs.append(idx)
        work = jnp.where(lane == idx, -jnp.inf, work)
    exps = [jnp.exp(v - vals[0]) for v in vals]
    den = exps[0]
    for e in exps[1:]:
        den = den + e
    picked = jnp.zeros(logits.shape, F32)
    for k in range(TOP_K):
        picked = picked + jnp.where(lane == idxs[k], 1.0, 0.0)
    tri = jnp.where(lax.broadcasted_iota(I32, (tm, tm), 0) > lax.broadcasted_iota(I32, (tm, tm), 1), 1.0, 0.0)
    before = jnp.dot(tri, picked, preferred_element_type=F32) + seen[...]
    ti = jnp.zeros(logits.shape, I32)
    tg = jnp.zeros(logits.shape, F32)
    for k in range(TOP_K):
        rank = jnp.sum(jnp.where(lane == idxs[k], before, 0.0), axis=-1, keepdims=True)
        ti = jnp.where(lane == k, idxs[k], ti)
        ti = jnp.where(lane == TOP_K + k, rank.astype(I32), ti)
        tg = jnp.where(lane == k, exps[k] / den, tg)
    ti_ref[...] = ti
    tg_ref[...] = tg
    seen[...] = seen[...] + jnp.sum(picked, axis=0, keepdims=True)
    cnt_ref[...] = seen[...]


def _finish(x2d, attn, ssm_y, p, seen, tm):
    n, d = x2d.shape
    a = attn.shape[1]
    sw = ssm_y.shape[1]
    ne = p["w_router"].shape[1]
    w_router = jnp.pad(p["w_router"], ((0, 0), (0, V7X_LANES - ne)))
    w_hi = w_router.astype(jnp.bfloat16).astype(F32)
    w_router = jnp.stack([w_hi, w_router - w_hi])
    b_router = jnp.pad(p["b_router"], (0, V7X_LANES - ne), constant_values=-jnp.inf).reshape(1, V7X_LANES)
    row = lambda i: (i, 0)
    full = lambda arr: pl.BlockSpec(arr.shape, lambda i: (0,) * arr.ndim)
    ins = [x2d, attn, ssm_y, p["attn_out_norm_g"].reshape(1, a), p["w_glu"], p["b_glu"].reshape(1, sw),
           p["ssm_out_norm_g"].reshape(1, sw), p["w_out"], p["norm_ffn_g"].reshape(1, d),
           w_router, b_router, seen]
    in_specs = [pl.BlockSpec((tm, d), row), pl.BlockSpec((tm, a), row), pl.BlockSpec((tm, sw), row)]
    in_specs += [full(arr) for arr in ins[3:]]
    vmem = 2 * 4 * (tm * (3 * d + a + sw + 2 * V7X_LANES) + sw * sw + d * d + 2 * d * V7X_LANES) + 16 * tm * d
    return pl.pallas_call(
        _finish_kernel,
        grid=(n // tm,),
        in_specs=in_specs,
        out_specs=(pl.BlockSpec((tm, d), row), pl.BlockSpec((tm * V7X_SUBLANES, V7X_LANES), row),
                   pl.BlockSpec((tm, V7X_LANES), row), pl.BlockSpec((tm, V7X_LANES), row),
                   pl.BlockSpec((1, V7X_LANES), lambda i: (0, 0))),
        out_shape=(jax.ShapeDtypeStruct((n, d), F32), jax.ShapeDtypeStruct((n * V7X_SUBLANES, V7X_LANES), F32),
                   jax.ShapeDtypeStruct((n, V7X_LANES), I32), jax.ShapeDtypeStruct((n, V7X_LANES), F32),
                   jax.ShapeDtypeStruct((1, V7X_LANES), F32)),
        scratch_shapes=[pltpu.VMEM((1, V7X_LANES), F32)],
        compiler_params=pltpu.CompilerParams(dimension_semantics=("arbitrary",),
                                             vmem_limit_bytes=_vmem_limit(vmem)),
        name="finish",
    )(*ins)


def _tile_copy(src_ref, src_row, dst_ref, dst_row, sem):
    return pltpu.make_async_copy(src_ref.at[src_row], dst_ref.at[dst_row], sem)


def _dispatch_kernel(dest_ref, hp_ref, hs_ref, xs_ref, sem, *, n_tiles_p):
    def scatter(hf_ref):
        tm = hf_ref.shape[0]

        def tokens(i, carry):
            for u in range(V7X_SUBLANES):
                r = i * V7X_SUBLANES + u
                for k in range(TOP_K):
                    _tile_copy(hf_ref, r, xs_ref, dest_ref[0, 0, r * TOP_K + k], sem).start(priority=k % 2)
            return carry

        lax.fori_loop(0, tm // V7X_SUBLANES, tokens, 0)
        rows = pl.ds(0, tm * TOP_K)
        pltpu.make_async_copy(xs_ref.at[rows], xs_ref.at[rows], sem).wait()

    @pl.when(pl.program_id(0) < n_tiles_p)
    def _():
        scatter(hp_ref)

    @pl.when(pl.program_id(0) == n_tiles_p)
    def _():
        scatter(hs_ref)


def _dispatch(hf_p, dest_p, hf_s, dest_s, tm):
    tile = (V7X_SUBLANES, V7X_LANES)
    hf_p, hf_s = hf_p.reshape(-1, *tile), hf_s.reshape(-1, *tile)
    n_p, n_s = hf_p.shape[0], hf_s.shape[0]
    assert n_p % tm == 0 and n_s <= tm
    n_tiles_p = n_p // tm
    dest3 = jnp.concatenate([dest_p.reshape(n_tiles_p, tm * TOP_K),
                             jnp.pad(dest_s.reshape(1, n_s * TOP_K), ((0, 0), (0, (tm - n_s) * TOP_K)))])
    dest3 = dest3.reshape(n_tiles_p + 1, 1, tm * TOP_K)
    return pl.pallas_call(
        functools.partial(_dispatch_kernel, n_tiles_p=n_tiles_p),
        grid=(n_tiles_p + 1,),
        in_specs=[pl.BlockSpec((1, 1, tm * TOP_K), lambda i: (i, 0, 0), memory_space=pltpu.SMEM),
                  pl.BlockSpec((tm, *tile), lambda i: (jnp.minimum(i, n_tiles_p - 1), 0, 0)),
                  pl.BlockSpec((n_s, *tile), lambda i: (0, 0, 0))],
        out_specs=pl.BlockSpec(memory_space=pl.ANY),
        out_shape=jax.ShapeDtypeStruct(((n_p + n_s) * TOP_K, *tile), F32),
        scratch_shapes=[pltpu.SemaphoreType.DMA(())],
        compiler_params=pltpu.CompilerParams(dimension_semantics=("arbitrary",)),
        name="dispatch",
    )(dest3, hf_p, hf_s)


def _moe_kernel(vt_ref, ve_ref, nx_ref, lo_ref, hi_ref, x_ref, w1_ref, b1_ref, w2_ref, b2_ref, y_ref,
                w1s, w2s, w1b, w2b, sem, *, d_ff):
    v = pl.program_id(0)
    lo, hi = lo_ref[v], hi_ref[v]
    prev = jnp.maximum(v - 1, 0)
    first = jnp.logical_or(v == 0, vt_ref[v] != vt_ref[prev])

    def fetch(e):
        return (pltpu.make_async_copy(w1_ref.at[e], w1s, sem.at[0]),
                pltpu.make_async_copy(w2_ref.at[e], w2s, sem.at[1]))

    @pl.when(v == 0)
    def _():
        for cp in fetch(ve_ref[0]):
            cp.start()

    @pl.when(jnp.logical_or(v == 0, ve_ref[v] != ve_ref[prev]))
    def _():
        for cp in fetch(ve_ref[v]):
            cp.wait()
        w1b[...] = w1s[...].astype(jnp.bfloat16)
        w2b[...] = w2s[...].astype(jnp.bfloat16)

        @pl.when(nx_ref[v] >= 0)
        def _():
            for cp in fetch(nx_ref[v]):
                cp.start()

    def swiglu(x_tiles):
        x = _from_token_tiles(x_tiles).astype(jnp.bfloat16)
        a = jnp.dot(x, w1b[...], preferred_element_type=F32) + b1_ref[0]
        x_glu = jnp.minimum(a[:, :d_ff], SWIGLU_LIMIT)
        x_lin = jnp.clip(a[:, d_ff:], -SWIGLU_LIMIT, SWIGLU_LIMIT)
        mid = x_glu * jax.nn.sigmoid(SWIGLU_ALPHA * x_glu) * (x_lin + 1.0)
        return jnp.dot(mid.astype(jnp.bfloat16), w2b[...], preferred_element_type=F32) + b2_ref[0]

    whole = jnp.logical_and(lo == 0, hi == MOE_TILE)

    @pl.when(whole)
    def _():
        _to_token_tiles(y_ref, swiglu(x_ref))

    for r0 in range(0, MOE_TILE, MOE_SUB):
        sub = pl.ds(r0 * V7X_SUBLANES, MOE_SUB * V7X_SUBLANES)
        x_sub, y_sub = x_ref.at[sub, :], y_ref.at[sub, :]
        touched = jnp.logical_and(jnp.logical_not(whole), jnp.logical_and(lo < r0 + MOE_SUB, hi > r0))

        @pl.when(touched)
        def _(r0=r0, x_sub=x_sub, y_sub=y_sub):
            y = swiglu(x_sub)
            rows = r0 + lax.broadcasted_iota(I32, (MOE_SUB, 1), 0)
            mine = (rows >= lo) & (rows < hi)

            @pl.when(first)
            def _():
                _to_token_tiles(y_sub, jnp.where(mine, y, 0.0))

            @pl.when(jnp.logical_not(first))
            def _():
                _to_token_tiles(y_sub, jnp.where(mine, y, _from_token_tiles(y_sub)))

        @pl.when(jnp.logical_and(first, jnp.logical_not(jnp.logical_or(touched, whole))))
        def _(y_sub=y_sub):
            y_sub[...] = jnp.zeros(y_sub.shape, F32)


def _moe_experts(xs, visits, w1, b1, w2, b2):
    ne, d, ff2 = w1.shape
    d_ff = ff2 // 2
    n_visits = visits[0].shape[0]
    row = lambda v, vt, ve, nx, lo, hi: (vt[v], 0)
    exp = lambda v, vt, ve, nx, lo, hi: (ve[v], 0, 0)
    vmem = (4 * 2 * 2 * MOE_TILE * d + (4 + 2) * (d * ff2 + d_ff * d) + 2 * 4 * (ff2 + d)
            + 4 * 4 * MOE_TILE * ff2)
    tiles = pl.BlockSpec((MOE_TILE * V7X_SUBLANES, V7X_LANES), row)
    grid_spec = pltpu.PrefetchScalarGridSpec(
        num_scalar_prefetch=5,
        grid=(n_visits,),
        in_specs=[tiles,
                  pl.BlockSpec(memory_space=pl.ANY), pl.BlockSpec((1, 1, ff2), exp),
                  pl.BlockSpec(memory_space=pl.ANY), pl.BlockSpec((1, 1, d), exp)],
        out_specs=tiles,
        scratch_shapes=[pltpu.VMEM((d, ff2), F32), pltpu.VMEM((d_ff, d), F32),
                        pltpu.VMEM((d, ff2), jnp.bfloat16), pltpu.VMEM((d_ff, d), jnp.bfloat16),
                        pltpu.SemaphoreType.DMA((2,))],
    )
    return pl.pallas_call(
        functools.partial(_moe_kernel, d_ff=d_ff),
        grid_spec=grid_spec,
        out_shape=jax.ShapeDtypeStruct(xs.shape, F32),
        compiler_params=pltpu.CompilerParams(dimension_semantics=("arbitrary",),
                                             vmem_limit_bytes=_vmem_limit(vmem)),
        name="moe",
    )(*visits, xs, w1, b1.reshape(ne, 1, ff2), w2, b2.reshape(ne, 1, d))


def _moe_visits(sizes, n_rows):
    n_tiles = n_rows // MOE_TILE
    n_visits = n_tiles + N_EXPERTS - 1
    ends = jnp.cumsum(sizes)
    starts = ends - sizes
    first_tile = starts // MOE_TILE
    n_vis = jnp.where(sizes > 0, (ends - 1) // MOE_TILE - first_tile + 1, 0)
    vis_end = jnp.cumsum(n_vis)
    total = vis_end[-1]
    v = jnp.clip(jnp.arange(n_visits, dtype=I32), 0, jnp.maximum(total - 1, 0))
    e = jnp.sum((vis_end[None, :] <= v[:, None]).astype(I32), axis=1)
    pick = lambda tab: jnp.sum(jnp.where(e[:, None] == jnp.arange(N_EXPERTS)[None, :], tab[None, :], 0), axis=1)
    tile = pick(first_tile) + v - pick(vis_end - n_vis)
    lo = jnp.maximum(pick(starts), tile * MOE_TILE) - tile * MOE_TILE
    hi = jnp.minimum(pick(ends), (tile + 1) * MOE_TILE) - tile * MOE_TILE
    live = jnp.arange(n_visits) < total
    ids = jnp.arange(N_EXPERTS, dtype=I32)
    later = jnp.where((ids[None, :] > ids[:, None]) & (sizes[None, :] > 0), ids[None, :], N_EXPERTS)
    nxt = jnp.min(later, axis=1)
    nxt = pick(jnp.where(nxt < N_EXPERTS, nxt, -1))
    return (tile.astype(I32), e.astype(I32), nxt.astype(I32), jnp.where(live, lo, 0).astype(I32),
            jnp.where(live, hi, 0).astype(I32))


def _final_kernel(dest_ref, nxt_ref, x_ref, tg_ref, g_ref, ys_ref, y_ref, buf, sem, *, tm):
    i = pl.program_id(0)
    n = pl.num_programs(0)

    def gather(d_ref, slot):
        def tokens(i8, carry):
            for u in range(V7X_SUBLANES):
                r = i8 * V7X_SUBLANES + u
                tile = pl.ds(pl.multiple_of(r * V7X_SUBLANES, V7X_SUBLANES), V7X_SUBLANES)
                for k in range(TOP_K):
                    pltpu.make_async_copy(ys_ref.at[d_ref[0, 0, r * TOP_K + k]], buf.at[slot, k, tile, :],
                                          sem.at[slot]).start(priority=k % 2)
            return carry
        lax.fori_loop(0, tm // V7X_SUBLANES, tokens, 0)

    @pl.when(i == 0)
    def _():
        gather(dest_ref, 0)

    for slot in range(2):
        @pl.when(jnp.logical_and(i + 1 < n, (i + 1) % 2 == slot))
        def _(slot=slot):
            gather(nxt_ref, slot)

    for slot in range(2):
        @pl.when(i % 2 == slot)
        def _(slot=slot):
            pltpu.make_async_copy(buf.at[slot], buf.at[slot], sem.at[slot]).wait()
            tg = tg_ref[...]
            parts = []
            for j in range(V7X_SUBLANES):
                f = None
                for k in range(TOP_K):
                    rows = buf[slot, k, pl.ds(j, tm, stride=V7X_SUBLANES), :]
                    f = tg[:, k:k + 1] * rows if f is None else f + tg[:, k:k + 1] * rows
                parts.append(f)
            y_ref[...] = _rms(x_ref[...] + jnp.concatenate(parts, axis=1), g_ref[...])


def _final(x1, tg, dest, ys, g, tm):
    n, d = x1.shape
    n_tiles = n // tm
    tile = (V7X_SUBLANES, V7X_LANES)
    dest3 = dest.reshape(n_tiles, 1, tm * TOP_K)
    row = lambda i: (i, 0)
    smem = lambda index_map: pl.BlockSpec((1, 1, tm * TOP_K), index_map, memory_space=pltpu.SMEM)
    vmem = 4 * (2 * TOP_K * tm * d + 2 * 2 * tm * d + 2 * tm * V7X_LANES + 4 * tm * d)
    return pl.pallas_call(
        functools.partial(_final_kernel, tm=tm),
        grid=(n_tiles,),
        in_specs=[smem(lambda i: (i, 0, 0)), smem(lambda i: (jnp.minimum(i + 1, n_tiles - 1), 0, 0)),
                  pl.BlockSpec((tm, d), row), pl.BlockSpec((tm, V7X_LANES), row),
                  pl.BlockSpec((1, d), lambda i: (0, 0)), pl.BlockSpec(memory_space=pl.ANY)],
        out_specs=pl.BlockSpec((tm, d), row),
        out_shape=jax.ShapeDtypeStruct((n, d), F32),
        scratch_shapes=[pltpu.VMEM((2, TOP_K, tm * V7X_SUBLANES, V7X_LANES), F32), pltpu.SemaphoreType.DMA((2,))],
        compiler_params=pltpu.CompilerParams(dimension_semantics=("arbitrary",),
                                             vmem_limit_bytes=_vmem_limit(vmem)),
        name="final",
    )(dest3, dest3, x1, tg, g.reshape(1, d), ys.reshape(-1, *tile))


def kernel(x_prompt, x_sample, cache_k, cache_v, state_ssm_re, state_ssm_im, norm_mix_g, w_in, attn_out_norm_g, ssm_a_re, ssm_a_im, ssm_log_dt, ssm_b_re, ssm_b_im, ssm_c_re, ssm_c_im, ssm_d, w_glu, b_glu, ssm_out_norm_g, w_out, norm_ffn_g, w_router, b_router, w_moe1, b_moe1, w_moe2, b_moe2, norm_final_g):
    depth = w_in.shape[0]
    assert depth == 1, "single-layer trunk"
    bp, s, d = x_prompt.shape
    bs, t, _ = x_sample.shape
    n_buf = cache_k.shape[2]
    n_heads, head_dim = cache_k.shape[3], cache_k.shape[4]
    assert head_dim == HEAD_DIM
    a = n_heads * head_dim
    g = ssm_a_re.shape[1]
    gn = g * SSM_STATE
    assert s % (max(dd for _, dd in DILATED_GROUPS) * KEYS_BACK) == 0
    np_, ns = bp * s, bs * t
    tm_p, tm_s = 256, ns
    l = 0
    ssm_p = (ssm_a_re[l], ssm_a_im[l], ssm_log_dt[l], ssm_b_re[l], ssm_b_im[l], ssm_c_re[l], ssm_c_im[l],
             ssm_d[l])
    tail = dict(attn_out_norm_g=attn_out_norm_g[l], w_glu=w_glu[l], b_glu=b_glu[l],
                ssm_out_norm_g=ssm_out_norm_g[l], w_out=w_out[l], norm_ffn_g=norm_ffn_g[l],
                w_router=w_router[l], b_router=b_router[l])

    xp = x_prompt.reshape(np_, d)
    qp, kp, vp, up = _proj(xp, norm_mix_g[l], w_in[l], a, tm_p)
    attn_p = _attn_prompt(qp.reshape(bp, s, a), kp.reshape(bp, s, a), vp.reshape(bp, s, a))
    zeros = jnp.zeros((bp, gn), F32)
    tables = _ssm_tables(*ssm_p, PROMPT_CHUNK)
    y_p, hp_re, hp_im = _ssm_chunks_on_lanes(up.reshape(bp, s, -1), zeros, zeros, _ssm_tables_lanes(tables),
                                             PROMPT_CHUNK)
    seen = jnp.zeros((1, V7X_LANES), F32)
    x1_p, hf_p, ti_p, tg_p, seen = _finish(xp, attn_p.reshape(np_, a), y_p.reshape(np_, -1), tail, seen, tm_p)

    xs = x_sample.reshape(ns, d)
    qs, ks, vs, us = _proj(xs, norm_mix_g[l], w_in[l], a, tm_s)
    by_dim = lambda c: c.transpose(0, 2, 3, 1).reshape(bs, a, n_buf)
    attn_s = _attn_sample(qs.reshape(bs, t, a), ks.reshape(bs, t, a), vs.reshape(bs, t, a),
                          by_dim(cache_k[l]), by_dim(cache_v[l]))
    y_s, hs_re, hs_im = _ssm(us.reshape(bs, t, -1), state_ssm_re[l].reshape(bs, gn),
                             state_ssm_im[l].reshape(bs, gn), _ssm_tables_rows(tables, t), t)
    x1_s, hf_s, ti_s, tg_s, seen = _finish(xs, attn_s.reshape(ns, a), y_s.reshape(ns, -1), tail, seen, tm_s)

    sizes = seen[0, :N_EXPERTS].astype(I32)
    starts = jnp.cumsum(sizes) - sizes
    experts = jnp.arange(N_EXPERTS, dtype=I32)[None, None, :]

    def sorted_row(ti):
        hit = ti[:, :TOP_K, None] == experts
        return jnp.sum(jnp.where(hit, starts[None, None, :], 0), axis=-1) + ti[:, TOP_K:2 * TOP_K]

    dest_p, dest_s = sorted_row(ti_p), sorted_row(ti_s)
    n_rows = (np_ + ns) * TOP_K
    assert n_rows % MOE_TILE == 0
    xs_sorted = _dispatch(hf_p, dest_p, hf_s, dest_s, tm_p).reshape(-1, V7X_LANES)
    ys = _moe_experts(xs_sorted, _moe_visits(sizes, n_rows), w_moe1[l], b_moe1[l], w_moe2[l], b_moe2[l])
    y_prompt = _final(x1_p, tg_p, dest_p, ys, norm_final_g, tm_p).reshape(bp, s, d)
    y_sample = _final(x1_s, tg_s, dest_s, ys, norm_final_g, tm_s).reshape(bs, t, d)

    keep = min(max(w for w, _ in DILATED_GROUPS), s)
    k5 = lambda z, b_, s_: z.reshape(1, b_, s_, n_heads, head_dim)
    st = lambda z, b_: z.reshape(1, b_, g, SSM_STATE)
    return (y_prompt, y_sample,
            k5(kp, bp, s)[:, :, s - keep:], k5(vp, bp, s)[:, :, s - keep:], st(hp_re, bp), st(hp_im, bp),
            k5(ks, bs, t), k5(vs, bs, t), st(hs_re, bs), st(hs_im, bs))
```

```python
import functools
import math

import numpy as np
import jax
import jax.numpy as jnp
from jax import lax
from jax.experimental import pallas as pl
from jax.experimental.pallas import tpu as pltpu

F32 = jnp.float32
I32 = jnp.int32

V7X_LANES = 128
V7X_SUBLANES = 8
V7X_VMEM_BYTES = 64 * 1024 * 1024

HEAD_DIM = 64
HEADS_PER_LANE_TILE = V7X_LANES // HEAD_DIM
DILATED_GROUPS = ((128, 1), (512, 4), (2048, 16))
KEYS_BACK = 128
SSM_GROUP = 16
SSM_STATE = 64
N_EXPERTS = 32
TOP_K = 4
SWIGLU_LIMIT = 7.0
SWIGLU_ALPHA = 1.702
RMS_EPS = 1e-5
NEG_INF = -1e30
PROMPT_CHUNK = 16
MOE_TILE = 512
MOE_SUB = 256


def _vmem_limit(nbytes):
    return int(min(nbytes + (8 << 20), V7X_VMEM_BYTES - (8 << 20)))


def _rms(x, g):
    return x * lax.rsqrt(jnp.mean(x * x, axis=-1, keepdims=True) + RMS_EPS) * g


def _to_token_tiles(ref, val):
    rows, width = val.shape
    assert width == V7X_SUBLANES * V7X_LANES and ref.shape == (rows * V7X_SUBLANES, V7X_LANES)
    for j in range(V7X_SUBLANES):
        ref[pl.ds(j, rows, stride=V7X_SUBLANES), :] = val[:, j * V7X_LANES:(j + 1) * V7X_LANES]


def _from_token_tiles(ref):
    rows = ref.shape[0] // V7X_SUBLANES
    return jnp.concatenate([ref[pl.ds(j, rows, stride=V7X_SUBLANES), :] for j in range(V7X_SUBLANES)], axis=1)


def _proj_kernel(x_ref, g_ref, w_ref, q_ref, k_ref, v_ref, u_ref, *kv_t, attn_width, q_scale):
    h = _rms(x_ref[...], g_ref[...])
    z = jnp.dot(h, w_ref[...], preferred_element_type=F32)
    a = attn_width
    q_ref[...] = z[:, :a] * q_scale
    k_ref[...] = z[:, a:2 * a]
    v_ref[...] = z[:, 2 * a:3 * a]
    u_ref[...] = z[:, 3 * a:]
    if kv_t:
        kt_ref, vt_ref = kv_t
        kt_ref[...] = z[:, a:2 * a].T
        vt_ref[...] = z[:, 2 * a:3 * a].T


def _proj(x2d, norm_g, w_in, attn_width, tm, seq=None):
    n, d = x2d.shape
    mix = w_in.shape[1]
    ssm_width = mix - 3 * attn_width
    row = lambda i: (i, 0)
    const = lambda i: (0, 0)
    out_shape = (jax.ShapeDtypeStruct((n, attn_width), F32),) * 3 + (
        jax.ShapeDtypeStruct((n, ssm_width), F32),)
    out_specs = (pl.BlockSpec((tm, attn_width), row),) * 3 + (pl.BlockSpec((tm, ssm_width), row),)
    if seq is not None:
        assert seq % tm == 0 and tm % V7X_LANES == 0
        per_seq = seq // tm
        out_shape += (jax.ShapeDtypeStruct((n // seq, attn_width, seq), F32),) * 2
        out_specs += (pl.BlockSpec((None, attn_width, tm), lambda i: (i // per_seq, 0, i % per_seq)),) * 2
    vmem = 2 * 4 * (tm * d + d * mix + tm * mix + 2 * tm * attn_width) + 4 * tm * mix
    return pl.pallas_call(
        functools.partial(_proj_kernel, attn_width=attn_width, q_scale=HEAD_DIM ** -0.5),
        grid=(n // tm,),
        in_specs=[pl.BlockSpec((tm, d), row), pl.BlockSpec((1, d), const),
                  pl.BlockSpec((d, mix), const)],
        out_specs=out_specs,
        out_shape=out_shape,
        compiler_params=pltpu.CompilerParams(dimension_semantics=("arbitrary",),
                                             vmem_limit_bytes=_vmem_limit(vmem)),
        name="proj",
    )(x2d, norm_g.reshape(1, d), w_in)


def _attn_prompt_kernel(q_ref, k_ref, v_ref, o_ref, *scratch, seq, unroll):
    nb = KEYS_BACK
    n_groups = len(DILATED_GROUPS)
    m_sc, l_sc, acc_sc = scratch[:n_groups], scratch[n_groups:2 * n_groups], scratch[2 * n_groups:]
    lane = lax.broadcasted_iota(I32, (1, V7X_LANES), 1)
    head_a = lane < HEAD_DIM
    qi = lax.broadcasted_iota(I32, (nb, 2 * nb), 0)
    kj = lax.broadcasted_iota(I32, (nb, 2 * nb), 1)
    rel = nb + qi - kj
    band = (rel >= 0) & (rel <= nb)

    def rows(ref, start, d):
        if d == 1:
            return ref[pl.ds(pl.multiple_of(start, nb), nb), :]
        return ref[pl.ds(start, nb, stride=d), :]

    def put(ref, start, d, val):
        if d == 1:
            ref[pl.ds(pl.multiple_of(start, nb), nb), :] = val
        else:
            ref[pl.ds(start, nb, stride=d), :] = val

    for gi, (w, d) in enumerate(DILATED_GROUPS):
        assert w // d == nb
        nblk = seq // (d * nb)

        def block(t, gi=gi, d=d, nblk=nblk):
            r = t // nblk
            n = t % nblk
            cur = r + d * nb * n
            prev = r + d * nb * jnp.maximum(n - 1, 0)
            q = rows(q_ref, cur, d)
            kk = jnp.concatenate([rows(k_ref, prev, d), rows(k_ref, cur, d)], axis=0)
            vv = jnp.concatenate([rows(v_ref, prev, d), rows(v_ref, cur, d)], axis=0)
            qs = jnp.concatenate([jnp.where(head_a, q, 0.0), jnp.where(head_a, 0.0, q)], axis=0)
            s = lax.dot_general(qs, kk, (((1,), (1,)), ((), ())), preferred_element_type=F32)
            valid = band & (kj >= jnp.where(n > 0, 0, nb))
            s = jnp.where(jnp.concatenate([valid, valid], axis=0), s, NEG_INF)
            m = jnp.max(s, axis=-1, keepdims=True)
            p = jnp.exp(s - m)
            l = jnp.sum(p, axis=-1, keepdims=True)
            o = jnp.dot(p, vv, preferred_element_type=F32)
            o = jnp.where(head_a, o[:nb], o[nb:])
            put(m_sc[gi], cur, d, jnp.where(head_a, m[:nb], m[nb:]))
            put(l_sc[gi], cur, d, jnp.where(head_a, l[:nb], l[nb:]))
            put(acc_sc[gi], cur, d, o)

        def blocks(tt, carry, block=block):
            for u in range(unroll):
                block(tt * unroll + u)
            return carry

        lax.fori_loop(0, seq // nb // unroll, blocks, 0)

    def merge(n, carry):
        sl = pl.ds(pl.multiple_of(n * nb, nb), nb)
        ms = [m[sl, :] for m in m_sc]
        m_all = functools.reduce(jnp.maximum, ms)
        cs = [jnp.exp(m - m_all) for m in ms]
        den = functools.reduce(jnp.add, [c * l[sl, :] for c, l in zip(cs, l_sc)])
        num = functools.reduce(jnp.add, [c * acc[sl, :] for c, acc in zip(cs, acc_sc)])
        o_ref[sl, :] = num / den
        return carry

    lax.fori_loop(0, seq // nb, merge, 0, unroll=2)


def _attn_prompt(q, k, v, unroll=8):
    b, s, a = q.shape
    n_scratch = 3 * len(DILATED_GROUPS)
    spec = pl.BlockSpec((None, s, V7X_LANES), lambda i, j: (i, 0, j))
    vmem = 4 * s * V7X_LANES * (2 * 4 + n_scratch)
    return pl.pallas_call(
        functools.partial(_attn_prompt_kernel, seq=s, unroll=unroll),
        grid=(b, a // V7X_LANES),
        in_specs=[spec, spec, spec],
        out_specs=spec,
        out_shape=jax.ShapeDtypeStruct((b, s, a), F32),
        scratch_shapes=[pltpu.VMEM((s, V7X_LANES), F32)] * n_scratch,
        compiler_params=pltpu.CompilerParams(dimension_semantics=("arbitrary", "arbitrary"),
                                             vmem_limit_bytes=_vmem_limit(vmem)),
        name="attn_prompt",
    )(q, k, v)


def _sample_key_counts(n_buf, n_new):
    cnt = np.zeros((n_new, n_buf + n_new), np.float32)
    for w, d in DILATED_GROUPS:
        for i in range(n_new):
            for j in range(w // d + 1):
                idx = n_buf + i - d * j
                if idx >= 0:
                    cnt[i, idx] += 1.0
    return cnt[:, :n_buf], cnt[:, n_buf:]


def _attn_sample_kernel(q_ref, kn_ref, vn_ref, kt_ref, vt_ref, cnt_ref, o_ref, *, n_heads, cnt_new):
    n_new, a = q_ref.shape
    rows = n_new * n_heads
    lane_head = lax.broadcasted_iota(I32, (rows, a), 1) // HEAD_DIM
    row_head = lax.broadcasted_iota(I32, (rows, a), 0) % n_heads
    row_query = lax.broadcasted_iota(I32, (rows, 1), 0) // n_heads
    own = lane_head == row_head
    q = q_ref[...]
    qb = jnp.where(own, jnp.broadcast_to(q[:, None, :], (n_new, n_heads, a)).reshape(rows, a), 0.0)
    s_buf = jnp.dot(qb, kt_ref[...], preferred_element_type=F32)
    cnt = cnt_ref[...]
    cnt = jnp.broadcast_to(cnt[:, None, :], (n_new, n_heads, cnt.shape[-1])).reshape(rows, -1)
    valid = cnt > 0.0
    m = jnp.max(jnp.where(valid, s_buf, NEG_INF), axis=-1, keepdims=True)
    s_new, c_new = [], []
    for j in range(n_new):
        c = functools.reduce(lambda acc, i: jnp.where(row_query == i, float(cnt_new[i, j]), acc),
                             range(n_new), jnp.zeros((rows, 1), F32))
        s = jnp.sum(qb * kn_ref[pl.ds(j, 1), :], axis=-1, keepdims=True)
        m = jnp.maximum(m, jnp.where(c > 0.0, s, NEG_INF))
        s_new.append(s)
        c_new.append(c)
    p = jnp.where(valid, cnt * jnp.exp(s_buf - m), 0.0)
    den = jnp.sum(p, axis=-1, keepdims=True)
    o = lax.dot_general(p, vt_ref[...], (((1,), (1,)), ((), ())), preferred_element_type=F32)
    for j in range(n_new):
        pj = jnp.where(c_new[j] > 0.0, c_new[j] * jnp.exp(s_new[j] - m), 0.0)
        den = den + pj
        o = o + pj * vn_ref[pl.ds(j, 1), :]
    o = jnp.where(own, o / den, 0.0)
    o_ref[...] = jnp.sum(o.reshape(n_new, n_heads, a), axis=1)


def _attn_sample(q, k_new, v_new, cache_kt, cache_vt):
    b, t, a = q.shape
    n_buf = cache_kt.shape[-1]
    n_heads = a // HEAD_DIM
    cnt_buf, cnt_new = _sample_key_counts(n_buf, t)
    new = pl.BlockSpec((None, t, a), lambda i: (i, 0, 0))
    buf = pl.BlockSpec((None, a, n_buf), lambda i: (i, 0, 0))
    vmem = 4 * (2 * 2 * n_buf * a + 8 * t * n_heads * n_buf)
    return pl.pallas_call(
        functools.partial(_attn_sample_kernel, n_heads=n_heads, cnt_new=cnt_new),
        grid=(b,),
        in_specs=[new, new, new, buf, buf, pl.BlockSpec((t, n_buf), lambda i: (0, 0))],
        out_specs=new,
        out_shape=jax.ShapeDtypeStruct((b, t, a), F32),
        compiler_params=pltpu.CompilerParams(dimension_semantics=("arbitrary",),
                                             vmem_limit_bytes=_vmem_limit(vmem)),
        name="attn_sample",
    )(q, k_new, v_new, cache_kt, cache_vt, jnp.asarray(cnt_buf))


def _block_toeplitz_kernel(k_ref, toep_ref, tt_ref, *, chunk):
    c = k_ref.shape[1]
    for i in range(k_ref.shape[0]):
        k = k_ref[i]
        for s in range(chunk):
            shifted = k if s == 0 else jnp.concatenate(
                [jnp.zeros((c, s * c), F32), k[:, :(chunk - s) * c]], axis=1)
            toep_ref[i, pl.ds(s * c, c), :] = shifted
        tt_ref[i] = toep_ref[i].T


def _block_toeplitz(kern, chunk):
    g, c, lc = kern.shape
    per_step = 4
    spec = pl.BlockSpec((per_step, lc, lc), lambda i: (i, 0, 0))
    return pl.pallas_call(
        functools.partial(_block_toeplitz_kernel, chunk=chunk),
        grid=(g // per_step,),
        in_specs=[pl.BlockSpec((per_step, c, lc), lambda i: (i, 0, 0))],
        out_specs=(spec, spec),
        out_shape=(jax.ShapeDtypeStruct((g, lc, lc), F32),) * 2,
        compiler_params=pltpu.CompilerParams(dimension_semantics=("arbitrary",)),
        name="ssm_toeplitz",
    )(kern)


def _pad_states(m, g):
    half = (np.arange(g) % 2)[:, None, None]
    z = jnp.zeros_like(m)
    return jnp.where(half == 0, jnp.concatenate([m, z], -1), jnp.concatenate([z, m], -1))


def _ssm_tables(a_re, a_im, log_dt, b_re, b_im, c_re, c_im, d_skip, chunk):
    g, n = a_re.shape
    c = b_re.shape[-1]
    lc = chunk * c
    dt = jnp.exp(log_dt)[:, None]
    x, y = a_re * dt, a_im * dt
    ex = jnp.exp(x)
    ar, ai = ex * jnp.cos(y), ex * jnp.sin(y)
    nr = jnp.expm1(x) * jnp.cos(y) - 2.0 * jnp.sin(0.5 * y) ** 2
    ni = ai
    den = a_re * a_re + a_im * a_im
    fr = (nr * a_re + ni * a_im) / den
    fi = (ni * a_re - nr * a_im) / den
    b_re_t, b_im_t = b_re.transpose(0, 2, 1), b_im.transpose(0, 2, 1)
    bbr = fr[:, None, :] * b_re_t - fi[:, None, :] * b_im_t
    bbi = fr[:, None, :] * b_im_t + fi[:, None, :] * b_re_t
    pr, pi = [jnp.ones_like(ar)], [jnp.zeros_like(ar)]
    for _ in range(chunk):
        pr.append(pr[-1] * ar - pi[-1] * ai)
        pi.append(pr[-2] * ai + pi[-1] * ar)
    pr, pi = jnp.stack(pr, 1), jnp.stack(pi, 1)
    cpr = c_re[:, None] * pr[:, :, None, :] - c_im[:, None] * pi[:, :, None, :]
    cpi = c_re[:, None] * pi[:, :, None, :] + c_im[:, None] * pr[:, :, None, :]
    kern = jnp.sum(cpr[:, None, :chunk] * bbr[:, :, None, None] - cpi[:, None, :chunk] * bbi[:, :, None, None],
                   axis=-1).reshape(g, c, lc)
    toep, tt = _block_toeplitz(kern, chunk)
    qr, qi = pr[:, chunk - 1::-1][:, :, None, :], pi[:, chunk - 1::-1][:, :, None, :]
    wr = (qr * bbr[:, None] - qi * bbi[:, None]).reshape(g, lc, n)
    wi = (qr * bbi[:, None] + qi * bbr[:, None]).reshape(g, lc, n)
    vr = cpr[:, 1:].reshape(g, lc, n)
    vi = -cpi[:, 1:].reshape(g, lc, n)
    return dict(chunk=chunk, toep=toep, tt=tt, wr=wr, wi=wi, vr=vr, vi=vi, d=d_skip, pr=pr, pi=pi)


def _ssm_tables_lanes(tab):
    g, chunk = tab["d"].shape[0], tab["chunk"]
    tr = lambda m: m.transpose(0, 2, 1)
    return dict(tt=tab["tt"], wt_re=tr(_pad_states(tab["wr"], g)), wt_im=tr(_pad_states(tab["wi"], g)),
                v_re=_pad_states(tab["vr"], g), v_im=_pad_states(tab["vi"], g),
                dcol=jnp.tile(tab["d"], (1, chunk))[:, :, None],
                al_re=tab["pr"][:, chunk].reshape(1, -1), al_im=tab["pi"][:, chunk].reshape(1, -1))


def _ssm_tables_rows(tab, chunk):
    g, c = tab["d"].shape
    big, lc = tab["chunk"], chunk * c
    tr = lambda m: m.transpose(0, 2, 1)
    tail = slice((big - chunk) * c, big * c)
    return dict(toep=tab["toep"][:, :lc, :lc],
                wr=_pad_states(tab["wr"][:, tail], g), wi=_pad_states(tab["wi"][:, tail], g),
                vr=tr(_pad_states(tab["vr"][:, :lc], g)), vi=tr(_pad_states(tab["vi"][:, :lc], g)),
                d=jnp.tile(tab["d"], (1, chunk))[:, None, :],
                al_re=tab["pr"][:, chunk].reshape(1, -1), al_im=tab["pi"][:, chunk].reshape(1, -1))


def _ssm_state_kernel(u_ref, wr_ref, wi_ref, er_ref, ei_ref):
    er = jnp.dot(u_ref[0], wr_ref[0], preferred_element_type=F32)
    ei = jnp.dot(u_ref[0], wi_ref[0], preferred_element_type=F32)
    er_ref[...] = er + jnp.dot(u_ref[1], wr_ref[1], preferred_element_type=F32)
    ei_ref[...] = ei + jnp.dot(u_ref[1], wi_ref[1], preferred_element_type=F32)


def _ssm_scan_kernel(er_ref, ei_ref, ar_ref, ai_ref, hr_ref, hi_ref, xr_ref, xi_ref, lr_ref, li_ref,
                     *, n_chunks, nb):
    shape = hr_ref.shape
    ar = jnp.broadcast_to(ar_ref[...], shape)
    ai = jnp.broadcast_to(ai_ref[...], shape)

    def step(j, carry):
        xr, xi = carry
        sl = pl.ds(pl.multiple_of(j * nb, nb), nb)
        xr_ref[sl, :] = xr
        xi_ref[sl, :] = xi
        return (ar * xr - ai * xi + er_ref[sl, :], ar * xi + ai * xr + ei_ref[sl, :])

    xr, xi = lax.fori_loop(0, n_chunks, step, (hr_ref[...], hi_ref[...]))
    lr_ref[...] = xr
    li_ref[...] = xi


def _ssm_out_kernel(u_ref, t_ref, d_ref, xr_ref, xi_ref, vr_ref, vi_ref, y_ref):
    xr = xr_ref[...]
    xi = xi_ref[...]
    for a in range(2):
        u = u_ref[a]
        y = jnp.dot(u, t_ref[a], preferred_element_type=F32) + d_ref[a] * u
        y = y + jnp.dot(xr, vr_ref[a], preferred_element_type=F32)
        y_ref[a] = y + jnp.dot(xi, vi_ref[a], preferred_element_type=F32)


def _ssm_scan(er, ei, tab, h0_re, h0_im, n_chunks):
    r, gn = er.shape
    b = r // n_chunks
    lanes = 4 * V7X_LANES
    col = lambda i: (0, i)
    return pl.pallas_call(
        functools.partial(_ssm_scan_kernel, n_chunks=n_chunks, nb=b),
        grid=(gn // lanes,),
        in_specs=[pl.BlockSpec((r, lanes), col), pl.BlockSpec((r, lanes), col),
                  pl.BlockSpec((1, lanes), col), pl.BlockSpec((1, lanes), col),
                  pl.BlockSpec((b, lanes), col), pl.BlockSpec((b, lanes), col)],
        out_specs=(pl.BlockSpec((r, lanes), col),) * 2 + (pl.BlockSpec((b, lanes), col),) * 2,
        out_shape=(jax.ShapeDtypeStruct((r, gn), F32),) * 2 + (jax.ShapeDtypeStruct((b, gn), F32),) * 2,
        compiler_params=pltpu.CompilerParams(dimension_semantics=("arbitrary",),
                                             vmem_limit_bytes=_vmem_limit(2 * 4 * 4 * r * lanes)),
        name="ssm_scan",
    )(er, ei, tab["al_re"], tab["al_im"], h0_re, h0_im)


def _ssm_state_t_kernel(a_ref, wr_ref, wi_ref, er_ref, ei_ref):
    def seq(b, carry):
        a0, a1 = a_ref[0, b], a_ref[1, b]
        er_ref[b] = (jnp.dot(wr_ref[0], a0, preferred_element_type=F32)
                     + jnp.dot(wr_ref[1], a1, preferred_element_type=F32))
        ei_ref[b] = (jnp.dot(wi_ref[0], a0, preferred_element_type=F32)
                     + jnp.dot(wi_ref[1], a1, preferred_element_type=F32))
        return carry

    lax.fori_loop(0, a_ref.shape[1], seq, 0)


def _ssm_out_t_kernel(a_ref, t_ref, d_ref, xr_ref, xi_ref, vr_ref, vi_ref, y_ref):
    def seq(b, carry):
        xr, xi = xr_ref[b], xi_ref[b]
        for a in range(2):
            at = a_ref[a, b]
            y = jnp.dot(t_ref[a], at, preferred_element_type=F32) + d_ref[a] * at
            y = y + jnp.dot(vr_ref[a], xr, preferred_element_type=F32)
            y_ref[a, b] = y + jnp.dot(vi_ref[a], xi, preferred_element_type=F32)
        return carry

    lax.fori_loop(0, a_ref.shape[1], seq, 0)


def _ssm_chunks_on_lanes(u, h0_re, h0_im, tab, chunk):
    b, s, width = u.shape
    g = width // SSM_GROUP
    n_chunks = s // chunk
    lc = chunk * SSM_GROUP
    assert n_chunks == V7X_LANES
    at = (u.reshape(b, n_chunks, chunk * width).transpose(0, 2, 1)
          .reshape(b, chunk, g, SSM_GROUP, n_chunks).transpose(2, 0, 1, 3, 4).reshape(g, b, lc, n_chunks))
    pair4 = lambda i: (i, 0, 0, 0)
    pair3 = lambda i: (i, 0, 0)
    cp = lambda vmem: pltpu.CompilerParams(dimension_semantics=("arbitrary",),
                                           vmem_limit_bytes=_vmem_limit(vmem))
    a_spec = pl.BlockSpec((2, b, lc, n_chunks), pair4)
    st_spec = pl.BlockSpec((None, b, V7X_LANES, n_chunks), pair4)
    w_spec = pl.BlockSpec((2, V7X_LANES, lc), pair3)
    a_bytes = 2 * b * lc * n_chunks
    st_bytes = b * V7X_LANES * n_chunks
    et_re, et_im = pl.pallas_call(
        _ssm_state_t_kernel,
        grid=(g // 2,),
        in_specs=[a_spec, w_spec, w_spec],
        out_specs=(st_spec,) * 2,
        out_shape=(jax.ShapeDtypeStruct((g // 2, b, V7X_LANES, n_chunks), F32),) * 2,
        compiler_params=cp(2 * 4 * (a_bytes + 2 * st_bytes + 4 * V7X_LANES * lc)),
        name="ssm_state_t",
    )(at, tab["wt_re"], tab["wt_im"])
    rows = lambda m: m.transpose(3, 1, 0, 2).reshape(n_chunks * b, g * SSM_STATE)
    xr, xi, lr, li = _ssm_scan(rows(et_re), rows(et_im), tab, h0_re, h0_im, n_chunks)
    cols = lambda m: m.reshape(n_chunks, b, g // 2, V7X_LANES).transpose(2, 1, 3, 0)
    yt = pl.pallas_call(
        _ssm_out_t_kernel,
        grid=(g // 2,),
        in_specs=[a_spec, pl.BlockSpec((2, lc, lc), pair3), pl.BlockSpec((2, lc, 1), pair3),
                  st_spec, st_spec, pl.BlockSpec((2, lc, V7X_LANES), pair3),
                  pl.BlockSpec((2, lc, V7X_LANES), pair3)],
        out_specs=a_spec,
        out_shape=jax.ShapeDtypeStruct((g, b, lc, n_chunks), F32),
        compiler_params=cp(2 * 4 * (2 * a_bytes + 2 * st_bytes + 2 * lc * lc + 6 * V7X_LANES * lc)),
        name="ssm_out_t",
    )(at, tab["tt"], tab["dcol"], cols(xr), cols(xi), tab["v_re"], tab["v_im"])
    y = (yt.reshape(g, b, chunk, SSM_GROUP, n_chunks).transpose(1, 2, 0, 3, 4)
         .reshape(b, chunk * width, n_chunks).transpose(0, 2, 1).reshape(b, s, width))
    return y, lr, li


def _ssm(u, h0_re, h0_im, tab, chunk):
    b, s, width = u.shape
    g = width // SSM_GROUP
    n_chunks = s // chunk
    lc = chunk * SSM_GROUP
    r = n_chunks * b
    gn = g * SSM_STATE
    ut = u.reshape(b, n_chunks, chunk, g, SSM_GROUP).transpose(3, 1, 0, 2, 4).reshape(g, r, lc)
    pair = lambda i: (i, 0, 0)
    cp = lambda vmem: pltpu.CompilerParams(dimension_semantics=("arbitrary",),
                                           vmem_limit_bytes=_vmem_limit(vmem))
    er, ei = pl.pallas_call(
        _ssm_state_kernel,
        grid=(g // 2,),
        in_specs=[pl.BlockSpec((2, r, lc), pair), pl.BlockSpec((2, lc, V7X_LANES), pair),
                  pl.BlockSpec((2, lc, V7X_LANES), pair)],
        out_specs=(pl.BlockSpec((r, V7X_LANES), lambda i: (0, i)),) * 2,
        out_shape=(jax.ShapeDtypeStruct((r, gn), F32),) * 2,
        compiler_params=cp(2 * 4 * (2 * r * lc + 4 * lc * V7X_LANES + 2 * r * V7X_LANES)),
        name="ssm_state",
    )(ut, tab["wr"], tab["wi"])
    xr, xi, lr, li = _ssm_scan(er, ei, tab, h0_re, h0_im, n_chunks)
    y = pl.pallas_call(
        _ssm_out_kernel,
        grid=(g // 2,),
        in_specs=[pl.BlockSpec((2, r, lc), pair), pl.BlockSpec((2, lc, lc), pair),
                  pl.BlockSpec((2, 1, lc), pair),
                  pl.BlockSpec((r, V7X_LANES), lambda i: (0, i)),
                  pl.BlockSpec((r, V7X_LANES), lambda i: (0, i)),
                  pl.BlockSpec((2, V7X_LANES, lc), pair), pl.BlockSpec((2, V7X_LANES, lc), pair)],
        out_specs=pl.BlockSpec((2, r, lc), pair),
        out_shape=jax.ShapeDtypeStruct((g, r, lc), F32),
        compiler_params=cp(2 * 4 * (4 * r * lc + 2 * lc * lc + 2 * r * V7X_LANES + 4 * V7X_LANES * lc)),
        name="ssm_out",
    )(ut, tab["toep"], tab["d"], xr, xi, tab["vr"], tab["vi"])
    y = y.reshape(g, n_chunks, b, chunk, SSM_GROUP).transpose(2, 1, 3, 0, 4).reshape(b, s, width)
    return y, lr, li


def _finish_kernel(x_ref, attn_ref, ssm_ref, ga_ref, wg_ref, bg_ref, gs_ref, wo_ref, gf_ref,
                   wr_ref, br_ref, base_ref, x1_ref, hf_ref, ti_ref, tg_ref, cnt_ref, seen):
    a = attn_ref.shape[-1]
    tm = x_ref.shape[0]

    @pl.when(pl.program_id(0) == 0)
    def _():
        seen[...] = base_ref[...]

    g = jax.nn.gelu(ssm_ref[...])
    gate = jax.nn.sigmoid(jnp.dot(g, wg_ref[...], preferred_element_type=F32) + bg_ref[...])
    mixed_a = _rms(attn_ref[...], ga_ref[...])
    mixed_s = _rms(g * gate, gs_ref[...])
    x1 = (x_ref[...] + jnp.dot(mixed_a, wo_ref[pl.ds(0, a), :], preferred_element_type=F32)
          + jnp.dot(mixed_s, wo_ref[pl.ds(a, wo_ref.shape[0] - a), :], preferred_element_type=F32))
    x1_ref[...] = x1
    hf = _rms(x1, gf_ref[...])
    _to_token_tiles(hf_ref, hf)
    hf_hi = hf.astype(jnp.bfloat16).astype(F32)
    hf_lo = hf - hf_hi
    small = (jnp.dot(hf_lo, wr_ref[1], preferred_element_type=F32)
             + jnp.dot(hf_hi, wr_ref[1], preferred_element_type=F32)
             + jnp.dot(hf_lo, wr_ref[0], preferred_element_type=F32))
    logits = jnp.dot(hf_hi, wr_ref[0], preferred_element_type=F32) + small + br_ref[...]
    lane = lax.broadcasted_iota(I32, logits.shape, 1)
    work = logits
    vals, idxs = [], []
    for _ in range(TOP_K):
        m = jnp.max(work, axis=-1, keepdims=True)
        idx = jnp.min(jnp.where(work == m, lane, V7X_LANES), axis=-1, keepdims=True)
        vals.append(m)
        idxs.append(idx)
        work = jnp.where(lane == idx, -jnp.inf, work)
    exps = [jnp.exp(v - vals[0]) for v in vals]
    den = exps[0]
    for e in exps[1:]:
        den = den + e
    picked = jnp.zeros(logits.shape, F32)
    for k in range(TOP_K):
        picked = picked + jnp.where(lane == idxs[k], 1.0, 0.0)
    tri = jnp.where(lax.broadcasted_iota(I32, (tm, tm), 0) > lax.broadcasted_iota(I32, (tm, tm), 1), 1.0, 0.0)
    before = jnp.dot(tri, picked, preferred_element_type=F32) + seen[...]
    ti = jnp.zeros(logits.shape, I32)
    tg = jnp.zeros(logits.shape, F32)
    for k in range(TOP_K):
        rank = jnp.sum(jnp.where(lane == idxs[k], before, 0.0), axis=-1, keepdims=True)
        ti = jnp.where(lane == k, idxs[k], ti)
        ti = jnp.where(lane == TOP_K + k, rank.astype(I32), ti)
        tg = jnp.where(lane == k, exps[k] / den, tg)
    ti_ref[...] = ti
    tg_ref[...] = tg
    seen[...] = seen[...] + jnp.sum(picked, axis=0, keepdims=True)
    cnt_ref[...] = seen[...]


def _finish(x2d, attn, ssm_y, p, seen, tm):
    n, d = x2d.shape
    a = attn.shape[1]
    sw = ssm_y.shape[1]
    ne = p["w_router"].shape[1]
    w_router = jnp.pad(p["w_router"], ((0, 0), (0, V7X_LANES - ne)))
    w_hi = w_router.astype(jnp.bfloat16).astype(F32)
    w_router = jnp.stack([w_hi, w_router - w_hi])
    b_router = jnp.pad(p["b_router"], (0, V7X_LANES - ne), constant_values=-jnp.inf).reshape(1, V7X_LANES)
    row = lambda i: (i, 0)
    full = lambda arr: pl.BlockSpec(arr.shape, lambda i: (0,) * arr.ndim)
    ins = [x2d, attn, ssm_y, p["attn_out_norm_g"].reshape(1, a), p["w_glu"], p["b_glu"].reshape(1, sw),
           p["ssm_out_norm_g"].reshape(1, sw), p["w_out"], p["norm_ffn_g"].reshape(1, d),
           w_router, b_router, seen]
    in_specs = [pl.BlockSpec((tm, d), row), pl.BlockSpec((tm, a), row), pl.BlockSpec((tm, sw), row)]
    in_specs += [full(arr) for arr in ins[3:]]
    vmem = 2 * 4 * (tm * (3 * d + a + sw + 2 * V7X_LANES) + sw * sw + d * d + 2 * d * V7X_LANES) + 16 * tm * d
    return pl.pallas_call(
        _finish_kernel,
        grid=(n // tm,),
        in_specs=in_specs,
        out_specs=(pl.BlockSpec((tm, d), row), pl.BlockSpec((tm * V7X_SUBLANES, V7X_LANES), row),
                   pl.BlockSpec((tm, V7X_LANES), row), pl.BlockSpec((tm, V7X_LANES), row),
                   pl.BlockSpec((1, V7X_LANES), lambda i: (0, 0))),
        out_shape=(jax.ShapeDtypeStruct((n, d), F32), jax.ShapeDtypeStruct((n * V7X_SUBLANES, V7X_LANES), F32),
                   jax.ShapeDtypeStruct((n, V7X_LANES), I32), jax.ShapeDtypeStruct((n, V7X_LANES), F32),
                   jax.ShapeDtypeStruct((1, V7X_LANES), F32)),
        scratch_shapes=[pltpu.VMEM((1, V7X_LANES), F32)],
        compiler_params=pltpu.CompilerParams(dimension_semantics=("arbitrary",),
                                             vmem_limit_bytes=_vmem_limit(vmem)),
        name="finish",
    )(*ins)


def _tile_copy(src_ref, src_row, dst_ref, dst_row, sem):
    return pltpu.make_async_copy(src_ref.at[src_row], dst_ref.at[dst_row], sem)


def _dispatch_kernel(dest_ref, hp_ref, hs_ref, xs_ref, sem, *, n_tiles_p):
    def scatter(hf_ref):
        tm = hf_ref.shape[0]

        def tokens(i, carry):
            for u in range(V7X_SUBLANES):
                r = i * V7X_SUBLANES + u
                for k in range(TOP_K):
                    _tile_copy(hf_ref, r, xs_ref, dest_ref[0, 0, r * TOP_K + k], sem).start(priority=k % 2)
            return carry

        lax.fori_loop(0, tm // V7X_SUBLANES, tokens, 0)
        rows = pl.ds(0, tm * TOP_K)
        pltpu.make_async_copy(xs_ref.at[rows], xs_ref.at[rows], sem).wait()

    @pl.when(pl.program_id(0) < n_tiles_p)
    def _():
        scatter(hp_ref)

    @pl.when(pl.program_id(0) == n_tiles_p)
    def _():
        scatter(hs_ref)


def _dispatch(hf_p, dest_p, hf_s, dest_s, tm):
    tile = (V7X_SUBLANES, V7X_LANES)
    hf_p, hf_s = hf_p.reshape(-1, *tile), hf_s.reshape(-1, *tile)
    n_p, n_s = hf_p.shape[0], hf_s.shape[0]
    assert n_p % tm == 0 and n_s <= tm
    n_tiles_p = n_p // tm
    dest3 = jnp.concatenate([dest_p.reshape(n_tiles_p, tm * TOP_K),
                             jnp.pad(dest_s.reshape(1, n_s * TOP_K), ((0, 0), (0, (tm - n_s) * TOP_K)))])
    dest3 = dest3.reshape(n_tiles_p + 1, 1, tm * TOP_K)
    return pl.pallas_call(
        functools.partial(_dispatch_kernel, n_tiles_p=n_tiles_p),
        grid=(n_tiles_p + 1,),
        in_specs=[pl.BlockSpec((1, 1, tm * TOP_K), lambda i: (i, 0, 0), memory_space=pltpu.SMEM),
                  pl.BlockSpec((tm, *tile), lambda i: (jnp.minimum(i, n_tiles_p - 1), 0, 0)),
                  pl.BlockSpec((n_s, *tile), lambda i: (0, 0, 0))],
        out_specs=pl.BlockSpec(memory_space=pl.ANY),
        out_shape=jax.ShapeDtypeStruct(((n_p + n_s) * TOP_K, *tile), F32),
        scratch_shapes=[pltpu.SemaphoreType.DMA(())],
        compiler_params=pltpu.CompilerParams(dimension_semantics=("arbitrary",)),
        name="dispatch",
    )(dest3, hf_p, hf_s)


def _moe_kernel(vt_ref, ve_ref, nx_ref, lo_ref, hi_ref, x_ref, w1_ref, b1_ref, w2_ref, b2_ref, y_ref,
                w1s, w2s, w1b, w2b, sem, *, d_ff):
    v = pl.program_id(0)
    lo, hi = lo_ref[v], hi_ref[v]
    prev = jnp.maximum(v - 1, 0)
    first = jnp.logical_or(v == 0, vt_ref[v] != vt_ref[prev])

    def fetch(e):
        return (pltpu.make_async_copy(w1_ref.at[e], w1s, sem.at[0]),
                pltpu.make_async_copy(w2_ref.at[e], w2s, sem.at[1]))

    @pl.when(v == 0)
    def _():
        for cp in fetch(ve_ref[0]):
            cp.start()

    @pl.when(jnp.logical_or(v == 0, ve_ref[v] != ve_ref[prev]))
    def _():
        for cp in fetch(ve_ref[v]):
            cp.wait()
        w1b[...] = w1s[...].astype(jnp.bfloat16)
        w2b[...] = w2s[...].astype(jnp.bfloat16)

        @pl.when(nx_ref[v] >= 0)
        def _():
            for cp in fetch(nx_ref[v]):
                cp.start()

    def swiglu(x_tiles):
        x = _from_token_tiles(x_tiles).astype(jnp.bfloat16)
        a = jnp.dot(x, w1b[...], preferred_element_type=F32) + b1_ref[0]
        x_glu = jnp.minimum(a[:, :d_ff], SWIGLU_LIMIT)
        x_lin = jnp.clip(a[:, d_ff:], -SWIGLU_LIMIT, SWIGLU_LIMIT)
        mid = x_glu * jax.nn.sigmoid(SWIGLU_ALPHA * x_glu) * (x_lin + 1.0)
        return jnp.dot(mid.astype(jnp.bfloat16), w2b[...], preferred_element_type=F32) + b2_ref[0]

    whole = jnp.logical_and(lo == 0, hi == MOE_TILE)

    @pl.when(whole)
    def _():
        _to_token_tiles(y_ref, swiglu(x_ref))

    for r0 in range(0, MOE_TILE, MOE_SUB):
        sub = pl.ds(r0 * V7X_SUBLANES, MOE_SUB * V7X_SUBLANES)
        x_sub, y_sub = x_ref.at[sub, :], y_ref.at[sub, :]
        touched = jnp.logical_and(jnp.logical_not(whole), jnp.logical_and(lo < r0 + MOE_SUB, hi > r0))

        @pl.when(touched)
        def _(r0=r0, x_sub=x_sub, y_sub=y_sub):
            y = swiglu(x_sub)
            rows = r0 + lax.broadcasted_iota(I32, (MOE_SUB, 1), 0)
            mine = (rows >= lo) & (rows < hi)

            @pl.when(first)
            def _():
                _to_token_tiles(y_sub, jnp.where(mine, y, 0.0))

            @pl.when(jnp.logical_not(first))
            def _():
                _to_token_tiles(y_sub, jnp.where(mine, y, _from_token_tiles(y_sub)))

        @pl.when(jnp.logical_and(first, jnp.logical_not(jnp.logical_or(touched, whole))))
        def _(y_sub=y_sub):
            y_sub[...] = jnp.zeros(y_sub.shape, F32)


def _moe_experts(xs, visits, w1, b1, w2, b2):
    ne, d, ff2 = w1.shape
    d_ff = ff2 // 2
    n_visits = visits[0].shape[0]
    row = lambda v, vt, ve, nx, lo, hi: (vt[v], 0)
    exp = lambda v, vt, ve, nx, lo, hi: (ve[v], 0, 0)
    vmem = (4 * 2 * 2 * MOE_TILE * d + (4 + 2) * (d * ff2 + d_ff * d) + 2 * 4 * (ff2 + d)
            + 4 * 4 * MOE_TILE * ff2)
    tiles = pl.BlockSpec((MOE_TILE * V7X_SUBLANES, V7X_LANES), row)
    grid_spec = pltpu.PrefetchScalarGridSpec(
        num_scalar_prefetch=5,
        grid=(n_visits,),
        in_specs=[tiles,
                  pl.BlockSpec(memory_space=pl.ANY), pl.BlockSpec((1, 1, ff2), exp),
                  pl.BlockSpec(memory_space=pl.ANY), pl.BlockSpec((1, 1, d), exp)],
        out_specs=tiles,
        scratch_shapes=[pltpu.VMEM((d, ff2), F32), pltpu.VMEM((d_ff, d), F32),
                        pltpu.VMEM((d, ff2), jnp.bfloat16), pltpu.VMEM((d_ff, d), jnp.bfloat16),
                        pltpu.SemaphoreType.DMA((2,))],
    )
    return pl.pallas_call(
        functools.partial(_moe_kernel, d_ff=d_ff),
        grid_spec=grid_spec,
        out_shape=jax.ShapeDtypeStruct(xs.shape, F32),
        compiler_params=pltpu.CompilerParams(dimension_semantics=("arbitrary",),
                                             vmem_limit_bytes=_vmem_limit(vmem)),
        name="moe",
    )(*visits, xs, w1, b1.reshape(ne, 1, ff2), w2, b2.reshape(ne, 1, d))


def _moe_visits(sizes, n_rows):
    n_tiles = n_rows // MOE_TILE
    n_visits = n_tiles + N_EXPERTS - 1
    ends = jnp.cumsum(sizes)
    starts = ends - sizes
    first_tile = starts // MOE_TILE
    n_vis = jnp.where(sizes > 0, (ends - 1) // MOE_TILE - first_tile + 1, 0)
    vis_end = jnp.cumsum(n_vis)
    total = vis_end[-1]
    v = jnp.clip(jnp.arange(n_visits, dtype=I32), 0, jnp.maximum(total - 1, 0))
    e = jnp.sum((vis_end[None, :] <= v[:, None]).astype(I32), axis=1)
    pick = lambda tab: jnp.sum(jnp.where(e[:, None] == jnp.arange(N_EXPERTS)[None, :], tab[None, :], 0), axis=1)
    tile = pick(first_tile) + v - pick(vis_end - n_vis)
    lo = jnp.maximum(pick(starts), tile * MOE_TILE) - tile * MOE_TILE
    hi = jnp.minimum(pick(ends), (tile + 1) * MOE_TILE) - tile * MOE_TILE
    live = jnp.arange(n_visits) < total
    ids = jnp.arange(N_EXPERTS, dtype=I32)
    later = jnp.where((ids[None, :] > ids[:, None]) & (sizes[None, :] > 0), ids[None, :], N_EXPERTS)
    nxt = jnp.min(later, axis=1)
    nxt = pick(jnp.where(nxt < N_EXPERTS, nxt, -1))
    return (tile.astype(I32), e.astype(I32), nxt.astype(I32), jnp.where(live, lo, 0).astype(I32),
            jnp.where(live, hi, 0).astype(I32))


def _final_kernel(dest_ref, nxt_ref, x_ref, tg_ref, g_ref, ys_ref, y_ref, buf, sem, *, tm):
    i = pl.program_id(0)
    n = pl.num_programs(0)

    def gather(d_ref, slot):
        def tokens(i8, carry):
            for u in range(V7X_SUBLANES):
                r = i8 * V7X_SUBLANES + u
                tile = pl.ds(pl.multiple_of(r * V7X_SUBLANES, V7X_SUBLANES), V7X_SUBLANES)
                for k in range(TOP_K):
                    pltpu.make_async_copy(ys_ref.at[d_ref[0, 0, r * TOP_K + k]], buf.at[slot, k, tile, :],
                                          sem.at[slot]).start(priority=k % 2)
            return carry
        lax.fori_loop(0, tm // V7X_SUBLANES, tokens, 0)

    @pl.when(i == 0)
    def _():
        gather(dest_ref, 0)

    for slot in range(2):
        @pl.when(jnp.logical_and(i + 1 < n, (i + 1) % 2 == slot))
        def _(slot=slot):
            gather(nxt_ref, slot)

    for slot in range(2):
        @pl.when(i % 2 == slot)
        def _(slot=slot):
            pltpu.make_async_copy(buf.at[slot], buf.at[slot], sem.at[slot]).wait()
            tg = tg_ref[...]
            parts = []
            for j in range(V7X_SUBLANES):
                f = None
                for k in range(TOP_K):
                    rows = buf[slot, k, pl.ds(j, tm, stride=V7X_SUBLANES), :]
                    f = tg[:, k:k + 1] * rows if f is None else f + tg[:, k:k + 1] * rows
                parts.append(f)
            y_ref[...] = _rms(x_ref[...] + jnp.concatenate(parts, axis=1), g_ref[...])


def _final(x1, tg, dest, ys, g, tm):
    n, d = x1.shape
    n_tiles = n // tm
    tile = (V7X_SUBLANES, V7X_LANES)
    dest3 = dest.reshape(n_tiles, 1, tm * TOP_K)
    row = lambda i: (i, 0)
    smem = lambda index_map: pl.BlockSpec((1, 1, tm * TOP_K), index_map, memory_space=pltpu.SMEM)
    vmem = 4 * (2 * TOP_K * tm * d + 2 * 2 * tm * d + 2 * tm * V7X_LANES + 4 * tm * d)
    return pl.pallas_call(
        functools.partial(_final_kernel, tm=tm),
        grid=(n_tiles,),
        in_specs=[smem(lambda i: (i, 0, 0)), smem(lambda i: (jnp.minimum(i + 1, n_tiles - 1), 0, 0)),
                  pl.BlockSpec((tm, d), row), pl.BlockSpec((tm, V7X_LANES), row),
                  pl.BlockSpec((1, d), lambda i: (0, 0)), pl.BlockSpec(memory_space=pl.ANY)],
        out_specs=pl.BlockSpec((tm, d), row),
        out_shape=jax.ShapeDtypeStruct((n, d), F32),
        scratch_shapes=[pltpu.VMEM((2, TOP_K, tm * V7X_SUBLANES, V7X_LANES), F32), pltpu.SemaphoreType.DMA((2,))],
        compiler_params=pltpu.CompilerParams(dimension_semantics=("arbitrary",),
                                             vmem_limit_bytes=_vmem_limit(vmem)),
        name="final",
    )(dest3, dest3, x1, tg, g.reshape(1, d), ys.reshape(-1, *tile))


def kernel(x_prompt, x_sample, cache_k, cache_v, state_ssm_re, state_ssm_im, norm_mix_g, w_in, attn_out_norm_g, ssm_a_re, ssm_a_im, ssm_log_dt, ssm_b_re, ssm_b_im, ssm_c_re, ssm_c_im, ssm_d, w_glu, b_glu, ssm_out_norm_g, w_out, norm_ffn_g, w_router, b_router, w_moe1, b_moe1, w_moe2, b_moe2, norm_final_g):
    depth = w_in.shape[0]
    assert depth == 1, "single-layer trunk"
    bp, s, d = x_prompt.shape
    bs, t, _ = x_sample.shape
    n_buf = cache_k.shape[2]
    n_heads, head_dim = cache_k.shape[3], cache_k.shape[4]
    assert head_dim == HEAD_DIM
    a = n_heads * head_dim
    g = ssm_a_re.shape[1]
    gn = g * SSM_STATE
    assert s % (max(dd for _, dd in DILATED_GROUPS) * KEYS_BACK) == 0
    np_, ns = bp * s, bs * t
    tm_p, tm_s = 256, ns
    l = 0
    ssm_p = (ssm_a_re[l], ssm_a_im[l], ssm_log_dt[l], ssm_b_re[l], ssm_b_im[l], ssm_c_re[l], ssm_c_im[l],
             ssm_d[l])
    tail = dict(attn_out_norm_g=attn_out_norm_g[l], w_glu=w_glu[l], b_glu=b_glu[l],
                ssm_out_norm_g=ssm_out_norm_g[l], w_out=w_out[l], norm_ffn_g=norm_ffn_g[l],
                w_router=w_router[l], b_router=b_router[l])

    xp = x_prompt.reshape(np_, d)
    qp, kp, vp, up, ktp, vtp = _proj(xp, norm_mix_g[l], w_in[l], a, tm_p, seq=s)
    attn_p = _attn_prompt(qp.reshape(bp, s, a), kp.reshape(bp, s, a), vp.reshape(bp, s, a))
    zeros = jnp.zeros((bp, gn), F32)
    tables = _ssm_tables(*ssm_p, PROMPT_CHUNK)
    y_p, hp_re, hp_im = _ssm_chunks_on_lanes(up.reshape(bp, s, -1), zeros, zeros, _ssm_tables_lanes(tables),
                                             PROMPT_CHUNK)
    seen = jnp.zeros((1, V7X_LANES), F32)
    x1_p, hf_p, ti_p, tg_p, seen = _finish(xp, attn_p.reshape(np_, a), y_p.reshape(np_, -1), tail, seen, tm_p)

    xs = x_sample.reshape(ns, d)
    qs, ks, vs, us = _proj(xs, norm_mix_g[l], w_in[l], a, tm_s)
    by_dim = lambda c: c.transpose(0, 2, 3, 1).reshape(bs, a, n_buf)
    attn_s = _attn_sample(qs.reshape(bs, t, a), ks.reshape(bs, t, a), vs.reshape(bs, t, a),
                          by_dim(cache_k[l]), by_dim(cache_v[l]))
    y_s, hs_re, hs_im = _ssm(us.reshape(bs, t, -1), state_ssm_re[l].reshape(bs, gn),
                             state_ssm_im[l].reshape(bs, gn), _ssm_tables_rows(tables, t), t)
    x1_s, hf_s, ti_s, tg_s, seen = _finish(xs, attn_s.reshape(ns, a), y_s.reshape(ns, -1), tail, seen, tm_s)

    sizes = seen[0, :N_EXPERTS].astype(I32)
    starts = jnp.cumsum(sizes) - sizes
    experts = jnp.arange(N_EXPERTS, dtype=I32)[None, None, :]

    def sorted_row(ti):
        hit = ti[:, :TOP_K, None] == experts
        return jnp.sum(jnp.where(hit, starts[None, None, :], 0), axis=-1) + ti[:, TOP_K:2 * TOP_K]

    dest_p, dest_s = sorted_row(ti_p), sorted_row(ti_s)
    n_rows = (np_ + ns) * TOP_K
    assert n_rows % MOE_TILE == 0
    xs_sorted = _dispatch(hf_p, dest_p, hf_s, dest_s, tm_p).reshape(-1, V7X_LANES)
    ys = _moe_experts(xs_sorted, _moe_visits(sizes, n_rows), w_moe1[l], b_moe1[l], w_moe2[l], b_moe2[l])
    y_prompt = _final(x1_p, tg_p, dest_p, ys, norm_final_g, tm_p).reshape(bp, s, d)
    y_sample = _final(x1_s, tg_s, dest_s, ys, norm_final_g, tm_s).reshape(bs, t, d)

    keep = min(max(w for w, _ in DILATED_GROUPS), s)
    k5 = lambda z, b_, s_: z.reshape(1, b_, s_, n_heads, head_dim)
    by_pos = lambda zt: zt.reshape(bp, n_heads, head_dim, s).transpose(0, 3, 1, 2)[None]
    st = lambda z, b_: z.reshape(1, b_, g, SSM_STATE)
    return (y_prompt, y_sample,
            by_pos(ktp)[:, :, s - keep:], by_pos(vtp)[:, :, s - keep:], st(hp_re, bp), st(hp_im, bp),
            k5(ks, bs, t), k5(vs, bs, t), st(hs_re, bs), st(hs_im, bs))
```

```python
import functools
import math

import numpy as np
import jax
import jax.numpy as jnp
from jax import lax
from jax.experimental import pallas as pl
from jax.experimental.pallas import tpu as pltpu

F32 = jnp.float32
I32 = jnp.int32

V7X_LANES = 128
V7X_SUBLANES = 8
V7X_VMEM_BYTES = 64 * 1024 * 1024

HEAD_DIM = 64
HEADS_PER_LANE_TILE = V7X_LANES // HEAD_DIM
DILATED_GROUPS = ((128, 1), (512, 4), (2048, 16))
KEYS_BACK = 128
SSM_GROUP = 16
SSM_STATE = 64
N_EXPERTS = 32
TOP_K = 4
SWIGLU_LIMIT = 7.0
SWIGLU_ALPHA = 1.702
RMS_EPS = 1e-5
NEG_INF = -1e30
PROMPT_CHUNK = 16
MOE_TILE = 512
MOE_SUB = 256


def _vmem_limit(nbytes):
    return int(min(nbytes + (8 << 20), V7X_VMEM_BYTES - (8 << 20)))


def _rms(x, g):
    return x * lax.rsqrt(jnp.mean(x * x, axis=-1, keepdims=True) + RMS_EPS) * g


def _to_token_tiles(ref, val):
    rows, width = val.shape
    assert width == V7X_SUBLANES * V7X_LANES and ref.shape == (rows * V7X_SUBLANES, V7X_LANES)
    for j in range(V7X_SUBLANES):
        ref[pl.ds(j, rows, stride=V7X_SUBLANES), :] = val[:, j * V7X_LANES:(j + 1) * V7X_LANES]


def _from_token_tiles(ref):
    rows = ref.shape[0] // V7X_SUBLANES
    return jnp.concatenate([ref[pl.ds(j, rows, stride=V7X_SUBLANES), :] for j in range(V7X_SUBLANES)], axis=1)


def _proj_kernel(x_ref, g_ref, w_ref, q_ref, k_ref, v_ref, u_ref, *kv_t, attn_width, q_scale):
    h = _rms(x_ref[...], g_ref[...])
    z = jnp.dot(h, w_ref[...], preferred_element_type=F32)
    a = attn_width
    q_ref[...] = z[:, :a] * q_scale
    k_ref[...] = z[:, a:2 * a]
    v_ref[...] = z[:, 2 * a:3 * a]
    u_ref[...] = z[:, 3 * a:]
    if kv_t:
        kt_ref, vt_ref = kv_t
        kt_ref[...] = z[:, a:2 * a].T
        vt_ref[...] = z[:, 2 * a:3 * a].T


def _proj(x2d, norm_g, w_in, attn_width, tm, seq=None):
    n, d = x2d.shape
    mix = w_in.shape[1]
    ssm_width = mix - 3 * attn_width
    row = lambda i: (i, 0)
    const = lambda i: (0, 0)
    out_shape = (jax.ShapeDtypeStruct((n, attn_width), F32),) * 3 + (
        jax.ShapeDtypeStruct((n, ssm_width), F32),)
    out_specs = (pl.BlockSpec((tm, attn_width), row),) * 3 + (pl.BlockSpec((tm, ssm_width), row),)
    if seq is not None:
        assert seq % tm == 0 and tm % V7X_LANES == 0
        per_seq = seq // tm
        out_shape += (jax.ShapeDtypeStruct((n // seq, attn_width, seq), F32),) * 2
        out_specs += (pl.BlockSpec((None, attn_width, tm), lambda i: (i // per_seq, 0, i % per_seq)),) * 2
    vmem = 2 * 4 * (tm * d + d * mix + tm * mix + 2 * tm * attn_width) + 4 * tm * mix
    return pl.pallas_call(
        functools.partial(_proj_kernel, attn_width=attn_width, q_scale=HEAD_DIM ** -0.5),
        grid=(n // tm,),
        in_specs=[pl.BlockSpec((tm, d), row), pl.BlockSpec((1, d), const),
                  pl.BlockSpec((d, mix), const)],
        out_specs=out_specs,
        out_shape=out_shape,
        compiler_params=pltpu.CompilerParams(dimension_semantics=("arbitrary",),
                                             vmem_limit_bytes=_vmem_limit(vmem)),
        name="proj",
    )(x2d, norm_g.reshape(1, d), w_in)


def _attn_prompt_kernel(q_ref, k_ref, v_ref, o_ref, *scratch, seq, unroll):
    nb = KEYS_BACK
    n_groups = len(DILATED_GROUPS)
    m_sc, l_sc, acc_sc = scratch[:n_groups], scratch[n_groups:2 * n_groups], scratch[2 * n_groups:]
    lane = lax.broadcasted_iota(I32, (1, V7X_LANES), 1)
    head_a = lane < HEAD_DIM
    qi = lax.broadcasted_iota(I32, (nb, 2 * nb), 0)
    kj = lax.broadcasted_iota(I32, (nb, 2 * nb), 1)
    rel = nb + qi - kj
    band = (rel >= 0) & (rel <= nb)

    def rows(ref, start, d):
        if d == 1:
            return ref[pl.ds(pl.multiple_of(start, nb), nb), :]
        return ref[pl.ds(start, nb, stride=d), :]

    def put(ref, start, d, val):
        if d == 1:
            ref[pl.ds(pl.multiple_of(start, nb), nb), :] = val
        else:
            ref[pl.ds(start, nb, stride=d), :] = val

    for gi, (w, d) in enumerate(DILATED_GROUPS):
        assert w // d == nb
        nblk = seq // (d * nb)

        def block(t, gi=gi, d=d, nblk=nblk):
            r = t // nblk
            n = t % nblk
            cur = r + d * nb * n
            prev = r + d * nb * jnp.maximum(n - 1, 0)
            q = rows(q_ref, cur, d)
            kk = jnp.concatenate([rows(k_ref, prev, d), rows(k_ref, cur, d)], axis=0)
            vv = jnp.concatenate([rows(v_ref, prev, d), rows(v_ref, cur, d)], axis=0)
            qs = jnp.concatenate([jnp.where(head_a, q, 0.0), jnp.where(head_a, 0.0, q)], axis=0)
            s = lax.dot_general(qs, kk, (((1,), (1,)), ((), ())), preferred_element_type=F32)
            valid = band & (kj >= jnp.where(n > 0, 0, nb))
            s = jnp.where(jnp.concatenate([valid, valid], axis=0), s, NEG_INF)
            m = jnp.max(s, axis=-1, keepdims=True)
            p = jnp.exp(s - m)
            l = jnp.sum(p, axis=-1, keepdims=True)
            o = jnp.dot(p, vv, preferred_element_type=F32)
            o = jnp.where(head_a, o[:nb], o[nb:])
            put(m_sc[gi], cur, d, jnp.where(head_a, m[:nb], m[nb:]))
            put(l_sc[gi], cur, d, jnp.where(head_a, l[:nb], l[nb:]))
            put(acc_sc[gi], cur, d, o)

        def blocks(tt, carry, block=block):
            for u in range(unroll):
                block(tt * unroll + u)
            return carry

        lax.fori_loop(0, seq // nb // unroll, blocks, 0)

    def merge(n, carry):
        sl = pl.ds(pl.multiple_of(n * nb, nb), nb)
        ms = [m[sl, :] for m in m_sc]
        m_all = functools.reduce(jnp.maximum, ms)
        cs = [jnp.exp(m - m_all) for m in ms]
        den = functools.reduce(jnp.add, [c * l[sl, :] for c, l in zip(cs, l_sc)])
        num = functools.reduce(jnp.add, [c * acc[sl, :] for c, acc in zip(cs, acc_sc)])
        o_ref[sl, :] = num / den
        return carry

    lax.fori_loop(0, seq // nb, merge, 0, unroll=2)


def _attn_prompt(q, k, v, unroll=16):
    b, s, a = q.shape
    n_scratch = 3 * len(DILATED_GROUPS)
    spec = pl.BlockSpec((None, s, V7X_LANES), lambda i, j: (i, 0, j))
    vmem = 4 * s * V7X_LANES * (2 * 4 + n_scratch)
    return pl.pallas_call(
        functools.partial(_attn_prompt_kernel, seq=s, unroll=unroll),
        grid=(b, a // V7X_LANES),
        in_specs=[spec, spec, spec],
        out_specs=spec,
        out_shape=jax.ShapeDtypeStruct((b, s, a), F32),
        scratch_shapes=[pltpu.VMEM((s, V7X_LANES), F32)] * n_scratch,
        compiler_params=pltpu.CompilerParams(dimension_semantics=("arbitrary", "arbitrary"),
                                             vmem_limit_bytes=_vmem_limit(vmem)),
        name="attn_prompt",
    )(q, k, v)


def _sample_key_counts(n_buf, n_new):
    cnt = np.zeros((n_new, n_buf + n_new), np.float32)
    for w, d in DILATED_GROUPS:
        for i in range(n_new):
            for j in range(w // d + 1):
                idx = n_buf + i - d * j
                if idx >= 0:
                    cnt[i, idx] += 1.0
    return cnt[:, :n_buf], cnt[:, n_buf:]


def _attn_sample_kernel(q_ref, kn_ref, vn_ref, kt_ref, vt_ref, cnt_ref, o_ref, *, n_heads, cnt_new):
    n_new, a = q_ref.shape
    rows = n_new * n_heads
    lane_head = lax.broadcasted_iota(I32, (rows, a), 1) // HEAD_DIM
    row_head = lax.broadcasted_iota(I32, (rows, a), 0) % n_heads
    row_query = lax.broadcasted_iota(I32, (rows, 1), 0) // n_heads
    own = lane_head == row_head
    q = q_ref[...]
    qb = jnp.where(own, jnp.broadcast_to(q[:, None, :], (n_new, n_heads, a)).reshape(rows, a), 0.0)
    s_buf = jnp.dot(qb, kt_ref[...], preferred_element_type=F32)
    cnt = cnt_ref[...]
    cnt = jnp.broadcast_to(cnt[:, None, :], (n_new, n_heads, cnt.shape[-1])).reshape(rows, -1)
    valid = cnt > 0.0
    m = jnp.max(jnp.where(valid, s_buf, NEG_INF), axis=-1, keepdims=True)
    s_new, c_new = [], []
    for j in range(n_new):
        c = functools.reduce(lambda acc, i: jnp.where(row_query == i, float(cnt_new[i, j]), acc),
                             range(n_new), jnp.zeros((rows, 1), F32))
        s = jnp.sum(qb * kn_ref[pl.ds(j, 1), :], axis=-1, keepdims=True)
        m = jnp.maximum(m, jnp.where(c > 0.0, s, NEG_INF))
        s_new.append(s)
        c_new.append(c)
    p = jnp.where(valid, cnt * jnp.exp(s_buf - m), 0.0)
    den = jnp.sum(p, axis=-1, keepdims=True)
    o = lax.dot_general(p, vt_ref[...], (((1,), (1,)), ((), ())), preferred_element_type=F32)
    for j in range(n_new):
        pj = jnp.where(c_new[j] > 0.0, c_new[j] * jnp.exp(s_new[j] - m), 0.0)
        den = den + pj
        o = o + pj * vn_ref[pl.ds(j, 1), :]
    o = jnp.where(own, o / den, 0.0)
    o_ref[...] = jnp.sum(o.reshape(n_new, n_heads, a), axis=1)


def _attn_sample(q, k_new, v_new, cache_kt, cache_vt):
    b, t, a = q.shape
    n_buf = cache_kt.shape[-1]
    n_heads = a // HEAD_DIM
    cnt_buf, cnt_new = _sample_key_counts(n_buf, t)
    new = pl.BlockSpec((None, t, a), lambda i: (i, 0, 0))
    buf = pl.BlockSpec((None, a, n_buf), lambda i: (i, 0, 0))
    vmem = 4 * (2 * 2 * n_buf * a + 8 * t * n_heads * n_buf)
    return pl.pallas_call(
        functools.partial(_attn_sample_kernel, n_heads=n_heads, cnt_new=cnt_new),
        grid=(b,),
        in_specs=[new, new, new, buf, buf, pl.BlockSpec((t, n_buf), lambda i: (0, 0))],
        out_specs=new,
        out_shape=jax.ShapeDtypeStruct((b, t, a), F32),
        compiler_params=pltpu.CompilerParams(dimension_semantics=("arbitrary",),
                                             vmem_limit_bytes=_vmem_limit(vmem)),
        name="attn_sample",
    )(q, k_new, v_new, cache_kt, cache_vt, jnp.asarray(cnt_buf))


def _block_toeplitz_kernel(k_ref, toep_ref, tt_ref, *, chunk):
    c = k_ref.shape[1]
    for i in range(k_ref.shape[0]):
        k = k_ref[i]
        for s in range(chunk):
            shifted = k if s == 0 else jnp.concatenate(
                [jnp.zeros((c, s * c), F32), k[:, :(chunk - s) * c]], axis=1)
            toep_ref[i, pl.ds(s * c, c), :] = shifted
        tt_ref[i] = toep_ref[i].T


def _block_toeplitz(kern, chunk):
    g, c, lc = kern.shape
    per_step = 4
    spec = pl.BlockSpec((per_step, lc, lc), lambda i: (i, 0, 0))
    return pl.pallas_call(
        functools.partial(_block_toeplitz_kernel, chunk=chunk),
        grid=(g // per_step,),
        in_specs=[pl.BlockSpec((per_step, c, lc), lambda i: (i, 0, 0))],
        out_specs=(spec, spec),
        out_shape=(jax.ShapeDtypeStruct((g, lc, lc), F32),) * 2,
        compiler_params=pltpu.CompilerParams(dimension_semantics=("arbitrary",)),
        name="ssm_toeplitz",
    )(kern)


def _pad_states(m, g):
    half = (np.arange(g) % 2)[:, None, None]
    z = jnp.zeros_like(m)
    return jnp.where(half == 0, jnp.concatenate([m, z], -1), jnp.concatenate([z, m], -1))


def _ssm_tables(a_re, a_im, log_dt, b_re, b_im, c_re, c_im, d_skip, chunk):
    g, n = a_re.shape
    c = b_re.shape[-1]
    lc = chunk * c
    dt = jnp.exp(log_dt)[:, None]
    x, y = a_re * dt, a_im * dt
    ex = jnp.exp(x)
    ar, ai = ex * jnp.cos(y), ex * jnp.sin(y)
    nr = jnp.expm1(x) * jnp.cos(y) - 2.0 * jnp.sin(0.5 * y) ** 2
    ni = ai
    den = a_re * a_re + a_im * a_im
    fr = (nr * a_re + ni * a_im) / den
    fi = (ni * a_re - nr * a_im) / den
    b_re_t, b_im_t = b_re.transpose(0, 2, 1), b_im.transpose(0, 2, 1)
    bbr = fr[:, None, :] * b_re_t - fi[:, None, :] * b_im_t
    bbi = fr[:, None, :] * b_im_t + fi[:, None, :] * b_re_t
    pr, pi = [jnp.ones_like(ar)], [jnp.zeros_like(ar)]
    for _ in range(chunk):
        pr.append(pr[-1] * ar - pi[-1] * ai)
        pi.append(pr[-2] * ai + pi[-1] * ar)
    pr, pi = jnp.stack(pr, 1), jnp.stack(pi, 1)
    cpr = c_re[:, None] * pr[:, :, None, :] - c_im[:, None] * pi[:, :, None, :]
    cpi = c_re[:, None] * pi[:, :, None, :] + c_im[:, None] * pr[:, :, None, :]
    kern = jnp.sum(cpr[:, None, :chunk] * bbr[:, :, None, None] - cpi[:, None, :chunk] * bbi[:, :, None, None],
                   axis=-1).reshape(g, c, lc)
    toep, tt = _block_toeplitz(kern, chunk)
    qr, qi = pr[:, chunk - 1::-1][:, :, None, :], pi[:, chunk - 1::-1][:, :, None, :]
    wr = (qr * bbr[:, None] - qi * bbi[:, None]).reshape(g, lc, n)
    wi = (qr * bbi[:, None] + qi * bbr[:, None]).reshape(g, lc, n)
    vr = cpr[:, 1:].reshape(g, lc, n)
    vi = -cpi[:, 1:].reshape(g, lc, n)
    return dict(chunk=chunk, toep=toep, tt=tt, wr=wr, wi=wi, vr=vr, vi=vi, d=d_skip, pr=pr, pi=pi)


def _ssm_tables_lanes(tab):
    g, chunk = tab["d"].shape[0], tab["chunk"]
    tr = lambda m: m.transpose(0, 2, 1)
    return dict(tt=tab["tt"], wt_re=tr(_pad_states(tab["wr"], g)), wt_im=tr(_pad_states(tab["wi"], g)),
                v_re=_pad_states(tab["vr"], g), v_im=_pad_states(tab["vi"], g),
                dcol=jnp.tile(tab["d"], (1, chunk))[:, :, None],
                al_re=tab["pr"][:, chunk].reshape(1, -1), al_im=tab["pi"][:, chunk].reshape(1, -1))


def _ssm_tables_rows(tab, chunk):
    g, c = tab["d"].shape
    big, lc = tab["chunk"], chunk * c
    tr = lambda m: m.transpose(0, 2, 1)
    tail = slice((big - chunk) * c, big * c)
    return dict(toep=tab["toep"][:, :lc, :lc],
                wr=_pad_states(tab["wr"][:, tail], g), wi=_pad_states(tab["wi"][:, tail], g),
                vr=tr(_pad_states(tab["vr"][:, :lc], g)), vi=tr(_pad_states(tab["vi"][:, :lc], g)),
                d=jnp.tile(tab["d"], (1, chunk))[:, None, :],
                al_re=tab["pr"][:, chunk].reshape(1, -1), al_im=tab["pi"][:, chunk].reshape(1, -1))


def _ssm_state_kernel(u_ref, wr_ref, wi_ref, er_ref, ei_ref):
    er = jnp.dot(u_ref[0], wr_ref[0], preferred_element_type=F32)
    ei = jnp.dot(u_ref[0], wi_ref[0], preferred_element_type=F32)
    er_ref[...] = er + jnp.dot(u_ref[1], wr_ref[1], preferred_element_type=F32)
    ei_ref[...] = ei + jnp.dot(u_ref[1], wi_ref[1], preferred_element_type=F32)


def _ssm_scan_kernel(er_ref, ei_ref, ar_ref, ai_ref, hr_ref, hi_ref, xr_ref, xi_ref, lr_ref, li_ref,
                     *, n_chunks, nb):
    shape = hr_ref.shape
    ar = jnp.broadcast_to(ar_ref[...], shape)
    ai = jnp.broadcast_to(ai_ref[...], shape)

    def step(j, carry):
        xr, xi = carry
        sl = pl.ds(pl.multiple_of(j * nb, nb), nb)
        xr_ref[sl, :] = xr
        xi_ref[sl, :] = xi
        return (ar * xr - ai * xi + er_ref[sl, :], ar * xi + ai * xr + ei_ref[sl, :])

    xr, xi = lax.fori_loop(0, n_chunks, step, (hr_ref[...], hi_ref[...]))
    lr_ref[...] = xr
    li_ref[...] = xi


def _ssm_out_kernel(u_ref, t_ref, d_ref, xr_ref, xi_ref, vr_ref, vi_ref, y_ref):
    xr = xr_ref[...]
    xi = xi_ref[...]
    for a in range(2):
        u = u_ref[a]
        y = jnp.dot(u, t_ref[a], preferred_element_type=F32) + d_ref[a] * u
        y = y + jnp.dot(xr, vr_ref[a], preferred_element_type=F32)
        y_ref[a] = y + jnp.dot(xi, vi_ref[a], preferred_element_type=F32)


def _ssm_scan(er, ei, tab, h0_re, h0_im, n_chunks):
    r, gn = er.shape
    b = r // n_chunks
    lanes = 4 * V7X_LANES
    col = lambda i: (0, i)
    return pl.pallas_call(
        functools.partial(_ssm_scan_kernel, n_chunks=n_chunks, nb=b),
        grid=(gn // lanes,),
        in_specs=[pl.BlockSpec((r, lanes), col), pl.BlockSpec((r, lanes), col),
                  pl.BlockSpec((1, lanes), col), pl.BlockSpec((1, lanes), col),
                  pl.BlockSpec((b, lanes), col), pl.BlockSpec((b, lanes), col)],
        out_specs=(pl.BlockSpec((r, lanes), col),) * 2 + (pl.BlockSpec((b, lanes), col),) * 2,
        out_shape=(jax.ShapeDtypeStruct((r, gn), F32),) * 2 + (jax.ShapeDtypeStruct((b, gn), F32),) * 2,
        compiler_params=pltpu.CompilerParams(dimension_semantics=("arbitrary",),
                                             vmem_limit_bytes=_vmem_limit(2 * 4 * 4 * r * lanes)),
        name="ssm_scan",
    )(er, ei, tab["al_re"], tab["al_im"], h0_re, h0_im)


def _ssm_state_t_kernel(a_ref, wr_ref, wi_ref, er_ref, ei_ref):
    def seq(b, carry):
        a0, a1 = a_ref[0, b], a_ref[1, b]
        er_ref[b] = (jnp.dot(wr_ref[0], a0, preferred_element_type=F32)
                     + jnp.dot(wr_ref[1], a1, preferred_element_type=F32))
        ei_ref[b] = (jnp.dot(wi_ref[0], a0, preferred_element_type=F32)
                     + jnp.dot(wi_ref[1], a1, preferred_element_type=F32))
        return carry

    lax.fori_loop(0, a_ref.shape[1], seq, 0)


def _ssm_out_t_kernel(a_ref, t_ref, d_ref, xr_ref, xi_ref, vr_ref, vi_ref, y_ref):
    def seq(b, carry):
        xr, xi = xr_ref[b], xi_ref[b]
        for a in range(2):
            at = a_ref[a, b]
            y = jnp.dot(t_ref[a], at, preferred_element_type=F32) + d_ref[a] * at
            y = y + jnp.dot(vr_ref[a], xr, preferred_element_type=F32)
            y_ref[a, b] = y + jnp.dot(vi_ref[a], xi, preferred_element_type=F32)
        return carry

    lax.fori_loop(0, a_ref.shape[1], seq, 0)


def _ssm_chunks_on_lanes(u, h0_re, h0_im, tab, chunk):
    b, s, width = u.shape
    g = width // SSM_GROUP
    n_chunks = s // chunk
    lc = chunk * SSM_GROUP
    assert n_chunks == V7X_LANES
    at = (u.reshape(b, n_chunks, chunk * width).transpose(0, 2, 1)
          .reshape(b, chunk, g, SSM_GROUP, n_chunks).transpose(2, 0, 1, 3, 4).reshape(g, b, lc, n_chunks))
    pair4 = lambda i: (i, 0, 0, 0)
    pair3 = lambda i: (i, 0, 0)
    cp = lambda vmem: pltpu.CompilerParams(dimension_semantics=("arbitrary",),
                                           vmem_limit_bytes=_vmem_limit(vmem))
    a_spec = pl.BlockSpec((2, b, lc, n_chunks), pair4)
    st_spec = pl.BlockSpec((None, b, V7X_LANES, n_chunks), pair4)
    w_spec = pl.BlockSpec((2, V7X_LANES, lc), pair3)
    a_bytes = 2 * b * lc * n_chunks
    st_bytes = b * V7X_LANES * n_chunks
    et_re, et_im = pl.pallas_call(
        _ssm_state_t_kernel,
        grid=(g // 2,),
        in_specs=[a_spec, w_spec, w_spec],
        out_specs=(st_spec,) * 2,
        out_shape=(jax.ShapeDtypeStruct((g // 2, b, V7X_LANES, n_chunks), F32),) * 2,
        compiler_params=cp(2 * 4 * (a_bytes + 2 * st_bytes + 4 * V7X_LANES * lc)),
        name="ssm_state_t",
    )(at, tab["wt_re"], tab["wt_im"])
    rows = lambda m: m.transpose(3, 1, 0, 2).reshape(n_chunks * b, g * SSM_STATE)
    xr, xi, lr, li = _ssm_scan(rows(et_re), rows(et_im), tab, h0_re, h0_im, n_chunks)
    cols = lambda m: m.reshape(n_chunks, b, g // 2, V7X_LANES).transpose(2, 1, 3, 0)
    yt = pl.pallas_call(
        _ssm_out_t_kernel,
        grid=(g // 2,),
        in_specs=[a_spec, pl.BlockSpec((2, lc, lc), pair3), pl.BlockSpec((2, lc, 1), pair3),
                  st_spec, st_spec, pl.BlockSpec((2, lc, V7X_LANES), pair3),
                  pl.BlockSpec((2, lc, V7X_LANES), pair3)],
        out_specs=a_spec,
        out_shape=jax.ShapeDtypeStruct((g, b, lc, n_chunks), F32),
        compiler_params=cp(2 * 4 * (2 * a_bytes + 2 * st_bytes + 2 * lc * lc + 6 * V7X_LANES * lc)),
        name="ssm_out_t",
    )(at, tab["tt"], tab["dcol"], cols(xr), cols(xi), tab["v_re"], tab["v_im"])
    y = (yt.reshape(g, b, chunk, SSM_GROUP, n_chunks).transpose(1, 2, 0, 3, 4)
         .reshape(b, chunk * width, n_chunks).transpose(0, 2, 1).reshape(b, s, width))
    return y, lr, li


def _ssm(u, h0_re, h0_im, tab, chunk):
    b, s, width = u.shape
    g = width // SSM_GROUP
    n_chunks = s // chunk
    lc = chunk * SSM_GROUP
    r = n_chunks * b
    gn = g * SSM_STATE
    ut = u.reshape(b, n_chunks, chunk, g, SSM_GROUP).transpose(3, 1, 0, 2, 4).reshape(g, r, lc)
    pair = lambda i: (i, 0, 0)
    cp = lambda vmem: pltpu.CompilerParams(dimension_semantics=("arbitrary",),
                                           vmem_limit_bytes=_vmem_limit(vmem))
    er, ei = pl.pallas_call(
        _ssm_state_kernel,
        grid=(g // 2,),
        in_specs=[pl.BlockSpec((2, r, lc), pair), pl.BlockSpec((2, lc, V7X_LANES), pair),
                  pl.BlockSpec((2, lc, V7X_LANES), pair)],
        out_specs=(pl.BlockSpec((r, V7X_LANES), lambda i: (0, i)),) * 2,
        out_shape=(jax.ShapeDtypeStruct((r, gn), F32),) * 2,
        compiler_params=cp(2 * 4 * (2 * r * lc + 4 * lc * V7X_LANES + 2 * r * V7X_LANES)),
        name="ssm_state",
    )(ut, tab["wr"], tab["wi"])
    xr, xi, lr, li = _ssm_scan(er, ei, tab, h0_re, h0_im, n_chunks)
    y = pl.pallas_call(
        _ssm_out_kernel,
        grid=(g // 2,),
        in_specs=[pl.BlockSpec((2, r, lc), pair), pl.BlockSpec((2, lc, lc), pair),
                  pl.BlockSpec((2, 1, lc), pair),
                  pl.BlockSpec((r, V7X_LANES), lambda i: (0, i)),
                  pl.BlockSpec((r, V7X_LANES), lambda i: (0, i)),
                  pl.BlockSpec((2, V7X_LANES, lc), pair), pl.BlockSpec((2, V7X_LANES, lc), pair)],
        out_specs=pl.BlockSpec((2, r, lc), pair),
        out_shape=jax.ShapeDtypeStruct((g, r, lc), F32),
        compiler_params=cp(2 * 4 * (4 * r * lc + 2 * lc * lc + 2 * r * V7X_LANES + 4 * V7X_LANES * lc)),
        name="ssm_out",
    )(ut, tab["toep"], tab["d"], xr, xi, tab["vr"], tab["vi"])
    y = y.reshape(g, n_chunks, b, chunk, SSM_GROUP).transpose(2, 1, 3, 0, 4).reshape(b, s, width)
    return y, lr, li


def _finish_kernel(x_ref, attn_ref, ssm_ref, ga_ref, wg_ref, bg_ref, gs_ref, wo_ref, gf_ref,
                   wr_ref, br_ref, base_ref, x1_ref, hf_ref, ti_ref, tg_ref, cnt_ref, seen):
    a = attn_ref.shape[-1]
    tm = x_ref.shape[0]

    @pl.when(pl.program_id(0) == 0)
    def _():
        seen[...] = base_ref[...]

    g = jax.nn.gelu(ssm_ref[...])
    gate = jax.nn.sigmoid(jnp.dot(g, wg_ref[...], preferred_element_type=F32) + bg_ref[...])
    mixed_a = _rms(attn_ref[...], ga_ref[...])
    mixed_s = _rms(g * gate, gs_ref[...])
    x1 = (x_ref[...] + jnp.dot(mixed_a, wo_ref[pl.ds(0, a), :], preferred_element_type=F32)
          + jnp.dot(mixed_s, wo_ref[pl.ds(a, wo_ref.shape[0] - a), :], preferred_element_type=F32))
    x1_ref[...] = x1
    hf = _rms(x1, gf_ref[...])
    _to_token_tiles(hf_ref, hf)
    hf_hi = hf.astype(jnp.bfloat16).astype(F32)
    hf_lo = hf - hf_hi
    small = (jnp.dot(hf_lo, wr_ref[1], preferred_element_type=F32)
             + jnp.dot(hf_hi, wr_ref[1], preferred_element_type=F32)
             + jnp.dot(hf_lo, wr_ref[0], preferred_element_type=F32))
    logits = jnp.dot(hf_hi, wr_ref[0], preferred_element_type=F32) + small + br_ref[...]
    lane = lax.broadcasted_iota(I32, logits.shape, 1)
    work = logits
    vals, idxs = [], []
    for _ in range(TOP_K):
        m = jnp.max(work, axis=-1, keepdims=True)
        idx = jnp.min(jnp.where(work == m, lane, V7X_LANES), axis=-1, keepdims=True)
        vals.append(m)
        idxs.append(idx)
        work = jnp.where(lane == idx, -jnp.inf, work)
    exps = [jnp.exp(v - vals[0]) for v in vals]
    den = exps[0]
    for e in exps[1:]:
        den = den + e
    picked = jnp.zeros(logits.shape, F32)
    for k in range(TOP_K):
        picked = picked + jnp.where(lane == idxs[k], 1.0, 0.0)
    tri = jnp.where(lax.broadcasted_iota(I32, (tm, tm), 0) > lax.broadcasted_iota(I32, (tm, tm), 1), 1.0, 0.0)
    before = jnp.dot(tri, picked, preferred_element_type=F32) + seen[...]
    ti = jnp.zeros(logits.shape, I32)
    tg = jnp.zeros(logits.shape, F32)
    for k in range(TOP_K):
        rank = jnp.sum(jnp.where(lane == idxs[k], before, 0.0), axis=-1, keepdims=True)
        ti = jnp.where(lane == k, idxs[k], ti)
        ti = jnp.where(lane == TOP_K + k, rank.astype(I32), ti)
        tg = jnp.where(lane == k, exps[k] / den, tg)
    ti_ref[...] = ti
    tg_ref[...] = tg
    seen[...] = seen[...] + jnp.sum(picked, axis=0, keepdims=True)
    cnt_ref[...] = seen[...]


def _finish(x2d, attn, ssm_y, p, seen, tm):
    n, d = x2d.shape
    a = attn.shape[1]
    sw = ssm_y.shape[1]
    ne = p["w_router"].shape[1]
    w_router = jnp.pad(p["w_router"], ((0, 0), (0, V7X_LANES - ne)))
    w_hi = w_router.astype(jnp.bfloat16).astype(F32)
    w_router = jnp.stack([w_hi, w_router - w_hi])
    b_router = jnp.pad(p["b_router"], (0, V7X_LANES - ne), constant_values=-jnp.inf).reshape(1, V7X_LANES)
    row = lambda i: (i, 0)
    full = lambda arr: pl.BlockSpec(arr.shape, lambda i: (0,) * arr.ndim)
    ins = [x2d, attn, ssm_y, p["attn_out_norm_g"].reshape(1, a), p["w_glu"], p["b_glu"].reshape(1, sw),
           p["ssm_out_norm_g"].reshape(1, sw), p["w_out"], p["norm_ffn_g"].reshape(1, d),
           w_router, b_router, seen]
    in_specs = [pl.BlockSpec((tm, d), row), pl.BlockSpec((tm, a), row), pl.BlockSpec((tm, sw), row)]
    in_specs += [full(arr) for arr in ins[3:]]
    vmem = 2 * 4 * (tm * (3 * d + a + sw + 2 * V7X_LANES) + sw * sw + d * d + 2 * d * V7X_LANES) + 16 * tm * d
    return pl.pallas_call(
        _finish_kernel,
        grid=(n // tm,),
        in_specs=in_specs,
        out_specs=(pl.BlockSpec((tm, d), row), pl.BlockSpec((tm * V7X_SUBLANES, V7X_LANES), row),
                   pl.BlockSpec((tm, V7X_LANES), row), pl.BlockSpec((tm, V7X_LANES), row),
                   pl.BlockSpec((1, V7X_LANES), lambda i: (0, 0))),
        out_shape=(jax.ShapeDtypeStruct((n, d), F32), jax.ShapeDtypeStruct((n * V7X_SUBLANES, V7X_LANES), F32),
                   jax.ShapeDtypeStruct((n, V7X_LANES), I32), jax.ShapeDtypeStruct((n, V7X_LANES), F32),
                   jax.ShapeDtypeStruct((1, V7X_LANES), F32)),
        scratch_shapes=[pltpu.VMEM((1, V7X_LANES), F32)],
        compiler_params=pltpu.CompilerParams(dimension_semantics=("arbitrary",),
                                             vmem_limit_bytes=_vmem_limit(vmem)),
        name="finish",
    )(*ins)


def _tile_copy(src_ref, src_row, dst_ref, dst_row, sem):
    return pltpu.make_async_copy(src_ref.at[src_row], dst_ref.at[dst_row], sem)


def _dispatch_kernel(dest_ref, hp_ref, hs_ref, xs_ref, sem, *, n_tiles_p):
    def scatter(hf_ref):
        tm = hf_ref.shape[0]

        def tokens(i, carry):
            for u in range(V7X_SUBLANES):
                r = i * V7X_SUBLANES + u
                for k in range(TOP_K):
                    _tile_copy(hf_ref, r, xs_ref, dest_ref[0, 0, r * TOP_K + k], sem).start(priority=k % 2)
            return carry

        lax.fori_loop(0, tm // V7X_SUBLANES, tokens, 0)
        rows = pl.ds(0, tm * TOP_K)
        pltpu.make_async_copy(xs_ref.at[rows], xs_ref.at[rows], sem).wait()

    @pl.when(pl.program_id(0) < n_tiles_p)
    def _():
        scatter(hp_ref)

    @pl.when(pl.program_id(0) == n_tiles_p)
    def _():
        scatter(hs_ref)


def _dispatch(hf_p, dest_p, hf_s, dest_s, tm):
    tile = (V7X_SUBLANES, V7X_LANES)
    hf_p, hf_s = hf_p.reshape(-1, *tile), hf_s.reshape(-1, *tile)
    n_p, n_s = hf_p.shape[0], hf_s.shape[0]
    assert n_p % tm == 0 and n_s <= tm
    n_tiles_p = n_p // tm
    dest3 = jnp.concatenate([dest_p.reshape(n_tiles_p, tm * TOP_K),
                             jnp.pad(dest_s.reshape(1, n_s * TOP_K), ((0, 0), (0, (tm - n_s) * TOP_K)))])
    dest3 = dest3.reshape(n_tiles_p + 1, 1, tm * TOP_K)
    return pl.pallas_call(
        functools.partial(_dispatch_kernel, n_tiles_p=n_tiles_p),
        grid=(n_tiles_p + 1,),
        in_specs=[pl.BlockSpec((1, 1, tm * TOP_K), lambda i: (i, 0, 0), memory_space=pltpu.SMEM),
                  pl.BlockSpec((tm, *tile), lambda i: (jnp.minimum(i, n_tiles_p - 1), 0, 0)),
                  pl.BlockSpec((n_s, *tile), lambda i: (0, 0, 0))],
        out_specs=pl.BlockSpec(memory_space=pl.ANY),
        out_shape=jax.ShapeDtypeStruct(((n_p + n_s) * TOP_K, *tile), F32),
        scratch_shapes=[pltpu.SemaphoreType.DMA(())],
        compiler_params=pltpu.CompilerParams(dimension_semantics=("arbitrary",)),
        name="dispatch",
    )(dest3, hf_p, hf_s)


def _moe_kernel(vt_ref, ve_ref, nx_ref, lo_ref, hi_ref, x_ref, w1_ref, b1_ref, w2_ref, b2_ref, y_ref,
                w1s, w2s, w1b, w2b, sem, *, d_ff):
    v = pl.program_id(0)
    lo, hi = lo_ref[v], hi_ref[v]
    prev = jnp.maximum(v - 1, 0)
    first = jnp.logical_or(v == 0, vt_ref[v] != vt_ref[prev])

    def fetch(e):
        return (pltpu.make_async_copy(w1_ref.at[e], w1s, sem.at[0]),
                pltpu.make_async_copy(w2_ref.at[e], w2s, sem.at[1]))

    @pl.when(v == 0)
    def _():
        for cp in fetch(ve_ref[0]):
            cp.start()

    @pl.when(jnp.logical_or(v == 0, ve_ref[v] != ve_ref[prev]))
    def _():
        for cp in fetch(ve_ref[v]):
            cp.wait()
        w1b[...] = w1s[...].astype(jnp.bfloat16)
        w2b[...] = w2s[...].astype(jnp.bfloat16)

        @pl.when(nx_ref[v] >= 0)
        def _():
            for cp in fetch(nx_ref[v]):
                cp.start()

    def swiglu(x_tiles):
        x = _from_token_tiles(x_tiles).astype(jnp.bfloat16)
        a = jnp.dot(x, w1b[...], preferred_element_type=F32) + b1_ref[0]
        x_glu = jnp.minimum(a[:, :d_ff], SWIGLU_LIMIT)
        x_lin = jnp.clip(a[:, d_ff:], -SWIGLU_LIMIT, SWIGLU_LIMIT)
        mid = x_glu * jax.nn.sigmoid(SWIGLU_ALPHA * x_glu) * (x_lin + 1.0)
        return jnp.dot(mid.astype(jnp.bfloat16), w2b[...], preferred_element_type=F32) + b2_ref[0]

    whole = jnp.logical_and(lo == 0, hi == MOE_TILE)

    @pl.when(whole)
    def _():
        _to_token_tiles(y_ref, swiglu(x_ref))

    for r0 in range(0, MOE_TILE, MOE_SUB):
        sub = pl.ds(r0 * V7X_SUBLANES, MOE_SUB * V7X_SUBLANES)
        x_sub, y_sub = x_ref.at[sub, :], y_ref.at[sub, :]
        touched = jnp.logical_and(jnp.logical_not(whole), jnp.logical_and(lo < r0 + MOE_SUB, hi > r0))

        @pl.when(touched)
        def _(r0=r0, x_sub=x_sub, y_sub=y_sub):
            y = swiglu(x_sub)
            rows = r0 + lax.broadcasted_iota(I32, (MOE_SUB, 1), 0)
            mine = (rows >= lo) & (rows < hi)

            @pl.when(first)
            def _():
                _to_token_tiles(y_sub, jnp.where(mine, y, 0.0))

            @pl.when(jnp.logical_not(first))
            def _():
                _to_token_tiles(y_sub, jnp.where(mine, y, _from_token_tiles(y_sub)))

        @pl.when(jnp.logical_and(first, jnp.logical_not(jnp.logical_or(touched, whole))))
        def _(y_sub=y_sub):
            y_sub[...] = jnp.zeros(y_sub.shape, F32)


def _moe_experts(xs, visits, w1, b1, w2, b2):
    ne, d, ff2 = w1.shape
    d_ff = ff2 // 2
    n_visits = visits[0].shape[0]
    row = lambda v, vt, ve, nx, lo, hi: (vt[v], 0)
    exp = lambda v, vt, ve, nx, lo, hi: (ve[v], 0, 0)
    vmem = (4 * 2 * 2 * MOE_TILE * d + (4 + 2) * (d * ff2 + d_ff * d) + 2 * 4 * (ff2 + d)
            + 4 * 4 * MOE_TILE * ff2)
    tiles = pl.BlockSpec((MOE_TILE * V7X_SUBLANES, V7X_LANES), row)
    grid_spec = pltpu.PrefetchScalarGridSpec(
        num_scalar_prefetch=5,
        grid=(n_visits,),
        in_specs=[tiles,
                  pl.BlockSpec(memory_space=pl.ANY), pl.BlockSpec((1, 1, ff2), exp),
                  pl.BlockSpec(memory_space=pl.ANY), pl.BlockSpec((1, 1, d), exp)],
        out_specs=tiles,
        scratch_shapes=[pltpu.VMEM((d, ff2), F32), pltpu.VMEM((d_ff, d), F32),
                        pltpu.VMEM((d, ff2), jnp.bfloat16), pltpu.VMEM((d_ff, d), jnp.bfloat16),
                        pltpu.SemaphoreType.DMA((2,))],
    )
    return pl.pallas_call(
        functools.partial(_moe_kernel, d_ff=d_ff),
        grid_spec=grid_spec,
        out_shape=jax.ShapeDtypeStruct(xs.shape, F32),
        compiler_params=pltpu.CompilerParams(dimension_semantics=("arbitrary",),
                                             vmem_limit_bytes=_vmem_limit(vmem)),
        name="moe",
    )(*visits, xs, w1, b1.reshape(ne, 1, ff2), w2, b2.reshape(ne, 1, d))


def _moe_visits(sizes, n_rows):
    n_tiles = n_rows // MOE_TILE
    n_visits = n_tiles + N_EXPERTS - 1
    ends = jnp.cumsum(sizes)
    starts = ends - sizes
    first_tile = starts // MOE_TILE
    n_vis = jnp.where(sizes > 0, (ends - 1) // MOE_TILE - first_tile + 1, 0)
    vis_end = jnp.cumsum(n_vis)
    total = vis_end[-1]
    v = jnp.clip(jnp.arange(n_visits, dtype=I32), 0, jnp.maximum(total - 1, 0))
    e = jnp.sum((vis_end[None, :] <= v[:, None]).astype(I32), axis=1)
    pick = lambda tab: jnp.sum(jnp.where(e[:, None] == jnp.arange(N_EXPERTS)[None, :], tab[None, :], 0), axis=1)
    tile = pick(first_tile) + v - pick(vis_end - n_vis)
    lo = jnp.maximum(pick(starts), tile * MOE_TILE) - tile * MOE_TILE
    hi = jnp.minimum(pick(ends), (tile + 1) * MOE_TILE) - tile * MOE_TILE
    live = jnp.arange(n_visits) < total
    ids = jnp.arange(N_EXPERTS, dtype=I32)
    later = jnp.where((ids[None, :] > ids[:, None]) & (sizes[None, :] > 0), ids[None, :], N_EXPERTS)
    nxt = jnp.min(later, axis=1)
    nxt = pick(jnp.where(nxt < N_EXPERTS, nxt, -1))
    return (tile.astype(I32), e.astype(I32), nxt.astype(I32), jnp.where(live, lo, 0).astype(I32),
            jnp.where(live, hi, 0).astype(I32))


def _final_kernel(dest_ref, nxt_ref, x_ref, tg_ref, g_ref, ys_ref, y_ref, buf, sem, *, tm):
    i = pl.program_id(0)
    n = pl.num_programs(0)

    def gather(d_ref, slot):
        def tokens(i8, carry):
            for u in range(V7X_SUBLANES):
                r = i8 * V7X_SUBLANES + u
                tile = pl.ds(pl.multiple_of(r * V7X_SUBLANES, V7X_SUBLANES), V7X_SUBLANES)
                for k in range(TOP_K):
                    pltpu.make_async_copy(ys_ref.at[d_ref[0, 0, r * TOP_K + k]], buf.at[slot, k, tile, :],
                                          sem.at[slot]).start(priority=k % 2)
            return carry
        lax.fori_loop(0, tm // V7X_SUBLANES, tokens, 0)

    @pl.when(i == 0)
    def _():
        gather(dest_ref, 0)

    for slot in range(2):
        @pl.when(jnp.logical_and(i + 1 < n, (i + 1) % 2 == slot))
        def _(slot=slot):
            gather(nxt_ref, slot)

    for slot in range(2):
        @pl.when(i % 2 == slot)
        def _(slot=slot):
            pltpu.make_async_copy(buf.at[slot], buf.at[slot], sem.at[slot]).wait()
            tg = tg_ref[...]
            parts = []
            for j in range(V7X_SUBLANES):
                f = None
                for k in range(TOP_K):
                    rows = buf[slot, k, pl.ds(j, tm, stride=V7X_SUBLANES), :]
                    f = tg[:, k:k + 1] * rows if f is None else f + tg[:, k:k + 1] * rows
                parts.append(f)
            y_ref[...] = _rms(x_ref[...] + jnp.concatenate(parts, axis=1), g_ref[...])


def _final(x1, tg, dest, ys, g, tm):
    n, d = x1.shape
    n_tiles = n // tm
    tile = (V7X_SUBLANES, V7X_LANES)
    dest3 = dest.reshape(n_tiles, 1, tm * TOP_K)
    row = lambda i: (i, 0)
    smem = lambda index_map: pl.BlockSpec((1, 1, tm * TOP_K), index_map, memory_space=pltpu.SMEM)
    vmem = 4 * (2 * TOP_K * tm * d + 2 * 2 * tm * d + 2 * tm * V7X_LANES + 4 * tm * d)
    return pl.pallas_call(
        functools.partial(_final_kernel, tm=tm),
        grid=(n_tiles,),
        in_specs=[smem(lambda i: (i, 0, 0)), smem(lambda i: (jnp.minimum(i + 1, n_tiles - 1), 0, 0)),
                  pl.BlockSpec((tm, d), row), pl.BlockSpec((tm, V7X_LANES), row),
                  pl.BlockSpec((1, d), lambda i: (0, 0)), pl.BlockSpec(memory_space=pl.ANY)],
        out_specs=pl.BlockSpec((tm, d), row),
        out_shape=jax.ShapeDtypeStruct((n, d), F32),
        scratch_shapes=[pltpu.VMEM((2, TOP_K, tm * V7X_SUBLANES, V7X_LANES), F32), pltpu.SemaphoreType.DMA((2,))],
        compiler_params=pltpu.CompilerParams(dimension_semantics=("arbitrary",),
                                             vmem_limit_bytes=_vmem_limit(vmem)),
        name="final",
    )(dest3, dest3, x1, tg, g.reshape(1, d), ys.reshape(-1, *tile))


def kernel(x_prompt, x_sample, cache_k, cache_v, state_ssm_re, state_ssm_im, norm_mix_g, w_in, attn_out_norm_g, ssm_a_re, ssm_a_im, ssm_log_dt, ssm_b_re, ssm_b_im, ssm_c_re, ssm_c_im, ssm_d, w_glu, b_glu, ssm_out_norm_g, w_out, norm_ffn_g, w_router, b_router, w_moe1, b_moe1, w_moe2, b_moe2, norm_final_g):
    depth = w_in.shape[0]
    assert depth == 1, "single-layer trunk"
    bp, s, d = x_prompt.shape
    bs, t, _ = x_sample.shape
    n_buf = cache_k.shape[2]
    n_heads, head_dim = cache_k.shape[3], cache_k.shape[4]
    assert head_dim == HEAD_DIM
    a = n_heads * head_dim
    g = ssm_a_re.shape[1]
    gn = g * SSM_STATE
    assert s % (max(dd for _, dd in DILATED_GROUPS) * KEYS_BACK) == 0
    np_, ns = bp * s, bs * t
    tm_p, tm_s = 256, ns
    tm_wide = 2 * tm_p
    l = 0
    ssm_p = (ssm_a_re[l], ssm_a_im[l], ssm_log_dt[l], ssm_b_re[l], ssm_b_im[l], ssm_c_re[l], ssm_c_im[l],
             ssm_d[l])
    tail = dict(attn_out_norm_g=attn_out_norm_g[l], w_glu=w_glu[l], b_glu=b_glu[l],
                ssm_out_norm_g=ssm_out_norm_g[l], w_out=w_out[l], norm_ffn_g=norm_ffn_g[l],
                w_router=w_router[l], b_router=b_router[l])

    xp = x_prompt.reshape(np_, d)
    qp, kp, vp, up, ktp, vtp = _proj(xp, norm_mix_g[l], w_in[l], a, tm_wide, seq=s)
    attn_p = _attn_prompt(qp.reshape(bp, s, a), kp.reshape(bp, s, a), vp.reshape(bp, s, a))
    zeros = jnp.zeros((bp, gn), F32)
    tables = _ssm_tables(*ssm_p, PROMPT_CHUNK)
    y_p, hp_re, hp_im = _ssm_chunks_on_lanes(up.reshape(bp, s, -1), zeros, zeros, _ssm_tables_lanes(tables),
                                             PROMPT_CHUNK)
    seen = jnp.zeros((1, V7X_LANES), F32)
    x1_p, hf_p, ti_p, tg_p, seen = _finish(xp, attn_p.reshape(np_, a), y_p.reshape(np_, -1), tail, seen, tm_p)

    xs = x_sample.reshape(ns, d)
    qs, ks, vs, us = _proj(xs, norm_mix_g[l], w_in[l], a, tm_s)
    by_dim = lambda c: c.transpose(0, 2, 3, 1).reshape(bs, a, n_buf)
    attn_s = _attn_sample(qs.reshape(bs, t, a), ks.reshape(bs, t, a), vs.reshape(bs, t, a),
                          by_dim(cache_k[l]), by_dim(cache_v[l]))
    y_s, hs_re, hs_im = _ssm(us.reshape(bs, t, -1), state_ssm_re[l].reshape(bs, gn),
                             state_ssm_im[l].reshape(bs, gn), _ssm_tables_rows(tables, t), t)
    x1_s, hf_s, ti_s, tg_s, seen = _finish(xs, attn_s.reshape(ns, a), y_s.reshape(ns, -1), tail, seen, tm_s)

    sizes = seen[0, :N_EXPERTS].astype(I32)
    starts = jnp.cumsum(sizes) - sizes
    experts = jnp.arange(N_EXPERTS, dtype=I32)[None, None, :]

    def sorted_row(ti):
        hit = ti[:, :TOP_K, None] == experts
        return jnp.sum(jnp.where(hit, starts[None, None, :], 0), axis=-1) + ti[:, TOP_K:2 * TOP_K]

    dest_p, dest_s = sorted_row(ti_p), sorted_row(ti_s)
    n_rows = (np_ + ns) * TOP_K
    assert n_rows % MOE_TILE == 0
    xs_sorted = _dispatch(hf_p, dest_p, hf_s, dest_s, tm_wide).reshape(-1, V7X_LANES)
    ys = _moe_experts(xs_sorted, _moe_visits(sizes, n_rows), w_moe1[l], b_moe1[l], w_moe2[l], b_moe2[l])
    y_prompt = _final(x1_p, tg_p, dest_p, ys, norm_final_g, tm_p).reshape(bp, s, d)
    y_sample = _final(x1_s, tg_s, dest_s, ys, norm_final_g, tm_s).reshape(bs, t, d)

    keep = min(max(w for w, _ in DILATED_GROUPS), s)
    k5 = lambda z, b_, s_: z.reshape(1, b_, s_, n_heads, head_dim)
    by_pos = lambda zt: zt.reshape(bp, n_heads, head_dim, s).transpose(0, 3, 1, 2)[None]
    st = lambda z, b_: z.reshape(1, b_, g, SSM_STATE)
    return (y_prompt, y_sample,
            by_pos(ktp)[:, :, s - keep:], by_pos(vtp)[:, :, s - keep:], st(hp_re, bp), st(hp_im, bp),
            k5(ks, bs, t), k5(vs, bs, t), st(hs_re, bs), st(hs_im, bs))
```

```python
import functools
import math

import numpy as np
import jax
import jax.numpy as jnp
from jax import lax
from jax.experimental import pallas as pl
from jax.experimental.pallas import tpu as pltpu

F32 = jnp.float32
I32 = jnp.int32

V7X_LANES = 128
V7X_SUBLANES = 8
V7X_VMEM_BYTES = 64 * 1024 * 1024

HEAD_DIM = 64
HEADS_PER_LANE_TILE = V7X_LANES // HEAD_DIM
DILATED_GROUPS = ((128, 1), (512, 4), (2048, 16))
KEYS_BACK = 128
SSM_GROUP = 16
SSM_STATE = 64
N_EXPERTS = 32
TOP_K = 4
SWIGLU_LIMIT = 7.0
SWIGLU_ALPHA = 1.702
RMS_EPS = 1e-5
NEG_INF = -1e30
PROMPT_CHUNK = 16
MOE_TILE = 512
MOE_SUB = 256


def _vmem_limit(nbytes):
    return int(min(nbytes + (8 << 20), V7X_VMEM_BYTES - (8 << 20)))


def _rms(x, g):
    return x * lax.rsqrt(jnp.mean(x * x, axis=-1, keepdims=True) + RMS_EPS) * g


def _to_token_tiles(ref, val):
    rows, width = val.shape
    assert width == V7X_SUBLANES * V7X_LANES and ref.shape == (rows * V7X_SUBLANES, V7X_LANES)
    for j in range(V7X_SUBLANES):
        ref[pl.ds(j, rows, stride=V7X_SUBLANES), :] = val[:, j * V7X_LANES:(j + 1) * V7X_LANES]


def _from_token_tiles(ref):
    rows = ref.shape[0] // V7X_SUBLANES
    return jnp.concatenate([ref[pl.ds(j, rows, stride=V7X_SUBLANES), :] for j in range(V7X_SUBLANES)], axis=1)


def _proj_kernel(x_ref, g_ref, w_ref, q_ref, k_ref, v_ref, u_ref, *kv_t, attn_width, q_scale):
    h = _rms(x_ref[...], g_ref[...])
    z = jnp.dot(h, w_ref[...], preferred_element_type=F32)
    a = attn_width
    q_ref[...] = z[:, :a] * q_scale
    k_ref[...] = z[:, a:2 * a]
    v_ref[...] = z[:, 2 * a:3 * a]
    u_ref[...] = z[:, 3 * a:]
    if kv_t:
        kt_ref, vt_ref = kv_t
        kt_ref[...] = z[:, a:2 * a].T
        vt_ref[...] = z[:, 2 * a:3 * a].T


def _proj(x2d, norm_g, w_in, attn_width, tm, seq=None):
    n, d = x2d.shape
    mix = w_in.shape[1]
    ssm_width = mix - 3 * attn_width
    row = lambda i: (i, 0)
    const = lambda i: (0, 0)
    out_shape = (jax.ShapeDtypeStruct((n, attn_width), F32),) * 3 + (
        jax.ShapeDtypeStruct((n, ssm_width), F32),)
    out_specs = (pl.BlockSpec((tm, attn_width), row),) * 3 + (pl.BlockSpec((tm, ssm_width), row),)
    if seq is not None:
        assert seq % tm == 0 and tm % V7X_LANES == 0
        per_seq = seq // tm
        out_shape += (jax.ShapeDtypeStruct((n // seq, attn_width, seq), F32),) * 2
        out_specs += (pl.BlockSpec((None, attn_width, tm), lambda i: (i // per_seq, 0, i % per_seq)),) * 2
    vmem = 2 * 4 * (tm * d + d * mix + tm * mix + 2 * tm * attn_width) + 4 * tm * mix
    return pl.pallas_call(
        functools.partial(_proj_kernel, attn_width=attn_width, q_scale=HEAD_DIM ** -0.5),
        grid=(n // tm,),
        in_specs=[pl.BlockSpec((tm, d), row), pl.BlockSpec((1, d), const),
                  pl.BlockSpec((d, mix), const)],
        out_specs=out_specs,
        out_shape=out_shape,
        compiler_params=pltpu.CompilerParams(dimension_semantics=("arbitrary",),
                                             vmem_limit_bytes=_vmem_limit(vmem)),
        name="proj",
    )(x2d, norm_g.reshape(1, d), w_in)


def _attn_prompt_kernel(q_ref, k_ref, v_ref, o_ref, *scratch, seq, unroll):
    nb = KEYS_BACK
    n_groups = len(DILATED_GROUPS)
    m_sc, l_sc, acc_sc = scratch[:n_groups], scratch[n_groups:2 * n_groups], scratch[2 * n_groups:]
    lane = lax.broadcasted_iota(I32, (1, V7X_LANES), 1)
    head_a = lane < HEAD_DIM
    qi = lax.broadcasted_iota(I32, (nb, 2 * nb), 0)
    kj = lax.broadcasted_iota(I32, (nb, 2 * nb), 1)
    rel = nb + qi - kj
    band = (rel >= 0) & (rel <= nb)

    def rows(ref, start, d):
        if d == 1:
            return ref[pl.ds(pl.multiple_of(start, nb), nb), :]
        return ref[pl.ds(start, nb, stride=d), :]

    def put(ref, start, d, val):
        if d == 1:
            ref[pl.ds(pl.multiple_of(start, nb), nb), :] = val
        else:
            ref[pl.ds(start, nb, stride=d), :] = val

    for gi, (w, d) in enumerate(DILATED_GROUPS):
        assert w // d == nb
        nblk = seq // (d * nb)

        def block(t, gi=gi, d=d, nblk=nblk):
            r = t // nblk
            n = t % nblk
            cur = r + d * nb * n
            prev = r + d * nb * jnp.maximum(n - 1, 0)
            q = rows(q_ref, cur, d)
            kk = jnp.concatenate([rows(k_ref, prev, d), rows(k_ref, cur, d)], axis=0)
            vv = jnp.concatenate([rows(v_ref, prev, d), rows(v_ref, cur, d)], axis=0)
            qs = jnp.concatenate([jnp.where(head_a, q, 0.0), jnp.where(head_a, 0.0, q)], axis=0)
            s = lax.dot_general(qs, kk, (((1,), (1,)), ((), ())), preferred_element_type=F32)
            valid = band & (kj >= jnp.where(n > 0, 0, nb))
            s = jnp.where(jnp.concatenate([valid, valid], axis=0), s, NEG_INF)
            m = jnp.max(s, axis=-1, keepdims=True)
            p = jnp.exp(s - m)
            l = jnp.sum(p, axis=-1, keepdims=True)
            o = jnp.dot(p, vv, preferred_element_type=F32)
            o = jnp.where(head_a, o[:nb], o[nb:])
            put(m_sc[gi], cur, d, jnp.where(head_a, m[:nb], m[nb:]))
            put(l_sc[gi], cur, d, jnp.where(head_a, l[:nb], l[nb:]))
            put(acc_sc[gi], cur, d, o)

        def blocks(tt, carry, block=block):
            for u in range(unroll):
                block(tt * unroll + u)
            return carry

        lax.fori_loop(0, seq // nb // unroll, blocks, 0)

    def merge(n, carry):
        sl = pl.ds(pl.multiple_of(n * nb, nb), nb)
        ms = [m[sl, :] for m in m_sc]
        m_all = functools.reduce(jnp.maximum, ms)
        cs = [jnp.exp(m - m_all) for m in ms]
        den = functools.reduce(jnp.add, [c * l[sl, :] for c, l in zip(cs, l_sc)])
        num = functools.reduce(jnp.add, [c * acc[sl, :] for c, acc in zip(cs, acc_sc)])
        o_ref[sl, :] = num / den
        return carry

    lax.fori_loop(0, seq // nb, merge, 0, unroll=2)


def _attn_prompt(q, k, v, unroll=16):
    b, s, a = q.shape
    n_scratch = 3 * len(DILATED_GROUPS)
    spec = pl.BlockSpec((None, s, V7X_LANES), lambda i, j: (i, 0, j))
    vmem = 4 * s * V7X_LANES * (2 * 4 + n_scratch)
    return pl.pallas_call(
        functools.partial(_attn_prompt_kernel, seq=s, unroll=unroll),
        grid=(b, a // V7X_LANES),
        in_specs=[spec, spec, spec],
        out_specs=spec,
        out_shape=jax.ShapeDtypeStruct((b, s, a), F32),
        scratch_shapes=[pltpu.VMEM((s, V7X_LANES), F32)] * n_scratch,
        compiler_params=pltpu.CompilerParams(dimension_semantics=("arbitrary", "arbitrary"),
                                             vmem_limit_bytes=_vmem_limit(vmem)),
        name="attn_prompt",
    )(q, k, v)


def _sample_key_counts(n_buf, n_new):
    cnt = np.zeros((n_new, n_buf + n_new), np.float32)
    for w, d in DILATED_GROUPS:
        for i in range(n_new):
            for j in range(w // d + 1):
                idx = n_buf + i - d * j
                if idx >= 0:
                    cnt[i, idx] += 1.0
    return cnt[:, :n_buf], cnt[:, n_buf:]


def _attn_sample_kernel(q_ref, kn_ref, vn_ref, kt_ref, vt_ref, cnt_ref, o_ref, *, n_heads, cnt_new):
    n_new, a = q_ref.shape
    rows = n_new * n_heads
    lane_head = lax.broadcasted_iota(I32, (rows, a), 1) // HEAD_DIM
    row_head = lax.broadcasted_iota(I32, (rows, a), 0) % n_heads
    row_query = lax.broadcasted_iota(I32, (rows, 1), 0) // n_heads
    own = lane_head == row_head
    q = q_ref[...]
    qb = jnp.where(own, jnp.broadcast_to(q[:, None, :], (n_new, n_heads, a)).reshape(rows, a), 0.0)
    s_buf = jnp.dot(qb, kt_ref[...], preferred_element_type=F32)
    cnt = cnt_ref[...]
    cnt = jnp.broadcast_to(cnt[:, None, :], (n_new, n_heads, cnt.shape[-1])).reshape(rows, -1)
    valid = cnt > 0.0
    m = jnp.max(jnp.where(valid, s_buf, NEG_INF), axis=-1, keepdims=True)
    s_new, c_new = [], []
    for j in range(n_new):
        c = functools.reduce(lambda acc, i: jnp.where(row_query == i, float(cnt_new[i, j]), acc),
                             range(n_new), jnp.zeros((rows, 1), F32))
        s = jnp.sum(qb * kn_ref[pl.ds(j, 1), :], axis=-1, keepdims=True)
        m = jnp.maximum(m, jnp.where(c > 0.0, s, NEG_INF))
        s_new.append(s)
        c_new.append(c)
    p = jnp.where(valid, cnt * jnp.exp(s_buf - m), 0.0)
    den = jnp.sum(p, axis=-1, keepdims=True)
    o = lax.dot_general(p, vt_ref[...], (((1,), (1,)), ((), ())), preferred_element_type=F32)
    for j in range(n_new):
        pj = jnp.where(c_new[j] > 0.0, c_new[j] * jnp.exp(s_new[j] - m), 0.0)
        den = den + pj
        o = o + pj * vn_ref[pl.ds(j, 1), :]
    o = jnp.where(own, o / den, 0.0)
    o_ref[...] = jnp.sum(o.reshape(n_new, n_heads, a), axis=1)


def _attn_sample(q, k_new, v_new, cache_kt, cache_vt):
    b, t, a = q.shape
    n_buf = cache_kt.shape[-1]
    n_heads = a // HEAD_DIM
    cnt_buf, cnt_new = _sample_key_counts(n_buf, t)
    new = pl.BlockSpec((None, t, a), lambda i: (i, 0, 0))
    buf = pl.BlockSpec((None, a, n_buf), lambda i: (i, 0, 0))
    vmem = 4 * (2 * 2 * n_buf * a + 8 * t * n_heads * n_buf)
    return pl.pallas_call(
        functools.partial(_attn_sample_kernel, n_heads=n_heads, cnt_new=cnt_new),
        grid=(b,),
        in_specs=[new, new, new, buf, buf, pl.BlockSpec((t, n_buf), lambda i: (0, 0))],
        out_specs=new,
        out_shape=jax.ShapeDtypeStruct((b, t, a), F32),
        compiler_params=pltpu.CompilerParams(dimension_semantics=("arbitrary",),
                                             vmem_limit_bytes=_vmem_limit(vmem)),
        name="attn_sample",
    )(q, k_new, v_new, cache_kt, cache_vt, jnp.asarray(cnt_buf))


def _block_toeplitz_kernel(k_ref, toep_ref, tt_ref, *, chunk):
    c = k_ref.shape[1]
    for i in range(k_ref.shape[0]):
        k = k_ref[i]
        for s in range(chunk):
            shifted = k if s == 0 else jnp.concatenate(
                [jnp.zeros((c, s * c), F32), k[:, :(chunk - s) * c]], axis=1)
            toep_ref[i, pl.ds(s * c, c), :] = shifted
        tt_ref[i] = toep_ref[i].T


def _block_toeplitz(kern, chunk):
    g, c, lc = kern.shape
    per_step = 4
    spec = pl.BlockSpec((per_step, lc, lc), lambda i: (i, 0, 0))
    return pl.pallas_call(
        functools.partial(_block_toeplitz_kernel, chunk=chunk),
        grid=(g // per_step,),
        in_specs=[pl.BlockSpec((per_step, c, lc), lambda i: (i, 0, 0))],
        out_specs=(spec, spec),
        out_shape=(jax.ShapeDtypeStruct((g, lc, lc), F32),) * 2,
        compiler_params=pltpu.CompilerParams(dimension_semantics=("arbitrary",)),
        name="ssm_toeplitz",
    )(kern)


def _pad_states(m, g):
    half = (np.arange(g) % 2)[:, None, None]
    z = jnp.zeros_like(m)
    return jnp.where(half == 0, jnp.concatenate([m, z], -1), jnp.concatenate([z, m], -1))


def _ssm_tables(a_re, a_im, log_dt, b_re, b_im, c_re, c_im, d_skip, chunk):
    g, n = a_re.shape
    c = b_re.shape[-1]
    lc = chunk * c
    dt = jnp.exp(log_dt)[:, None]
    x, y = a_re * dt, a_im * dt
    ex = jnp.exp(x)
    ar, ai = ex * jnp.cos(y), ex * jnp.sin(y)
    nr = jnp.expm1(x) * jnp.cos(y) - 2.0 * jnp.sin(0.5 * y) ** 2
    ni = ai
    den = a_re * a_re + a_im * a_im
    fr = (nr * a_re + ni * a_im) / den
    fi = (ni * a_re - nr * a_im) / den
    b_re_t, b_im_t = b_re.transpose(0, 2, 1), b_im.transpose(0, 2, 1)
    bbr = fr[:, None, :] * b_re_t - fi[:, None, :] * b_im_t
    bbi = fr[:, None, :] * b_im_t + fi[:, None, :] * b_re_t
    pr, pi = [jnp.ones_like(ar)], [jnp.zeros_like(ar)]
    for _ in range(chunk):
        pr.append(pr[-1] * ar - pi[-1] * ai)
        pi.append(pr[-2] * ai + pi[-1] * ar)
    pr, pi = jnp.stack(pr, 1), jnp.stack(pi, 1)
    cpr = c_re[:, None] * pr[:, :, None, :] - c_im[:, None] * pi[:, :, None, :]
    cpi = c_re[:, None] * pi[:, :, None, :] + c_im[:, None] * pr[:, :, None, :]
    kern = jnp.sum(cpr[:, None, :chunk] * bbr[:, :, None, None] - cpi[:, None, :chunk] * bbi[:, :, None, None],
                   axis=-1).reshape(g, c, lc)
    toep, tt = _block_toeplitz(kern, chunk)
    qr, qi = pr[:, chunk - 1::-1][:, :, None, :], pi[:, chunk - 1::-1][:, :, None, :]
    wr = (qr * bbr[:, None] - qi * bbi[:, None]).reshape(g, lc, n)
    wi = (qr * bbi[:, None] + qi * bbr[:, None]).reshape(g, lc, n)
    vr = cpr[:, 1:].reshape(g, lc, n)
    vi = -cpi[:, 1:].reshape(g, lc, n)
    return dict(chunk=chunk, toep=toep, tt=tt, wr=wr, wi=wi, vr=vr, vi=vi, d=d_skip, pr=pr, pi=pi)


def _ssm_tables_lanes(tab):
    g, chunk = tab["d"].shape[0], tab["chunk"]
    tr = lambda m: m.transpose(0, 2, 1)
    return dict(tt=tab["tt"], wt_re=tr(_pad_states(tab["wr"], g)), wt_im=tr(_pad_states(tab["wi"], g)),
                v_re=_pad_states(tab["vr"], g), v_im=_pad_states(tab["vi"], g),
                dcol=jnp.tile(tab["d"], (1, chunk))[:, :, None],
                al_re=tab["pr"][:, chunk].reshape(1, -1), al_im=tab["pi"][:, chunk].reshape(1, -1))


def _ssm_tables_rows(tab, chunk):
    g, c = tab["d"].shape
    big, lc = tab["chunk"], chunk * c
    tr = lambda m: m.transpose(0, 2, 1)
    tail = slice((big - chunk) * c, big * c)
    return dict(toep=tab["toep"][:, :lc, :lc],
                wr=_pad_states(tab["wr"][:, tail], g), wi=_pad_states(tab["wi"][:, tail], g),
                vr=tr(_pad_states(tab["vr"][:, :lc], g)), vi=tr(_pad_states(tab["vi"][:, :lc], g)),
                d=jnp.tile(tab["d"], (1, chunk))[:, None, :],
                al_re=tab["pr"][:, chunk].reshape(1, -1), al_im=tab["pi"][:, chunk].reshape(1, -1))


def _ssm_state_kernel(u_ref, wr_ref, wi_ref, er_ref, ei_ref):
    er = jnp.dot(u_ref[0], wr_ref[0], preferred_element_type=F32)
    ei = jnp.dot(u_ref[0], wi_ref[0], preferred_element_type=F32)
    er_ref[...] = er + jnp.dot(u_ref[1], wr_ref[1], preferred_element_type=F32)
    ei_ref[...] = ei + jnp.dot(u_ref[1], wi_ref[1], preferred_element_type=F32)


def _ssm_scan_kernel(er_ref, ei_ref, ar_ref, ai_ref, hr_ref, hi_ref, xr_ref, xi_ref, lr_ref, li_ref,
                     *, n_chunks, nb):
    shape = hr_ref.shape
    ar = jnp.broadcast_to(ar_ref[...], shape)
    ai = jnp.broadcast_to(ai_ref[...], shape)

    def step(j, carry):
        xr, xi = carry
        sl = pl.ds(pl.multiple_of(j * nb, nb), nb)
        xr_ref[sl, :] = xr
        xi_ref[sl, :] = xi
        return (ar * xr - ai * xi + er_ref[sl, :], ar * xi + ai * xr + ei_ref[sl, :])

    xr, xi = lax.fori_loop(0, n_chunks, step, (hr_ref[...], hi_ref[...]))
    lr_ref[...] = xr
    li_ref[...] = xi


def _ssm_out_kernel(u_ref, t_ref, d_ref, xr_ref, xi_ref, vr_ref, vi_ref, y_ref):
    xr = xr_ref[...]
    xi = xi_ref[...]
    for a in range(2):
        u = u_ref[a]
        y = jnp.dot(u, t_ref[a], preferred_element_type=F32) + d_ref[a] * u
        y = y + jnp.dot(xr, vr_ref[a], preferred_element_type=F32)
        y_ref[a] = y + jnp.dot(xi, vi_ref[a], preferred_element_type=F32)


def _ssm_scan(er, ei, tab, h0_re, h0_im, n_chunks):
    r, gn = er.shape
    b = r // n_chunks
    lanes = 4 * V7X_LANES
    col = lambda i: (0, i)
    return pl.pallas_call(
        functools.partial(_ssm_scan_kernel, n_chunks=n_chunks, nb=b),
        grid=(gn // lanes,),
        in_specs=[pl.BlockSpec((r, lanes), col), pl.BlockSpec((r, lanes), col),
                  pl.BlockSpec((1, lanes), col), pl.BlockSpec((1, lanes), col),
                  pl.BlockSpec((b, lanes), col), pl.BlockSpec((b, lanes), col)],
        out_specs=(pl.BlockSpec((r, lanes), col),) * 2 + (pl.BlockSpec((b, lanes), col),) * 2,
        out_shape=(jax.ShapeDtypeStruct((r, gn), F32),) * 2 + (jax.ShapeDtypeStruct((b, gn), F32),) * 2,
        compiler_params=pltpu.CompilerParams(dimension_semantics=("arbitrary",),
                                             vmem_limit_bytes=_vmem_limit(2 * 4 * 4 * r * lanes)),
        name="ssm_scan",
    )(er, ei, tab["al_re"], tab["al_im"], h0_re, h0_im)


def _ssm_state_t_kernel(a_ref, wr_ref, wi_ref, er_ref, ei_ref):
    def seq(b, carry):
        a0, a1 = a_ref[0, b], a_ref[1, b]
        er_ref[b] = (jnp.dot(wr_ref[0], a0, preferred_element_type=F32)
                     + jnp.dot(wr_ref[1], a1, preferred_element_type=F32))
        ei_ref[b] = (jnp.dot(wi_ref[0], a0, preferred_element_type=F32)
                     + jnp.dot(wi_ref[1], a1, preferred_element_type=F32))
        return carry

    lax.fori_loop(0, a_ref.shape[1], seq, 0)


def _ssm_out_t_kernel(a_ref, t_ref, d_ref, xr_ref, xi_ref, vr_ref, vi_ref, y_ref):
    def seq(b, carry):
        xr, xi = xr_ref[b], xi_ref[b]
        for a in range(2):
            at = a_ref[a, b]
            y = jnp.dot(t_ref[a], at, preferred_element_type=F32) + d_ref[a] * at
            y = y + jnp.dot(vr_ref[a], xr, preferred_element_type=F32)
            y_ref[a, b] = y + jnp.dot(vi_ref[a], xi, preferred_element_type=F32)
        return carry

    lax.fori_loop(0, a_ref.shape[1], seq, 0)


def _ssm_chunks_on_lanes(u, h0_re, h0_im, tab, chunk):
    b, s, width = u.shape
    g = width // SSM_GROUP
    n_chunks = s // chunk
    lc = chunk * SSM_GROUP
    assert n_chunks == V7X_LANES
    at = (u.reshape(b, n_chunks, chunk * width).transpose(0, 2, 1)
          .reshape(b, chunk, g, SSM_GROUP, n_chunks).transpose(2, 0, 1, 3, 4).reshape(g, b, lc, n_chunks))
    pair4 = lambda i: (i, 0, 0, 0)
    pair3 = lambda i: (i, 0, 0)
    cp = lambda vmem: pltpu.CompilerParams(dimension_semantics=("arbitrary",),
                                           vmem_limit_bytes=_vmem_limit(vmem))
    a_spec = pl.BlockSpec((2, b, lc, n_chunks), pair4)
    st_spec = pl.BlockSpec((None, b, V7X_LANES, n_chunks), pair4)
    w_spec = pl.BlockSpec((2, V7X_LANES, lc), pair3)
    a_bytes = 2 * b * lc * n_chunks
    st_bytes = b * V7X_LANES * n_chunks
    et_re, et_im = pl.pallas_call(
        _ssm_state_t_kernel,
        grid=(g // 2,),
        in_specs=[a_spec, w_spec, w_spec],
        out_specs=(st_spec,) * 2,
        out_shape=(jax.ShapeDtypeStruct((g // 2, b, V7X_LANES, n_chunks), F32),) * 2,
        compiler_params=cp(2 * 4 * (a_bytes + 2 * st_bytes + 4 * V7X_LANES * lc)),
        name="ssm_state_t",
    )(at, tab["wt_re"], tab["wt_im"])
    rows = lambda m: m.transpose(3, 1, 0, 2).reshape(n_chunks * b, g * SSM_STATE)
    xr, xi, lr, li = _ssm_scan(rows(et_re), rows(et_im), tab, h0_re, h0_im, n_chunks)
    cols = lambda m: m.reshape(n_chunks, b, g // 2, V7X_LANES).transpose(2, 1, 3, 0)
    yt = pl.pallas_call(
        _ssm_out_t_kernel,
        grid=(g // 2,),
        in_specs=[a_spec, pl.BlockSpec((2, lc, lc), pair3), pl.BlockSpec((2, lc, 1), pair3),
                  st_spec, st_spec, pl.BlockSpec((2, lc, V7X_LANES), pair3),
                  pl.BlockSpec((2, lc, V7X_LANES), pair3)],
        out_specs=a_spec,
        out_shape=jax.ShapeDtypeStruct((g, b, lc, n_chunks), F32),
        compiler_params=cp(2 * 4 * (2 * a_bytes + 2 * st_bytes + 2 * lc * lc + 6 * V7X_LANES * lc)),
        name="ssm_out_t",
    )(at, tab["tt"], tab["dcol"], cols(xr), cols(xi), tab["v_re"], tab["v_im"])
    y = (yt.reshape(g, b, chunk, SSM_GROUP, n_chunks).transpose(1, 2, 0, 3, 4)
         .reshape(b, chunk * width, n_chunks).transpose(0, 2, 1).reshape(b, s, width))
    return y, lr, li


def _ssm(u, h0_re, h0_im, tab, chunk):
    b, s, width = u.shape
    g = width // SSM_GROUP
    n_chunks = s // chunk
    lc = chunk * SSM_GROUP
    r = n_chunks * b
    gn = g * SSM_STATE
    ut = u.reshape(b, n_chunks, chunk, g, SSM_GROUP).transpose(3, 1, 0, 2, 4).reshape(g, r, lc)
    pair = lambda i: (i, 0, 0)
    cp = lambda vmem: pltpu.CompilerParams(dimension_semantics=("arbitrary",),
                                           vmem_limit_bytes=_vmem_limit(vmem))
    er, ei = pl.pallas_call(
        _ssm_state_kernel,
        grid=(g // 2,),
        in_specs=[pl.BlockSpec((2, r, lc), pair), pl.BlockSpec((2, lc, V7X_LANES), pair),
                  pl.BlockSpec((2, lc, V7X_LANES), pair)],
        out_specs=(pl.BlockSpec((r, V7X_LANES), lambda i: (0, i)),) * 2,
        out_shape=(jax.ShapeDtypeStruct((r, gn), F32),) * 2,
        compiler_params=cp(2 * 4 * (2 * r * lc + 4 * lc * V7X_LANES + 2 * r * V7X_LANES)),
        name="ssm_state",
    )(ut, tab["wr"], tab["wi"])
    xr, xi, lr, li = _ssm_scan(er, ei, tab, h0_re, h0_im, n_chunks)
    y = pl.pallas_call(
        _ssm_out_kernel,
        grid=(g // 2,),
        in_specs=[pl.BlockSpec((2, r, lc), pair), pl.BlockSpec((2, lc, lc), pair),
                  pl.BlockSpec((2, 1, lc), pair),
                  pl.BlockSpec((r, V7X_LANES), lambda i: (0, i)),
                  pl.BlockSpec((r, V7X_LANES), lambda i: (0, i)),
                  pl.BlockSpec((2, V7X_LANES, lc), pair), pl.BlockSpec((2, V7X_LANES, lc), pair)],
        out_specs=pl.BlockSpec((2, r, lc), pair),
        out_shape=jax.ShapeDtypeStruct((g, r, lc), F32),
        compiler_params=cp(2 * 4 * (4 * r * lc + 2 * lc * lc + 2 * r * V7X_LANES + 4 * V7X_LANES * lc)),
        name="ssm_out",
    )(ut, tab["toep"], tab["d"], xr, xi, tab["vr"], tab["vi"])
    y = y.reshape(g, n_chunks, b, chunk, SSM_GROUP).transpose(2, 1, 3, 0, 4).reshape(b, s, width)
    return y, lr, li


def _finish_kernel(x_ref, attn_ref, ssm_ref, ga_ref, wg_ref, bg_ref, gs_ref, wo_ref, gf_ref,
                   wr_ref, br_ref, base_ref, x1_ref, hf_ref, ti_ref, tg_ref, cnt_ref, seen):
    a = attn_ref.shape[-1]
    tm = x_ref.shape[0]

    @pl.when(pl.program_id(0) == 0)
    def _():
        seen[...] = base_ref[...]

    g = jax.nn.gelu(ssm_ref[...])
    gate = jax.nn.sigmoid(jnp.dot(g, wg_ref[...], preferred_element_type=F32) + bg_ref[...])
    mixed_a = _rms(attn_ref[...], ga_ref[...])
    mixed_s = _rms(g * gate, gs_ref[...])
    x1 = (x_ref[...] + jnp.dot(mixed_a, wo_ref[pl.ds(0, a), :], preferred_element_type=F32)
          + jnp.dot(mixed_s, wo_ref[pl.ds(a, wo_ref.shape[0] - a), :], preferred_element_type=F32))
    x1_ref[...] = x1
    hf = _rms(x1, gf_ref[...])
    _to_token_tiles(hf_ref, hf)
    hf_hi = hf.astype(jnp.bfloat16).astype(F32)
    hf_lo = hf - hf_hi
    small = (jnp.dot(hf_lo, wr_ref[1], preferred_element_type=F32)
             + jnp.dot(hf_hi, wr_ref[1], preferred_element_type=F32)
             + jnp.dot(hf_lo, wr_ref[0], preferred_element_type=F32))
    logits = jnp.dot(hf_hi, wr_ref[0], preferred_element_type=F32) + small + br_ref[...]
    lane = lax.broadcasted_iota(I32, logits.shape, 1)
    work = logits
    vals, idxs = [], []
    for _ in range(TOP_K):
        m = jnp.max(work, axis=-1, keepdims=True)
        idx = jnp.min(jnp.where(work == m, lane, V7X_LANES), axis=-1, keepdims=True)
        vals.append(m)
        idxs.append(idx)
        work = jnp.where(lane == idx, -jnp.inf, work)
    exps = [jnp.exp(v - vals[0]) for v in vals]
    den = exps[0]
    for e in exps[1:]:
        den = den + e
    picked = jnp.zeros(logits.shape, F32)
    for k in range(TOP_K):
        picked = picked + jnp.where(lane == idxs[k], 1.0, 0.0)
    tri = jnp.where(lax.broadcasted_iota(I32, (tm, tm), 0) > lax.broadcasted_iota(I32, (tm, tm), 1), 1.0, 0.0)
    before = jnp.dot(tri, picked, preferred_element_type=F32) + seen[...]
    ti = jnp.zeros(logits.shape, I32)
    tg = jnp.zeros(logits.shape, F32)
    for k in range(TOP_K):
        rank = jnp.sum(jnp.where(lane == idxs[k], before, 0.0), axis=-1, keepdims=True)
        ti = jnp.where(lane == k, idxs[k], ti)
        ti = jnp.where(lane == TOP_K + k, rank.astype(I32), ti)
        tg = jnp.where(lane == k, exps[k] / den, tg)
    ti_ref[...] = ti
    tg_ref[...] = tg
    seen[...] = seen[...] + jnp.sum(picked, axis=0, keepdims=True)
    cnt_ref[...] = seen[...]


def _finish(x2d, attn, ssm_y, p, seen, tm):
    n, d = x2d.shape
    a = attn.shape[1]
    sw = ssm_y.shape[1]
    ne = p["w_router"].shape[1]
    w_router = jnp.pad(p["w_router"], ((0, 0), (0, V7X_LANES - ne)))
    w_hi = w_router.astype(jnp.bfloat16).astype(F32)
    w_router = jnp.stack([w_hi, w_router - w_hi])
    b_router = jnp.pad(p["b_router"], (0, V7X_LANES - ne), constant_values=-jnp.inf).reshape(1, V7X_LANES)
    row = lambda i: (i, 0)
    full = lambda arr: pl.BlockSpec(arr.shape, lambda i: (0,) * arr.ndim)
    ins = [x2d, attn, ssm_y, p["attn_out_norm_g"].reshape(1, a), p["w_glu"], p["b_glu"].reshape(1, sw),
           p["ssm_out_norm_g"].reshape(1, sw), p["w_out"], p["norm_ffn_g"].reshape(1, d),
           w_router, b_router, seen]
    in_specs = [pl.BlockSpec((tm, d), row), pl.BlockSpec((tm, a), row), pl.BlockSpec((tm, sw), row)]
    in_specs += [full(arr) for arr in ins[3:]]
    vmem = 2 * 4 * (tm * (3 * d + a + sw + 2 * V7X_LANES) + sw * sw + d * d + 2 * d * V7X_LANES) + 16 * tm * d
    return pl.pallas_call(
        _finish_kernel,
        grid=(n // tm,),
        in_specs=in_specs,
        out_specs=(pl.BlockSpec((tm, d), row), pl.BlockSpec((tm * V7X_SUBLANES, V7X_LANES), row),
                   pl.BlockSpec((tm, V7X_LANES), row), pl.BlockSpec((tm, V7X_LANES), row),
                   pl.BlockSpec((1, V7X_LANES), lambda i: (0, 0))),
        out_shape=(jax.ShapeDtypeStruct((n, d), F32), jax.ShapeDtypeStruct((n * V7X_SUBLANES, V7X_LANES), F32),
                   jax.ShapeDtypeStruct((n, V7X_LANES), I32), jax.ShapeDtypeStruct((n, V7X_LANES), F32),
                   jax.ShapeDtypeStruct((1, V7X_LANES), F32)),
        scratch_shapes=[pltpu.VMEM((1, V7X_LANES), F32)],
        compiler_params=pltpu.CompilerParams(dimension_semantics=("arbitrary",),
                                             vmem_limit_bytes=_vmem_limit(vmem)),
        name="finish",
    )(*ins)


def _tile_copy(src_ref, src_row, dst_ref, dst_row, sem):
    return pltpu.make_async_copy(src_ref.at[src_row], dst_ref.at[dst_row], sem)


def _dispatch_kernel(dest_ref, hp_ref, hs_ref, xs_ref, sem, *, n_tiles_p):
    def scatter(hf_ref):
        tm = hf_ref.shape[0]

        def tokens(i, carry):
            for u in range(V7X_SUBLANES):
                r = i * V7X_SUBLANES + u
                for k in range(TOP_K):
                    _tile_copy(hf_ref, r, xs_ref, dest_ref[0, 0, r * TOP_K + k], sem).start(priority=k % 2)
            return carry

        lax.fori_loop(0, tm // V7X_SUBLANES, tokens, 0)
        rows = pl.ds(0, tm * TOP_K)
        pltpu.make_async_copy(xs_ref.at[rows], xs_ref.at[rows], sem).wait()

    @pl.when(pl.program_id(0) < n_tiles_p)
    def _():
        scatter(hp_ref)

    @pl.when(pl.program_id(0) == n_tiles_p)
    def _():
        scatter(hs_ref)


def _dispatch(hf_p, dest_p, hf_s, dest_s, tm):
    tile = (V7X_SUBLANES, V7X_LANES)
    hf_p, hf_s = hf_p.reshape(-1, *tile), hf_s.reshape(-1, *tile)
    n_p, n_s = hf_p.shape[0], hf_s.shape[0]
    assert n_p % tm == 0 and n_s <= tm
    n_tiles_p = n_p // tm
    dest3 = jnp.concatenate([dest_p.reshape(n_tiles_p, tm * TOP_K),
                             jnp.pad(dest_s.reshape(1, n_s * TOP_K), ((0, 0), (0, (tm - n_s) * TOP_K)))])
    dest3 = dest3.reshape(n_tiles_p + 1, 1, tm * TOP_K)
    return pl.pallas_call(
        functools.partial(_dispatch_kernel, n_tiles_p=n_tiles_p),
        grid=(n_tiles_p + 1,),
        in_specs=[pl.BlockSpec((1, 1, tm * TOP_K), lambda i: (i, 0, 0), memory_space=pltpu.SMEM),
                  pl.BlockSpec((tm, *tile), lambda i: (jnp.minimum(i, n_tiles_p - 1), 0, 0)),
                  pl.BlockSpec((n_s, *tile), lambda i: (0, 0, 0))],
        out_specs=pl.BlockSpec(memory_space=pl.ANY),
        out_shape=jax.ShapeDtypeStruct(((n_p + n_s) * TOP_K, *tile), F32),
        scratch_shapes=[pltpu.SemaphoreType.DMA(())],
        compiler_params=pltpu.CompilerParams(dimension_semantics=("arbitrary",)),
        name="dispatch",
    )(dest3, hf_p, hf_s)


def _moe_kernel(vt_ref, ve_ref, nx_ref, lo_ref, hi_ref, x_ref, w1_ref, b1_ref, w2_ref, b2_ref, y_ref,
                w1s, w2s, w1b, w2b, sem, *, d_ff):
    v = pl.program_id(0)
    lo, hi = lo_ref[v], hi_ref[v]
    prev = jnp.maximum(v - 1, 0)
    first = jnp.logical_or(v == 0, vt_ref[v] != vt_ref[prev])

    def fetch(e):
        return (pltpu.make_async_copy(w1_ref.at[e], w1s, sem.at[0]),
                pltpu.make_async_copy(w2_ref.at[e], w2s, sem.at[1]))

    @pl.when(v == 0)
    def _():
        for cp in fetch(ve_ref[0]):
            cp.start()

    @pl.when(jnp.logical_or(v == 0, ve_ref[v] != ve_ref[prev]))
    def _():
        for cp in fetch(ve_ref[v]):
            cp.wait()
        w1b[...] = w1s[...].astype(jnp.bfloat16)
        w2b[...] = w2s[...].astype(jnp.bfloat16)

        @pl.when(nx_ref[v] >= 0)
        def _():
            for cp in fetch(nx_ref[v]):
                cp.start()

    def swiglu(x_tiles):
        x = _from_token_tiles(x_tiles).astype(jnp.bfloat16)
        a = jnp.dot(x, w1b[...], preferred_element_type=F32) + b1_ref[0]
        x_glu = jnp.minimum(a[:, :d_ff], SWIGLU_LIMIT)
        x_lin = jnp.clip(a[:, d_ff:], -SWIGLU_LIMIT, SWIGLU_LIMIT)
        mid = x_glu * jax.nn.sigmoid(SWIGLU_ALPHA * x_glu) * (x_lin + 1.0)
        return jnp.dot(mid.astype(jnp.bfloat16), w2b[...], preferred_element_type=F32) + b2_ref[0]

    whole = jnp.logical_and(lo == 0, hi == MOE_TILE)

    @pl.when(whole)
    def _():
        _to_token_tiles(y_ref, swiglu(x_ref))

    for r0 in range(0, MOE_TILE, MOE_SUB):
        sub = pl.ds(r0 * V7X_SUBLANES, MOE_SUB * V7X_SUBLANES)
        x_sub, y_sub = x_ref.at[sub, :], y_ref.at[sub, :]
        touched = jnp.logical_and(jnp.logical_not(whole), jnp.logical_and(lo < r0 + MOE_SUB, hi > r0))

        @pl.when(touched)
        def _(r0=r0, x_sub=x_sub, y_sub=y_sub):
            y = swiglu(x_sub)
            rows = r0 + lax.broadcasted_iota(I32, (MOE_SUB, 1), 0)
            mine = (rows >= lo) & (rows < hi)

            @pl.when(first)
            def _():
                _to_token_tiles(y_sub, jnp.where(mine, y, 0.0))

            @pl.when(jnp.logical_not(first))
            def _():
                _to_token_tiles(y_sub, jnp.where(mine, y, _from_token_tiles(y_sub)))

        @pl.when(jnp.logical_and(first, jnp.logical_not(jnp.logical_or(touched, whole))))
        def _(y_sub=y_sub):
            y_sub[...] = jnp.zeros(y_sub.shape, F32)


def _moe_experts(xs, visits, w1, b1, w2, b2):
    ne, d, ff2 = w1.shape
    d_ff = ff2 // 2
    n_visits = visits[0].shape[0]
    row = lambda v, vt, ve, nx, lo, hi: (vt[v], 0)
    exp = lambda v, vt, ve, nx, lo, hi: (ve[v], 0, 0)
    vmem = (4 * 2 * 2 * MOE_TILE * d + (4 + 2) * (d * ff2 + d_ff * d) + 2 * 4 * (ff2 + d)
            + 4 * 4 * MOE_TILE * ff2)
    tiles = pl.BlockSpec((MOE_TILE * V7X_SUBLANES, V7X_LANES), row)
    grid_spec = pltpu.PrefetchScalarGridSpec(
        num_scalar_prefetch=5,
        grid=(n_visits,),
        in_specs=[tiles,
                  pl.BlockSpec(memory_space=pl.ANY), pl.BlockSpec((1, 1, ff2), exp),
                  pl.BlockSpec(memory_space=pl.ANY), pl.BlockSpec((1, 1, d), exp)],
        out_specs=tiles,
        scratch_shapes=[pltpu.VMEM((d, ff2), F32), pltpu.VMEM((d_ff, d), F32),
                        pltpu.VMEM((d, ff2), jnp.bfloat16), pltpu.VMEM((d_ff, d), jnp.bfloat16),
                        pltpu.SemaphoreType.DMA((2,))],
    )
    return pl.pallas_call(
        functools.partial(_moe_kernel, d_ff=d_ff),
        grid_spec=grid_spec,
        out_shape=jax.ShapeDtypeStruct(xs.shape, F32),
        compiler_params=pltpu.CompilerParams(dimension_semantics=("arbitrary",),
                                             vmem_limit_bytes=_vmem_limit(vmem)),
        name="moe",
    )(*visits, xs, w1, b1.reshape(ne, 1, ff2), w2, b2.reshape(ne, 1, d))


def _moe_visits(sizes, n_rows):
    n_tiles = n_rows // MOE_TILE
    n_visits = n_tiles + N_EXPERTS - 1
    ends = jnp.cumsum(sizes)
    starts = ends - sizes
    first_tile = starts // MOE_TILE
    n_vis = jnp.where(sizes > 0, (ends - 1) // MOE_TILE - first_tile + 1, 0)
    vis_end = jnp.cumsum(n_vis)
    total = vis_end[-1]
    v = jnp.clip(jnp.arange(n_visits, dtype=I32), 0, jnp.maximum(total - 1, 0))
    e = jnp.sum((vis_end[None, :] <= v[:, None]).astype(I32), axis=1)
    pick = lambda tab: jnp.sum(jnp.where(e[:, None] == jnp.arange(N_EXPERTS)[None, :], tab[None, :], 0), axis=1)
    tile = pick(first_tile) + v - pick(vis_end - n_vis)
    lo = jnp.maximum(pick(starts), tile * MOE_TILE) - tile * MOE_TILE
    hi = jnp.minimum(pick(ends), (tile + 1) * MOE_TILE) - tile * MOE_TILE
    live = jnp.arange(n_visits) < total
    ids = jnp.arange(N_EXPERTS, dtype=I32)
    later = jnp.where((ids[None, :] > ids[:, None]) & (sizes[None, :] > 0), ids[None, :], N_EXPERTS)
    nxt = jnp.min(later, axis=1)
    nxt = pick(jnp.where(nxt < N_EXPERTS, nxt, -1))
    return (tile.astype(I32), e.astype(I32), nxt.astype(I32), jnp.where(live, lo, 0).astype(I32),
            jnp.where(live, hi, 0).astype(I32))


def _final_kernel(dest_ref, nxt_ref, x_ref, tg_ref, g_ref, ys_ref, y_ref, buf, sem, *, tm):
    i = pl.program_id(0)
    n = pl.num_programs(0)

    def gather(d_ref, slot):
        def tokens(i8, carry):
            for u in range(V7X_SUBLANES):
                r = i8 * V7X_SUBLANES + u
                tile = pl.ds(pl.multiple_of(r * V7X_SUBLANES, V7X_SUBLANES), V7X_SUBLANES)
                for k in range(TOP_K):
                    pltpu.make_async_copy(ys_ref.at[d_ref[0, 0, r * TOP_K + k]], buf.at[slot, k, tile, :],
                                          sem.at[slot]).start(priority=k % 2)
            return carry
        lax.fori_loop(0, tm // V7X_SUBLANES, tokens, 0)

    @pl.when(i == 0)
    def _():
        gather(dest_ref, 0)

    for slot in range(2):
        @pl.when(jnp.logical_and(i + 1 < n, (i + 1) % 2 == slot))
        def _(slot=slot):
            gather(nxt_ref, slot)

    for slot in range(2):
        @pl.when(i % 2 == slot)
        def _(slot=slot):
            pltpu.make_async_copy(buf.at[slot], buf.at[slot], sem.at[slot]).wait()
            tg = tg_ref[...]
            parts = []
            for j in range(V7X_SUBLANES):
                f = None
                for k in range(TOP_K):
                    rows = buf[slot, k, pl.ds(j, tm, stride=V7X_SUBLANES), :]
                    f = tg[:, k:k + 1] * rows if f is None else f + tg[:, k:k + 1] * rows
                parts.append(f)
            y_ref[...] = _rms(x_ref[...] + jnp.concatenate(parts, axis=1), g_ref[...])


def _final(x1, tg, dest, ys, g, tm):
    n, d = x1.shape
    n_tiles = n // tm
    tile = (V7X_SUBLANES, V7X_LANES)
    dest3 = dest.reshape(n_tiles, 1, tm * TOP_K)
    row = lambda i: (i, 0)
    smem = lambda index_map: pl.BlockSpec((1, 1, tm * TOP_K), index_map, memory_space=pltpu.SMEM)
    vmem = 4 * (2 * TOP_K * tm * d + 2 * 2 * tm * d + 2 * tm * V7X_LANES + 4 * tm * d)
    return pl.pallas_call(
        functools.partial(_final_kernel, tm=tm),
        grid=(n_tiles,),
        in_specs=[smem(lambda i: (i, 0, 0)), smem(lambda i: (jnp.minimum(i + 1, n_tiles - 1), 0, 0)),
                  pl.BlockSpec((tm, d), row), pl.BlockSpec((tm, V7X_LANES), row),
                  pl.BlockSpec((1, d), lambda i: (0, 0)), pl.BlockSpec(memory_space=pl.ANY)],
        out_specs=pl.BlockSpec((tm, d), row),
        out_shape=jax.ShapeDtypeStruct((n, d), F32),
        scratch_shapes=[pltpu.VMEM((2, TOP_K, tm * V7X_SUBLANES, V7X_LANES), F32), pltpu.SemaphoreType.DMA((2,))],
        compiler_params=pltpu.CompilerParams(dimension_semantics=("arbitrary",),
                                             vmem_limit_bytes=_vmem_limit(vmem)),
        name="final",
    )(dest3, dest3, x1, tg, g.reshape(1, d), ys.reshape(-1, *tile))


def kernel(x_prompt, x_sample, cache_k, cache_v, state_ssm_re, state_ssm_im, norm_mix_g, w_in, attn_out_norm_g, ssm_a_re, ssm_a_im, ssm_log_dt, ssm_b_re, ssm_b_im, ssm_c_re, ssm_c_im, ssm_d, w_glu, b_glu, ssm_out_norm_g, w_out, norm_ffn_g, w_router, b_router, w_moe1, b_moe1, w_moe2, b_moe2, norm_final_g):
    depth = w_in.shape[0]
    assert depth == 1, "single-layer trunk"
    bp, s, d = x_prompt.shape
    bs, t, _ = x_sample.shape
    n_buf = cache_k.shape[2]
    n_heads, head_dim = cache_k.shape[3], cache_k.shape[4]
    assert head_dim == HEAD_DIM
    a = n_heads * head_dim
    g = ssm_a_re.shape[1]
    gn = g * SSM_STATE
    assert s % (max(dd for _, dd in DILATED_GROUPS) * KEYS_BACK) == 0
    np_, ns = bp * s, bs * t
    tm_p, tm_s = 512, ns
    l = 0
    ssm_p = (ssm_a_re[l], ssm_a_im[l], ssm_log_dt[l], ssm_b_re[l], ssm_b_im[l], ssm_c_re[l], ssm_c_im[l],
             ssm_d[l])
    tail = dict(attn_out_norm_g=attn_out_norm_g[l], w_glu=w_glu[l], b_glu=b_glu[l],
                ssm_out_norm_g=ssm_out_norm_g[l], w_out=w_out[l], norm_ffn_g=norm_ffn_g[l],
                w_router=w_router[l], b_router=b_router[l])

    xp = x_prompt.reshape(np_, d)
    qp, kp, vp, up, ktp, vtp = _proj(xp, norm_mix_g[l], w_in[l], a, tm_p, seq=s)
    attn_p = _attn_prompt(qp.reshape(bp, s, a), kp.reshape(bp, s, a), vp.reshape(bp, s, a))
    zeros = jnp.zeros((bp, gn), F32)
    tables = _ssm_tables(*ssm_p, PROMPT_CHUNK)
    y_p, hp_re, hp_im = _ssm_chunks_on_lanes(up.reshape(bp, s, -1), zeros, zeros, _ssm_tables_lanes(tables),
                                             PROMPT_CHUNK)
    seen = jnp.zeros((1, V7X_LANES), F32)
    x1_p, hf_p, ti_p, tg_p, seen = _finish(xp, attn_p.reshape(np_, a), y_p.reshape(np_, -1), tail, seen, tm_p)

    xs = x_sample.reshape(ns, d)
    qs, ks, vs, us = _proj(xs, norm_mix_g[l], w_in[l], a, tm_s)
    by_dim = lambda c: c.transpose(0, 2, 3, 1).reshape(bs, a, n_buf)
    attn_s = _attn_sample(qs.reshape(bs, t, a), ks.reshape(bs, t, a), vs.reshape(bs, t, a),
                          by_dim(cache_k[l]), by_dim(cache_v[l]))
    y_s, hs_re, hs_im = _ssm(us.reshape(bs, t, -1), state_ssm_re[l].reshape(bs, gn),
                             state_ssm_im[l].reshape(bs, gn), _ssm_tables_rows(tables, t), t)
    x1_s, hf_s, ti_s, tg_s, seen = _finish(xs, attn_s.reshape(ns, a), y_s.reshape(ns, -1), tail, seen, tm_s)

    sizes = seen[0, :N_EXPERTS].astype(I32)
    starts = jnp.cumsum(sizes) - sizes
    experts = jnp.arange(N_EXPERTS, dtype=I32)[None, None, :]

    def sorted_row(ti):
        hit = ti[:, :TOP_K, None] == experts
        return jnp.sum(jnp.where(hit, starts[None, None, :], 0), axis=-1) + ti[:, TOP_K:2 * TOP_K]

    dest_p, dest_s = sorted_row(ti_p), sorted_row(ti_s)
    n_rows = (np_ + ns) * TOP_K
    assert n_rows % MOE_TILE == 0
    xs_sorted = _dispatch(hf_p, dest_p, hf_s, dest_s, tm_p).reshape(-1, V7X_LANES)
    ys = _moe_experts(xs_sorted, _moe_visits(sizes, n_rows), w_moe1[l], b_moe1[l], w_moe2[l], b_moe2[l])
    y_prompt = _final(x1_p, tg_p, dest_p, ys, norm_final_g, tm_p).reshape(bp, s, d)
    y_sample = _final(x1_s, tg_s, dest_s, ys, norm_final_g, tm_s).reshape(bs, t, d)

    keep = min(max(w for w, _ in DILATED_GROUPS), s)
    k5 = lambda z, b_, s_: z.reshape(1, b_, s_, n_heads, head_dim)
    by_pos = lambda zt: zt.reshape(bp, n_heads, head_dim, s).transpose(0, 3, 1, 2)[None]
    st = lambda z, b_: z.reshape(1, b_, g, SSM_STATE)
    return (y_prompt, y_sample,
            by_pos(ktp)[:, :, s - keep:], by_pos(vtp)[:, :, s - keep:], st(hp_re, bp), st(hp_im, bp),
            k5(ks, bs, t), k5(vs, bs, t), st(hs_re, bs), st(hs_im, bs))
```

```python
import functools
import math

import numpy as np
import jax
import jax.numpy as jnp
from jax import lax
from jax.experimental import pallas as pl
from jax.experimental.pallas import tpu as pltpu

F32 = jnp.float32
I32 = jnp.int32

V7X_LANES = 128
V7X_SUBLANES = 8
V7X_VMEM_BYTES = 64 * 1024 * 1024

HEAD_DIM = 64
HEADS_PER_LANE_TILE = V7X_LANES // HEAD_DIM
DILATED_GROUPS = ((128, 1), (512, 4), (2048, 16))
KEYS_BACK = 128
SSM_GROUP = 16
SSM_STATE = 64
N_EXPERTS = 32
TOP_K = 4
SWIGLU_LIMIT = 7.0
SWIGLU_ALPHA = 1.702
RMS_EPS = 1e-5
NEG_INF = -1e30
PROMPT_CHUNK = 16
MOE_TILE = 512
MOE_SUB = 256


def _vmem_limit(nbytes):
    return int(min(nbytes + (8 << 20), V7X_VMEM_BYTES - (8 << 20)))


def _rms(x, g):
    return x * lax.rsqrt(jnp.mean(x * x, axis=-1, keepdims=True) + RMS_EPS) * g


def _to_token_tiles(ref, val):
    rows, width = val.shape
    assert width == V7X_SUBLANES * V7X_LANES and ref.shape == (rows * V7X_SUBLANES, V7X_LANES)
    for j in range(V7X_SUBLANES):
        ref[pl.ds(j, rows, stride=V7X_SUBLANES), :] = val[:, j * V7X_LANES:(j + 1) * V7X_LANES]


def _from_token_tiles(ref):
    rows = ref.shape[0] // V7X_SUBLANES
    return jnp.concatenate([ref[pl.ds(j, rows, stride=V7X_SUBLANES), :] for j in range(V7X_SUBLANES)], axis=1)


def _proj_kernel(x_ref, g_ref, w_ref, q_ref, k_ref, v_ref, u_ref, *kv_t, attn_width, q_scale):
    h = _rms(x_ref[...], g_ref[...])
    z = jnp.dot(h, w_ref[...], preferred_element_type=F32)
    a = attn_width
    q_ref[...] = z[:, :a] * q_scale
    k_ref[...] = z[:, a:2 * a]
    v_ref[...] = z[:, 2 * a:3 * a]
    u_ref[...] = z[:, 3 * a:]
    if kv_t:
        kt_ref, vt_ref = kv_t
        kt_ref[...] = z[:, a:2 * a].T
        vt_ref[...] = z[:, 2 * a:3 * a].T


def _proj(x2d, norm_g, w_in, attn_width, tm, seq=None):
    n, d = x2d.shape
    mix = w_in.shape[1]
    ssm_width = mix - 3 * attn_width
    row = lambda i: (i, 0)
    const = lambda i: (0, 0)
    out_shape = (jax.ShapeDtypeStruct((n, attn_width), F32),) * 3 + (
        jax.ShapeDtypeStruct((n, ssm_width), F32),)
    out_specs = (pl.BlockSpec((tm, attn_width), row),) * 3 + (pl.BlockSpec((tm, ssm_width), row),)
    if seq is not None:
        assert seq % tm == 0 and tm % V7X_LANES == 0
        per_seq = seq // tm
        out_shape += (jax.ShapeDtypeStruct((n // seq, attn_width, seq), F32),) * 2
        out_specs += (pl.BlockSpec((None, attn_width, tm), lambda i: (i // per_seq, 0, i % per_seq)),) * 2
    vmem = 2 * 4 * (tm * d + d * mix + tm * mix + 2 * tm * attn_width) + 4 * tm * mix
    return pl.pallas_call(
        functools.partial(_proj_kernel, attn_width=attn_width, q_scale=HEAD_DIM ** -0.5),
        grid=(n // tm,),
        in_specs=[pl.BlockSpec((tm, d), row), pl.BlockSpec((1, d), const),
                  pl.BlockSpec((d, mix), const)],
        out_specs=out_specs,
        out_shape=out_shape,
        compiler_params=pltpu.CompilerParams(dimension_semantics=("arbitrary",),
                                             vmem_limit_bytes=_vmem_limit(vmem)),
        name="proj",
    )(x2d, norm_g.reshape(1, d), w_in)


def _attn_prompt_kernel(q_ref, k_ref, v_ref, o_ref, *scratch, seq, unroll):
    nb = KEYS_BACK
    n_groups = len(DILATED_GROUPS)
    m_sc, l_sc, acc_sc = scratch[:n_groups], scratch[n_groups:2 * n_groups], scratch[2 * n_groups:]
    lane = lax.broadcasted_iota(I32, (1, V7X_LANES), 1)
    head_a = lane < HEAD_DIM
    qi = lax.broadcasted_iota(I32, (nb, 2 * nb), 0)
    kj = lax.broadcasted_iota(I32, (nb, 2 * nb), 1)
    rel = nb + qi - kj
    band = (rel >= 0) & (rel <= nb)

    def rows(ref, start, d):
        if d == 1:
            return ref[pl.ds(pl.multiple_of(start, nb), nb), :]
        return ref[pl.ds(start, nb, stride=d), :]

    def put(ref, start, d, val):
        if d == 1:
            ref[pl.ds(pl.multiple_of(start, nb), nb), :] = val
        else:
            ref[pl.ds(start, nb, stride=d), :] = val

    for gi, (w, d) in enumerate(DILATED_GROUPS):
        assert w // d == nb
        nblk = seq // (d * nb)

        def block(t, gi=gi, d=d, nblk=nblk):
            r = t // nblk
            n = t % nblk
            cur = r + d * nb * n
            prev = r + d * nb * jnp.maximum(n - 1, 0)
            q = rows(q_ref, cur, d)
            kk = jnp.concatenate([rows(k_ref, prev, d), rows(k_ref, cur, d)], axis=0)
            vv = jnp.concatenate([rows(v_ref, prev, d), rows(v_ref, cur, d)], axis=0)
            qs = jnp.concatenate([jnp.where(head_a, q, 0.0), jnp.where(head_a, 0.0, q)], axis=0)
            s = lax.dot_general(qs, kk, (((1,), (1,)), ((), ())), preferred_element_type=F32)
            valid = band & (kj >= jnp.where(n > 0, 0, nb))
            s = jnp.where(jnp.concatenate([valid, valid], axis=0), s, NEG_INF)
            m = jnp.max(s, axis=-1, keepdims=True)
            p = jnp.exp(s - m)
            l = jnp.sum(p, axis=-1, keepdims=True)
            o = jnp.dot(p, vv, preferred_element_type=F32)
            o = jnp.where(head_a, o[:nb], o[nb:])
            put(m_sc[gi], cur, d, jnp.where(head_a, m[:nb], m[nb:]))
            put(l_sc[gi], cur, d, jnp.where(head_a, l[:nb], l[nb:]))
            put(acc_sc[gi], cur, d, o)

        def blocks(tt, carry, block=block):
            for u in range(unroll):
                block(tt * unroll + u)
            return carry

        lax.fori_loop(0, seq // nb // unroll, blocks, 0)

    def merge(n, carry):
        sl = pl.ds(pl.multiple_of(n * nb, nb), nb)
        ms = [m[sl, :] for m in m_sc]
        m_all = functools.reduce(jnp.maximum, ms)
        cs = [jnp.exp(m - m_all) for m in ms]
        den = functools.reduce(jnp.add, [c * l[sl, :] for c, l in zip(cs, l_sc)])
        num = functools.reduce(jnp.add, [c * acc[sl, :] for c, acc in zip(cs, acc_sc)])
        o_ref[sl, :] = num / den
        return carry

    lax.fori_loop(0, seq // nb, merge, 0, unroll=2)


def _attn_prompt(q, k, v, unroll=16):
    b, s, a = q.shape
    n_scratch = 3 * len(DILATED_GROUPS)
    spec = pl.BlockSpec((None, s, V7X_LANES), lambda i, j: (i, 0, j))
    vmem = 4 * s * V7X_LANES * (2 * 4 + n_scratch)
    return pl.pallas_call(
        functools.partial(_attn_prompt_kernel, seq=s, unroll=unroll),
        grid=(b, a // V7X_LANES),
        in_specs=[spec, spec, spec],
        out_specs=spec,
        out_shape=jax.ShapeDtypeStruct((b, s, a), F32),
        scratch_shapes=[pltpu.VMEM((s, V7X_LANES), F32)] * n_scratch,
        compiler_params=pltpu.CompilerParams(dimension_semantics=("arbitrary", "arbitrary"),
                                             vmem_limit_bytes=_vmem_limit(vmem)),
        name="attn_prompt",
    )(q, k, v)


def _sample_key_counts(n_buf, n_new):
    cnt = np.zeros((n_new, n_buf + n_new), np.float32)
    for w, d in DILATED_GROUPS:
        for i in range(n_new):
            for j in range(w // d + 1):
                idx = n_buf + i - d * j
                if idx >= 0:
                    cnt[i, idx] += 1.0
    return cnt[:, :n_buf], cnt[:, n_buf:]


def _attn_sample_kernel(q_ref, kn_ref, vn_ref, kt_ref, vt_ref, cnt_ref, o_ref, *, n_heads, cnt_new):
    n_new, a = q_ref.shape
    rows = n_new * n_heads
    lane_head = lax.broadcasted_iota(I32, (rows, a), 1) // HEAD_DIM
    row_head = lax.broadcasted_iota(I32, (rows, a), 0) % n_heads
    row_query = lax.broadcasted_iota(I32, (rows, 1), 0) // n_heads
    own = lane_head == row_head
    q = q_ref[...]
    qb = jnp.where(own, jnp.broadcast_to(q[:, None, :], (n_new, n_heads, a)).reshape(rows, a), 0.0)
    s_buf = jnp.dot(qb, kt_ref[...], preferred_element_type=F32)
    cnt = cnt_ref[...]
    cnt = jnp.broadcast_to(cnt[:, None, :], (n_new, n_heads, cnt.shape[-1])).reshape(rows, -1)
    valid = cnt > 0.0
    m = jnp.max(jnp.where(valid, s_buf, NEG_INF), axis=-1, keepdims=True)
    s_new, c_new = [], []
    for j in range(n_new):
        c = functools.reduce(lambda acc, i: jnp.where(row_query == i, float(cnt_new[i, j]), acc),
                             range(n_new), jnp.zeros((rows, 1), F32))
        s = jnp.sum(qb * kn_ref[pl.ds(j, 1), :], axis=-1, keepdims=True)
        m = jnp.maximum(m, jnp.where(c > 0.0, s, NEG_INF))
        s_new.append(s)
        c_new.append(c)
    p = jnp.where(valid, cnt * jnp.exp(s_buf - m), 0.0)
    den = jnp.sum(p, axis=-1, keepdims=True)
    o = lax.dot_general(p, vt_ref[...], (((1,), (1,)), ((), ())), preferred_element_type=F32)
    for j in range(n_new):
        pj = jnp.where(c_new[j] > 0.0, c_new[j] * jnp.exp(s_new[j] - m), 0.0)
        den = den + pj
        o = o + pj * vn_ref[pl.ds(j, 1), :]
    o = jnp.where(own, o / den, 0.0)
    o_ref[...] = jnp.sum(o.reshape(n_new, n_heads, a), axis=1)


def _attn_sample(q, k_new, v_new, cache_kt, cache_vt):
    b, t, a = q.shape
    n_buf = cache_kt.shape[-1]
    n_heads = a // HEAD_DIM
    cnt_buf, cnt_new = _sample_key_counts(n_buf, t)
    new = pl.BlockSpec((None, t, a), lambda i: (i, 0, 0))
    buf = pl.BlockSpec((None, a, n_buf), lambda i: (i, 0, 0))
    vmem = 4 * (2 * 2 * n_buf * a + 8 * t * n_heads * n_buf)
    return pl.pallas_call(
        functools.partial(_attn_sample_kernel, n_heads=n_heads, cnt_new=cnt_new),
        grid=(b,),
        in_specs=[new, new, new, buf, buf, pl.BlockSpec((t, n_buf), lambda i: (0, 0))],
        out_specs=new,
        out_shape=jax.ShapeDtypeStruct((b, t, a), F32),
        compiler_params=pltpu.CompilerParams(dimension_semantics=("arbitrary",),
                                             vmem_limit_bytes=_vmem_limit(vmem)),
        name="attn_sample",
    )(q, k_new, v_new, cache_kt, cache_vt, jnp.asarray(cnt_buf))


def _block_toeplitz_kernel(k_ref, toep_ref, tt_ref, *, chunk):
    c = k_ref.shape[1]
    for i in range(k_ref.shape[0]):
        k = k_ref[i]
        for s in range(chunk):
            shifted = k if s == 0 else jnp.concatenate(
                [jnp.zeros((c, s * c), F32), k[:, :(chunk - s) * c]], axis=1)
            toep_ref[i, pl.ds(s * c, c), :] = shifted
        tt_ref[i] = toep_ref[i].T


def _block_toeplitz(kern, chunk):
    g, c, lc = kern.shape
    per_step = 4
    spec = pl.BlockSpec((per_step, lc, lc), lambda i: (i, 0, 0))
    return pl.pallas_call(
        functools.partial(_block_toeplitz_kernel, chunk=chunk),
        grid=(g // per_step,),
        in_specs=[pl.BlockSpec((per_step, c, lc), lambda i: (i, 0, 0))],
        out_specs=(spec, spec),
        out_shape=(jax.ShapeDtypeStruct((g, lc, lc), F32),) * 2,
        compiler_params=pltpu.CompilerParams(dimension_semantics=("arbitrary",)),
        name="ssm_toeplitz",
    )(kern)


def _pad_states(m, g):
    half = (np.arange(g) % 2)[:, None, None]
    z = jnp.zeros_like(m)
    return jnp.where(half == 0, jnp.concatenate([m, z], -1), jnp.concatenate([z, m], -1))


def _ssm_tables(a_re, a_im, log_dt, b_re, b_im, c_re, c_im, d_skip, chunk):
    g, n = a_re.shape
    c = b_re.shape[-1]
    lc = chunk * c
    dt = jnp.exp(log_dt)[:, None]
    x, y = a_re * dt, a_im * dt
    ex = jnp.exp(x)
    ar, ai = ex * jnp.cos(y), ex * jnp.sin(y)
    nr = jnp.expm1(x) * jnp.cos(y) - 2.0 * jnp.sin(0.5 * y) ** 2
    ni = ai
    den = a_re * a_re + a_im * a_im
    fr = (nr * a_re + ni * a_im) / den
    fi = (ni * a_re - nr * a_im) / den
    b_re_t, b_im_t = b_re.transpose(0, 2, 1), b_im.transpose(0, 2, 1)
    bbr = fr[:, None, :] * b_re_t - fi[:, None, :] * b_im_t
    bbi = fr[:, None, :] * b_im_t + fi[:, None, :] * b_re_t
    pr, pi = [jnp.ones_like(ar)], [jnp.zeros_like(ar)]
    for _ in range(chunk):
        pr.append(pr[-1] * ar - pi[-1] * ai)
        pi.append(pr[-2] * ai + pi[-1] * ar)
    pr, pi = jnp.stack(pr, 1), jnp.stack(pi, 1)
    cpr = c_re[:, None] * pr[:, :, None, :] - c_im[:, None] * pi[:, :, None, :]
    cpi = c_re[:, None] * pi[:, :, None, :] + c_im[:, None] * pr[:, :, None, :]
    kern = jnp.sum(cpr[:, None, :chunk] * bbr[:, :, None, None] - cpi[:, None, :chunk] * bbi[:, :, None, None],
                   axis=-1).reshape(g, c, lc)
    toep, tt = _block_toeplitz(kern, chunk)
    qr, qi = pr[:, chunk - 1::-1][:, :, None, :], pi[:, chunk - 1::-1][:, :, None, :]
    wr = (qr * bbr[:, None] - qi * bbi[:, None]).reshape(g, lc, n)
    wi = (qr * bbi[:, None] + qi * bbr[:, None]).reshape(g, lc, n)
    vr = cpr[:, 1:].reshape(g, lc, n)
    vi = -cpi[:, 1:].reshape(g, lc, n)
    return dict(chunk=chunk, toep=toep, tt=tt, wr=wr, wi=wi, vr=vr, vi=vi, d=d_skip, pr=pr, pi=pi)


def _ssm_tables_lanes(tab):
    g, chunk = tab["d"].shape[0], tab["chunk"]
    tr = lambda m: m.transpose(0, 2, 1)
    return dict(tt=tab["tt"], wt_re=tr(_pad_states(tab["wr"], g)), wt_im=tr(_pad_states(tab["wi"], g)),
                v_re=_pad_states(tab["vr"], g), v_im=_pad_states(tab["vi"], g),
                dcol=jnp.tile(tab["d"], (1, chunk))[:, :, None],
                al_re=tab["pr"][:, chunk].reshape(1, -1), al_im=tab["pi"][:, chunk].reshape(1, -1))


def _ssm_tables_rows(tab, chunk):
    g, c = tab["d"].shape
    big, lc = tab["chunk"], chunk * c
    tr = lambda m: m.transpose(0, 2, 1)
    tail = slice((big - chunk) * c, big * c)
    return dict(toep=tab["toep"][:, :lc, :lc],
                wr=_pad_states(tab["wr"][:, tail], g), wi=_pad_states(tab["wi"][:, tail], g),
                vr=tr(_pad_states(tab["vr"][:, :lc], g)), vi=tr(_pad_states(tab["vi"][:, :lc], g)),
                d=jnp.tile(tab["d"], (1, chunk))[:, None, :],
                al_re=tab["pr"][:, chunk].reshape(1, -1), al_im=tab["pi"][:, chunk].reshape(1, -1))


def _ssm_state_kernel(u_ref, wr_ref, wi_ref, er_ref, ei_ref):
    er = jnp.dot(u_ref[0], wr_ref[0], preferred_element_type=F32)
    ei = jnp.dot(u_ref[0], wi_ref[0], preferred_element_type=F32)
    er_ref[...] = er + jnp.dot(u_ref[1], wr_ref[1], preferred_element_type=F32)
    ei_ref[...] = ei + jnp.dot(u_ref[1], wi_ref[1], preferred_element_type=F32)


def _ssm_scan_kernel(er_ref, ei_ref, ar_ref, ai_ref, hr_ref, hi_ref, xr_ref, xi_ref, lr_ref, li_ref,
                     *, n_chunks, nb):
    shape = hr_ref.shape
    ar = jnp.broadcast_to(ar_ref[...], shape)
    ai = jnp.broadcast_to(ai_ref[...], shape)

    def step(j, carry):
        xr, xi = carry
        sl = pl.ds(pl.multiple_of(j * nb, nb), nb)
        xr_ref[sl, :] = xr
        xi_ref[sl, :] = xi
        return (ar * xr - ai * xi + er_ref[sl, :], ar * xi + ai * xr + ei_ref[sl, :])

    xr, xi = lax.fori_loop(0, n_chunks, step, (hr_ref[...], hi_ref[...]))
    lr_ref[...] = xr
    li_ref[...] = xi


def _ssm_out_kernel(u_ref, t_ref, d_ref, xr_ref, xi_ref, vr_ref, vi_ref, y_ref):
    xr = xr_ref[...]
    xi = xi_ref[...]
    for a in range(2):
        u = u_ref[a]
        y = jnp.dot(u, t_ref[a], preferred_element_type=F32) + d_ref[a] * u
        y = y + jnp.dot(xr, vr_ref[a], preferred_element_type=F32)
        y_ref[a] = y + jnp.dot(xi, vi_ref[a], preferred_element_type=F32)


def _ssm_scan(er, ei, tab, h0_re, h0_im, n_chunks):
    r, gn = er.shape
    b = r // n_chunks
    lanes = 4 * V7X_LANES
    col = lambda i: (0, i)
    return pl.pallas_call(
        functools.partial(_ssm_scan_kernel, n_chunks=n_chunks, nb=b),
        grid=(gn // lanes,),
        in_specs=[pl.BlockSpec((r, lanes), col), pl.BlockSpec((r, lanes), col),
                  pl.BlockSpec((1, lanes), col), pl.BlockSpec((1, lanes), col),
                  pl.BlockSpec((b, lanes), col), pl.BlockSpec((b, lanes), col)],
        out_specs=(pl.BlockSpec((r, lanes), col),) * 2 + (pl.BlockSpec((b, lanes), col),) * 2,
        out_shape=(jax.ShapeDtypeStruct((r, gn), F32),) * 2 + (jax.ShapeDtypeStruct((b, gn), F32),) * 2,
        compiler_params=pltpu.CompilerParams(dimension_semantics=("arbitrary",),
                                             vmem_limit_bytes=_vmem_limit(2 * 4 * 4 * r * lanes)),
        name="ssm_scan",
    )(er, ei, tab["al_re"], tab["al_im"], h0_re, h0_im)


def _ssm_state_t_kernel(a_ref, wr_ref, wi_ref, er_ref, ei_ref):
    def seq(b, carry):
        a0, a1 = a_ref[0, b], a_ref[1, b]
        er_ref[b] = (jnp.dot(wr_ref[0], a0, preferred_element_type=F32)
                     + jnp.dot(wr_ref[1], a1, preferred_element_type=F32))
        ei_ref[b] = (jnp.dot(wi_ref[0], a0, preferred_element_type=F32)
                     + jnp.dot(wi_ref[1], a1, preferred_element_type=F32))
        return carry

    lax.fori_loop(0, a_ref.shape[1], seq, 0, unroll=True)


def _ssm_out_t_kernel(a_ref, t_ref, d_ref, xr_ref, xi_ref, vr_ref, vi_ref, y_ref):
    def seq(b, carry):
        xr, xi = xr_ref[b], xi_ref[b]
        for a in range(2):
            at = a_ref[a, b]
            y = jnp.dot(t_ref[a], at, preferred_element_type=F32) + d_ref[a] * at
            y = y + jnp.dot(vr_ref[a], xr, preferred_element_type=F32)
            y_ref[a, b] = y + jnp.dot(vi_ref[a], xi, preferred_element_type=F32)
        return carry

    lax.fori_loop(0, a_ref.shape[1], seq, 0, unroll=True)


def _ssm_chunks_on_lanes(u, h0_re, h0_im, tab, chunk):
    b, s, width = u.shape
    g = width // SSM_GROUP
    n_chunks = s // chunk
    lc = chunk * SSM_GROUP
    assert n_chunks == V7X_LANES
    at = (u.reshape(b, n_chunks, chunk * width).transpose(0, 2, 1)
          .reshape(b, chunk, g, SSM_GROUP, n_chunks).transpose(2, 0, 1, 3, 4).reshape(g, b, lc, n_chunks))
    pair4 = lambda i: (i, 0, 0, 0)
    pair3 = lambda i: (i, 0, 0)
    cp = lambda vmem: pltpu.CompilerParams(dimension_semantics=("arbitrary",),
                                           vmem_limit_bytes=_vmem_limit(vmem))
    a_spec = pl.BlockSpec((2, b, lc, n_chunks), pair4)
    st_spec = pl.BlockSpec((None, b, V7X_LANES, n_chunks), pair4)
    w_spec = pl.BlockSpec((2, V7X_LANES, lc), pair3)
    a_bytes = 2 * b * lc * n_chunks
    st_bytes = b * V7X_LANES * n_chunks
    et_re, et_im = pl.pallas_call(
        _ssm_state_t_kernel,
        grid=(g // 2,),
        in_specs=[a_spec, w_spec, w_spec],
        out_specs=(st_spec,) * 2,
        out_shape=(jax.ShapeDtypeStruct((g // 2, b, V7X_LANES, n_chunks), F32),) * 2,
        compiler_params=cp(2 * 4 * (a_bytes + 2 * st_bytes + 4 * V7X_LANES * lc)),
        name="ssm_state_t",
    )(at, tab["wt_re"], tab["wt_im"])
    rows = lambda m: m.transpose(3, 1, 0, 2).reshape(n_chunks * b, g * SSM_STATE)
    xr, xi, lr, li = _ssm_scan(rows(et_re), rows(et_im), tab, h0_re, h0_im, n_chunks)
    cols = lambda m: m.reshape(n_chunks, b, g // 2, V7X_LANES).transpose(2, 1, 3, 0)
    yt = pl.pallas_call(
        _ssm_out_t_kernel,
        grid=(g // 2,),
        in_specs=[a_spec, pl.BlockSpec((2, lc, lc), pair3), pl.BlockSpec((2, lc, 1), pair3),
                  st_spec, st_spec, pl.BlockSpec((2, lc, V7X_LANES), pair3),
                  pl.BlockSpec((2, lc, V7X_LANES), pair3)],
        out_specs=a_spec,
        out_shape=jax.ShapeDtypeStruct((g, b, lc, n_chunks), F32),
        compiler_params=cp(2 * 4 * (2 * a_bytes + 2 * st_bytes + 2 * lc * lc + 6 * V7X_LANES * lc)),
        name="ssm_out_t",
    )(at, tab["tt"], tab["dcol"], cols(xr), cols(xi), tab["v_re"], tab["v_im"])
    y = (yt.reshape(g, b, chunk, SSM_GROUP, n_chunks).transpose(1, 2, 0, 3, 4)
         .reshape(b, chunk * width, n_chunks).transpose(0, 2, 1).reshape(b, s, width))
    return y, lr, li


def _ssm(u, h0_re, h0_im, tab, chunk):
    b, s, width = u.shape
    g = width // SSM_GROUP
    n_chunks = s // chunk
    lc = chunk * SSM_GROUP
    r = n_chunks * b
    gn = g * SSM_STATE
    ut = u.reshape(b, n_chunks, chunk, g, SSM_GROUP).transpose(3, 1, 0, 2, 4).reshape(g, r, lc)
    pair = lambda i: (i, 0, 0)
    cp = lambda vmem: pltpu.CompilerParams(dimension_semantics=("arbitrary",),
                                           vmem_limit_bytes=_vmem_limit(vmem))
    er, ei = pl.pallas_call(
        _ssm_state_kernel,
        grid=(g // 2,),
        in_specs=[pl.BlockSpec((2, r, lc), pair), pl.BlockSpec((2, lc, V7X_LANES), pair),
                  pl.BlockSpec((2, lc, V7X_LANES), pair)],
        out_specs=(pl.BlockSpec((r, V7X_LANES), lambda i: (0, i)),) * 2,
        out_shape=(jax.ShapeDtypeStruct((r, gn), F32),) * 2,
        compiler_params=cp(2 * 4 * (2 * r * lc + 4 * lc * V7X_LANES + 2 * r * V7X_LANES)),
        name="ssm_state",
    )(ut, tab["wr"], tab["wi"])
    xr, xi, lr, li = _ssm_scan(er, ei, tab, h0_re, h0_im, n_chunks)
    y = pl.pallas_call(
        _ssm_out_kernel,
        grid=(g // 2,),
        in_specs=[pl.BlockSpec((2, r, lc), pair), pl.BlockSpec((2, lc, lc), pair),
                  pl.BlockSpec((2, 1, lc), pair),
                  pl.BlockSpec((r, V7X_LANES), lambda i: (0, i)),
                  pl.BlockSpec((r, V7X_LANES), lambda i: (0, i)),
                  pl.BlockSpec((2, V7X_LANES, lc), pair), pl.BlockSpec((2, V7X_LANES, lc), pair)],
        out_specs=pl.BlockSpec((2, r, lc), pair),
        out_shape=jax.ShapeDtypeStruct((g, r, lc), F32),
        compiler_params=cp(2 * 4 * (4 * r * lc + 2 * lc * lc + 2 * r * V7X_LANES + 4 * V7X_LANES * lc)),
        name="ssm_out",
    )(ut, tab["toep"], tab["d"], xr, xi, tab["vr"], tab["vi"])
    y = y.reshape(g, n_chunks, b, chunk, SSM_GROUP).transpose(2, 1, 3, 0, 4).reshape(b, s, width)
    return y, lr, li


def _finish_kernel(x_ref, attn_ref, ssm_ref, ga_ref, wg_ref, bg_ref, gs_ref, wo_ref, gf_ref,
                   wr_ref, br_ref, base_ref, x1_ref, hf_ref, ti_ref, tg_ref, cnt_ref, seen):
    a = attn_ref.shape[-1]
    tm = x_ref.shape[0]

    @pl.when(pl.program_id(0) == 0)
    def _():
        seen[...] = base_ref[...]

    g = jax.nn.gelu(ssm_ref[...])
    gate = jax.nn.sigmoid(jnp.dot(g, wg_ref[...], preferred_element_type=F32) + bg_ref[...])
    mixed_a = _rms(attn_ref[...], ga_ref[...])
    mixed_s = _rms(g * gate, gs_ref[...])
    x1 = (x_ref[...] + jnp.dot(mixed_a, wo_ref[pl.ds(0, a), :], preferred_element_type=F32)
          + jnp.dot(mixed_s, wo_ref[pl.ds(a, wo_ref.shape[0] - a), :], preferred_element_type=F32))
    x1_ref[...] = x1
    hf = _rms(x1, gf_ref[...])
    _to_token_tiles(hf_ref, hf)
    hf_hi = hf.astype(jnp.bfloat16).astype(F32)
    hf_lo = hf - hf_hi
    small = (jnp.dot(hf_lo, wr_ref[1], preferred_element_type=F32)
             + jnp.dot(hf_hi, wr_ref[1], preferred_element_type=F32)
             + jnp.dot(hf_lo, wr_ref[0], preferred_element_type=F32))
    logits = jnp.dot(hf_hi, wr_ref[0], preferred_element_type=F32) + small + br_ref[...]
    lane = lax.broadcasted_iota(I32, logits.shape, 1)
    work = logits
    vals, idxs = [], []
    for _ in range(TOP_K):
        m = jnp.max(work, axis=-1, keepdims=True)
        idx = jnp.min(jnp.where(work == m, lane, V7X_LANES), axis=-1, keepdims=True)
        vals.append(m)
        idxs.append(idx)
        work = jnp.where(lane == idx, -jnp.inf, work)
    exps = [jnp.exp(v - vals[0]) for v in vals]
    den = exps[0]
    for e in exps[1:]:
        den = den + e
    picked = jnp.zeros(logits.shape, F32)
    for k in range(TOP_K):
        picked = picked + jnp.where(lane == idxs[k], 1.0, 0.0)
    tri = jnp.where(lax.broadcasted_iota(I32, (tm, tm), 0) > lax.broadcasted_iota(I32, (tm, tm), 1), 1.0, 0.0)
    before = jnp.dot(tri, picked, preferred_element_type=F32) + seen[...]
    ti = jnp.zeros(logits.shape, I32)
    tg = jnp.zeros(logits.shape, F32)
    for k in range(TOP_K):
        rank = jnp.sum(jnp.where(lane == idxs[k], before, 0.0), axis=-1, keepdims=True)
        ti = jnp.where(lane == k, idxs[k], ti)
        ti = jnp.where(lane == TOP_K + k, rank.astype(I32), ti)
        tg = jnp.where(lane == k, exps[k] / den, tg)
    ti_ref[...] = ti
    tg_ref[...] = tg
    seen[...] = seen[...] + jnp.sum(picked, axis=0, keepdims=True)
    cnt_ref[...] = seen[...]


def _finish(x2d, attn, ssm_y, p, seen, tm):
    n, d = x2d.shape
    a = attn.shape[1]
    sw = ssm_y.shape[1]
    ne = p["w_router"].shape[1]
    w_router = jnp.pad(p["w_router"], ((0, 0), (0, V7X_LANES - ne)))
    w_hi = w_router.astype(jnp.bfloat16).astype(F32)
    w_router = jnp.stack([w_hi, w_router - w_hi])
    b_router = jnp.pad(p["b_router"], (0, V7X_LANES - ne), constant_values=-jnp.inf).reshape(1, V7X_LANES)
    row = lambda i: (i, 0)
    full = lambda arr: pl.BlockSpec(arr.shape, lambda i: (0,) * arr.ndim)
    ins = [x2d, attn, ssm_y, p["attn_out_norm_g"].reshape(1, a), p["w_glu"], p["b_glu"].reshape(1, sw),
           p["ssm_out_norm_g"].reshape(1, sw), p["w_out"], p["norm_ffn_g"].reshape(1, d),
           w_router, b_router, seen]
    in_specs = [pl.BlockSpec((tm, d), row), pl.BlockSpec((tm, a), row), pl.BlockSpec((tm, sw), row)]
    in_specs += [full(arr) for arr in ins[3:]]
    vmem = 2 * 4 * (tm * (3 * d + a + sw + 2 * V7X_LANES) + sw * sw + d * d + 2 * d * V7X_LANES) + 16 * tm * d
    return pl.pallas_call(
        _finish_kernel,
        grid=(n // tm,),
        in_specs=in_specs,
        out_specs=(pl.BlockSpec((tm, d), row), pl.BlockSpec((tm * V7X_SUBLANES, V7X_LANES), row),
                   pl.BlockSpec((tm, V7X_LANES), row), pl.BlockSpec((tm, V7X_LANES), row),
                   pl.BlockSpec((1, V7X_LANES), lambda i: (0, 0))),
        out_shape=(jax.ShapeDtypeStruct((n, d), F32), jax.ShapeDtypeStruct((n * V7X_SUBLANES, V7X_LANES), F32),
                   jax.ShapeDtypeStruct((n, V7X_LANES), I32), jax.ShapeDtypeStruct((n, V7X_LANES), F32),
                   jax.ShapeDtypeStruct((1, V7X_LANES), F32)),
        scratch_shapes=[pltpu.VMEM((1, V7X_LANES), F32)],
        compiler_params=pltpu.CompilerParams(dimension_semantics=("arbitrary",),
                                             vmem_limit_bytes=_vmem_limit(vmem)),
        name="finish",
    )(*ins)


def _tile_copy(src_ref, src_row, dst_ref, dst_row, sem):
    return pltpu.make_async_copy(src_ref.at[src_row], dst_ref.at[dst_row], sem)


def _dispatch_kernel(dest_ref, hp_ref, hs_ref, xs_ref, sem, *, n_tiles_p):
    def scatter(hf_ref):
        tm = hf_ref.shape[0]

        def tokens(i, carry):
            for u in range(V7X_SUBLANES):
                r = i * V7X_SUBLANES + u
                for k in range(TOP_K):
                    _tile_copy(hf_ref, r, xs_ref, dest_ref[0, 0, r * TOP_K + k], sem).start(priority=k % 2)
            return carry

        lax.fori_loop(0, tm // V7X_SUBLANES, tokens, 0)
        rows = pl.ds(0, tm * TOP_K)
        pltpu.make_async_copy(xs_ref.at[rows], xs_ref.at[rows], sem).wait()

    @pl.when(pl.program_id(0) < n_tiles_p)
    def _():
        scatter(hp_ref)

    @pl.when(pl.program_id(0) == n_tiles_p)
    def _():
        scatter(hs_ref)


def _dispatch(hf_p, dest_p, hf_s, dest_s, tm):
    tile = (V7X_SUBLANES, V7X_LANES)
    hf_p, hf_s = hf_p.reshape(-1, *tile), hf_s.reshape(-1, *tile)
    n_p, n_s = hf_p.shape[0], hf_s.shape[0]
    assert n_p % tm == 0 and n_s <= tm
    n_tiles_p = n_p // tm
    dest3 = jnp.concatenate([dest_p.reshape(n_tiles_p, tm * TOP_K),
                             jnp.pad(dest_s.reshape(1, n_s * TOP_K), ((0, 0), (0, (tm - n_s) * TOP_K)))])
    dest3 = dest3.reshape(n_tiles_p + 1, 1, tm * TOP_K)
    return pl.pallas_call(
        functools.partial(_dispatch_kernel, n_tiles_p=n_tiles_p),
        grid=(n_tiles_p + 1,),
        in_specs=[pl.BlockSpec((1, 1, tm * TOP_K), lambda i: (i, 0, 0), memory_space=pltpu.SMEM),
                  pl.BlockSpec((tm, *tile), lambda i: (jnp.minimum(i, n_tiles_p - 1), 0, 0)),
                  pl.BlockSpec((n_s, *tile), lambda i: (0, 0, 0))],
        out_specs=pl.BlockSpec(memory_space=pl.ANY),
        out_shape=jax.ShapeDtypeStruct(((n_p + n_s) * TOP_K, *tile), F32),
        scratch_shapes=[pltpu.SemaphoreType.DMA(())],
        compiler_params=pltpu.CompilerParams(dimension_semantics=("arbitrary",)),
        name="dispatch",
    )(dest3, hf_p, hf_s)


def _moe_kernel(vt_ref, ve_ref, nx_ref, lo_ref, hi_ref, x_ref, w1_ref, b1_ref, w2_ref, b2_ref, y_ref,
                w1s, w2s, w1b, w2b, sem, *, d_ff):
    v = pl.program_id(0)
    lo, hi = lo_ref[v], hi_ref[v]
    prev = jnp.maximum(v - 1, 0)
    first = jnp.logical_or(v == 0, vt_ref[v] != vt_ref[prev])

    def fetch(e):
        return (pltpu.make_async_copy(w1_ref.at[e], w1s, sem.at[0]),
                pltpu.make_async_copy(w2_ref.at[e], w2s, sem.at[1]))

    @pl.when(v == 0)
    def _():
        for cp in fetch(ve_ref[0]):
            cp.start()

    @pl.when(jnp.logical_or(v == 0, ve_ref[v] != ve_ref[prev]))
    def _():
        for cp in fetch(ve_ref[v]):
            cp.wait()
        w1b[...] = w1s[...].astype(jnp.bfloat16)
        w2b[...] = w2s[...].astype(jnp.bfloat16)

        @pl.when(nx_ref[v] >= 0)
        def _():
            for cp in fetch(nx_ref[v]):
                cp.start()

    def swiglu(x_tiles):
        x = _from_token_tiles(x_tiles).astype(jnp.bfloat16)
        a = jnp.dot(x, w1b[...], preferred_element_type=F32) + b1_ref[0]
        x_glu = jnp.minimum(a[:, :d_ff], SWIGLU_LIMIT)
        x_lin = jnp.clip(a[:, d_ff:], -SWIGLU_LIMIT, SWIGLU_LIMIT)
        mid = x_glu * jax.nn.sigmoid(SWIGLU_ALPHA * x_glu) * (x_lin + 1.0)
        return jnp.dot(mid.astype(jnp.bfloat16), w2b[...], preferred_element_type=F32) + b2_ref[0]

    whole = jnp.logical_and(lo == 0, hi == MOE_TILE)

    @pl.when(whole)
    def _():
        _to_token_tiles(y_ref, swiglu(x_ref))

    for r0 in range(0, MOE_TILE, MOE_SUB):
        sub = pl.ds(r0 * V7X_SUBLANES, MOE_SUB * V7X_SUBLANES)
        x_sub, y_sub = x_ref.at[sub, :], y_ref.at[sub, :]
        touched = jnp.logical_and(jnp.logical_not(whole), jnp.logical_and(lo < r0 + MOE_SUB, hi > r0))

        @pl.when(touched)
        def _(r0=r0, x_sub=x_sub, y_sub=y_sub):
            y = swiglu(x_sub)
            rows = r0 + lax.broadcasted_iota(I32, (MOE_SUB, 1), 0)
            mine = (rows >= lo) & (rows < hi)

            @pl.when(first)
            def _():
                _to_token_tiles(y_sub, jnp.where(mine, y, 0.0))

            @pl.when(jnp.logical_not(first))
            def _():
                _to_token_tiles(y_sub, jnp.where(mine, y, _from_token_tiles(y_sub)))

        @pl.when(jnp.logical_and(first, jnp.logical_not(jnp.logical_or(touched, whole))))
        def _(y_sub=y_sub):
            y_sub[...] = jnp.zeros(y_sub.shape, F32)


def _moe_experts(xs, visits, w1, b1, w2, b2):
    ne, d, ff2 = w1.shape
    d_ff = ff2 // 2
    n_visits = visits[0].shape[0]
    row = lambda v, vt, ve, nx, lo, hi: (vt[v], 0)
    exp = lambda v, vt, ve, nx, lo, hi: (ve[v], 0, 0)
    vmem = (4 * 2 * 2 * MOE_TILE * d + (4 + 2) * (d * ff2 + d_ff * d) + 2 * 4 * (ff2 + d)
            + 4 * 4 * MOE_TILE * ff2)
    tiles = pl.BlockSpec((MOE_TILE * V7X_SUBLANES, V7X_LANES), row)
    grid_spec = pltpu.PrefetchScalarGridSpec(
        num_scalar_prefetch=5,
        grid=(n_visits,),
        in_specs=[tiles,
                  pl.BlockSpec(memory_space=pl.ANY), pl.BlockSpec((1, 1, ff2), exp),
                  pl.BlockSpec(memory_space=pl.ANY), pl.BlockSpec((1, 1, d), exp)],
        out_specs=tiles,
        scratch_shapes=[pltpu.VMEM((d, ff2), F32), pltpu.VMEM((d_ff, d), F32),
                        pltpu.VMEM((d, ff2), jnp.bfloat16), pltpu.VMEM((d_ff, d), jnp.bfloat16),
                        pltpu.SemaphoreType.DMA((2,))],
    )
    return pl.pallas_call(
        functools.partial(_moe_kernel, d_ff=d_ff),
        grid_spec=grid_spec,
        out_shape=jax.ShapeDtypeStruct(xs.shape, F32),
        compiler_params=pltpu.CompilerParams(dimension_semantics=("arbitrary",),
                                             vmem_limit_bytes=_vmem_limit(vmem)),
        name="moe",
    )(*visits, xs, w1, b1.reshape(ne, 1, ff2), w2, b2.reshape(ne, 1, d))


def _moe_visits(sizes, n_rows):
    n_tiles = n_rows // MOE_TILE
    n_visits = n_tiles + N_EXPERTS - 1
    ends = jnp.cumsum(sizes)
    starts = ends - sizes
    first_tile = starts // MOE_TILE
    n_vis = jnp.where(sizes > 0, (ends - 1) // MOE_TILE - first_tile + 1, 0)
    vis_end = jnp.cumsum(n_vis)
    total = vis_end[-1]
    v = jnp.clip(jnp.arange(n_visits, dtype=I32), 0, jnp.maximum(total - 1, 0))
    e = jnp.sum((vis_end[None, :] <= v[:, None]).astype(I32), axis=1)
    pick = lambda tab: jnp.sum(jnp.where(e[:, None] == jnp.arange(N_EXPERTS)[None, :], tab[None, :], 0), axis=1)
    tile = pick(first_tile) + v - pick(vis_end - n_vis)
    lo = jnp.maximum(pick(starts), tile * MOE_TILE) - tile * MOE_TILE
    hi = jnp.minimum(pick(ends), (tile + 1) * MOE_TILE) - tile * MOE_TILE
    live = jnp.arange(n_visits) < total
    ids = jnp.arange(N_EXPERTS, dtype=I32)
    later = jnp.where((ids[None, :] > ids[:, None]) & (sizes[None, :] > 0), ids[None, :], N_EXPERTS)
    nxt = jnp.min(later, axis=1)
    nxt = pick(jnp.where(nxt < N_EXPERTS, nxt, -1))
    return (tile.astype(I32), e.astype(I32), nxt.astype(I32), jnp.where(live, lo, 0).astype(I32),
            jnp.where(live, hi, 0).astype(I32))


def _final_kernel(dest_ref, nxt_ref, x_ref, tg_ref, g_ref, ys_ref, y_ref, buf, sem, *, tm):
    i = pl.program_id(0)
    n = pl.num_programs(0)

    def gather(d_ref, slot):
        def tokens(i8, carry):
            for u in range(V7X_SUBLANES):
                r = i8 * V7X_SUBLANES + u
                tile = pl.ds(pl.multiple_of(r * V7X_SUBLANES, V7X_SUBLANES), V7X_SUBLANES)
                for k in range(TOP_K):
                    pltpu.make_async_copy(ys_ref.at[d_ref[0, 0, r * TOP_K + k]], buf.at[slot, k, tile, :],
                                          sem.at[slot]).start(priority=k % 2)
            return carry
        lax.fori_loop(0, tm // V7X_SUBLANES, tokens, 0)

    @pl.when(i == 0)
    def _():
        gather(dest_ref, 0)

    for slot in range(2):
        @pl.when(jnp.logical_and(i + 1 < n, (i + 1) % 2 == slot))
        def _(slot=slot):
            gather(nxt_ref, slot)

    for slot in range(2):
        @pl.when(i % 2 == slot)
        def _(slot=slot):
            pltpu.make_async_copy(buf.at[slot], buf.at[slot], sem.at[slot]).wait()
            tg = tg_ref[...]
            parts = []
            for j in range(V7X_SUBLANES):
                f = None
                for k in range(TOP_K):
                    rows = buf[slot, k, pl.ds(j, tm, stride=V7X_SUBLANES), :]
                    f = tg[:, k:k + 1] * rows if f is None else f + tg[:, k:k + 1] * rows
                parts.append(f)
            y_ref[...] = _rms(x_ref[...] + jnp.concatenate(parts, axis=1), g_ref[...])


def _final(x1, tg, dest, ys, g, tm):
    n, d = x1.shape
    n_tiles = n // tm
    tile = (V7X_SUBLANES, V7X_LANES)
    dest3 = dest.reshape(n_tiles, 1, tm * TOP_K)
    row = lambda i: (i, 0)
    smem = lambda index_map: pl.BlockSpec((1, 1, tm * TOP_K), index_map, memory_space=pltpu.SMEM)
    vmem = 4 * (2 * TOP_K * tm * d + 2 * 2 * tm * d + 2 * tm * V7X_LANES + 4 * tm * d)
    return pl.pallas_call(
        functools.partial(_final_kernel, tm=tm),
        grid=(n_tiles,),
        in_specs=[smem(lambda i: (i, 0, 0)), smem(lambda i: (jnp.minimum(i + 1, n_tiles - 1), 0, 0)),
                  pl.BlockSpec((tm, d), row), pl.BlockSpec((tm, V7X_LANES), row),
                  pl.BlockSpec((1, d), lambda i: (0, 0)), pl.BlockSpec(memory_space=pl.ANY)],
        out_specs=pl.BlockSpec((tm, d), row),
        out_shape=jax.ShapeDtypeStruct((n, d), F32),
        scratch_shapes=[pltpu.VMEM((2, TOP_K, tm * V7X_SUBLANES, V7X_LANES), F32), pltpu.SemaphoreType.DMA((2,))],
        compiler_params=pltpu.CompilerParams(dimension_semantics=("arbitrary",),
                                             vmem_limit_bytes=_vmem_limit(vmem)),
        name="final",
    )(dest3, dest3, x1, tg, g.reshape(1, d), ys.reshape(-1, *tile))


def kernel(x_prompt, x_sample, cache_k, cache_v, state_ssm_re, state_ssm_im, norm_mix_g, w_in, attn_out_norm_g, ssm_a_re, ssm_a_im, ssm_log_dt, ssm_b_re, ssm_b_im, ssm_c_re, ssm_c_im, ssm_d, w_glu, b_glu, ssm_out_norm_g, w_out, norm_ffn_g, w_router, b_router, w_moe1, b_moe1, w_moe2, b_moe2, norm_final_g):
    depth = w_in.shape[0]
    assert depth == 1, "single-layer trunk"
    bp, s, d = x_prompt.shape
    bs, t, _ = x_sample.shape
    n_buf = cache_k.shape[2]
    n_heads, head_dim = cache_k.shape[3], cache_k.shape[4]
    assert head_dim == HEAD_DIM
    a = n_heads * head_dim
    g = ssm_a_re.shape[1]
    gn = g * SSM_STATE
    assert s % (max(dd for _, dd in DILATED_GROUPS) * KEYS_BACK) == 0
    np_, ns = bp * s, bs * t
    tm_p, tm_s = 512, ns
    l = 0
    ssm_p = (ssm_a_re[l], ssm_a_im[l], ssm_log_dt[l], ssm_b_re[l], ssm_b_im[l], ssm_c_re[l], ssm_c_im[l],
             ssm_d[l])
    tail = dict(attn_out_norm_g=attn_out_norm_g[l], w_glu=w_glu[l], b_glu=b_glu[l],
                ssm_out_norm_g=ssm_out_norm_g[l], w_out=w_out[l], norm_ffn_g=norm_ffn_g[l],
                w_router=w_router[l], b_router=b_router[l])

    xp = x_prompt.reshape(np_, d)
    qp, kp, vp, up, ktp, vtp = _proj(xp, norm_mix_g[l], w_in[l], a, tm_p, seq=s)
    attn_p = _attn_prompt(qp.reshape(bp, s, a), kp.reshape(bp, s, a), vp.reshape(bp, s, a))
    zeros = jnp.zeros((bp, gn), F32)
    tables = _ssm_tables(*ssm_p, PROMPT_CHUNK)
    y_p, hp_re, hp_im = _ssm_chunks_on_lanes(up.reshape(bp, s, -1), zeros, zeros, _ssm_tables_lanes(tables),
                                             PROMPT_CHUNK)
    seen = jnp.zeros((1, V7X_LANES), F32)
    x1_p, hf_p, ti_p, tg_p, seen = _finish(xp, attn_p.reshape(np_, a), y_p.reshape(np_, -1), tail, seen, tm_p)

    xs = x_sample.reshape(ns, d)
    qs, ks, vs, us = _proj(xs, norm_mix_g[l], w_in[l], a, tm_s)
    by_dim = lambda c: c.transpose(0, 2, 3, 1).reshape(bs, a, n_buf)
    attn_s = _attn_sample(qs.reshape(bs, t, a), ks.reshape(bs, t, a), vs.reshape(bs, t, a),
                          by_dim(cache_k[l]), by_dim(cache_v[l]))
    y_s, hs_re, hs_im = _ssm(us.reshape(bs, t, -1), state_ssm_re[l].reshape(bs, gn),
                             state_ssm_im[l].reshape(bs, gn), _ssm_tables_rows(tables, t), t)
    x1_s, hf_s, ti_s, tg_s, seen = _finish(xs, attn_s.reshape(ns, a), y_s.reshape(ns, -1), tail, seen, tm_s)

    sizes = seen[0, :N_EXPERTS].astype(I32)
    starts = jnp.cumsum(sizes) - sizes
    experts = jnp.arange(N_EXPERTS, dtype=I32)[None, None, :]

    def sorted_row(ti):
        hit = ti[:, :TOP_K, None] == experts
        return jnp.sum(jnp.where(hit, starts[None, None, :], 0), axis=-1) + ti[:, TOP_K:2 * TOP_K]

    dest_p, dest_s = sorted_row(ti_p), sorted_row(ti_s)
    n_rows = (np_ + ns) * TOP_K
    assert n_rows % MOE_TILE == 0
    xs_sorted = _dispatch(hf_p, dest_p, hf_s, dest_s, tm_p).reshape(-1, V7X_LANES)
    ys = _moe_experts(xs_sorted, _moe_visits(sizes, n_rows), w_moe1[l], b_moe1[l], w_moe2[l], b_moe2[l])
    y_prompt = _final(x1_p, tg_p, dest_p, ys, norm_final_g, tm_p // 2).reshape(bp, s, d)
    y_sample = _final(x1_s, tg_s, dest_s, ys, norm_final_g, tm_s).reshape(bs, t, d)

    keep = min(max(w for w, _ in DILATED_GROUPS), s)
    k5 = lambda z, b_, s_: z.reshape(1, b_, s_, n_heads, head_dim)
    by_pos = lambda zt: zt.reshape(bp, n_heads, head_dim, s).transpose(0, 3, 1, 2)[None]
    st = lambda z, b_: z.reshape(1, b_, g, SSM_STATE)
    return (y_prompt, y_sample,
            by_pos(ktp)[:, :, s - keep:], by_pos(vtp)[:, :, s - keep:], st(hp_re, bp), st(hp_im, bp),
            k5(ks, bs, t), k5(vs, bs, t), st(hs_re, bs), st(hs_im, bs))
```

```python
import functools

import numpy as np
import jax
import jax.numpy as jnp
from jax import lax
from jax.experimental import pallas as pl
from jax.experimental.pallas import tpu as pltpu

F32 = jnp.float32
I32 = jnp.int32

V7X_LANES = 128
V7X_SUBLANES = 8
V7X_VMEM_BYTES = 64 * 1024 * 1024
COMPILER_SCRATCH_BYTES = 8 * 1024 * 1024

HEAD_DIM = 64
DILATED_GROUPS = ((128, 1), (512, 4), (2048, 16))
KEYS_BACK = 128
SSM_GROUP = 16
SSM_STATE = 64
N_EXPERTS = 32
TOP_K = 4
SWIGLU_LIMIT = 7.0
SWIGLU_ALPHA = 1.702
RMS_EPS = 1e-5
NEG_INF = -1e30
PROMPT_CHUNK = 16
MOE_TILE = 512
MOE_SUB = 256
TOKEN_TILE = 512
COMBINE_TILE = 256


def _vmem_limit(nbytes):
    return int(min(nbytes + COMPILER_SCRATCH_BYTES, V7X_VMEM_BYTES - COMPILER_SCRATCH_BYTES))


def _rms(x, g):
    return x * lax.rsqrt(jnp.mean(x * x, axis=-1, keepdims=True) + RMS_EPS) * g


def _to_token_tiles(ref, val):
    rows, width = val.shape
    assert width == V7X_SUBLANES * V7X_LANES and ref.shape == (rows * V7X_SUBLANES, V7X_LANES)
    for j in range(V7X_SUBLANES):
        ref[pl.ds(j, rows, stride=V7X_SUBLANES), :] = val[:, j * V7X_LANES:(j + 1) * V7X_LANES]


def _from_token_tiles(ref):
    rows = ref.shape[0] // V7X_SUBLANES
    return jnp.concatenate([ref[pl.ds(j, rows, stride=V7X_SUBLANES), :] for j in range(V7X_SUBLANES)], axis=1)


def _proj_kernel(x_ref, g_ref, w_ref, q_ref, k_ref, v_ref, u_ref, *kv_t, attn_width, q_scale):
    h = _rms(x_ref[...], g_ref[...])
    z = jnp.dot(h, w_ref[...], preferred_element_type=F32)
    a = attn_width
    q_ref[...] = z[:, :a] * q_scale
    k_ref[...] = z[:, a:2 * a]
    v_ref[...] = z[:, 2 * a:3 * a]
    u_ref[...] = z[:, 3 * a:]
    if kv_t:
        kt_ref, vt_ref = kv_t
        kt_ref[...] = z[:, a:2 * a].T
        vt_ref[...] = z[:, 2 * a:3 * a].T


def _proj(x2d, norm_g, w_in, attn_width, tm, seq=None):
    n, d = x2d.shape
    mix = w_in.shape[1]
    ssm_width = mix - 3 * attn_width
    row = lambda i: (i, 0)
    const = lambda i: (0, 0)
    out_shape = (jax.ShapeDtypeStruct((n, attn_width), F32),) * 3 + (
        jax.ShapeDtypeStruct((n, ssm_width), F32),)
    out_specs = (pl.BlockSpec((tm, attn_width), row),) * 3 + (pl.BlockSpec((tm, ssm_width), row),)
    if seq is not None:
        assert seq % tm == 0 and tm % V7X_LANES == 0
        per_seq = seq // tm
        out_shape += (jax.ShapeDtypeStruct((n // seq, attn_width, seq), F32),) * 2
        out_specs += (pl.BlockSpec((None, attn_width, tm), lambda i: (i // per_seq, 0, i % per_seq)),) * 2
    vmem = 2 * 4 * (tm * d + d * mix + tm * mix + 2 * tm * attn_width) + 4 * tm * mix
    return pl.pallas_call(
        functools.partial(_proj_kernel, attn_width=attn_width, q_scale=HEAD_DIM ** -0.5),
        grid=(n // tm,),
        in_specs=[pl.BlockSpec((tm, d), row), pl.BlockSpec((1, d), const),
                  pl.BlockSpec((d, mix), const)],
        out_specs=out_specs,
        out_shape=out_shape,
        compiler_params=pltpu.CompilerParams(dimension_semantics=("arbitrary",),
                                             vmem_limit_bytes=_vmem_limit(vmem)),
        name="proj",
    )(x2d, norm_g.reshape(1, d), w_in)


def _attn_prompt_kernel(q_ref, k_ref, v_ref, o_ref, *scratch, seq, unroll):
    nb = KEYS_BACK
    n_groups = len(DILATED_GROUPS)
    m_sc, l_sc, acc_sc = scratch[:n_groups], scratch[n_groups:2 * n_groups], scratch[2 * n_groups:]
    lane = lax.broadcasted_iota(I32, (1, V7X_LANES), 1)
    head_a = lane < HEAD_DIM
    qi = lax.broadcasted_iota(I32, (nb, 2 * nb), 0)
    kj = lax.broadcasted_iota(I32, (nb, 2 * nb), 1)
    rel = nb + qi - kj
    band = (rel >= 0) & (rel <= nb)

    def rows(ref, start, d):
        if d == 1:
            return ref[pl.ds(pl.multiple_of(start, nb), nb), :]
        return ref[pl.ds(start, nb, stride=d), :]

    def put(ref, start, d, val):
        if d == 1:
            ref[pl.ds(pl.multiple_of(start, nb), nb), :] = val
        else:
            ref[pl.ds(start, nb, stride=d), :] = val

    for gi, (w, d) in enumerate(DILATED_GROUPS):
        assert w // d == nb
        nblk = seq // (d * nb)

        def block(t, gi=gi, d=d, nblk=nblk):
            r = t // nblk
            n = t % nblk
            cur = r + d * nb * n
            prev = r + d * nb * jnp.maximum(n - 1, 0)
            q = rows(q_ref, cur, d)
            kk = jnp.concatenate([rows(k_ref, prev, d), rows(k_ref, cur, d)], axis=0)
            vv = jnp.concatenate([rows(v_ref, prev, d), rows(v_ref, cur, d)], axis=0)
            qs = jnp.concatenate([jnp.where(head_a, q, 0.0), jnp.where(head_a, 0.0, q)], axis=0)
            s = lax.dot_general(qs, kk, (((1,), (1,)), ((), ())), preferred_element_type=F32)
            valid = band & (kj >= jnp.where(n > 0, 0, nb))
            s = jnp.where(jnp.concatenate([valid, valid], axis=0), s, NEG_INF)
            m = jnp.max(s, axis=-1, keepdims=True)
            p = jnp.exp(s - m)
            l = jnp.sum(p, axis=-1, keepdims=True)
            o = jnp.dot(p, vv, preferred_element_type=F32)
            o = jnp.where(head_a, o[:nb], o[nb:])
            put(m_sc[gi], cur, d, jnp.where(head_a, m[:nb], m[nb:]))
            put(l_sc[gi], cur, d, jnp.where(head_a, l[:nb], l[nb:]))
            put(acc_sc[gi], cur, d, o)

        def blocks(tt, carry, block=block):
            for u in range(unroll):
                block(tt * unroll + u)
            return carry

        lax.fori_loop(0, seq // nb // unroll, blocks, 0)

    def merge(n, carry):
        sl = pl.ds(pl.multiple_of(n * nb, nb), nb)
        ms = [m[sl, :] for m in m_sc]
        m_all = functools.reduce(jnp.maximum, ms)
        cs = [jnp.exp(m - m_all) for m in ms]
        den = functools.reduce(jnp.add, [c * l[sl, :] for c, l in zip(cs, l_sc)])
        num = functools.reduce(jnp.add, [c * acc[sl, :] for c, acc in zip(cs, acc_sc)])
        o_ref[sl, :] = num / den
        return carry

    lax.fori_loop(0, seq // nb, merge, 0, unroll=2)


def _attn_prompt(q, k, v, unroll=16):
    b, s, a = q.shape
    n_scratch = 3 * len(DILATED_GROUPS)
    spec = pl.BlockSpec((None, s, V7X_LANES), lambda i, j: (i, 0, j))
    vmem = 4 * s * V7X_LANES * (2 * 4 + n_scratch)
    return pl.pallas_call(
        functools.partial(_attn_prompt_kernel, seq=s, unroll=unroll),
        grid=(b, a // V7X_LANES),
        in_specs=[spec, spec, spec],
        out_specs=spec,
        out_shape=jax.ShapeDtypeStruct((b, s, a), F32),
        scratch_shapes=[pltpu.VMEM((s, V7X_LANES), F32)] * n_scratch,
        compiler_params=pltpu.CompilerParams(dimension_semantics=("arbitrary", "arbitrary"),
                                             vmem_limit_bytes=_vmem_limit(vmem)),
        name="attn_prompt",
    )(q, k, v)


def _sample_key_counts(n_buf, n_new):
    cnt = np.zeros((n_new, n_buf + n_new), np.float32)
    for w, d in DILATED_GROUPS:
        for i in range(n_new):
            for j in range(w // d + 1):
                idx = n_buf + i - d * j
                if idx >= 0:
                    cnt[i, idx] += 1.0
    return cnt[:, :n_buf], cnt[:, n_buf:]


def _attn_sample_kernel(q_ref, kn_ref, vn_ref, kt_ref, vt_ref, cnt_ref, o_ref, *, n_heads, cnt_new):
    n_new, a = q_ref.shape
    rows = n_new * n_heads
    lane_head = lax.broadcasted_iota(I32, (rows, a), 1) // HEAD_DIM
    row_head = lax.broadcasted_iota(I32, (rows, a), 0) % n_heads
    row_query = lax.broadcasted_iota(I32, (rows, 1), 0) // n_heads
    own = lane_head == row_head
    q = q_ref[...]
    qb = jnp.where(own, jnp.broadcast_to(q[:, None, :], (n_new, n_heads, a)).reshape(rows, a), 0.0)
    s_buf = jnp.dot(qb, kt_ref[...], preferred_element_type=F32)
    cnt = cnt_ref[...]
    cnt = jnp.broadcast_to(cnt[:, None, :], (n_new, n_heads, cnt.shape[-1])).reshape(rows, -1)
    valid = cnt > 0.0
    m = jnp.max(jnp.where(valid, s_buf, NEG_INF), axis=-1, keepdims=True)
    s_new, c_new = [], []
    for j in range(n_new):
        c = functools.reduce(lambda acc, i: jnp.where(row_query == i, float(cnt_new[i, j]), acc),
                             range(n_new), jnp.zeros((rows, 1), F32))
        s = jnp.sum(qb * kn_ref[pl.ds(j, 1), :], axis=-1, keepdims=True)
        m = jnp.maximum(m, jnp.where(c > 0.0, s, NEG_INF))
        s_new.append(s)
        c_new.append(c)
    p = jnp.where(valid, cnt * jnp.exp(s_buf - m), 0.0)
    den = jnp.sum(p, axis=-1, keepdims=True)
    o = lax.dot_general(p, vt_ref[...], (((1,), (1,)), ((), ())), preferred_element_type=F32)
    for j in range(n_new):
        pj = jnp.where(c_new[j] > 0.0, c_new[j] * jnp.exp(s_new[j] - m), 0.0)
        den = den + pj
        o = o + pj * vn_ref[pl.ds(j, 1), :]
    o = jnp.where(own, o / den, 0.0)
    o_ref[...] = jnp.sum(o.reshape(n_new, n_heads, a), axis=1)


def _attn_sample(q, k_new, v_new, cache_kt, cache_vt):
    b, t, a = q.shape
    n_buf = cache_kt.shape[-1]
    n_heads = a // HEAD_DIM
    cnt_buf, cnt_new = _sample_key_counts(n_buf, t)
    new = pl.BlockSpec((None, t, a), lambda i: (i, 0, 0))
    buf = pl.BlockSpec((None, a, n_buf), lambda i: (i, 0, 0))
    vmem = 4 * (2 * 2 * n_buf * a + 8 * t * n_heads * n_buf)
    return pl.pallas_call(
        functools.partial(_attn_sample_kernel, n_heads=n_heads, cnt_new=cnt_new),
        grid=(b,),
        in_specs=[new, new, new, buf, buf, pl.BlockSpec((t, n_buf), lambda i: (0, 0))],
        out_specs=new,
        out_shape=jax.ShapeDtypeStruct((b, t, a), F32),
        compiler_params=pltpu.CompilerParams(dimension_semantics=("arbitrary",),
                                             vmem_limit_bytes=_vmem_limit(vmem)),
        name="attn_sample",
    )(q, k_new, v_new, cache_kt, cache_vt, jnp.asarray(cnt_buf))


def _block_toeplitz_kernel(k_ref, toep_ref, tt_ref, *, chunk):
    c = k_ref.shape[1]
    for i in range(k_ref.shape[0]):
        k = k_ref[i]
        for s in range(chunk):
            shifted = k if s == 0 else jnp.concatenate(
                [jnp.zeros((c, s * c), F32), k[:, :(chunk - s) * c]], axis=1)
            toep_ref[i, pl.ds(s * c, c), :] = shifted
        tt_ref[i] = toep_ref[i].T


def _block_toeplitz(kern, chunk):
    g, c, lc = kern.shape
    per_step = 4
    spec = pl.BlockSpec((per_step, lc, lc), lambda i: (i, 0, 0))
    return pl.pallas_call(
        functools.partial(_block_toeplitz_kernel, chunk=chunk),
        grid=(g // per_step,),
        in_specs=[pl.BlockSpec((per_step, c, lc), lambda i: (i, 0, 0))],
        out_specs=(spec, spec),
        out_shape=(jax.ShapeDtypeStruct((g, lc, lc), F32),) * 2,
        compiler_params=pltpu.CompilerParams(dimension_semantics=("arbitrary",)),
        name="ssm_toeplitz",
    )(kern)


def _pad_states(m, g):
    half = (np.arange(g) % 2)[:, None, None]
    z = jnp.zeros_like(m)
    return jnp.where(half == 0, jnp.concatenate([m, z], -1), jnp.concatenate([z, m], -1))


def _ssm_tables(a_re, a_im, log_dt, b_re, b_im, c_re, c_im, d_skip, chunk):
    g, n = a_re.shape
    c = b_re.shape[-1]
    lc = chunk * c
    dt = jnp.exp(log_dt)[:, None]
    x, y = a_re * dt, a_im * dt
    ex = jnp.exp(x)
    ar, ai = ex * jnp.cos(y), ex * jnp.sin(y)
    nr = jnp.expm1(x) * jnp.cos(y) - 2.0 * jnp.sin(0.5 * y) ** 2
    ni = ai
    den = a_re * a_re + a_im * a_im
    fr = (nr * a_re + ni * a_im) / den
    fi = (ni * a_re - nr * a_im) / den
    b_re_t, b_im_t = b_re.transpose(0, 2, 1), b_im.transpose(0, 2, 1)
    bbr = fr[:, None, :] * b_re_t - fi[:, None, :] * b_im_t
    bbi = fr[:, None, :] * b_im_t + fi[:, None, :] * b_re_t
    pr, pi = [jnp.ones_like(ar)], [jnp.zeros_like(ar)]
    for _ in range(chunk):
        pr.append(pr[-1] * ar - pi[-1] * ai)
        pi.append(pr[-2] * ai + pi[-1] * ar)
    pr, pi = jnp.stack(pr, 1), jnp.stack(pi, 1)
    cpr = c_re[:, None] * pr[:, :, None, :] - c_im[:, None] * pi[:, :, None, :]
    cpi = c_re[:, None] * pi[:, :, None, :] + c_im[:, None] * pr[:, :, None, :]
    kern = jnp.sum(cpr[:, None, :chunk] * bbr[:, :, None, None] - cpi[:, None, :chunk] * bbi[:, :, None, None],
                   axis=-1).reshape(g, c, lc)
    toep, tt = _block_toeplitz(kern, chunk)
    qr, qi = pr[:, chunk - 1::-1][:, :, None, :], pi[:, chunk - 1::-1][:, :, None, :]
    wr = (qr * bbr[:, None] - qi * bbi[:, None]).reshape(g, lc, n)
    wi = (qr * bbi[:, None] + qi * bbr[:, None]).reshape(g, lc, n)
    vr = cpr[:, 1:].reshape(g, lc, n)
    vi = -cpi[:, 1:].reshape(g, lc, n)
    return dict(chunk=chunk, toep=toep, tt=tt, wr=wr, wi=wi, vr=vr, vi=vi, d=d_skip, pr=pr, pi=pi)


def _ssm_tables_lanes(tab):
    g, chunk = tab["d"].shape[0], tab["chunk"]
    tr = lambda m: m.transpose(0, 2, 1)
    return dict(tt=tab["tt"], wt_re=tr(_pad_states(tab["wr"], g)), wt_im=tr(_pad_states(tab["wi"], g)),
                v_re=_pad_states(tab["vr"], g), v_im=_pad_states(tab["vi"], g),
                dcol=jnp.tile(tab["d"], (1, chunk))[:, :, None],
                al_re=tab["pr"][:, chunk].reshape(1, -1), al_im=tab["pi"][:, chunk].reshape(1, -1))


def _ssm_tables_rows(tab, chunk):
    g, c = tab["d"].shape
    big, lc = tab["chunk"], chunk * c
    tr = lambda m: m.transpose(0, 2, 1)
    tail = slice((big - chunk) * c, big * c)
    return dict(toep=tab["toep"][:, :lc, :lc],
                wr=_pad_states(tab["wr"][:, tail], g), wi=_pad_states(tab["wi"][:, tail], g),
                vr=tr(_pad_states(tab["vr"][:, :lc], g)), vi=tr(_pad_states(tab["vi"][:, :lc], g)),
                d=jnp.tile(tab["d"], (1, chunk))[:, None, :],
                al_re=tab["pr"][:, chunk].reshape(1, -1), al_im=tab["pi"][:, chunk].reshape(1, -1))


def _ssm_state_kernel(u_ref, wr_ref, wi_ref, er_ref, ei_ref):
    er = jnp.dot(u_ref[0], wr_ref[0], preferred_element_type=F32)
    ei = jnp.dot(u_ref[0], wi_ref[0], preferred_element_type=F32)
    er_ref[...] = er + jnp.dot(u_ref[1], wr_ref[1], preferred_element_type=F32)
    ei_ref[...] = ei + jnp.dot(u_ref[1], wi_ref[1], preferred_element_type=F32)


def _ssm_scan_kernel(er_ref, ei_ref, ar_ref, ai_ref, hr_ref, hi_ref, xr_ref, xi_ref, lr_ref, li_ref,
                     *, n_chunks, nb):
    shape = hr_ref.shape
    ar = jnp.broadcast_to(ar_ref[...], shape)
    ai = jnp.broadcast_to(ai_ref[...], shape)

    def step(j, carry):
        xr, xi = carry
        sl = pl.ds(pl.multiple_of(j * nb, nb), nb)
        xr_ref[sl, :] = xr
        xi_ref[sl, :] = xi
        return (ar * xr - ai * xi + er_ref[sl, :], ar * xi + ai * xr + ei_ref[sl, :])

    xr, xi = lax.fori_loop(0, n_chunks, step, (hr_ref[...], hi_ref[...]))
    lr_ref[...] = xr
    li_ref[...] = xi


def _ssm_out_kernel(u_ref, t_ref, d_ref, xr_ref, xi_ref, vr_ref, vi_ref, y_ref):
    xr = xr_ref[...]
    xi = xi_ref[...]
    for a in range(2):
        u = u_ref[a]
        y = jnp.dot(u, t_ref[a], preferred_element_type=F32) + d_ref[a] * u
        y = y + jnp.dot(xr, vr_ref[a], preferred_element_type=F32)
        y_ref[a] = y + jnp.dot(xi, vi_ref[a], preferred_element_type=F32)


def _ssm_scan(er, ei, tab, h0_re, h0_im, n_chunks):
    r, gn = er.shape
    b = r // n_chunks
    lanes = 4 * V7X_LANES
    col = lambda i: (0, i)
    return pl.pallas_call(
        functools.partial(_ssm_scan_kernel, n_chunks=n_chunks, nb=b),
        grid=(gn // lanes,),
        in_specs=[pl.BlockSpec((r, lanes), col), pl.BlockSpec((r, lanes), col),
                  pl.BlockSpec((1, lanes), col), pl.BlockSpec((1, lanes), col),
                  pl.BlockSpec((b, lanes), col), pl.BlockSpec((b, lanes), col)],
        out_specs=(pl.BlockSpec((r, lanes), col),) * 2 + (pl.BlockSpec((b, lanes), col),) * 2,
        out_shape=(jax.ShapeDtypeStruct((r, gn), F32),) * 2 + (jax.ShapeDtypeStruct((b, gn), F32),) * 2,
        compiler_params=pltpu.CompilerParams(dimension_semantics=("arbitrary",),
                                             vmem_limit_bytes=_vmem_limit(2 * 4 * 4 * r * lanes)),
        name="ssm_scan",
    )(er, ei, tab["al_re"], tab["al_im"], h0_re, h0_im)


def _ssm_state_t_kernel(a_ref, wr_ref, wi_ref, er_ref, ei_ref):
    def seq(b, carry):
        a0, a1 = a_ref[0, b], a_ref[1, b]
        er_ref[b] = (jnp.dot(wr_ref[0], a0, preferred_element_type=F32)
                     + jnp.dot(wr_ref[1], a1, preferred_element_type=F32))
        ei_ref[b] = (jnp.dot(wi_ref[0], a0, preferred_element_type=F32)
                     + jnp.dot(wi_ref[1], a1, preferred_element_type=F32))
        return carry

    lax.fori_loop(0, a_ref.shape[1], seq, 0, unroll=True)


def _ssm_out_t_kernel(a_ref, t_ref, d_ref, xr_ref, xi_ref, vr_ref, vi_ref, y_ref):
    def seq(b, carry):
        xr, xi = xr_ref[b], xi_ref[b]
        for a in range(2):
            at = a_ref[a, b]
            y = jnp.dot(t_ref[a], at, preferred_element_type=F32) + d_ref[a] * at
            y = y + jnp.dot(vr_ref[a], xr, preferred_element_type=F32)
            y_ref[a, b] = y + jnp.dot(vi_ref[a], xi, preferred_element_type=F32)
        return carry

    lax.fori_loop(0, a_ref.shape[1], seq, 0, unroll=True)


def _ssm_chunks_on_lanes(u, h0_re, h0_im, tab, chunk):
    b, s, width = u.shape
    g = width // SSM_GROUP
    n_chunks = s // chunk
    lc = chunk * SSM_GROUP
    assert n_chunks == V7X_LANES
    at = (u.reshape(b, n_chunks, chunk * width).transpose(0, 2, 1)
          .reshape(b, chunk, g, SSM_GROUP, n_chunks).transpose(2, 0, 1, 3, 4).reshape(g, b, lc, n_chunks))
    pair4 = lambda i: (i, 0, 0, 0)
    pair3 = lambda i: (i, 0, 0)
    cp = lambda vmem: pltpu.CompilerParams(dimension_semantics=("arbitrary",),
                                           vmem_limit_bytes=_vmem_limit(vmem))
    a_spec = pl.BlockSpec((2, b, lc, n_chunks), pair4)
    st_spec = pl.BlockSpec((None, b, V7X_LANES, n_chunks), pair4)
    w_spec = pl.BlockSpec((2, V7X_LANES, lc), pair3)
    a_bytes = 2 * b * lc * n_chunks
    st_bytes = b * V7X_LANES * n_chunks
    et_re, et_im = pl.pallas_call(
        _ssm_state_t_kernel,
        grid=(g // 2,),
        in_specs=[a_spec, w_spec, w_spec],
        out_specs=(st_spec,) * 2,
        out_shape=(jax.ShapeDtypeStruct((g // 2, b, V7X_LANES, n_chunks), F32),) * 2,
        compiler_params=cp(2 * 4 * (a_bytes + 2 * st_bytes + 4 * V7X_LANES * lc)),
        name="ssm_state_t",
    )(at, tab["wt_re"], tab["wt_im"])
    rows = lambda m: m.transpose(3, 1, 0, 2).reshape(n_chunks * b, g * SSM_STATE)
    xr, xi, lr, li = _ssm_scan(rows(et_re), rows(et_im), tab, h0_re, h0_im, n_chunks)
    cols = lambda m: m.reshape(n_chunks, b, g // 2, V7X_LANES).transpose(2, 1, 3, 0)
    yt = pl.pallas_call(
        _ssm_out_t_kernel,
        grid=(g // 2,),
        in_specs=[a_spec, pl.BlockSpec((2, lc, lc), pair3), pl.BlockSpec((2, lc, 1), pair3),
                  st_spec, st_spec, pl.BlockSpec((2, lc, V7X_LANES), pair3),
                  pl.BlockSpec((2, lc, V7X_LANES), pair3)],
        out_specs=a_spec,
        out_shape=jax.ShapeDtypeStruct((g, b, lc, n_chunks), F32),
        compiler_params=cp(2 * 4 * (2 * a_bytes + 2 * st_bytes + 2 * lc * lc + 6 * V7X_LANES * lc)),
        name="ssm_out_t",
    )(at, tab["tt"], tab["dcol"], cols(xr), cols(xi), tab["v_re"], tab["v_im"])
    y = (yt.reshape(g, b, chunk, SSM_GROUP, n_chunks).transpose(1, 2, 0, 3, 4)
         .reshape(b, chunk * width, n_chunks).transpose(0, 2, 1).reshape(b, s, width))
    return y, lr, li


def _ssm(u, h0_re, h0_im, tab, chunk):
    b, s, width = u.shape
    g = width // SSM_GROUP
    n_chunks = s // chunk
    lc = chunk * SSM_GROUP
    r = n_chunks * b
    gn = g * SSM_STATE
    ut = u.reshape(b, n_chunks, chunk, g, SSM_GROUP).transpose(3, 1, 0, 2, 4).reshape(g, r, lc)
    pair = lambda i: (i, 0, 0)
    cp = lambda vmem: pltpu.CompilerParams(dimension_semantics=("arbitrary",),
                                           vmem_limit_bytes=_vmem_limit(vmem))
    er, ei = pl.pallas_call(
        _ssm_state_kernel,
        grid=(g // 2,),
        in_specs=[pl.BlockSpec((2, r, lc), pair), pl.BlockSpec((2, lc, V7X_LANES), pair),
                  pl.BlockSpec((2, lc, V7X_LANES), pair)],
        out_specs=(pl.BlockSpec((r, V7X_LANES), lambda i: (0, i)),) * 2,
        out_shape=(jax.ShapeDtypeStruct((r, gn), F32),) * 2,
        compiler_params=cp(2 * 4 * (2 * r * lc + 4 * lc * V7X_LANES + 2 * r * V7X_LANES)),
        name="ssm_state",
    )(ut, tab["wr"], tab["wi"])
    xr, xi, lr, li = _ssm_scan(er, ei, tab, h0_re, h0_im, n_chunks)
    y = pl.pallas_call(
        _ssm_out_kernel,
        grid=(g // 2,),
        in_specs=[pl.BlockSpec((2, r, lc), pair), pl.BlockSpec((2, lc, lc), pair),
                  pl.BlockSpec((2, 1, lc), pair),
                  pl.BlockSpec((r, V7X_LANES), lambda i: (0, i)),
                  pl.BlockSpec((r, V7X_LANES), lambda i: (0, i)),
                  pl.BlockSpec((2, V7X_LANES, lc), pair), pl.BlockSpec((2, V7X_LANES, lc), pair)],
        out_specs=pl.BlockSpec((2, r, lc), pair),
        out_shape=jax.ShapeDtypeStruct((g, r, lc), F32),
        compiler_params=cp(2 * 4 * (4 * r * lc + 2 * lc * lc + 2 * r * V7X_LANES + 4 * V7X_LANES * lc)),
        name="ssm_out",
    )(ut, tab["toep"], tab["d"], xr, xi, tab["vr"], tab["vi"])
    y = y.reshape(g, n_chunks, b, chunk, SSM_GROUP).transpose(2, 1, 3, 0, 4).reshape(b, s, width)
    return y, lr, li


def _finish_kernel(x_ref, attn_ref, ssm_ref, ga_ref, wg_ref, bg_ref, gs_ref, wo_ref, gf_ref,
                   wr_ref, br_ref, base_ref, x1_ref, hf_ref, ti_ref, tg_ref, cnt_ref, seen):
    a = attn_ref.shape[-1]
    tm = x_ref.shape[0]

    @pl.when(pl.program_id(0) == 0)
    def _():
        seen[...] = base_ref[...]

    g = jax.nn.gelu(ssm_ref[...])
    gate = jax.nn.sigmoid(jnp.dot(g, wg_ref[...], preferred_element_type=F32) + bg_ref[...])
    mixed_a = _rms(attn_ref[...], ga_ref[...])
    mixed_s = _rms(g * gate, gs_ref[...])
    x1 = (x_ref[...] + jnp.dot(mixed_a, wo_ref[pl.ds(0, a), :], preferred_element_type=F32)
          + jnp.dot(mixed_s, wo_ref[pl.ds(a, wo_ref.shape[0] - a), :], preferred_element_type=F32))
    x1_ref[...] = x1
    hf = _rms(x1, gf_ref[...])
    _to_token_tiles(hf_ref, hf)
    hf_hi = hf.astype(jnp.bfloat16).astype(F32)
    hf_lo = hf - hf_hi
    small = (jnp.dot(hf_lo, wr_ref[1], preferred_element_type=F32)
             + jnp.dot(hf_hi, wr_ref[1], preferred_element_type=F32)
             + jnp.dot(hf_lo, wr_ref[0], preferred_element_type=F32))
    logits = jnp.dot(hf_hi, wr_ref[0], preferred_element_type=F32) + small + br_ref[...]
    lane = lax.broadcasted_iota(I32, logits.shape, 1)
    work = logits
    vals, idxs = [], []
    for _ in range(TOP_K):
        m = jnp.max(work, axis=-1, keepdims=True)
        idx = jnp.min(jnp.where(work == m, lane, V7X_LANES), axis=-1, keepdims=True)
        vals.append(m)
        idxs.append(idx)
        work = jnp.where(lane == idx, -jnp.inf, work)
    exps = [jnp.exp(v - vals[0]) for v in vals]
    den = exps[0]
    for e in exps[1:]:
        den = den + e
    picked = jnp.zeros(logits.shape, F32)
    for k in range(TOP_K):
        picked = picked + jnp.where(lane == idxs[k], 1.0, 0.0)
    tri = jnp.where(lax.broadcasted_iota(I32, (tm, tm), 0) > lax.broadcasted_iota(I32, (tm, tm), 1), 1.0, 0.0)
    before = jnp.dot(tri, picked, preferred_element_type=F32) + seen[...]
    ti = jnp.zeros(logits.shape, I32)
    tg = jnp.zeros(logits.shape, F32)
    for k in range(TOP_K):
        rank = jnp.sum(jnp.where(lane == idxs[k], before, 0.0), axis=-1, keepdims=True)
        ti = jnp.where(lane == k, idxs[k], ti)
        ti = jnp.where(lane == TOP_K + k, rank.astype(I32), ti)
        tg = jnp.where(lane == k, exps[k] / den, tg)
    ti_ref[...] = ti
    tg_ref[...] = tg
    seen[...] = seen[...] + jnp.sum(picked, axis=0, keepdims=True)
    cnt_ref[...] = seen[...]


def _finish(x2d, attn, ssm_y, p, seen, tm):
    n, d = x2d.shape
    a = attn.shape[1]
    sw = ssm_y.shape[1]
    ne = p["w_router"].shape[1]
    w_router = jnp.pad(p["w_router"], ((0, 0), (0, V7X_LANES - ne)))
    w_hi = w_router.astype(jnp.bfloat16).astype(F32)
    w_router = jnp.stack([w_hi, w_router - w_hi])
    b_router = jnp.pad(p["b_router"], (0, V7X_LANES - ne), constant_values=-jnp.inf).reshape(1, V7X_LANES)
    row = lambda i: (i, 0)
    full = lambda arr: pl.BlockSpec(arr.shape, lambda i: (0,) * arr.ndim)
    ins = [x2d, attn, ssm_y, p["attn_out_norm_g"].reshape(1, a), p["w_glu"], p["b_glu"].reshape(1, sw),
           p["ssm_out_norm_g"].reshape(1, sw), p["w_out"], p["norm_ffn_g"].reshape(1, d),
           w_router, b_router, seen]
    in_specs = [pl.BlockSpec((tm, d), row), pl.BlockSpec((tm, a), row), pl.BlockSpec((tm, sw), row)]
    in_specs += [full(arr) for arr in ins[3:]]
    vmem = 2 * 4 * (tm * (3 * d + a + sw + 2 * V7X_LANES) + sw * sw + d * d + 2 * d * V7X_LANES) + 16 * tm * d
    return pl.pallas_call(
        _finish_kernel,
        grid=(n // tm,),
        in_specs=in_specs,
        out_specs=(pl.BlockSpec((tm, d), row), pl.BlockSpec((tm * V7X_SUBLANES, V7X_LANES), row),
                   pl.BlockSpec((tm, V7X_LANES), row), pl.BlockSpec((tm, V7X_LANES), row),
                   pl.BlockSpec((1, V7X_LANES), lambda i: (0, 0))),
        out_shape=(jax.ShapeDtypeStruct((n, d), F32), jax.ShapeDtypeStruct((n * V7X_SUBLANES, V7X_LANES), F32),
                   jax.ShapeDtypeStruct((n, V7X_LANES), I32), jax.ShapeDtypeStruct((n, V7X_LANES), F32),
                   jax.ShapeDtypeStruct((1, V7X_LANES), F32)),
        scratch_shapes=[pltpu.VMEM((1, V7X_LANES), F32)],
        compiler_params=pltpu.CompilerParams(dimension_semantics=("arbitrary",),
                                             vmem_limit_bytes=_vmem_limit(vmem)),
        name="finish",
    )(*ins)


def _tile_copy(src_ref, src_row, dst_ref, dst_row, sem):
    return pltpu.make_async_copy(src_ref.at[src_row], dst_ref.at[dst_row], sem)


def _dispatch_kernel(dest_ref, hp_ref, hs_ref, xs_ref, sem, *, n_tiles_p):
    def scatter(hf_ref):
        tm = hf_ref.shape[0]

        def tokens(i, carry):
            for u in range(V7X_SUBLANES):
                r = i * V7X_SUBLANES + u
                for k in range(TOP_K):
                    _tile_copy(hf_ref, r, xs_ref, dest_ref[0, 0, r * TOP_K + k], sem).start(priority=k % 2)
            return carry

        lax.fori_loop(0, tm // V7X_SUBLANES, tokens, 0)
        rows = pl.ds(0, tm * TOP_K)
        pltpu.make_async_copy(xs_ref.at[rows], xs_ref.at[rows], sem).wait()

    @pl.when(pl.program_id(0) < n_tiles_p)
    def _():
        scatter(hp_ref)

    @pl.when(pl.program_id(0) == n_tiles_p)
    def _():
        scatter(hs_ref)


def _dispatch(hf_p, dest_p, hf_s, dest_s, tm):
    tile = (V7X_SUBLANES, V7X_LANES)
    hf_p, hf_s = hf_p.reshape(-1, *tile), hf_s.reshape(-1, *tile)
    n_p, n_s = hf_p.shape[0], hf_s.shape[0]
    assert n_p % tm == 0 and n_s <= tm
    n_tiles_p = n_p // tm
    dest3 = jnp.concatenate([dest_p.reshape(n_tiles_p, tm * TOP_K),
                             jnp.pad(dest_s.reshape(1, n_s * TOP_K), ((0, 0), (0, (tm - n_s) * TOP_K)))])
    dest3 = dest3.reshape(n_tiles_p + 1, 1, tm * TOP_K)
    return pl.pallas_call(
        functools.partial(_dispatch_kernel, n_tiles_p=n_tiles_p),
        grid=(n_tiles_p + 1,),
        in_specs=[pl.BlockSpec((1, 1, tm * TOP_K), lambda i: (i, 0, 0), memory_space=pltpu.SMEM),
                  pl.BlockSpec((tm, *tile), lambda i: (jnp.minimum(i, n_tiles_p - 1), 0, 0)),
                  pl.BlockSpec((n_s, *tile), lambda i: (0, 0, 0))],
        out_specs=pl.BlockSpec(memory_space=pl.ANY),
        out_shape=jax.ShapeDtypeStruct(((n_p + n_s) * TOP_K, *tile), F32),
        scratch_shapes=[pltpu.SemaphoreType.DMA(())],
        compiler_params=pltpu.CompilerParams(dimension_semantics=("arbitrary",)),
        name="dispatch",
    )(dest3, hf_p, hf_s)


def _moe_kernel(vt_ref, ve_ref, nx_ref, lo_ref, hi_ref, x_ref, w1_ref, b1_ref, w2_ref, b2_ref, y_ref,
                w1s, w2s, w1b, w2b, sem, *, d_ff):
    v = pl.program_id(0)
    lo, hi = lo_ref[v], hi_ref[v]
    prev = jnp.maximum(v - 1, 0)
    first = jnp.logical_or(v == 0, vt_ref[v] != vt_ref[prev])

    def fetch(e):
        return (pltpu.make_async_copy(w1_ref.at[e], w1s, sem.at[0]),
                pltpu.make_async_copy(w2_ref.at[e], w2s, sem.at[1]))

    @pl.when(v == 0)
    def _():
        for cp in fetch(ve_ref[0]):
            cp.start()

    @pl.when(jnp.logical_or(v == 0, ve_ref[v] != ve_ref[prev]))
    def _():
        for cp in fetch(ve_ref[v]):
            cp.wait()
        w1b[...] = w1s[...].astype(jnp.bfloat16)
        w2b[...] = w2s[...].astype(jnp.bfloat16)

        @pl.when(nx_ref[v] >= 0)
        def _():
            for cp in fetch(nx_ref[v]):
                cp.start()

    def swiglu(x_tiles):
        x = _from_token_tiles(x_tiles).astype(jnp.bfloat16)
        a = jnp.dot(x, w1b[...], preferred_element_type=F32) + b1_ref[0]
        x_glu = jnp.minimum(a[:, :d_ff], SWIGLU_LIMIT)
        x_lin = jnp.clip(a[:, d_ff:], -SWIGLU_LIMIT, SWIGLU_LIMIT)
        mid = x_glu * jax.nn.sigmoid(SWIGLU_ALPHA * x_glu) * (x_lin + 1.0)
        return jnp.dot(mid.astype(jnp.bfloat16), w2b[...], preferred_element_type=F32) + b2_ref[0]

    whole = jnp.logical_and(lo == 0, hi == MOE_TILE)

    @pl.when(whole)
    def _():
        _to_token_tiles(y_ref, swiglu(x_ref))

    for r0 in range(0, MOE_TILE, MOE_SUB):
        sub = pl.ds(r0 * V7X_SUBLANES, MOE_SUB * V7X_SUBLANES)
        x_sub, y_sub = x_ref.at[sub, :], y_ref.at[sub, :]
        touched = jnp.logical_and(jnp.logical_not(whole), jnp.logical_and(lo < r0 + MOE_SUB, hi > r0))

        @pl.when(touched)
        def _(r0=r0, x_sub=x_sub, y_sub=y_sub):
            y = swiglu(x_sub)
            rows = r0 + lax.broadcasted_iota(I32, (MOE_SUB, 1), 0)
            mine = (rows >= lo) & (rows < hi)

            @pl.when(first)
            def _():
                _to_token_tiles(y_sub, jnp.where(mine, y, 0.0))

            @pl.when(jnp.logical_not(first))
            def _():
                _to_token_tiles(y_sub, jnp.where(mine, y, _from_token_tiles(y_sub)))

        @pl.when(jnp.logical_and(first, jnp.logical_not(jnp.logical_or(touched, whole))))
        def _(y_sub=y_sub):
            y_sub[...] = jnp.zeros(y_sub.shape, F32)


def _moe_experts(xs, visits, w1, b1, w2, b2):
    ne, d, ff2 = w1.shape
    d_ff = ff2 // 2
    n_visits = visits[0].shape[0]
    row = lambda v, vt, ve, nx, lo, hi: (vt[v], 0)
    exp = lambda v, vt, ve, nx, lo, hi: (ve[v], 0, 0)
    vmem = (4 * 2 * 2 * MOE_TILE * d + (4 + 2) * (d * ff2 + d_ff * d) + 2 * 4 * (ff2 + d)
            + 4 * 4 * MOE_TILE * ff2)
    tiles = pl.BlockSpec((MOE_TILE * V7X_SUBLANES, V7X_LANES), row)
    grid_spec = pltpu.PrefetchScalarGridSpec(
        num_scalar_prefetch=5,
        grid=(n_visits,),
        in_specs=[tiles,
                  pl.BlockSpec(memory_space=pl.ANY), pl.BlockSpec((1, 1, ff2), exp),
                  pl.BlockSpec(memory_space=pl.ANY), pl.BlockSpec((1, 1, d), exp)],
        out_specs=tiles,
        scratch_shapes=[pltpu.VMEM((d, ff2), F32), pltpu.VMEM((d_ff, d), F32),
                        pltpu.VMEM((d, ff2), jnp.bfloat16), pltpu.VMEM((d_ff, d), jnp.bfloat16),
                        pltpu.SemaphoreType.DMA((2,))],
    )
    return pl.pallas_call(
        functools.partial(_moe_kernel, d_ff=d_ff),
        grid_spec=grid_spec,
        out_shape=jax.ShapeDtypeStruct(xs.shape, F32),
        compiler_params=pltpu.CompilerParams(dimension_semantics=("arbitrary",),
                                             vmem_limit_bytes=_vmem_limit(vmem)),
        name="moe",
    )(*visits, xs, w1, b1.reshape(ne, 1, ff2), w2, b2.reshape(ne, 1, d))


def _moe_visits(sizes, n_rows):
    n_tiles = n_rows // MOE_TILE
    n_visits = n_tiles + N_EXPERTS - 1
    ends = jnp.cumsum(sizes)
    starts = ends - sizes
    first_tile = starts // MOE_TILE
    n_vis = jnp.where(sizes > 0, (ends - 1) // MOE_TILE - first_tile + 1, 0)
    vis_end = jnp.cumsum(n_vis)
    total = vis_end[-1]
    v = jnp.clip(jnp.arange(n_visits, dtype=I32), 0, jnp.maximum(total - 1, 0))
    e = jnp.sum((vis_end[None, :] <= v[:, None]).astype(I32), axis=1)
    pick = lambda tab: jnp.sum(jnp.where(e[:, None] == jnp.arange(N_EXPERTS)[None, :], tab[None, :], 0), axis=1)
    tile = pick(first_tile) + v - pick(vis_end - n_vis)
    lo = jnp.maximum(pick(starts), tile * MOE_TILE) - tile * MOE_TILE
    hi = jnp.minimum(pick(ends), (tile + 1) * MOE_TILE) - tile * MOE_TILE
    live = jnp.arange(n_visits) < total
    ids = jnp.arange(N_EXPERTS, dtype=I32)
    later = jnp.where((ids[None, :] > ids[:, None]) & (sizes[None, :] > 0), ids[None, :], N_EXPERTS)
    nxt = jnp.min(later, axis=1)
    nxt = pick(jnp.where(nxt < N_EXPERTS, nxt, -1))
    return (tile.astype(I32), e.astype(I32), nxt.astype(I32), jnp.where(live, lo, 0).astype(I32),
            jnp.where(live, hi, 0).astype(I32))


def _final_kernel(dest_ref, nxt_ref, x_ref, tg_ref, g_ref, ys_ref, y_ref, buf, sem, *, tm):
    i = pl.program_id(0)
    n = pl.num_programs(0)

    def gather(d_ref, slot):
        def tokens(i8, carry):
            for u in range(V7X_SUBLANES):
                r = i8 * V7X_SUBLANES + u
                tile = pl.ds(pl.multiple_of(r * V7X_SUBLANES, V7X_SUBLANES), V7X_SUBLANES)
                for k in range(TOP_K):
                    pltpu.make_async_copy(ys_ref.at[d_ref[0, 0, r * TOP_K + k]], buf.at[slot, k, tile, :],
                                          sem.at[slot]).start(priority=k % 2)
            return carry
        lax.fori_loop(0, tm // V7X_SUBLANES, tokens, 0)

    @pl.when(i == 0)
    def _():
        gather(dest_ref, 0)

    for slot in range(2):
        @pl.when(jnp.logical_and(i + 1 < n, (i + 1) % 2 == slot))
        def _(slot=slot):
            gather(nxt_ref, slot)

    for slot in range(2):
        @pl.when(i % 2 == slot)
        def _(slot=slot):
            pltpu.make_async_copy(buf.at[slot], buf.at[slot], sem.at[slot]).wait()
            tg = tg_ref[...]
            parts = []
            for j in range(V7X_SUBLANES):
                f = None
                for k in range(TOP_K):
                    rows = buf[slot, k, pl.ds(j, tm, stride=V7X_SUBLANES), :]
                    f = tg[:, k:k + 1] * rows if f is None else f + tg[:, k:k + 1] * rows
                parts.append(f)
            y_ref[...] = _rms(x_ref[...] + jnp.concatenate(parts, axis=1), g_ref[...])


def _final(x1, tg, dest, ys, g, tm):
    n, d = x1.shape
    n_tiles = n // tm
    tile = (V7X_SUBLANES, V7X_LANES)
    dest3 = dest.reshape(n_tiles, 1, tm * TOP_K)
    row = lambda i: (i, 0)
    smem = lambda index_map: pl.BlockSpec((1, 1, tm * TOP_K), index_map, memory_space=pltpu.SMEM)
    vmem = 4 * (2 * TOP_K * tm * d + 2 * 2 * tm * d + 2 * tm * V7X_LANES + 4 * tm * d)
    return pl.pallas_call(
        functools.partial(_final_kernel, tm=tm),
        grid=(n_tiles,),
        in_specs=[smem(lambda i: (i, 0, 0)), smem(lambda i: (jnp.minimum(i + 1, n_tiles - 1), 0, 0)),
                  pl.BlockSpec((tm, d), row), pl.BlockSpec((tm, V7X_LANES), row),
                  pl.BlockSpec((1, d), lambda i: (0, 0)), pl.BlockSpec(memory_space=pl.ANY)],
        out_specs=pl.BlockSpec((tm, d), row),
        out_shape=jax.ShapeDtypeStruct((n, d), F32),
        scratch_shapes=[pltpu.VMEM((2, TOP_K, tm * V7X_SUBLANES, V7X_LANES), F32), pltpu.SemaphoreType.DMA((2,))],
        compiler_params=pltpu.CompilerParams(dimension_semantics=("arbitrary",),
                                             vmem_limit_bytes=_vmem_limit(vmem)),
        name="final",
    )(dest3, dest3, x1, tg, g.reshape(1, d), ys.reshape(-1, *tile))


def kernel(x_prompt, x_sample, cache_k, cache_v, state_ssm_re, state_ssm_im, norm_mix_g, w_in, attn_out_norm_g, ssm_a_re, ssm_a_im, ssm_log_dt, ssm_b_re, ssm_b_im, ssm_c_re, ssm_c_im, ssm_d, w_glu, b_glu, ssm_out_norm_g, w_out, norm_ffn_g, w_router, b_router, w_moe1, b_moe1, w_moe2, b_moe2, norm_final_g):
    depth = w_in.shape[0]
    assert depth == 1, "single-layer trunk"
    bp, s, d = x_prompt.shape
    bs, t, _ = x_sample.shape
    n_buf = cache_k.shape[2]
    n_heads, head_dim = cache_k.shape[3], cache_k.shape[4]
    assert head_dim == HEAD_DIM
    a = n_heads * head_dim
    g = ssm_a_re.shape[1]
    gn = g * SSM_STATE
    assert s % (max(dd for _, dd in DILATED_GROUPS) * KEYS_BACK) == 0
    np_, ns = bp * s, bs * t
    tm_p, tm_s = TOKEN_TILE, ns
    l = 0
    ssm_p = (ssm_a_re[l], ssm_a_im[l], ssm_log_dt[l], ssm_b_re[l], ssm_b_im[l], ssm_c_re[l], ssm_c_im[l],
             ssm_d[l])
    tail = dict(attn_out_norm_g=attn_out_norm_g[l], w_glu=w_glu[l], b_glu=b_glu[l],
                ssm_out_norm_g=ssm_out_norm_g[l], w_out=w_out[l], norm_ffn_g=norm_ffn_g[l],
                w_router=w_router[l], b_router=b_router[l])

    xp = x_prompt.reshape(np_, d)
    qp, kp, vp, up, ktp, vtp = _proj(xp, norm_mix_g[l], w_in[l], a, tm_p, seq=s)
    attn_p = _attn_prompt(qp.reshape(bp, s, a), kp.reshape(bp, s, a), vp.reshape(bp, s, a))
    zeros = jnp.zeros((bp, gn), F32)
    tables = _ssm_tables(*ssm_p, PROMPT_CHUNK)
    y_p, hp_re, hp_im = _ssm_chunks_on_lanes(up.reshape(bp, s, -1), zeros, zeros, _ssm_tables_lanes(tables),
                                             PROMPT_CHUNK)
    seen = jnp.zeros((1, V7X_LANES), F32)
    x1_p, hf_p, ti_p, tg_p, seen = _finish(xp, attn_p.reshape(np_, a), y_p.reshape(np_, -1), tail, seen, tm_p)

    xs = x_sample.reshape(ns, d)
    qs, ks, vs, us = _proj(xs, norm_mix_g[l], w_in[l], a, tm_s)
    by_dim = lambda c: c.transpose(0, 2, 3, 1).reshape(bs, a, n_buf)
    attn_s = _attn_sample(qs.reshape(bs, t, a), ks.reshape(bs, t, a), vs.reshape(bs, t, a),
                          by_dim(cache_k[l]), by_dim(cache_v[l]))
    y_s, hs_re, hs_im = _ssm(us.reshape(bs, t, -1), state_ssm_re[l].reshape(bs, gn),
                             state_ssm_im[l].reshape(bs, gn), _ssm_tables_rows(tables, t), t)
    x1_s, hf_s, ti_s, tg_s, seen = _finish(xs, attn_s.reshape(ns, a), y_s.reshape(ns, -1), tail, seen, tm_s)

    sizes = seen[0, :N_EXPERTS].astype(I32)
    starts = jnp.cumsum(sizes) - sizes
    experts = jnp.arange(N_EXPERTS, dtype=I32)[None, None, :]

    def sorted_row(ti):
        hit = ti[:, :TOP_K, None] == experts
        return jnp.sum(jnp.where(hit, starts[None, None, :], 0), axis=-1) + ti[:, TOP_K:2 * TOP_K]

    dest_p, dest_s = sorted_row(ti_p), sorted_row(ti_s)
    n_rows = (np_ + ns) * TOP_K
    assert n_rows % MOE_TILE == 0
    xs_sorted = _dispatch(hf_p, dest_p, hf_s, dest_s, tm_p).reshape(-1, V7X_LANES)
    ys = _moe_experts(xs_sorted, _moe_visits(sizes, n_rows), w_moe1[l], b_moe1[l], w_moe2[l], b_moe2[l])
    y_prompt = _final(x1_p, tg_p, dest_p, ys, norm_final_g, COMBINE_TILE).reshape(bp, s, d)
    y_sample = _final(x1_s, tg_s, dest_s, ys, norm_final_g, tm_s).reshape(bs, t, d)

    keep = min(max(w for w, _ in DILATED_GROUPS), s)
    k5 = lambda z, b_, s_: z.reshape(1, b_, s_, n_heads, head_dim)
    by_pos = lambda zt: zt.reshape(bp, n_heads, head_dim, s).transpose(0, 3, 1, 2)[None]
    st = lambda z, b_: z.reshape(1, b_, g, SSM_STATE)
    return (y_prompt, y_sample,
            by_pos(ktp)[:, :, s - keep:], by_pos(vtp)[:, :, s - keep:], st(hp_re, bp), st(hp_im, bp),
            k5(ks, bs, t), k5(vs, bs, t), st(hs_re, bs), st(hs_im, bs))
```

```python
import functools

import numpy as np
import jax
import jax.numpy as jnp
from jax import lax
from jax.experimental import pallas as pl
from jax.experimental.pallas import tpu as pltpu

F32 = jnp.float32
I32 = jnp.int32

V7X_LANES = 128
V7X_SUBLANES = 8
V7X_VMEM_BYTES = 64 * 1024 * 1024
COMPILER_SCRATCH_BYTES = 8 * 1024 * 1024

HEAD_DIM = 64
DILATED_GROUPS = ((128, 1), (512, 4), (2048, 16))
KEYS_BACK = 128
SSM_GROUP = 16
SSM_STATE = 64
N_EXPERTS = 32
TOP_K = 4
SWIGLU_LIMIT = 7.0
SWIGLU_ALPHA = 1.702
RMS_EPS = 1e-5
NEG_INF = -1e30
PROMPT_CHUNK = 16
MOE_TILE = 512
MOE_SUB = 256
TOKEN_TILE = 512
COMBINE_TILE = 256


def _vmem_limit(nbytes):
    return int(min(nbytes + COMPILER_SCRATCH_BYTES, V7X_VMEM_BYTES - COMPILER_SCRATCH_BYTES))


def _rms(x, g):
    return x * lax.rsqrt(jnp.mean(x * x, axis=-1, keepdims=True) + RMS_EPS) * g


def _to_token_tiles(ref, val):
    rows, width = val.shape
    assert width == V7X_SUBLANES * V7X_LANES and ref.shape == (rows * V7X_SUBLANES, V7X_LANES)
    for j in range(V7X_SUBLANES):
        ref[pl.ds(j, rows, stride=V7X_SUBLANES), :] = val[:, j * V7X_LANES:(j + 1) * V7X_LANES]


def _from_token_tiles(ref):
    rows = ref.shape[0] // V7X_SUBLANES
    return jnp.concatenate([ref[pl.ds(j, rows, stride=V7X_SUBLANES), :] for j in range(V7X_SUBLANES)], axis=1)


def _proj_kernel(x_ref, g_ref, w_ref, q_ref, k_ref, v_ref, u_ref, *kv_t, attn_width, q_scale):
    h = _rms(x_ref[...], g_ref[...])
    z = jnp.dot(h, w_ref[...], preferred_element_type=F32)
    a = attn_width
    q_ref[...] = z[:, :a] * q_scale
    k_ref[...] = z[:, a:2 * a]
    v_ref[...] = z[:, 2 * a:3 * a]
    u_ref[...] = z[:, 3 * a:]
    if kv_t:
        kt_ref, vt_ref = kv_t
        kt_ref[...] = z[:, a:2 * a].T
        vt_ref[...] = z[:, 2 * a:3 * a].T


def _proj(x2d, norm_g, w_in, attn_width, tm, seq=None):
    n, d = x2d.shape
    mix = w_in.shape[1]
    ssm_width = mix - 3 * attn_width
    row = lambda i: (i, 0)
    const = lambda i: (0, 0)
    out_shape = (jax.ShapeDtypeStruct((n, attn_width), F32),) * 3 + (
        jax.ShapeDtypeStruct((n, ssm_width), F32),)
    out_specs = (pl.BlockSpec((tm, attn_width), row),) * 3 + (pl.BlockSpec((tm, ssm_width), row),)
    if seq is not None:
        assert seq % tm == 0 and tm % V7X_LANES == 0
        per_seq = seq // tm
        out_shape += (jax.ShapeDtypeStruct((n // seq, attn_width, seq), F32),) * 2
        out_specs += (pl.BlockSpec((None, attn_width, tm), lambda i: (i // per_seq, 0, i % per_seq)),) * 2
    vmem = 2 * 4 * (tm * d + d * mix + tm * mix + 2 * tm * attn_width) + 4 * tm * mix
    return pl.pallas_call(
        functools.partial(_proj_kernel, attn_width=attn_width, q_scale=HEAD_DIM ** -0.5),
        grid=(n // tm,),
        in_specs=[pl.BlockSpec((tm, d), row), pl.BlockSpec((1, d), const),
                  pl.BlockSpec((d, mix), const)],
        out_specs=out_specs,
        out_shape=out_shape,
        compiler_params=pltpu.CompilerParams(dimension_semantics=("arbitrary",),
                                             vmem_limit_bytes=_vmem_limit(vmem)),
        name="proj",
    )(x2d, norm_g.reshape(1, d), w_in)


def _attn_prompt_kernel(q_ref, k_ref, v_ref, o_ref, *scratch, seq, unroll):
    nb = KEYS_BACK
    n_groups = len(DILATED_GROUPS)
    m_sc, l_sc, acc_sc = scratch[:n_groups], scratch[n_groups:2 * n_groups], scratch[2 * n_groups:]
    lane = lax.broadcasted_iota(I32, (1, V7X_LANES), 1)
    head_a = lane < HEAD_DIM
    qi = lax.broadcasted_iota(I32, (nb, 2 * nb), 0)
    kj = lax.broadcasted_iota(I32, (nb, 2 * nb), 1)
    rel = nb + qi - kj
    band = (rel >= 0) & (rel <= nb)

    def rows(ref, start, d):
        if d == 1:
            return ref[pl.ds(pl.multiple_of(start, nb), nb), :]
        return ref[pl.ds(start, nb, stride=d), :]

    def put(ref, start, d, val):
        if d == 1:
            ref[pl.ds(pl.multiple_of(start, nb), nb), :] = val
        else:
            ref[pl.ds(start, nb, stride=d), :] = val

    for gi, (w, d) in enumerate(DILATED_GROUPS):
        assert w // d == nb
        nblk = seq // (d * nb)

        def block(t, gi=gi, d=d, nblk=nblk):
            r = t // nblk
            n = t % nblk
            cur = r + d * nb * n
            prev = r + d * nb * jnp.maximum(n - 1, 0)
            q = rows(q_ref, cur, d)
            kk = jnp.concatenate([rows(k_ref, prev, d), rows(k_ref, cur, d)], axis=0)
            vv = jnp.concatenate([rows(v_ref, prev, d), rows(v_ref, cur, d)], axis=0)
            qs = jnp.concatenate([jnp.where(head_a, q, 0.0), jnp.where(head_a, 0.0, q)], axis=0)
            s = lax.dot_general(qs, kk, (((1,), (1,)), ((), ())), preferred_element_type=F32)
            valid = band & (kj >= jnp.where(n > 0, 0, nb))
            s = jnp.where(jnp.concatenate([valid, valid], axis=0), s, NEG_INF)
            m = jnp.max(s, axis=-1, keepdims=True)
            p = jnp.exp(s - m)
            l = jnp.sum(p, axis=-1, keepdims=True)
            o = jnp.dot(p, vv, preferred_element_type=F32)
            o = jnp.where(head_a, o[:nb], o[nb:])
            put(m_sc[gi], cur, d, jnp.where(head_a, m[:nb], m[nb:]))
            put(l_sc[gi], cur, d, jnp.where(head_a, l[:nb], l[nb:]))
            put(acc_sc[gi], cur, d, o)

        def blocks(tt, carry, block=block):
            for u in range(unroll):
                block(tt * unroll + u)
            return carry

        lax.fori_loop(0, seq // nb // unroll, blocks, 0)

    def merge(n, carry):
        sl = pl.ds(pl.multiple_of(n * nb, nb), nb)
        ms = [m[sl, :] for m in m_sc]
        m_all = functools.reduce(jnp.maximum, ms)
        cs = [jnp.exp(m - m_all) for m in ms]
        den = functools.reduce(jnp.add, [c * l[sl, :] for c, l in zip(cs, l_sc)])
        num = functools.reduce(jnp.add, [c * acc[sl, :] for c, acc in zip(cs, acc_sc)])
        o_ref[sl, :] = num / den
        return carry

    lax.fori_loop(0, seq // nb, merge, 0, unroll=2)


def _attn_prompt(q, k, v, unroll=16):
    b, s, a = q.shape
    n_scratch = 3 * len(DILATED_GROUPS)
    spec = pl.BlockSpec((None, s, V7X_LANES), lambda i, j: (i, 0, j))
    vmem = 4 * s * V7X_LANES * (2 * 4 + n_scratch)
    return pl.pallas_call(
        functools.partial(_attn_prompt_kernel, seq=s, unroll=unroll),
        grid=(b, a // V7X_LANES),
        in_specs=[spec, spec, spec],
        out_specs=spec,
        out_shape=jax.ShapeDtypeStruct((b, s, a), F32),
        scratch_shapes=[pltpu.VMEM((s, V7X_LANES), F32)] * n_scratch,
        compiler_params=pltpu.CompilerParams(dimension_semantics=("arbitrary", "arbitrary"),
                                             vmem_limit_bytes=_vmem_limit(vmem)),
        name="attn_prompt",
    )(q, k, v)


def _sample_key_counts(n_buf, n_new):
    cnt = np.zeros((n_new, n_buf + n_new), np.float32)
    for w, d in DILATED_GROUPS:
        for i in range(n_new):
            for j in range(w // d + 1):
                idx = n_buf + i - d * j
                if idx >= 0:
                    cnt[i, idx] += 1.0
    return cnt[:, :n_buf], cnt[:, n_buf:]


def _attn_sample_kernel(q_ref, kn_ref, vn_ref, kt_ref, vt_ref, cnt_ref, o_ref, *, n_heads, cnt_new):
    n_new, a = q_ref.shape
    rows = n_new * n_heads
    lane_head = lax.broadcasted_iota(I32, (rows, a), 1) // HEAD_DIM
    row_head = lax.broadcasted_iota(I32, (rows, a), 0) % n_heads
    row_query = lax.broadcasted_iota(I32, (rows, 1), 0) // n_heads
    own = lane_head == row_head
    q = q_ref[...]
    qb = jnp.where(own, jnp.broadcast_to(q[:, None, :], (n_new, n_heads, a)).reshape(rows, a), 0.0)
    s_buf = jnp.dot(qb, kt_ref[...], preferred_element_type=F32)
    cnt = cnt_ref[...]
    cnt = jnp.broadcast_to(cnt[:, None, :], (n_new, n_heads, cnt.shape[-1])).reshape(rows, -1)
    valid = cnt > 0.0
    m = jnp.max(jnp.where(valid, s_buf, NEG_INF), axis=-1, keepdims=True)
    s_new, c_new = [], []
    for j in range(n_new):
        c = functools.reduce(lambda acc, i: jnp.where(row_query == i, float(cnt_new[i, j]), acc),
                             range(n_new), jnp.zeros((rows, 1), F32))
        s = jnp.sum(qb * kn_ref[pl.ds(j, 1), :], axis=-1, keepdims=True)
        m = jnp.maximum(m, jnp.where(c > 0.0, s, NEG_INF))
        s_new.append(s)
        c_new.append(c)
    p = jnp.where(valid, cnt * jnp.exp(s_buf - m), 0.0)
    den = jnp.sum(p, axis=-1, keepdims=True)
    o = lax.dot_general(p, vt_ref[...], (((1,), (1,)), ((), ())), preferred_element_type=F32)
    for j in range(n_new):
        pj = jnp.where(c_new[j] > 0.0, c_new[j] * jnp.exp(s_new[j] - m), 0.0)
        den = den + pj
        o = o + pj * vn_ref[pl.ds(j, 1), :]
    o = jnp.where(own, o / den, 0.0)
    o_ref[...] = jnp.sum(o.reshape(n_new, n_heads, a), axis=1)


def _attn_sample(q, k_new, v_new, cache_kt, cache_vt):
    b, t, a = q.shape
    n_buf = cache_kt.shape[-1]
    n_heads = a // HEAD_DIM
    cnt_buf, cnt_new = _sample_key_counts(n_buf, t)
    new = pl.BlockSpec((None, t, a), lambda i: (i, 0, 0))
    buf = pl.BlockSpec((None, a, n_buf), lambda i: (i, 0, 0))
    vmem = 4 * (2 * 2 * n_buf * a + 8 * t * n_heads * n_buf)
    return pl.pallas_call(
        functools.partial(_attn_sample_kernel, n_heads=n_heads, cnt_new=cnt_new),
        grid=(b,),
        in_specs=[new, new, new, buf, buf, pl.BlockSpec((t, n_buf), lambda i: (0, 0))],
        out_specs=new,
        out_shape=jax.ShapeDtypeStruct((b, t, a), F32),
        compiler_params=pltpu.CompilerParams(dimension_semantics=("arbitrary",),
                                             vmem_limit_bytes=_vmem_limit(vmem)),
        name="attn_sample",
    )(q, k_new, v_new, cache_kt, cache_vt, jnp.asarray(cnt_buf))


def _block_toeplitz_kernel(br_ref, bi_ref, cr_ref, ci_ref, toep_ref, tt_ref, *, chunk):
    c = br_ref.shape[1]
    over_states = (((1,), (1,)), ((), ()))
    full = dict(preferred_element_type=F32, precision=lax.Precision.HIGHEST)
    for i in range(br_ref.shape[0]):
        k = (lax.dot_general(br_ref[i], cr_ref[i], over_states, **full)
             - lax.dot_general(bi_ref[i], ci_ref[i], over_states, **full))
        for s in range(chunk):
            shifted = k if s == 0 else jnp.concatenate(
                [jnp.zeros((c, s * c), F32), k[:, :(chunk - s) * c]], axis=1)
            toep_ref[i, pl.ds(s * c, c), :] = shifted
        tt_ref[i] = toep_ref[i].T


def _block_toeplitz(bbr, bbi, cpr, cpi, chunk):
    g, c, n = bbr.shape
    lc = cpr.shape[1]
    per_step = 4
    step = lambda i: (i, 0, 0)
    spec = pl.BlockSpec((per_step, lc, lc), step)
    return pl.pallas_call(
        functools.partial(_block_toeplitz_kernel, chunk=chunk),
        grid=(g // per_step,),
        in_specs=[pl.BlockSpec((per_step, c, n), step)] * 2 + [pl.BlockSpec((per_step, lc, n), step)] * 2,
        out_specs=(spec, spec),
        out_shape=(jax.ShapeDtypeStruct((g, lc, lc), F32),) * 2,
        compiler_params=pltpu.CompilerParams(dimension_semantics=("arbitrary",)),
        name="ssm_toeplitz",
    )(bbr, bbi, cpr, cpi)


def _pad_states(m, g):
    half = (np.arange(g) % 2)[:, None, None]
    z = jnp.zeros_like(m)
    return jnp.where(half == 0, jnp.concatenate([m, z], -1), jnp.concatenate([z, m], -1))


def _ssm_tables(a_re, a_im, log_dt, b_re, b_im, c_re, c_im, d_skip, chunk):
    g, n = a_re.shape
    c = b_re.shape[-1]
    lc = chunk * c
    dt = jnp.exp(log_dt)[:, None]
    x, y = a_re * dt, a_im * dt
    ex = jnp.exp(x)
    ar, ai = ex * jnp.cos(y), ex * jnp.sin(y)
    nr = jnp.expm1(x) * jnp.cos(y) - 2.0 * jnp.sin(0.5 * y) ** 2
    ni = ai
    den = a_re * a_re + a_im * a_im
    fr = (nr * a_re + ni * a_im) / den
    fi = (ni * a_re - nr * a_im) / den
    b_re_t, b_im_t = b_re.transpose(0, 2, 1), b_im.transpose(0, 2, 1)
    bbr = fr[:, None, :] * b_re_t - fi[:, None, :] * b_im_t
    bbi = fr[:, None, :] * b_im_t + fi[:, None, :] * b_re_t
    pr, pi = [jnp.ones_like(ar)], [jnp.zeros_like(ar)]
    for _ in range(chunk):
        pr.append(pr[-1] * ar - pi[-1] * ai)
        pi.append(pr[-2] * ai + pi[-1] * ar)
    pr, pi = jnp.stack(pr, 1), jnp.stack(pi, 1)
    cpr = c_re[:, None] * pr[:, :, None, :] - c_im[:, None] * pi[:, :, None, :]
    cpi = c_re[:, None] * pi[:, :, None, :] + c_im[:, None] * pr[:, :, None, :]
    toep, tt = _block_toeplitz(bbr, bbi, cpr[:, :chunk].reshape(g, lc, n), cpi[:, :chunk].reshape(g, lc, n), chunk)
    qr, qi = pr[:, chunk - 1::-1][:, :, None, :], pi[:, chunk - 1::-1][:, :, None, :]
    wr = (qr * bbr[:, None] - qi * bbi[:, None]).reshape(g, lc, n)
    wi = (qr * bbi[:, None] + qi * bbr[:, None]).reshape(g, lc, n)
    vr = cpr[:, 1:].reshape(g, lc, n)
    vi = -cpi[:, 1:].reshape(g, lc, n)
    return dict(chunk=chunk, toep=toep, tt=tt, wr=wr, wi=wi, vr=vr, vi=vi, d=d_skip, pr=pr, pi=pi)


def _ssm_tables_lanes(tab):
    g, chunk = tab["d"].shape[0], tab["chunk"]
    tr = lambda m: m.transpose(0, 2, 1)
    return dict(tt=tab["tt"], wt_re=tr(_pad_states(tab["wr"], g)), wt_im=tr(_pad_states(tab["wi"], g)),
                v_re=_pad_states(tab["vr"], g), v_im=_pad_states(tab["vi"], g),
                dcol=jnp.tile(tab["d"], (1, chunk))[:, :, None],
                al_re=tab["pr"][:, chunk].reshape(1, -1), al_im=tab["pi"][:, chunk].reshape(1, -1))


def _ssm_tables_rows(tab, chunk):
    g, c = tab["d"].shape
    big, lc = tab["chunk"], chunk * c
    tr = lambda m: m.transpose(0, 2, 1)
    tail = slice((big - chunk) * c, big * c)
    return dict(toep=tab["toep"][:, :lc, :lc],
                wr=_pad_states(tab["wr"][:, tail], g), wi=_pad_states(tab["wi"][:, tail], g),
                vr=tr(_pad_states(tab["vr"][:, :lc], g)), vi=tr(_pad_states(tab["vi"][:, :lc], g)),
                d=jnp.tile(tab["d"], (1, chunk))[:, None, :],
                al_re=tab["pr"][:, chunk].reshape(1, -1), al_im=tab["pi"][:, chunk].reshape(1, -1))


def _ssm_state_kernel(u_ref, wr_ref, wi_ref, er_ref, ei_ref):
    er = jnp.dot(u_ref[0], wr_ref[0], preferred_element_type=F32)
    ei = jnp.dot(u_ref[0], wi_ref[0], preferred_element_type=F32)
    er_ref[...] = er + jnp.dot(u_ref[1], wr_ref[1], preferred_element_type=F32)
    ei_ref[...] = ei + jnp.dot(u_ref[1], wi_ref[1], preferred_element_type=F32)


def _ssm_scan_kernel(er_ref, ei_ref, ar_ref, ai_ref, hr_ref, hi_ref, xr_ref, xi_ref, lr_ref, li_ref,
                     *, n_chunks, nb):
    shape = hr_ref.shape
    ar = jnp.broadcast_to(ar_ref[...], shape)
    ai = jnp.broadcast_to(ai_ref[...], shape)

    def step(j, carry):
        xr, xi = carry
        sl = pl.ds(pl.multiple_of(j * nb, nb), nb)
        xr_ref[sl, :] = xr
        xi_ref[sl, :] = xi
        return (ar * xr - ai * xi + er_ref[sl, :], ar * xi + ai * xr + ei_ref[sl, :])

    xr, xi = lax.fori_loop(0, n_chunks, step, (hr_ref[...], hi_ref[...]))
    lr_ref[...] = xr
    li_ref[...] = xi


def _ssm_out_kernel(u_ref, t_ref, d_ref, xr_ref, xi_ref, vr_ref, vi_ref, y_ref):
    xr = xr_ref[...]
    xi = xi_ref[...]
    for a in range(2):
        u = u_ref[a]
        y = jnp.dot(u, t_ref[a], preferred_element_type=F32) + d_ref[a] * u
        y = y + jnp.dot(xr, vr_ref[a], preferred_element_type=F32)
        y_ref[a] = y + jnp.dot(xi, vi_ref[a], preferred_element_type=F32)


def _ssm_scan(er, ei, tab, h0_re, h0_im, n_chunks):
    r, gn = er.shape
    b = r // n_chunks
    lanes = 4 * V7X_LANES
    col = lambda i: (0, i)
    return pl.pallas_call(
        functools.partial(_ssm_scan_kernel, n_chunks=n_chunks, nb=b),
        grid=(gn // lanes,),
        in_specs=[pl.BlockSpec((r, lanes), col), pl.BlockSpec((r, lanes), col),
                  pl.BlockSpec((1, lanes), col), pl.BlockSpec((1, lanes), col),
                  pl.BlockSpec((b, lanes), col), pl.BlockSpec((b, lanes), col)],
        out_specs=(pl.BlockSpec((r, lanes), col),) * 2 + (pl.BlockSpec((b, lanes), col),) * 2,
        out_shape=(jax.ShapeDtypeStruct((r, gn), F32),) * 2 + (jax.ShapeDtypeStruct((b, gn), F32),) * 2,
        compiler_params=pltpu.CompilerParams(dimension_semantics=("arbitrary",),
                                             vmem_limit_bytes=_vmem_limit(2 * 4 * 4 * r * lanes)),
        name="ssm_scan",
    )(er, ei, tab["al_re"], tab["al_im"], h0_re, h0_im)


def _ssm_state_t_kernel(a_ref, wr_ref, wi_ref, er_ref, ei_ref):
    def seq(b, carry):
        a0, a1 = a_ref[0, b], a_ref[1, b]
        er_ref[b] = (jnp.dot(wr_ref[0], a0, preferred_element_type=F32)
                     + jnp.dot(wr_ref[1], a1, preferred_element_type=F32))
        ei_ref[b] = (jnp.dot(wi_ref[0], a0, preferred_element_type=F32)
                     + jnp.dot(wi_ref[1], a1, preferred_element_type=F32))
        return carry

    lax.fori_loop(0, a_ref.shape[1], seq, 0, unroll=True)


def _ssm_out_t_kernel(a_ref, t_ref, d_ref, xr_ref, xi_ref, vr_ref, vi_ref, y_ref):
    def seq(b, carry):
        xr, xi = xr_ref[b], xi_ref[b]
        for a in range(2):
            at = a_ref[a, b]
            y = jnp.dot(t_ref[a], at, preferred_element_type=F32) + d_ref[a] * at
            y = y + jnp.dot(vr_ref[a], xr, preferred_element_type=F32)
            y_ref[a, b] = y + jnp.dot(vi_ref[a], xi, preferred_element_type=F32)
        return carry

    lax.fori_loop(0, a_ref.shape[1], seq, 0, unroll=True)


def _ssm_chunks_on_lanes(u, h0_re, h0_im, tab, chunk):
    b, s, width = u.shape
    g = width // SSM_GROUP
    n_chunks = s // chunk
    lc = chunk * SSM_GROUP
    assert n_chunks == V7X_LANES
    at = (u.reshape(b, n_chunks, chunk * width).transpose(0, 2, 1)
          .reshape(b, chunk, g, SSM_GROUP, n_chunks).transpose(2, 0, 1, 3, 4).reshape(g, b, lc, n_chunks))
    pair4 = lambda i: (i, 0, 0, 0)
    pair3 = lambda i: (i, 0, 0)
    cp = lambda vmem: pltpu.CompilerParams(dimension_semantics=("arbitrary",),
                                           vmem_limit_bytes=_vmem_limit(vmem))
    a_spec = pl.BlockSpec((2, b, lc, n_chunks), pair4)
    st_spec = pl.BlockSpec((None, b, V7X_LANES, n_chunks), pair4)
    w_spec = pl.BlockSpec((2, V7X_LANES, lc), pair3)
    a_bytes = 2 * b * lc * n_chunks
    st_bytes = b * V7X_LANES * n_chunks
    et_re, et_im = pl.pallas_call(
        _ssm_state_t_kernel,
        grid=(g // 2,),
        in_specs=[a_spec, w_spec, w_spec],
        out_specs=(st_spec,) * 2,
        out_shape=(jax.ShapeDtypeStruct((g // 2, b, V7X_LANES, n_chunks), F32),) * 2,
        compiler_params=cp(2 * 4 * (a_bytes + 2 * st_bytes + 4 * V7X_LANES * lc)),
        name="ssm_state_t",
    )(at, tab["wt_re"], tab["wt_im"])
    rows = lambda m: m.transpose(3, 1, 0, 2).reshape(n_chunks * b, g * SSM_STATE)
    xr, xi, lr, li = _ssm_scan(rows(et_re), rows(et_im), tab, h0_re, h0_im, n_chunks)
    cols = lambda m: m.reshape(n_chunks, b, g // 2, V7X_LANES).transpose(2, 1, 3, 0)
    yt = pl.pallas_call(
        _ssm_out_t_kernel,
        grid=(g // 2,),
        in_specs=[a_spec, pl.BlockSpec((2, lc, lc), pair3), pl.BlockSpec((2, lc, 1), pair3),
                  st_spec, st_spec, pl.BlockSpec((2, lc, V7X_LANES), pair3),
                  pl.BlockSpec((2, lc, V7X_LANES), pair3)],
        out_specs=a_spec,
        out_shape=jax.ShapeDtypeStruct((g, b, lc, n_chunks), F32),
        compiler_params=cp(2 * 4 * (2 * a_bytes + 2 * st_bytes + 2 * lc * lc + 6 * V7X_LANES * lc)),
        name="ssm_out_t",
    )(at, tab["tt"], tab["dcol"], cols(xr), cols(xi), tab["v_re"], tab["v_im"])
    y = (yt.reshape(g, b, chunk, SSM_GROUP, n_chunks).transpose(1, 2, 0, 3, 4)
         .reshape(b, chunk * width, n_chunks).transpose(0, 2, 1).reshape(b, s, width))
    return y, lr, li


def _ssm(u, h0_re, h0_im, tab, chunk):
    b, s, width = u.shape
    g = width // SSM_GROUP
    n_chunks = s // chunk
    lc = chunk * SSM_GROUP
    r = n_chunks * b
    gn = g * SSM_STATE
    ut = u.reshape(b, n_chunks, chunk, g, SSM_GROUP).transpose(3, 1, 0, 2, 4).reshape(g, r, lc)
    pair = lambda i: (i, 0, 0)
    cp = lambda vmem: pltpu.CompilerParams(dimension_semantics=("arbitrary",),
                                           vmem_limit_bytes=_vmem_limit(vmem))
    er, ei = pl.pallas_call(
        _ssm_state_kernel,
        grid=(g // 2,),
        in_specs=[pl.BlockSpec((2, r, lc), pair), pl.BlockSpec((2, lc, V7X_LANES), pair),
                  pl.BlockSpec((2, lc, V7X_LANES), pair)],
        out_specs=(pl.BlockSpec((r, V7X_LANES), lambda i: (0, i)),) * 2,
        out_shape=(jax.ShapeDtypeStruct((r, gn), F32),) * 2,
        compiler_params=cp(2 * 4 * (2 * r * lc + 4 * lc * V7X_LANES + 2 * r * V7X_LANES)),
        name="ssm_state",
    )(ut, tab["wr"], tab["wi"])
    xr, xi, lr, li = _ssm_scan(er, ei, tab, h0_re, h0_im, n_chunks)
    y = pl.pallas_call(
        _ssm_out_kernel,
        grid=(g // 2,),
        in_specs=[pl.BlockSpec((2, r, lc), pair), pl.BlockSpec((2, lc, lc), pair),
                  pl.BlockSpec((2, 1, lc), pair),
                  pl.BlockSpec((r, V7X_LANES), lambda i: (0, i)),
                  pl.BlockSpec((r, V7X_LANES), lambda i: (0, i)),
                  pl.BlockSpec((2, V7X_LANES, lc), pair), pl.BlockSpec((2, V7X_LANES, lc), pair)],
        out_specs=pl.BlockSpec((2, r, lc), pair),
        out_shape=jax.ShapeDtypeStruct((g, r, lc), F32),
        compiler_params=cp(2 * 4 * (4 * r * lc + 2 * lc * lc + 2 * r * V7X_LANES + 4 * V7X_LANES * lc)),
        name="ssm_out",
    )(ut, tab["toep"], tab["d"], xr, xi, tab["vr"], tab["vi"])
    y = y.reshape(g, n_chunks, b, chunk, SSM_GROUP).transpose(2, 1, 3, 0, 4).reshape(b, s, width)
    return y, lr, li


def _finish_kernel(x_ref, attn_ref, ssm_ref, ga_ref, wg_ref, bg_ref, gs_ref, wo_ref, gf_ref,
                   wr_ref, br_ref, base_ref, x1_ref, hf_ref, ti_ref, tg_ref, cnt_ref, seen):
    a = attn_ref.shape[-1]
    tm = x_ref.shape[0]

    @pl.when(pl.program_id(0) == 0)
    def _():
        seen[...] = base_ref[...]

    g = jax.nn.gelu(ssm_ref[...])
    gate = jax.nn.sigmoid(jnp.dot(g, wg_ref[...], preferred_element_type=F32) + bg_ref[...])
    mixed_a = _rms(attn_ref[...], ga_ref[...])
    mixed_s = _rms(g * gate, gs_ref[...])
    x1 = (x_ref[...] + jnp.dot(mixed_a, wo_ref[pl.ds(0, a), :], preferred_element_type=F32)
          + jnp.dot(mixed_s, wo_ref[pl.ds(a, wo_ref.shape[0] - a), :], preferred_element_type=F32))
    x1_ref[...] = x1
    hf = _rms(x1, gf_ref[...])
    _to_token_tiles(hf_ref, hf)
    hf_hi = hf.astype(jnp.bfloat16).astype(F32)
    hf_lo = hf - hf_hi
    small = (jnp.dot(hf_lo, wr_ref[1], preferred_element_type=F32)
             + jnp.dot(hf_hi, wr_ref[1], preferred_element_type=F32)
             + jnp.dot(hf_lo, wr_ref[0], preferred_element_type=F32))
    logits = jnp.dot(hf_hi, wr_ref[0], preferred_element_type=F32) + small + br_ref[...]
    lane = lax.broadcasted_iota(I32, logits.shape, 1)
    work = logits
    vals, idxs = [], []
    for _ in range(TOP_K):
        m = jnp.max(work, axis=-1, keepdims=True)
        idx = jnp.min(jnp.where(work == m, lane, V7X_LANES), axis=-1, keepdims=True)
        vals.append(m)
        idxs.append(idx)
        work = jnp.where(lane == idx, -jnp.inf, work)
    exps = [jnp.exp(v - vals[0]) for v in vals]
    den = exps[0]
    for e in exps[1:]:
        den = den + e
    picked = jnp.zeros(logits.shape, F32)
    for k in range(TOP_K):
        picked = picked + jnp.where(lane == idxs[k], 1.0, 0.0)
    tri = jnp.where(lax.broadcasted_iota(I32, (tm, tm), 0) > lax.broadcasted_iota(I32, (tm, tm), 1), 1.0, 0.0)
    before = jnp.dot(tri, picked, preferred_element_type=F32) + seen[...]
    ti = jnp.zeros(logits.shape, I32)
    tg = jnp.zeros(logits.shape, F32)
    for k in range(TOP_K):
        rank = jnp.sum(jnp.where(lane == idxs[k], before, 0.0), axis=-1, keepdims=True)
        ti = jnp.where(lane == k, idxs[k], ti)
        ti = jnp.where(lane == TOP_K + k, rank.astype(I32), ti)
        tg = jnp.where(lane == k, exps[k] / den, tg)
    ti_ref[...] = ti
    tg_ref[...] = tg
    seen[...] = seen[...] + jnp.sum(picked, axis=0, keepdims=True)
    cnt_ref[...] = seen[...]


def _finish(x2d, attn, ssm_y, p, seen, tm):
    n, d = x2d.shape
    a = attn.shape[1]
    sw = ssm_y.shape[1]
    ne = p["w_router"].shape[1]
    w_router = jnp.pad(p["w_router"], ((0, 0), (0, V7X_LANES - ne)))
    w_hi = w_router.astype(jnp.bfloat16).astype(F32)
    w_router = jnp.stack([w_hi, w_router - w_hi])
    b_router = jnp.pad(p["b_router"], (0, V7X_LANES - ne), constant_values=-jnp.inf).reshape(1, V7X_LANES)
    row = lambda i: (i, 0)
    full = lambda arr: pl.BlockSpec(arr.shape, lambda i: (0,) * arr.ndim)
    ins = [x2d, attn, ssm_y, p["attn_out_norm_g"].reshape(1, a), p["w_glu"], p["b_glu"].reshape(1, sw),
           p["ssm_out_norm_g"].reshape(1, sw), p["w_out"], p["norm_ffn_g"].reshape(1, d),
           w_router, b_router, seen]
    in_specs = [pl.BlockSpec((tm, d), row), pl.BlockSpec((tm, a), row), pl.BlockSpec((tm, sw), row)]
    in_specs += [full(arr) for arr in ins[3:]]
    vmem = 2 * 4 * (tm * (3 * d + a + sw + 2 * V7X_LANES) + sw * sw + d * d + 2 * d * V7X_LANES) + 16 * tm * d
    return pl.pallas_call(
        _finish_kernel,
        grid=(n // tm,),
        in_specs=in_specs,
        out_specs=(pl.BlockSpec((tm, d), row), pl.BlockSpec((tm * V7X_SUBLANES, V7X_LANES), row),
                   pl.BlockSpec((tm, V7X_LANES), row), pl.BlockSpec((tm, V7X_LANES), row),
                   pl.BlockSpec((1, V7X_LANES), lambda i: (0, 0))),
        out_shape=(jax.ShapeDtypeStruct((n, d), F32), jax.ShapeDtypeStruct((n * V7X_SUBLANES, V7X_LANES), F32),
                   jax.ShapeDtypeStruct((n, V7X_LANES), I32), jax.ShapeDtypeStruct((n, V7X_LANES), F32),
                   jax.ShapeDtypeStruct((1, V7X_LANES), F32)),
        scratch_shapes=[pltpu.VMEM((1, V7X_LANES), F32)],
        compiler_params=pltpu.CompilerParams(dimension_semantics=("arbitrary",),
                                             vmem_limit_bytes=_vmem_limit(vmem)),
        name="finish",
    )(*ins)


def _tile_copy(src_ref, src_row, dst_ref, dst_row, sem):
    return pltpu.make_async_copy(src_ref.at[src_row], dst_ref.at[dst_row], sem)


def _dispatch_kernel(dest_ref, hp_ref, hs_ref, xs_ref, sem, *, n_tiles_p):
    def scatter(hf_ref):
        tm = hf_ref.shape[0]

        def tokens(i, carry):
            for u in range(V7X_SUBLANES):
                r = i * V7X_SUBLANES + u
                for k in range(TOP_K):
                    _tile_copy(hf_ref, r, xs_ref, dest_ref[0, 0, r * TOP_K + k], sem).start(priority=k % 2)
            return carry

        lax.fori_loop(0, tm // V7X_SUBLANES, tokens, 0)
        rows = pl.ds(0, tm * TOP_K)
        pltpu.make_async_copy(xs_ref.at[rows], xs_ref.at[rows], sem).wait()

    @pl.when(pl.program_id(0) < n_tiles_p)
    def _():
        scatter(hp_ref)

    @pl.when(pl.program_id(0) == n_tiles_p)
    def _():
        scatter(hs_ref)


def _dispatch(hf_p, dest_p, hf_s, dest_s, tm):
    tile = (V7X_SUBLANES, V7X_LANES)
    hf_p, hf_s = hf_p.reshape(-1, *tile), hf_s.reshape(-1, *tile)
    n_p, n_s = hf_p.shape[0], hf_s.shape[0]
    assert n_p % tm == 0 and n_s <= tm
    n_tiles_p = n_p // tm
    dest3 = jnp.concatenate([dest_p.reshape(n_tiles_p, tm * TOP_K),
                             jnp.pad(dest_s.reshape(1, n_s * TOP_K), ((0, 0), (0, (tm - n_s) * TOP_K)))])
    dest3 = dest3.reshape(n_tiles_p + 1, 1, tm * TOP_K)
    return pl.pallas_call(
        functools.partial(_dispatch_kernel, n_tiles_p=n_tiles_p),
        grid=(n_tiles_p + 1,),
        in_specs=[pl.BlockSpec((1, 1, tm * TOP_K), lambda i: (i, 0, 0), memory_space=pltpu.SMEM),
                  pl.BlockSpec((tm, *tile), lambda i: (jnp.minimum(i, n_tiles_p - 1), 0, 0)),
                  pl.BlockSpec((n_s, *tile), lambda i: (0, 0, 0))],
        out_specs=pl.BlockSpec(memory_space=pl.ANY),
        out_shape=jax.ShapeDtypeStruct(((n_p + n_s) * TOP_K, *tile), F32),
        scratch_shapes=[pltpu.SemaphoreType.DMA(())],
        compiler_params=pltpu.CompilerParams(dimension_semantics=("arbitrary",)),
        name="dispatch",
    )(dest3, hf_p, hf_s)


def _moe_kernel(vt_ref, ve_ref, nx_ref, lo_ref, hi_ref, x_ref, w1_ref, b1_ref, w2_ref, b2_ref, y_ref,
                w1s, w2s, w1b, w2b, sem, *, d_ff):
    v = pl.program_id(0)
    lo, hi = lo_ref[v], hi_ref[v]
    prev = jnp.maximum(v - 1, 0)
    first = jnp.logical_or(v == 0, vt_ref[v] != vt_ref[prev])

    def fetch(e):
        return (pltpu.make_async_copy(w1_ref.at[e], w1s, sem.at[0]),
                pltpu.make_async_copy(w2_ref.at[e], w2s, sem.at[1]))

    @pl.when(v == 0)
    def _():
        for cp in fetch(ve_ref[0]):
            cp.start()

    @pl.when(jnp.logical_or(v == 0, ve_ref[v] != ve_ref[prev]))
    def _():
        for cp in fetch(ve_ref[v]):
            cp.wait()
        w1b[...] = w1s[...].astype(jnp.bfloat16)
        w2b[...] = w2s[...].astype(jnp.bfloat16)

        @pl.when(nx_ref[v] >= 0)
        def _():
            for cp in fetch(nx_ref[v]):
                cp.start()

    def swiglu(x_tiles):
        x = _from_token_tiles(x_tiles).astype(jnp.bfloat16)
        a = jnp.dot(x, w1b[...], preferred_element_type=F32) + b1_ref[0]
        x_glu = jnp.minimum(a[:, :d_ff], SWIGLU_LIMIT)
        x_lin = jnp.clip(a[:, d_ff:], -SWIGLU_LIMIT, SWIGLU_LIMIT)
        mid = x_glu * jax.nn.sigmoid(SWIGLU_ALPHA * x_glu) * (x_lin + 1.0)
        return jnp.dot(mid.astype(jnp.bfloat16), w2b[...], preferred_element_type=F32) + b2_ref[0]

    whole = jnp.logical_and(lo == 0, hi == MOE_TILE)

    @pl.when(whole)
    def _():
        _to_token_tiles(y_ref, swiglu(x_ref))

    for r0 in range(0, MOE_TILE, MOE_SUB):
        sub = pl.ds(r0 * V7X_SUBLANES, MOE_SUB * V7X_SUBLANES)
        x_sub, y_sub = x_ref.at[sub, :], y_ref.at[sub, :]
        touched = jnp.logical_and(jnp.logical_not(whole), jnp.logical_and(lo < r0 + MOE_SUB, hi > r0))

        @pl.when(touched)
        def _(r0=r0, x_sub=x_sub, y_sub=y_sub):
            y = swiglu(x_sub)
            rows = r0 + lax.broadcasted_iota(I32, (MOE_SUB, 1), 0)
            mine = (rows >= lo) & (rows < hi)

            @pl.when(first)
            def _():
                _to_token_tiles(y_sub, jnp.where(mine, y, 0.0))

            @pl.when(jnp.logical_not(first))
            def _():
                _to_token_tiles(y_sub, jnp.where(mine, y, _from_token_tiles(y_sub)))

        @pl.when(jnp.logical_and(first, jnp.logical_not(jnp.logical_or(touched, whole))))
        def _(y_sub=y_sub):
            y_sub[...] = jnp.zeros(y_sub.shape, F32)


def _moe_experts(xs, visits, w1, b1, w2, b2):
    ne, d, ff2 = w1.shape
    d_ff = ff2 // 2
    n_visits = visits[0].shape[0]
    row = lambda v, vt, ve, nx, lo, hi: (vt[v], 0)
    exp = lambda v, vt, ve, nx, lo, hi: (ve[v], 0, 0)
    vmem = (4 * 2 * 2 * MOE_TILE * d + (4 + 2) * (d * ff2 + d_ff * d) + 2 * 4 * (ff2 + d)
            + 4 * 4 * MOE_TILE * ff2)
    tiles = pl.BlockSpec((MOE_TILE * V7X_SUBLANES, V7X_LANES), row)
    grid_spec = pltpu.PrefetchScalarGridSpec(
        num_scalar_prefetch=5,
        grid=(n_visits,),
        in_specs=[tiles,
                  pl.BlockSpec(memory_space=pl.ANY), pl.BlockSpec((1, 1, ff2), exp),
                  pl.BlockSpec(memory_space=pl.ANY), pl.BlockSpec((1, 1, d), exp)],
        out_specs=tiles,
        scratch_shapes=[pltpu.VMEM((d, ff2), F32), pltpu.VMEM((d_ff, d), F32),
                        pltpu.VMEM((d, ff2), jnp.bfloat16), pltpu.VMEM((d_ff, d), jnp.bfloat16),
                        pltpu.SemaphoreType.DMA((2,))],
    )
    return pl.pallas_call(
        functools.partial(_moe_kernel, d_ff=d_ff),
        grid_spec=grid_spec,
        out_shape=jax.ShapeDtypeStruct(xs.shape, F32),
        compiler_params=pltpu.CompilerParams(dimension_semantics=("arbitrary",),
                                             vmem_limit_bytes=_vmem_limit(vmem)),
        name="moe",
    )(*visits, xs, w1, b1.reshape(ne, 1, ff2), w2, b2.reshape(ne, 1, d))


def _moe_visits(sizes, n_rows):
    n_tiles = n_rows // MOE_TILE
    n_visits = n_tiles + N_EXPERTS - 1
    ends = jnp.cumsum(sizes)
    starts = ends - sizes
    first_tile = starts // MOE_TILE
    n_vis = jnp.where(sizes > 0, (ends - 1) // MOE_TILE - first_tile + 1, 0)
    vis_end = jnp.cumsum(n_vis)
    total = vis_end[-1]
    v = jnp.clip(jnp.arange(n_visits, dtype=I32), 0, jnp.maximum(total - 1, 0))
    e = jnp.sum((vis_end[None, :] <= v[:, None]).astype(I32), axis=1)
    pick = lambda tab: jnp.sum(jnp.where(e[:, None] == jnp.arange(N_EXPERTS)[None, :], tab[None, :], 0), axis=1)
    tile = pick(first_tile) + v - pick(vis_end - n_vis)
    lo = jnp.maximum(pick(starts), tile * MOE_TILE) - tile * MOE_TILE
    hi = jnp.minimum(pick(ends), (tile + 1) * MOE_TILE) - tile * MOE_TILE
    live = jnp.arange(n_visits) < total
    ids = jnp.arange(N_EXPERTS, dtype=I32)
    later = jnp.where((ids[None, :] > ids[:, None]) & (sizes[None, :] > 0), ids[None, :], N_EXPERTS)
    nxt = jnp.min(later, axis=1)
    nxt = pick(jnp.where(nxt < N_EXPERTS, nxt, -1))
    return (tile.astype(I32), e.astype(I32), nxt.astype(I32), jnp.where(live, lo, 0).astype(I32),
            jnp.where(live, hi, 0).astype(I32))


def _final_kernel(dest_ref, nxt_ref, x_ref, tg_ref, g_ref, ys_ref, y_ref, buf, sem, *, tm):
    i = pl.program_id(0)
    n = pl.num_programs(0)

    def gather(d_ref, slot):
        def tokens(i8, carry):
            for u in range(V7X_SUBLANES):
                r = i8 * V7X_SUBLANES + u
                tile = pl.ds(pl.multiple_of(r * V7X_SUBLANES, V7X_SUBLANES), V7X_SUBLANES)
                for k in range(TOP_K):
                    pltpu.make_async_copy(ys_ref.at[d_ref[0, 0, r * TOP_K + k]], buf.at[slot, k, tile, :],
                                          sem.at[slot]).start(priority=k % 2)
            return carry
        lax.fori_loop(0, tm // V7X_SUBLANES, tokens, 0)

    @pl.when(i == 0)
    def _():
        gather(dest_ref, 0)

    for slot in range(2):
        @pl.when(jnp.logical_and(i + 1 < n, (i + 1) % 2 == slot))
        def _(slot=slot):
            gather(nxt_ref, slot)

    for slot in range(2):
        @pl.when(i % 2 == slot)
        def _(slot=slot):
            pltpu.make_async_copy(buf.at[slot], buf.at[slot], sem.at[slot]).wait()
            tg = tg_ref[...]
            parts = []
            for j in range(V7X_SUBLANES):
                f = None
                for k in range(TOP_K):
                    rows = buf[slot, k, pl.ds(j, tm, stride=V7X_SUBLANES), :]
                    f = tg[:, k:k + 1] * rows if f is None else f + tg[:, k:k + 1] * rows
                parts.append(f)
            y_ref[...] = _rms(x_ref[...] + jnp.concatenate(parts, axis=1), g_ref[...])


def _final(x1, tg, dest, ys, g, tm):
    n, d = x1.shape
    n_tiles = n // tm
    tile = (V7X_SUBLANES, V7X_LANES)
    dest3 = dest.reshape(n_tiles, 1, tm * TOP_K)
    row = lambda i: (i, 0)
    smem = lambda index_map: pl.BlockSpec((1, 1, tm * TOP_K), index_map, memory_space=pltpu.SMEM)
    vmem = 4 * (2 * TOP_K * tm * d + 2 * 2 * tm * d + 2 * tm * V7X_LANES + 4 * tm * d)
    return pl.pallas_call(
        functools.partial(_final_kernel, tm=tm),
        grid=(n_tiles,),
        in_specs=[smem(lambda i: (i, 0, 0)), smem(lambda i: (jnp.minimum(i + 1, n_tiles - 1), 0, 0)),
                  pl.BlockSpec((tm, d), row), pl.BlockSpec((tm, V7X_LANES), row),
                  pl.BlockSpec((1, d), lambda i: (0, 0)), pl.BlockSpec(memory_space=pl.ANY)],
        out_specs=pl.BlockSpec((tm, d), row),
        out_shape=jax.ShapeDtypeStruct((n, d), F32),
        scratch_shapes=[pltpu.VMEM((2, TOP_K, tm * V7X_SUBLANES, V7X_LANES), F32), pltpu.SemaphoreType.DMA((2,))],
        compiler_params=pltpu.CompilerParams(dimension_semantics=("arbitrary",),
                                             vmem_limit_bytes=_vmem_limit(vmem)),
        name="final",
    )(dest3, dest3, x1, tg, g.reshape(1, d), ys.reshape(-1, *tile))


def kernel(x_prompt, x_sample, cache_k, cache_v, state_ssm_re, state_ssm_im, norm_mix_g, w_in, attn_out_norm_g, ssm_a_re, ssm_a_im, ssm_log_dt, ssm_b_re, ssm_b_im, ssm_c_re, ssm_c_im, ssm_d, w_glu, b_glu, ssm_out_norm_g, w_out, norm_ffn_g, w_router, b_router, w_moe1, b_moe1, w_moe2, b_moe2, norm_final_g):
    depth = w_in.shape[0]
    assert depth == 1, "single-layer trunk"
    bp, s, d = x_prompt.shape
    bs, t, _ = x_sample.shape
    n_buf = cache_k.shape[2]
    n_heads, head_dim = cache_k.shape[3], cache_k.shape[4]
    assert head_dim == HEAD_DIM
    a = n_heads * head_dim
    g = ssm_a_re.shape[1]
    gn = g * SSM_STATE
    assert s % (max(dd for _, dd in DILATED_GROUPS) * KEYS_BACK) == 0
    np_, ns = bp * s, bs * t
    tm_p, tm_s = TOKEN_TILE, ns
    l = 0
    ssm_p = (ssm_a_re[l], ssm_a_im[l], ssm_log_dt[l], ssm_b_re[l], ssm_b_im[l], ssm_c_re[l], ssm_c_im[l],
             ssm_d[l])
    tail = dict(attn_out_norm_g=attn_out_norm_g[l], w_glu=w_glu[l], b_glu=b_glu[l],
                ssm_out_norm_g=ssm_out_norm_g[l], w_out=w_out[l], norm_ffn_g=norm_ffn_g[l],
                w_router=w_router[l], b_router=b_router[l])

    xp = x_prompt.reshape(np_, d)
    qp, kp, vp, up, ktp, vtp = _proj(xp, norm_mix_g[l], w_in[l], a, tm_p, seq=s)
    attn_p = _attn_prompt(qp.reshape(bp, s, a), kp.reshape(bp, s, a), vp.reshape(bp, s, a))
    zeros = jnp.zeros((bp, gn), F32)
    tables = _ssm_tables(*ssm_p, PROMPT_CHUNK)
    y_p, hp_re, hp_im = _ssm_chunks_on_lanes(up.reshape(bp, s, -1), zeros, zeros, _ssm_tables_lanes(tables),
                                             PROMPT_CHUNK)
    seen = jnp.zeros((1, V7X_LANES), F32)
    x1_p, hf_p, ti_p, tg_p, seen = _finish(xp, attn_p.reshape(np_, a), y_p.reshape(np_, -1), tail, seen, tm_p)

    xs = x_sample.reshape(ns, d)
    qs, ks, vs, us = _proj(xs, norm_mix_g[l], w_in[l], a, tm_s)
    by_dim = lambda c: c.transpose(0, 2, 3, 1).reshape(bs, a, n_buf)
    attn_s = _attn_sample(qs.reshape(bs, t, a), ks.reshape(bs, t, a), vs.reshape(bs, t, a),
                          by_dim(cache_k[l]), by_dim(cache_v[l]))
    y_s, hs_re, hs_im = _ssm(us.reshape(bs, t, -1), state_ssm_re[l].reshape(bs, gn),
                             state_ssm_im[l].reshape(bs, gn), _ssm_tables_rows(tables, t), t)
    x1_s, hf_s, ti_s, tg_s, seen = _finish(xs, attn_s.reshape(ns, a), y_s.reshape(ns, -1), tail, seen, tm_s)

    sizes = seen[0, :N_EXPERTS].astype(I32)
    starts = jnp.cumsum(sizes) - sizes
    experts = jnp.arange(N_EXPERTS, dtype=I32)[None, None, :]

    def sorted_row(ti):
        hit = ti[:, :TOP_K, None] == experts
        return jnp.sum(jnp.where(hit, starts[None, None, :], 0), axis=-1) + ti[:, TOP_K:2 * TOP_K]

    dest_p, dest_s = sorted_row(ti_p), sorted_row(ti_s)
    n_rows = (np_ + ns) * TOP_K
    assert n_rows % MOE_TILE == 0
    xs_sorted = _dispatch(hf_p, dest_p, hf_s, dest_s, tm_p).reshape(-1, V7X_LANES)
    ys = _moe_experts(xs_sorted, _moe_visits(sizes, n_rows), w_moe1[l], b_moe1[l], w_moe2[l], b_moe2[l])
    y_prompt = _final(x1_p, tg_p, dest_p, ys, norm_final_g, COMBINE_TILE).reshape(bp, s, d)
    y_sample = _final(x1_s, tg_s, dest_s, ys, norm_final_g, tm_s).reshape(bs, t, d)

    keep = min(max(w for w, _ in DILATED_GROUPS), s)
    k5 = lambda z, b_, s_: z.reshape(1, b_, s_, n_heads, head_dim)
    by_pos = lambda zt: zt.reshape(bp, n_heads, head_dim, s).transpose(0, 3, 1, 2)[None]
    st = lambda z, b_: z.reshape(1, b_, g, SSM_STATE)
    return (y_prompt, y_sample,
            by_pos(ktp)[:, :, s - keep:], by_pos(vtp)[:, :, s - keep:], st(hp_re, bp), st(hp_im, bp),
            k5(ks, bs, t), k5(vs, bs, t), st(hs_re, bs), st(hs_im, bs))
```

```python
import functools

import numpy as np
import jax
import jax.numpy as jnp
from jax import lax
from jax.experimental import pallas as pl
from jax.experimental.pallas import tpu as pltpu

F32 = jnp.float32
I32 = jnp.int32

V7X_LANES = 128
V7X_SUBLANES = 8
V7X_VMEM_BYTES = 64 * 1024 * 1024
COMPILER_SCRATCH_BYTES = 8 * 1024 * 1024

HEAD_DIM = 64
DILATED_GROUPS = ((128, 1), (512, 4), (2048, 16))
KEYS_BACK = 128
SSM_GROUP = 16
SSM_STATE = 64
N_EXPERTS = 32
TOP_K = 4
SWIGLU_LIMIT = 7.0
SWIGLU_ALPHA = 1.702
RMS_EPS = 1e-5
NEG_INF = -1e30
PROMPT_CHUNK = 16
MOE_TILE = 512
MOE_SUB = 256
TOKEN_TILE = 512
COMBINE_TILE = 256


def _vmem_limit(nbytes):
    return int(min(nbytes + COMPILER_SCRATCH_BYTES, V7X_VMEM_BYTES - COMPILER_SCRATCH_BYTES))


def _rms(x, g):
    return x * lax.rsqrt(jnp.mean(x * x, axis=-1, keepdims=True) + RMS_EPS) * g


def _to_token_tiles(ref, val):
    rows, width = val.shape
    assert width == V7X_SUBLANES * V7X_LANES and ref.shape == (rows * V7X_SUBLANES, V7X_LANES)
    for j in range(V7X_SUBLANES):
        ref[pl.ds(j, rows, stride=V7X_SUBLANES), :] = val[:, j * V7X_LANES:(j + 1) * V7X_LANES]


def _from_token_tiles(ref):
    rows = ref.shape[0] // V7X_SUBLANES
    return jnp.concatenate([ref[pl.ds(j, rows, stride=V7X_SUBLANES), :] for j in range(V7X_SUBLANES)], axis=1)


def _proj_kernel(x_ref, g_ref, w_ref, q_ref, k_ref, v_ref, u_ref, *kv_t, attn_width, q_scale):
    h = _rms(x_ref[...], g_ref[...])
    z = jnp.dot(h, w_ref[...], preferred_element_type=F32)
    a = attn_width
    q_ref[...] = z[:, :a] * q_scale
    k_ref[...] = z[:, a:2 * a]
    v_ref[...] = z[:, 2 * a:3 * a]
    u_ref[...] = z[:, 3 * a:]
    if kv_t:
        kt_ref, vt_ref = kv_t
        kt_ref[...] = z[:, a:2 * a].T
        vt_ref[...] = z[:, 2 * a:3 * a].T


def _proj(x2d, norm_g, w_in, attn_width, tm, seq=None):
    n, d = x2d.shape
    mix = w_in.shape[1]
    ssm_width = mix - 3 * attn_width
    row = lambda i: (i, 0)
    const = lambda i: (0, 0)
    out_shape = (jax.ShapeDtypeStruct((n, attn_width), F32),) * 3 + (
        jax.ShapeDtypeStruct((n, ssm_width), F32),)
    out_specs = (pl.BlockSpec((tm, attn_width), row),) * 3 + (pl.BlockSpec((tm, ssm_width), row),)
    if seq is not None:
        assert seq % tm == 0 and tm % V7X_LANES == 0
        per_seq = seq // tm
        out_shape += (jax.ShapeDtypeStruct((n // seq, attn_width, seq), F32),) * 2
        out_specs += (pl.BlockSpec((None, attn_width, tm), lambda i: (i // per_seq, 0, i % per_seq)),) * 2
    vmem = 2 * 4 * (tm * d + d * mix + tm * mix + 2 * tm * attn_width) + 4 * tm * mix
    return pl.pallas_call(
        functools.partial(_proj_kernel, attn_width=attn_width, q_scale=HEAD_DIM ** -0.5),
        grid=(n // tm,),
        in_specs=[pl.BlockSpec((tm, d), row), pl.BlockSpec((1, d), const),
                  pl.BlockSpec((d, mix), const)],
        out_specs=out_specs,
        out_shape=out_shape,
        compiler_params=pltpu.CompilerParams(dimension_semantics=("arbitrary",),
                                             vmem_limit_bytes=_vmem_limit(vmem)),
        name="proj",
    )(x2d, norm_g.reshape(1, d), w_in)


def _attn_prompt_kernel(q_ref, k_ref, v_ref, o_ref, *scratch, seq, unroll):
    nb = KEYS_BACK
    n_groups = len(DILATED_GROUPS)
    m_sc, l_sc, acc_sc = scratch[:n_groups], scratch[n_groups:2 * n_groups], scratch[2 * n_groups:]
    lane = lax.broadcasted_iota(I32, (1, V7X_LANES), 1)
    head_a = lane < HEAD_DIM
    qi = lax.broadcasted_iota(I32, (nb, 2 * nb), 0)
    kj = lax.broadcasted_iota(I32, (nb, 2 * nb), 1)
    rel = nb + qi - kj
    band = (rel >= 0) & (rel <= nb)

    def rows(ref, start, d):
        if d == 1:
            return ref[pl.ds(pl.multiple_of(start, nb), nb), :]
        return ref[pl.ds(start, nb, stride=d), :]

    def put(ref, start, d, val):
        if d == 1:
            ref[pl.ds(pl.multiple_of(start, nb), nb), :] = val
        else:
            ref[pl.ds(start, nb, stride=d), :] = val

    for gi, (w, d) in enumerate(DILATED_GROUPS):
        assert w // d == nb
        nblk = seq // (d * nb)

        def block(t, gi=gi, d=d, nblk=nblk):
            r = t // nblk
            n = t % nblk
            cur = r + d * nb * n
            prev = r + d * nb * jnp.maximum(n - 1, 0)
            q = rows(q_ref, cur, d)
            kk = jnp.concatenate([rows(k_ref, prev, d), rows(k_ref, cur, d)], axis=0)
            vv = jnp.concatenate([rows(v_ref, prev, d), rows(v_ref, cur, d)], axis=0)
            qs = jnp.concatenate([jnp.where(head_a, q, 0.0), jnp.where(head_a, 0.0, q)], axis=0)
            s = lax.dot_general(qs, kk, (((1,), (1,)), ((), ())), preferred_element_type=F32)
            valid = band & (kj >= jnp.where(n > 0, 0, nb))
            s = jnp.where(jnp.concatenate([valid, valid], axis=0), s, NEG_INF)
            m = jnp.max(s, axis=-1, keepdims=True)
            p = jnp.exp(s - m)
            l = jnp.sum(p, axis=-1, keepdims=True)
            o = jnp.dot(p, vv, preferred_element_type=F32)
            o = jnp.where(head_a, o[:nb], o[nb:])
            put(m_sc[gi], cur, d, jnp.where(head_a, m[:nb], m[nb:]))
            put(l_sc[gi], cur, d, jnp.where(head_a, l[:nb], l[nb:]))
            put(acc_sc[gi], cur, d, o)

        def blocks(tt, carry, block=block):
            for u in range(unroll):
                block(tt * unroll + u)
            return carry

        lax.fori_loop(0, seq // nb // unroll, blocks, 0)

    def merge(n, carry):
        sl = pl.ds(pl.multiple_of(n * nb, nb), nb)
        ms = [m[sl, :] for m in m_sc]
        m_all = functools.reduce(jnp.maximum, ms)
        cs = [jnp.exp(m - m_all) for m in ms]
        den = functools.reduce(jnp.add, [c * l[sl, :] for c, l in zip(cs, l_sc)])
        num = functools.reduce(jnp.add, [c * acc[sl, :] for c, acc in zip(cs, acc_sc)])
        o_ref[sl, :] = num / den
        return carry

    lax.fori_loop(0, seq // nb, merge, 0, unroll=2)


def _attn_prompt(q, k, v, unroll=16):
    b, s, a = q.shape
    n_scratch = 3 * len(DILATED_GROUPS)
    spec = pl.BlockSpec((None, s, V7X_LANES), lambda i, j: (i, 0, j))
    vmem = 4 * s * V7X_LANES * (2 * 4 + n_scratch)
    return pl.pallas_call(
        functools.partial(_attn_prompt_kernel, seq=s, unroll=unroll),
        grid=(b, a // V7X_LANES),
        in_specs=[spec, spec, spec],
        out_specs=spec,
        out_shape=jax.ShapeDtypeStruct((b, s, a), F32),
        scratch_shapes=[pltpu.VMEM((s, V7X_LANES), F32)] * n_scratch,
        compiler_params=pltpu.CompilerParams(dimension_semantics=("arbitrary", "arbitrary"),
                                             vmem_limit_bytes=_vmem_limit(vmem)),
        name="attn_prompt",
    )(q, k, v)


def _sample_key_counts(n_buf, n_new):
    cnt = np.zeros((n_new, n_buf + n_new), np.float32)
    for w, d in DILATED_GROUPS:
        for i in range(n_new):
            for j in range(w // d + 1):
                idx = n_buf + i - d * j
                if idx >= 0:
                    cnt[i, idx] += 1.0
    return cnt[:, :n_buf], cnt[:, n_buf:]


def _attn_sample_kernel(q_ref, kn_ref, vn_ref, kt_ref, vt_ref, cnt_ref, o_ref, *, n_heads, cnt_new):
    n_new, a = q_ref.shape
    rows = n_new * n_heads
    lane_head = lax.broadcasted_iota(I32, (rows, a), 1) // HEAD_DIM
    row_head = lax.broadcasted_iota(I32, (rows, a), 0) % n_heads
    row_query = lax.broadcasted_iota(I32, (rows, 1), 0) // n_heads
    own = lane_head == row_head
    q = q_ref[...]
    qb = jnp.where(own, jnp.broadcast_to(q[:, None, :], (n_new, n_heads, a)).reshape(rows, a), 0.0)
    s_buf = jnp.dot(qb, kt_ref[...], preferred_element_type=F32)
    cnt = cnt_ref[...]
    cnt = jnp.broadcast_to(cnt[:, None, :], (n_new, n_heads, cnt.shape[-1])).reshape(rows, -1)
    valid = cnt > 0.0
    m = jnp.max(jnp.where(valid, s_buf, NEG_INF), axis=-1, keepdims=True)
    s_new, c_new = [], []
    for j in range(n_new):
        c = functools.reduce(lambda acc, i: jnp.where(row_query == i, float(cnt_new[i, j]), acc),
                             range(n_new), jnp.zeros((rows, 1), F32))
        s = jnp.sum(qb * kn_ref[pl.ds(j, 1), :], axis=-1, keepdims=True)
        m = jnp.maximum(m, jnp.where(c > 0.0, s, NEG_INF))
        s_new.append(s)
        c_new.append(c)
    p = jnp.where(valid, cnt * jnp.exp(s_buf - m), 0.0)
    den = jnp.sum(p, axis=-1, keepdims=True)
    o = lax.dot_general(p, vt_ref[...], (((1,), (1,)), ((), ())), preferred_element_type=F32)
    for j in range(n_new):
        pj = jnp.where(c_new[j] > 0.0, c_new[j] * jnp.exp(s_new[j] - m), 0.0)
        den = den + pj
        o = o + pj * vn_ref[pl.ds(j, 1), :]
    o = jnp.where(own, o / den, 0.0)
    o_ref[...] = jnp.sum(o.reshape(n_new, n_heads, a), axis=1)


def _attn_sample(q, k_new, v_new, cache_kt, cache_vt):
    b, t, a = q.shape
    n_buf = cache_kt.shape[-1]
    n_heads = a // HEAD_DIM
    cnt_buf, cnt_new = _sample_key_counts(n_buf, t)
    new = pl.BlockSpec((None, t, a), lambda i: (i, 0, 0))
    buf = pl.BlockSpec((None, a, n_buf), lambda i: (i, 0, 0))
    vmem = 4 * (2 * 2 * n_buf * a + 8 * t * n_heads * n_buf)
    return pl.pallas_call(
        functools.partial(_attn_sample_kernel, n_heads=n_heads, cnt_new=cnt_new),
        grid=(b,),
        in_specs=[new, new, new, buf, buf, pl.BlockSpec((t, n_buf), lambda i: (0, 0))],
        out_specs=new,
        out_shape=jax.ShapeDtypeStruct((b, t, a), F32),
        compiler_params=pltpu.CompilerParams(dimension_semantics=("arbitrary",),
                                             vmem_limit_bytes=_vmem_limit(vmem)),
        name="attn_sample",
    )(q, k_new, v_new, cache_kt, cache_vt, jnp.asarray(cnt_buf))


def _block_toeplitz_kernel(br_ref, bi_ref, cr_ref, ci_ref, toep_ref, tt_ref, *, chunk):
    c = br_ref.shape[1]
    over_states = (((1,), (1,)), ((), ()))
    full = dict(preferred_element_type=F32, precision=lax.Precision.HIGHEST)
    for i in range(br_ref.shape[0]):
        k = (lax.dot_general(br_ref[i], cr_ref[i], over_states, **full)
             - lax.dot_general(bi_ref[i], ci_ref[i], over_states, **full))
        for s in range(chunk):
            shifted = k if s == 0 else jnp.concatenate(
                [jnp.zeros((c, s * c), F32), k[:, :(chunk - s) * c]], axis=1)
            toep_ref[i, pl.ds(s * c, c), :] = shifted
        tt_ref[i] = toep_ref[i].T


def _block_toeplitz(bbr, bbi, cpr, cpi, chunk):
    g, c, n = bbr.shape
    lc = cpr.shape[1]
    per_step = 4
    step = lambda i: (i, 0, 0)
    spec = pl.BlockSpec((per_step, lc, lc), step)
    return pl.pallas_call(
        functools.partial(_block_toeplitz_kernel, chunk=chunk),
        grid=(g // per_step,),
        in_specs=[pl.BlockSpec((per_step, c, n), step)] * 2 + [pl.BlockSpec((per_step, lc, n), step)] * 2,
        out_specs=(spec, spec),
        out_shape=(jax.ShapeDtypeStruct((g, lc, lc), F32),) * 2,
        compiler_params=pltpu.CompilerParams(dimension_semantics=("arbitrary",)),
        name="ssm_toeplitz",
    )(bbr, bbi, cpr, cpi)


def _pad_states(m, g):
    half = (np.arange(g) % 2)[:, None, None]
    z = jnp.zeros_like(m)
    return jnp.where(half == 0, jnp.concatenate([m, z], -1), jnp.concatenate([z, m], -1))


def _ssm_tables(a_re, a_im, log_dt, b_re, b_im, c_re, c_im, d_skip, chunk):
    g, n = a_re.shape
    c = b_re.shape[-1]
    lc = chunk * c
    dt = jnp.exp(log_dt)[:, None]
    x, y = a_re * dt, a_im * dt
    ex = jnp.exp(x)
    ar, ai = ex * jnp.cos(y), ex * jnp.sin(y)
    nr = jnp.expm1(x) * jnp.cos(y) - 2.0 * jnp.sin(0.5 * y) ** 2
    ni = ai
    den = a_re * a_re + a_im * a_im
    fr = (nr * a_re + ni * a_im) / den
    fi = (ni * a_re - nr * a_im) / den
    b_re_t, b_im_t = b_re.transpose(0, 2, 1), b_im.transpose(0, 2, 1)
    bbr = fr[:, None, :] * b_re_t - fi[:, None, :] * b_im_t
    bbi = fr[:, None, :] * b_im_t + fi[:, None, :] * b_re_t
    pr, pi = [jnp.ones_like(ar)], [jnp.zeros_like(ar)]
    for _ in range(chunk):
        pr.append(pr[-1] * ar - pi[-1] * ai)
        pi.append(pr[-2] * ai + pi[-1] * ar)
    pr, pi = jnp.stack(pr, 1), jnp.stack(pi, 1)
    cpr = c_re[:, None] * pr[:, :, None, :] - c_im[:, None] * pi[:, :, None, :]
    cpi = c_re[:, None] * pi[:, :, None, :] + c_im[:, None] * pr[:, :, None, :]
    toep, tt = _block_toeplitz(bbr, bbi, cpr[:, :chunk].reshape(g, lc, n), cpi[:, :chunk].reshape(g, lc, n), chunk)
    qr, qi = pr[:, chunk - 1::-1][:, :, None, :], pi[:, chunk - 1::-1][:, :, None, :]
    wr = (qr * bbr[:, None] - qi * bbi[:, None]).reshape(g, lc, n)
    wi = (qr * bbi[:, None] + qi * bbr[:, None]).reshape(g, lc, n)
    vr = cpr[:, 1:].reshape(g, lc, n)
    vi = cpi[:, 1:].reshape(g, lc, n)
    return dict(chunk=chunk, toep=toep, tt=tt, wr=wr, wi=wi, vr=vr, vi=vi, d=d_skip, pr=pr, pi=pi)


def _ssm_tables_lanes(tab):
    g, chunk = tab["d"].shape[0], tab["chunk"]
    tr = lambda m: m.transpose(0, 2, 1)
    return dict(tt=tab["tt"], wt_re=tr(tab["wr"]), wt_im=tr(tab["wi"]), v_re=tab["vr"], v_im=tab["vi"],
                dcol=jnp.tile(tab["d"], (1, chunk))[:, :, None],
                al_re=tab["pr"][:, chunk].reshape(1, -1), al_im=tab["pi"][:, chunk].reshape(1, -1))


def _ssm_tables_rows(tab, chunk):
    g, c = tab["d"].shape
    big, lc = tab["chunk"], chunk * c
    tr = lambda m: m.transpose(0, 2, 1)
    tail = slice((big - chunk) * c, big * c)
    return dict(toep=tab["toep"][:, :lc, :lc],
                wr=_pad_states(tab["wr"][:, tail], g), wi=_pad_states(tab["wi"][:, tail], g),
                vr=tr(_pad_states(tab["vr"][:, :lc], g)), vi=tr(_pad_states(-tab["vi"][:, :lc], g)),
                d=jnp.tile(tab["d"], (1, chunk))[:, None, :],
                al_re=tab["pr"][:, chunk].reshape(1, -1), al_im=tab["pi"][:, chunk].reshape(1, -1))


def _ssm_state_kernel(u_ref, wr_ref, wi_ref, er_ref, ei_ref):
    er = jnp.dot(u_ref[0], wr_ref[0], preferred_element_type=F32)
    ei = jnp.dot(u_ref[0], wi_ref[0], preferred_element_type=F32)
    er_ref[...] = er + jnp.dot(u_ref[1], wr_ref[1], preferred_element_type=F32)
    ei_ref[...] = ei + jnp.dot(u_ref[1], wi_ref[1], preferred_element_type=F32)


def _ssm_scan_kernel(er_ref, ei_ref, ar_ref, ai_ref, hr_ref, hi_ref, xr_ref, xi_ref, lr_ref, li_ref,
                     *, n_chunks, nb):
    shape = hr_ref.shape
    ar = jnp.broadcast_to(ar_ref[...], shape)
    ai = jnp.broadcast_to(ai_ref[...], shape)

    def step(j, carry):
        xr, xi = carry
        sl = pl.ds(pl.multiple_of(j * nb, nb), nb)
        xr_ref[sl, :] = xr
        xi_ref[sl, :] = xi
        return (ar * xr - ai * xi + er_ref[sl, :], ar * xi + ai * xr + ei_ref[sl, :])

    xr, xi = lax.fori_loop(0, n_chunks, step, (hr_ref[...], hi_ref[...]))
    lr_ref[...] = xr
    li_ref[...] = xi


def _ssm_out_kernel(u_ref, t_ref, d_ref, xr_ref, xi_ref, vr_ref, vi_ref, y_ref):
    xr = xr_ref[...]
    xi = xi_ref[...]
    for a in range(2):
        u = u_ref[a]
        y = jnp.dot(u, t_ref[a], preferred_element_type=F32) + d_ref[a] * u
        y = y + jnp.dot(xr, vr_ref[a], preferred_element_type=F32)
        y_ref[a] = y + jnp.dot(xi, vi_ref[a], preferred_element_type=F32)


def _ssm_scan(er, ei, tab, h0_re, h0_im, n_chunks):
    r, gn = er.shape
    b = r // n_chunks
    lanes = 4 * V7X_LANES
    col = lambda i: (0, i)
    return pl.pallas_call(
        functools.partial(_ssm_scan_kernel, n_chunks=n_chunks, nb=b),
        grid=(gn // lanes,),
        in_specs=[pl.BlockSpec((r, lanes), col), pl.BlockSpec((r, lanes), col),
                  pl.BlockSpec((1, lanes), col), pl.BlockSpec((1, lanes), col),
                  pl.BlockSpec((b, lanes), col), pl.BlockSpec((b, lanes), col)],
        out_specs=(pl.BlockSpec((r, lanes), col),) * 2 + (pl.BlockSpec((b, lanes), col),) * 2,
        out_shape=(jax.ShapeDtypeStruct((r, gn), F32),) * 2 + (jax.ShapeDtypeStruct((b, gn), F32),) * 2,
        compiler_params=pltpu.CompilerParams(dimension_semantics=("arbitrary",),
                                             vmem_limit_bytes=_vmem_limit(2 * 4 * 4 * r * lanes)),
        name="ssm_scan",
    )(er, ei, tab["al_re"], tab["al_im"], h0_re, h0_im)


def _ssm_state_t_kernel(a_ref, wr_ref, wi_ref, er_ref, ei_ref):
    def seq(b, carry):
        a0, a1 = a_ref[0, b], a_ref[1, b]
        er_ref[b] = jnp.concatenate([jnp.dot(wr_ref[0], a0, preferred_element_type=F32),
                                     jnp.dot(wr_ref[1], a1, preferred_element_type=F32)], axis=0)
        ei_ref[b] = jnp.concatenate([jnp.dot(wi_ref[0], a0, preferred_element_type=F32),
                                     jnp.dot(wi_ref[1], a1, preferred_element_type=F32)], axis=0)
        return carry

    lax.fori_loop(0, a_ref.shape[1], seq, 0, unroll=True)


def _ssm_out_t_kernel(a_ref, t_ref, d_ref, xr_ref, xi_ref, vr_ref, vi_ref, y_ref):
    def seq(b, carry):
        xr, xi = xr_ref[b], xi_ref[b]
        n = vr_ref.shape[-1]
        for a in range(2):
            at = a_ref[a, b]
            own = slice(a * n, (a + 1) * n)
            y = jnp.dot(t_ref[a], at, preferred_element_type=F32) + d_ref[a] * at
            y = y + jnp.dot(vr_ref[a], xr[own], preferred_element_type=F32)
            y_ref[a, b] = y - jnp.dot(vi_ref[a], xi[own], preferred_element_type=F32)
        return carry

    lax.fori_loop(0, a_ref.shape[1], seq, 0, unroll=True)


def _ssm_chunks_on_lanes(u, h0_re, h0_im, tab, chunk):
    b, s, width = u.shape
    g = width // SSM_GROUP
    n_chunks = s // chunk
    lc = chunk * SSM_GROUP
    assert n_chunks == V7X_LANES
    at = (u.reshape(b, n_chunks, chunk * width).transpose(0, 2, 1)
          .reshape(b, chunk, g, SSM_GROUP, n_chunks).transpose(2, 0, 1, 3, 4).reshape(g, b, lc, n_chunks))
    pair4 = lambda i: (i, 0, 0, 0)
    pair3 = lambda i: (i, 0, 0)
    cp = lambda vmem: pltpu.CompilerParams(dimension_semantics=("arbitrary",),
                                           vmem_limit_bytes=_vmem_limit(vmem))
    a_spec = pl.BlockSpec((2, b, lc, n_chunks), pair4)
    st_spec = pl.BlockSpec((None, b, V7X_LANES, n_chunks), pair4)
    w_spec = pl.BlockSpec((2, SSM_STATE, lc), pair3)
    v_spec = pl.BlockSpec((2, lc, SSM_STATE), pair3)
    a_bytes = 2 * b * lc * n_chunks
    st_bytes = b * V7X_LANES * n_chunks
    et_re, et_im = pl.pallas_call(
        _ssm_state_t_kernel,
        grid=(g // 2,),
        in_specs=[a_spec, w_spec, w_spec],
        out_specs=(st_spec,) * 2,
        out_shape=(jax.ShapeDtypeStruct((g // 2, b, V7X_LANES, n_chunks), F32),) * 2,
        compiler_params=cp(2 * 4 * (a_bytes + 2 * st_bytes + 4 * SSM_STATE * lc)),
        name="ssm_state_t",
    )(at, tab["wt_re"], tab["wt_im"])
    rows = lambda m: m.transpose(3, 1, 0, 2).reshape(n_chunks * b, g * SSM_STATE)
    xr, xi, lr, li = _ssm_scan(rows(et_re), rows(et_im), tab, h0_re, h0_im, n_chunks)
    cols = lambda m: m.reshape(n_chunks, b, g // 2, V7X_LANES).transpose(2, 1, 3, 0)
    yt = pl.pallas_call(
        _ssm_out_t_kernel,
        grid=(g // 2,),
        in_specs=[a_spec, pl.BlockSpec((2, lc, lc), pair3), pl.BlockSpec((2, lc, 1), pair3),
                  st_spec, st_spec, v_spec, v_spec],
        out_specs=a_spec,
        out_shape=jax.ShapeDtypeStruct((g, b, lc, n_chunks), F32),
        compiler_params=cp(2 * 4 * (2 * a_bytes + 2 * st_bytes + 2 * lc * lc + 6 * SSM_STATE * lc)),
        name="ssm_out_t",
    )(at, tab["tt"], tab["dcol"], cols(xr), cols(xi), tab["v_re"], tab["v_im"])
    y = (yt.reshape(g, b, chunk, SSM_GROUP, n_chunks).transpose(1, 2, 0, 3, 4)
         .reshape(b, chunk * width, n_chunks).transpose(0, 2, 1).reshape(b, s, width))
    return y, lr, li


def _ssm(u, h0_re, h0_im, tab, chunk):
    b, s, width = u.shape
    g = width // SSM_GROUP
    n_chunks = s // chunk
    lc = chunk * SSM_GROUP
    r = n_chunks * b
    gn = g * SSM_STATE
    ut = u.reshape(b, n_chunks, chunk, g, SSM_GROUP).transpose(3, 1, 0, 2, 4).reshape(g, r, lc)
    pair = lambda i: (i, 0, 0)
    cp = lambda vmem: pltpu.CompilerParams(dimension_semantics=("arbitrary",),
                                           vmem_limit_bytes=_vmem_limit(vmem))
    er, ei = pl.pallas_call(
        _ssm_state_kernel,
        grid=(g // 2,),
        in_specs=[pl.BlockSpec((2, r, lc), pair), pl.BlockSpec((2, lc, V7X_LANES), pair),
                  pl.BlockSpec((2, lc, V7X_LANES), pair)],
        out_specs=(pl.BlockSpec((r, V7X_LANES), lambda i: (0, i)),) * 2,
        out_shape=(jax.ShapeDtypeStruct((r, gn), F32),) * 2,
        compiler_params=cp(2 * 4 * (2 * r * lc + 4 * lc * V7X_LANES + 2 * r * V7X_LANES)),
        name="ssm_state",
    )(ut, tab["wr"], tab["wi"])
    xr, xi, lr, li = _ssm_scan(er, ei, tab, h0_re, h0_im, n_chunks)
    y = pl.pallas_call(
        _ssm_out_kernel,
        grid=(g // 2,),
        in_specs=[pl.BlockSpec((2, r, lc), pair), pl.BlockSpec((2, lc, lc), pair),
                  pl.BlockSpec((2, 1, lc), pair),
                  pl.BlockSpec((r, V7X_LANES), lambda i: (0, i)),
                  pl.BlockSpec((r, V7X_LANES), lambda i: (0, i)),
                  pl.BlockSpec((2, V7X_LANES, lc), pair), pl.BlockSpec((2, V7X_LANES, lc), pair)],
        out_specs=pl.BlockSpec((2, r, lc), pair),
        out_shape=jax.ShapeDtypeStruct((g, r, lc), F32),
        compiler_params=cp(2 * 4 * (4 * r * lc + 2 * lc * lc + 2 * r * V7X_LANES + 4 * V7X_LANES * lc)),
        name="ssm_out",
    )(ut, tab["toep"], tab["d"], xr, xi, tab["vr"], tab["vi"])
    y = y.reshape(g, n_chunks, b, chunk, SSM_GROUP).transpose(2, 1, 3, 0, 4).reshape(b, s, width)
    return y, lr, li


def _finish_kernel(x_ref, attn_ref, ssm_ref, ga_ref, wg_ref, bg_ref, gs_ref, wo_ref, gf_ref,
                   wr_ref, br_ref, base_ref, x1_ref, hf_ref, ti_ref, tg_ref, cnt_ref, seen):
    a = attn_ref.shape[-1]
    tm = x_ref.shape[0]

    @pl.when(pl.program_id(0) == 0)
    def _():
        seen[...] = base_ref[...]

    g = jax.nn.gelu(ssm_ref[...])
    gate = jax.nn.sigmoid(jnp.dot(g, wg_ref[...], preferred_element_type=F32) + bg_ref[...])
    mixed_a = _rms(attn_ref[...], ga_ref[...])
    mixed_s = _rms(g * gate, gs_ref[...])
    x1 = (x_ref[...] + jnp.dot(mixed_a, wo_ref[pl.ds(0, a), :], preferred_element_type=F32)
          + jnp.dot(mixed_s, wo_ref[pl.ds(a, wo_ref.shape[0] - a), :], preferred_element_type=F32))
    x1_ref[...] = x1
    hf = _rms(x1, gf_ref[...])
    _to_token_tiles(hf_ref, hf)
    hf_hi = hf.astype(jnp.bfloat16).astype(F32)
    hf_lo = hf - hf_hi
    small = (jnp.dot(hf_lo, wr_ref[1], preferred_element_type=F32)
             + jnp.dot(hf_hi, wr_ref[1], preferred_element_type=F32)
             + jnp.dot(hf_lo, wr_ref[0], preferred_element_type=F32))
    logits = jnp.dot(hf_hi, wr_ref[0], preferred_element_type=F32) + small + br_ref[...]
    lane = lax.broadcasted_iota(I32, logits.shape, 1)
    work = logits
    vals, idxs = [], []
    for _ in range(TOP_K):
        m = jnp.max(work, axis=-1, keepdims=True)
        idx = jnp.min(jnp.where(work == m, lane, V7X_LANES), axis=-1, keepdims=True)
        vals.append(m)
        idxs.append(idx)
        work = jnp.where(lane == idx, -jnp.inf, work)
    exps = [jnp.exp(v - vals[0]) for v in vals]
    den = exps[0]
    for e in exps[1:]:
        den = den + e
    picked = jnp.zeros(logits.shape, F32)
    for k in range(TOP_K):
        picked = picked + jnp.where(lane == idxs[k], 1.0, 0.0)
    tri = jnp.where(lax.broadcasted_iota(I32, (tm, tm), 0) > lax.broadcasted_iota(I32, (tm, tm), 1), 1.0, 0.0)
    before = jnp.dot(tri, picked, preferred_element_type=F32) + seen[...]
    ti = jnp.zeros(logits.shape, I32)
    tg = jnp.zeros(logits.shape, F32)
    for k in range(TOP_K):
        rank = jnp.sum(jnp.where(lane == idxs[k], before, 0.0), axis=-1, keepdims=True)
        ti = jnp.where(lane == k, idxs[k], ti)
        ti = jnp.where(lane == TOP_K + k, rank.astype(I32), ti)
        tg = jnp.where(lane == k, exps[k] / den, tg)
    ti_ref[...] = ti
    tg_ref[...] = tg
    seen[...] = seen[...] + jnp.sum(picked, axis=0, keepdims=True)
    cnt_ref[...] = seen[...]


def _finish(x2d, attn, ssm_y, p, seen, tm):
    n, d = x2d.shape
    a = attn.shape[1]
    sw = ssm_y.shape[1]
    ne = p["w_router"].shape[1]
    w_router = jnp.pad(p["w_router"], ((0, 0), (0, V7X_LANES - ne)))
    w_hi = w_router.astype(jnp.bfloat16).astype(F32)
    w_router = jnp.stack([w_hi, w_router - w_hi])
    b_router = jnp.pad(p["b_router"], (0, V7X_LANES - ne), constant_values=-jnp.inf).reshape(1, V7X_LANES)
    row = lambda i: (i, 0)
    full = lambda arr: pl.BlockSpec(arr.shape, lambda i: (0,) * arr.ndim)
    ins = [x2d, attn, ssm_y, p["attn_out_norm_g"].reshape(1, a), p["w_glu"], p["b_glu"].reshape(1, sw),
           p["ssm_out_norm_g"].reshape(1, sw), p["w_out"], p["norm_ffn_g"].reshape(1, d),
           w_router, b_router, seen]
    in_specs = [pl.BlockSpec((tm, d), row), pl.BlockSpec((tm, a), row), pl.BlockSpec((tm, sw), row)]
    in_specs += [full(arr) for arr in ins[3:]]
    vmem = 2 * 4 * (tm * (3 * d + a + sw + 2 * V7X_LANES) + sw * sw + d * d + 2 * d * V7X_LANES) + 16 * tm * d
    return pl.pallas_call(
        _finish_kernel,
        grid=(n // tm,),
        in_specs=in_specs,
        out_specs=(pl.BlockSpec((tm, d), row), pl.BlockSpec((tm * V7X_SUBLANES, V7X_LANES), row),
                   pl.BlockSpec((tm, V7X_LANES), row), pl.BlockSpec((tm, V7X_LANES), row),
                   pl.BlockSpec((1, V7X_LANES), lambda i: (0, 0))),
        out_shape=(jax.ShapeDtypeStruct((n, d), F32), jax.ShapeDtypeStruct((n * V7X_SUBLANES, V7X_LANES), F32),
                   jax.ShapeDtypeStruct((n, V7X_LANES), I32), jax.ShapeDtypeStruct((n, V7X_LANES), F32),
                   jax.ShapeDtypeStruct((1, V7X_LANES), F32)),
        scratch_shapes=[pltpu.VMEM((1, V7X_LANES), F32)],
        compiler_params=pltpu.CompilerParams(dimension_semantics=("arbitrary",),
                                             vmem_limit_bytes=_vmem_limit(vmem)),
        name="finish",
    )(*ins)


def _tile_copy(src_ref, src_row, dst_ref, dst_row, sem):
    return pltpu.make_async_copy(src_ref.at[src_row], dst_ref.at[dst_row], sem)


def _dispatch_kernel(dest_ref, hp_ref, hs_ref, xs_ref, sem, *, n_tiles_p):
    def scatter(hf_ref):
        tm = hf_ref.shape[0]

        def tokens(i, carry):
            for u in range(V7X_SUBLANES):
                r = i * V7X_SUBLANES + u
                for k in range(TOP_K):
                    _tile_copy(hf_ref, r, xs_ref, dest_ref[0, 0, r * TOP_K + k], sem).start(priority=k % 2)
            return carry

        lax.fori_loop(0, tm // V7X_SUBLANES, tokens, 0)
        rows = pl.ds(0, tm * TOP_K)
        pltpu.make_async_copy(xs_ref.at[rows], xs_ref.at[rows], sem).wait()

    @pl.when(pl.program_id(0) < n_tiles_p)
    def _():
        scatter(hp_ref)

    @pl.when(pl.program_id(0) == n_tiles_p)
    def _():
        scatter(hs_ref)


def _dispatch(hf_p, dest_p, hf_s, dest_s, tm):
    tile = (V7X_SUBLANES, V7X_LANES)
    hf_p, hf_s = hf_p.reshape(-1, *tile), hf_s.reshape(-1, *tile)
    n_p, n_s = hf_p.shape[0], hf_s.shape[0]
    assert n_p % tm == 0 and n_s <= tm
    n_tiles_p = n_p // tm
    dest3 = jnp.concatenate([dest_p.reshape(n_tiles_p, tm * TOP_K),
                             jnp.pad(dest_s.reshape(1, n_s * TOP_K), ((0, 0), (0, (tm - n_s) * TOP_K)))])
    dest3 = dest3.reshape(n_tiles_p + 1, 1, tm * TOP_K)
    return pl.pallas_call(
        functools.partial(_dispatch_kernel, n_tiles_p=n_tiles_p),
        grid=(n_tiles_p + 1,),
        in_specs=[pl.BlockSpec((1, 1, tm * TOP_K), lambda i: (i, 0, 0), memory_space=pltpu.SMEM),
                  pl.BlockSpec((tm, *tile), lambda i: (jnp.minimum(i, n_tiles_p - 1), 0, 0)),
                  pl.BlockSpec((n_s, *tile), lambda i: (0, 0, 0))],
        out_specs=pl.BlockSpec(memory_space=pl.ANY),
        out_shape=jax.ShapeDtypeStruct(((n_p + n_s) * TOP_K, *tile), F32),
        scratch_shapes=[pltpu.SemaphoreType.DMA(())],
        compiler_params=pltpu.CompilerParams(dimension_semantics=("arbitrary",)),
        name="dispatch",
    )(dest3, hf_p, hf_s)


def _moe_kernel(vt_ref, ve_ref, nx_ref, lo_ref, hi_ref, x_ref, w1_ref, b1_ref, w2_ref, b2_ref, y_ref,
                w1s, w2s, w1b, w2b, sem, *, d_ff):
    v = pl.program_id(0)
    lo, hi = lo_ref[v], hi_ref[v]
    prev = jnp.maximum(v - 1, 0)
    first = jnp.logical_or(v == 0, vt_ref[v] != vt_ref[prev])

    def fetch(e):
        return (pltpu.make_async_copy(w1_ref.at[e], w1s, sem.at[0]),
                pltpu.make_async_copy(w2_ref.at[e], w2s, sem.at[1]))

    @pl.when(v == 0)
    def _():
        for cp in fetch(ve_ref[0]):
            cp.start()

    @pl.when(jnp.logical_or(v == 0, ve_ref[v] != ve_ref[prev]))
    def _():
        for cp in fetch(ve_ref[v]):
            cp.wait()
        w1b[...] = w1s[...].astype(jnp.bfloat16)
        w2b[...] = w2s[...].astype(jnp.bfloat16)

        @pl.when(nx_ref[v] >= 0)
        def _():
            for cp in fetch(nx_ref[v]):
                cp.start()

    def swiglu(x_tiles):
        x = _from_token_tiles(x_tiles).astype(jnp.bfloat16)
        a = jnp.dot(x, w1b[...], preferred_element_type=F32) + b1_ref[0]
        x_glu = jnp.minimum(a[:, :d_ff], SWIGLU_LIMIT)
        x_lin = jnp.clip(a[:, d_ff:], -SWIGLU_LIMIT, SWIGLU_LIMIT)
        mid = x_glu * jax.nn.sigmoid(SWIGLU_ALPHA * x_glu) * (x_lin + 1.0)
        return jnp.dot(mid.astype(jnp.bfloat16), w2b[...], preferred_element_type=F32) + b2_ref[0]

    whole = jnp.logical_and(lo == 0, hi == MOE_TILE)

    @pl.when(whole)
    def _():
        _to_token_tiles(y_ref, swiglu(x_ref))

    for r0 in range(0, MOE_TILE, MOE_SUB):
        sub = pl.ds(r0 * V7X_SUBLANES, MOE_SUB * V7X_SUBLANES)
        x_sub, y_sub = x_ref.at[sub, :], y_ref.at[sub, :]
        touched = jnp.logical_and(jnp.logical_not(whole), jnp.logical_and(lo < r0 + MOE_SUB, hi > r0))

        @pl.when(touched)
        def _(r0=r0, x_sub=x_sub, y_sub=y_sub):
            y = swiglu(x_sub)
            rows = r0 + lax.broadcasted_iota(I32, (MOE_SUB, 1), 0)
            mine = (rows >= lo) & (rows < hi)

            @pl.when(first)
            def _():
                _to_token_tiles(y_sub, jnp.where(mine, y, 0.0))

            @pl.when(jnp.logical_not(first))
            def _():
                _to_token_tiles(y_sub, jnp.where(mine, y, _from_token_tiles(y_sub)))

        @pl.when(jnp.logical_and(first, jnp.logical_not(jnp.logical_or(touched, whole))))
        def _(y_sub=y_sub):
            y_sub[...] = jnp.zeros(y_sub.shape, F32)


def _moe_experts(xs, visits, w1, b1, w2, b2):
    ne, d, ff2 = w1.shape
    d_ff = ff2 // 2
    n_visits = visits[0].shape[0]
    row = lambda v, vt, ve, nx, lo, hi: (vt[v], 0)
    exp = lambda v, vt, ve, nx, lo, hi: (ve[v], 0, 0)
    vmem = (4 * 2 * 2 * MOE_TILE * d + (4 + 2) * (d * ff2 + d_ff * d) + 2 * 4 * (ff2 + d)
            + 4 * 4 * MOE_TILE * ff2)
    tiles = pl.BlockSpec((MOE_TILE * V7X_SUBLANES, V7X_LANES), row)
    grid_spec = pltpu.PrefetchScalarGridSpec(
        num_scalar_prefetch=5,
        grid=(n_visits,),
        in_specs=[tiles,
                  pl.BlockSpec(memory_space=pl.ANY), pl.BlockSpec((1, 1, ff2), exp),
                  pl.BlockSpec(memory_space=pl.ANY), pl.BlockSpec((1, 1, d), exp)],
        out_specs=tiles,
        scratch_shapes=[pltpu.VMEM((d, ff2), F32), pltpu.VMEM((d_ff, d), F32),
                        pltpu.VMEM((d, ff2), jnp.bfloat16), pltpu.VMEM((d_ff, d), jnp.bfloat16),
                        pltpu.SemaphoreType.DMA((2,))],
    )
    return pl.pallas_call(
        functools.partial(_moe_kernel, d_ff=d_ff),
        grid_spec=grid_spec,
        out_shape=jax.ShapeDtypeStruct(xs.shape, F32),
        compiler_params=pltpu.CompilerParams(dimension_semantics=("arbitrary",),
                                             vmem_limit_bytes=_vmem_limit(vmem)),
        name="moe",
    )(*visits, xs, w1, b1.reshape(ne, 1, ff2), w2, b2.reshape(ne, 1, d))


def _moe_visits(sizes, n_rows):
    n_tiles = n_rows // MOE_TILE
    n_visits = n_tiles + N_EXPERTS - 1
    ends = jnp.cumsum(sizes)
    starts = ends - sizes
    first_tile = starts // MOE_TILE
    n_vis = jnp.where(sizes > 0, (ends - 1) // MOE_TILE - first_tile + 1, 0)
    vis_end = jnp.cumsum(n_vis)
    total = vis_end[-1]
    v = jnp.clip(jnp.arange(n_visits, dtype=I32), 0, jnp.maximum(total - 1, 0))
    e = jnp.sum((vis_end[None, :] <= v[:, None]).astype(I32), axis=1)
    pick = lambda tab: jnp.sum(jnp.where(e[:, None] == jnp.arange(N_EXPERTS)[None, :], tab[None, :], 0), axis=1)
    tile = pick(first_tile) + v - pick(vis_end - n_vis)
    lo = jnp.maximum(pick(starts), tile * MOE_TILE) - tile * MOE_TILE
    hi = jnp.minimum(pick(ends), (tile + 1) * MOE_TILE) - tile * MOE_TILE
    live = jnp.arange(n_visits) < total
    ids = jnp.arange(N_EXPERTS, dtype=I32)
    later = jnp.where((ids[None, :] > ids[:, None]) & (sizes[None, :] > 0), ids[None, :], N_EXPERTS)
    nxt = jnp.min(later, axis=1)
    nxt = pick(jnp.where(nxt < N_EXPERTS, nxt, -1))
    return (tile.astype(I32), e.astype(I32), nxt.astype(I32), jnp.where(live, lo, 0).astype(I32),
            jnp.where(live, hi, 0).astype(I32))


def _final_kernel(dest_ref, nxt_ref, x_ref, tg_ref, g_ref, ys_ref, y_ref, buf, sem, *, tm):
    i = pl.program_id(0)
    n = pl.num_programs(0)

    def gather(d_ref, slot):
        def tokens(i8, carry):
            for u in range(V7X_SUBLANES):
                r = i8 * V7X_SUBLANES + u
                tile = pl.ds(pl.multiple_of(r * V7X_SUBLANES, V7X_SUBLANES), V7X_SUBLANES)
                for k in range(TOP_K):
                    pltpu.make_async_copy(ys_ref.at[d_ref[0, 0, r * TOP_K + k]], buf.at[slot, k, tile, :],
                                          sem.at[slot]).start(priority=k % 2)
            return carry
        lax.fori_loop(0, tm // V7X_SUBLANES, tokens, 0)

    @pl.when(i == 0)
    def _():
        gather(dest_ref, 0)

    for slot in range(2):
        @pl.when(jnp.logical_and(i + 1 < n, (i + 1) % 2 == slot))
        def _(slot=slot):
            gather(nxt_ref, slot)

    for slot in range(2):
        @pl.when(i % 2 == slot)
        def _(slot=slot):
            pltpu.make_async_copy(buf.at[slot], buf.at[slot], sem.at[slot]).wait()
            tg = tg_ref[...]
            parts = []
            for j in range(V7X_SUBLANES):
                f = None
                for k in range(TOP_K):
                    rows = buf[slot, k, pl.ds(j, tm, stride=V7X_SUBLANES), :]
                    f = tg[:, k:k + 1] * rows if f is None else f + tg[:, k:k + 1] * rows
                parts.append(f)
            y_ref[...] = _rms(x_ref[...] + jnp.concatenate(parts, axis=1), g_ref[...])


def _final(x1, tg, dest, ys, g, tm):
    n, d = x1.shape
    n_tiles = n // tm
    tile = (V7X_SUBLANES, V7X_LANES)
    dest3 = dest.reshape(n_tiles, 1, tm * TOP_K)
    row = lambda i: (i, 0)
    smem = lambda index_map: pl.BlockSpec((1, 1, tm * TOP_K), index_map, memory_space=pltpu.SMEM)
    vmem = 4 * (2 * TOP_K * tm * d + 2 * 2 * tm * d + 2 * tm * V7X_LANES + 4 * tm * d)
    return pl.pallas_call(
        functools.partial(_final_kernel, tm=tm),
        grid=(n_tiles,),
        in_specs=[smem(lambda i: (i, 0, 0)), smem(lambda i: (jnp.minimum(i + 1, n_tiles - 1), 0, 0)),
                  pl.BlockSpec((tm, d), row), pl.BlockSpec((tm, V7X_LANES), row),
                  pl.BlockSpec((1, d), lambda i: (0, 0)), pl.BlockSpec(memory_space=pl.ANY)],
        out_specs=pl.BlockSpec((tm, d), row),
        out_shape=jax.ShapeDtypeStruct((n, d), F32),
        scratch_shapes=[pltpu.VMEM((2, TOP_K, tm * V7X_SUBLANES, V7X_LANES), F32), pltpu.SemaphoreType.DMA((2,))],
        compiler_params=pltpu.CompilerParams(dimension_semantics=("arbitrary",),
                                             vmem_limit_bytes=_vmem_limit(vmem)),
        name="final",
    )(dest3, dest3, x1, tg, g.reshape(1, d), ys.reshape(-1, *tile))


def kernel(x_prompt, x_sample, cache_k, cache_v, state_ssm_re, state_ssm_im, norm_mix_g, w_in, attn_out_norm_g, ssm_a_re, ssm_a_im, ssm_log_dt, ssm_b_re, ssm_b_im, ssm_c_re, ssm_c_im, ssm_d, w_glu, b_glu, ssm_out_norm_g, w_out, norm_ffn_g, w_router, b_router, w_moe1, b_moe1, w_moe2, b_moe2, norm_final_g):
    depth = w_in.shape[0]
    assert depth == 1, "single-layer trunk"
    bp, s, d = x_prompt.shape
    bs, t, _ = x_sample.shape
    n_buf = cache_k.shape[2]
    n_heads, head_dim = cache_k.shape[3], cache_k.shape[4]
    assert head_dim == HEAD_DIM
    a = n_heads * head_dim
    g = ssm_a_re.shape[1]
    gn = g * SSM_STATE
    assert s % (max(dd for _, dd in DILATED_GROUPS) * KEYS_BACK) == 0
    np_, ns = bp * s, bs * t
    tm_p, tm_s = TOKEN_TILE, ns
    l = 0
    ssm_p = (ssm_a_re[l], ssm_a_im[l], ssm_log_dt[l], ssm_b_re[l], ssm_b_im[l], ssm_c_re[l], ssm_c_im[l],
             ssm_d[l])
    tail = dict(attn_out_norm_g=attn_out_norm_g[l], w_glu=w_glu[l], b_glu=b_glu[l],
                ssm_out_norm_g=ssm_out_norm_g[l], w_out=w_out[l], norm_ffn_g=norm_ffn_g[l],
                w_router=w_router[l], b_router=b_router[l])

    xp = x_prompt.reshape(np_, d)
    qp, kp, vp, up, ktp, vtp = _proj(xp, norm_mix_g[l], w_in[l], a, tm_p, seq=s)
    attn_p = _attn_prompt(qp.reshape(bp, s, a), kp.reshape(bp, s, a), vp.reshape(bp, s, a))
    zeros = jnp.zeros((bp, gn), F32)
    tables = _ssm_tables(*ssm_p, PROMPT_CHUNK)
    y_p, hp_re, hp_im = _ssm_chunks_on_lanes(up.reshape(bp, s, -1), zeros, zeros, _ssm_tables_lanes(tables),
                                             PROMPT_CHUNK)
    seen = jnp.zeros((1, V7X_LANES), F32)
    x1_p, hf_p, ti_p, tg_p, seen = _finish(xp, attn_p.reshape(np_, a), y_p.reshape(np_, -1), tail, seen, tm_p)

    xs = x_sample.reshape(ns, d)
    qs, ks, vs, us = _proj(xs, norm_mix_g[l], w_in[l], a, tm_s)
    by_dim = lambda c: c.transpose(0, 2, 3, 1).reshape(bs, a, n_buf)
    attn_s = _attn_sample(qs.reshape(bs, t, a), ks.reshape(bs, t, a), vs.reshape(bs, t, a),
                          by_dim(cache_k[l]), by_dim(cache_v[l]))
    y_s, hs_re, hs_im = _ssm(us.reshape(bs, t, -1), state_ssm_re[l].reshape(bs, gn),
                             state_ssm_im[l].reshape(bs, gn), _ssm_tables_rows(tables, t), t)
    x1_s, hf_s, ti_s, tg_s, seen = _finish(xs, attn_s.reshape(ns, a), y_s.reshape(ns, -1), tail, seen, tm_s)

    sizes = seen[0, :N_EXPERTS].astype(I32)
    starts = jnp.cumsum(sizes) - sizes
    experts = jnp.arange(N_EXPERTS, dtype=I32)[None, None, :]

    def sorted_row(ti):
        hit = ti[:, :TOP_K, None] == experts
        return jnp.sum(jnp.where(hit, starts[None, None, :], 0), axis=-1) + ti[:, TOP_K:2 * TOP_K]

    dest_p, dest_s = sorted_row(ti_p), sorted_row(ti_s)
    n_rows = (np_ + ns) * TOP_K
    assert n_rows % MOE_TILE == 0
    xs_sorted = _dispatch(hf_p, dest_p, hf_s, dest_s, tm_p).reshape(-1, V7X_LANES)
    ys = _moe_experts(xs_sorted, _moe_visits(sizes, n_rows), w_moe1[l], b_moe1[l], w_moe2[l], b_moe2[l])
    y_prompt = _final(x1_p, tg_p, dest_p, ys, norm_final_g, COMBINE_TILE).reshape(bp, s, d)
    y_sample = _final(x1_s, tg_s, dest_s, ys, norm_final_g, tm_s).reshape(bs, t, d)

    keep = min(max(w for w, _ in DILATED_GROUPS), s)
    k5 = lambda z, b_, s_: z.reshape(1, b_, s_, n_heads, head_dim)
    by_pos = lambda zt: zt.reshape(bp, n_heads, head_dim, s).transpose(0, 3, 1, 2)[None]
    st = lambda z, b_: z.reshape(1, b_, g, SSM_STATE)
    return (y_prompt, y_sample,
            by_pos(ktp)[:, :, s - keep:], by_pos(vtp)[:, :, s - keep:], st(hp_re, bp), st(hp_im, bp),
            k5(ks, bs, t), k5(vs, bs, t), st(hs_re, bs), st(hs_im, bs))
```

```python
import functools

import numpy as np
import jax
import jax.numpy as jnp
from jax import lax
from jax.experimental import pallas as pl
from jax.experimental.pallas import tpu as pltpu

F32 = jnp.float32
I32 = jnp.int32

V7X_LANES = 128
V7X_SUBLANES = 8
V7X_VMEM_BYTES = 64 * 1024 * 1024
COMPILER_SCRATCH_BYTES = 8 * 1024 * 1024

HEAD_DIM = 64
DILATED_GROUPS = ((128, 1), (512, 4), (2048, 16))
KEYS_BACK = 128
SSM_GROUP = 16
SSM_STATE = 64
N_EXPERTS = 32
TOP_K = 4
SWIGLU_LIMIT = 7.0
SWIGLU_ALPHA = 1.702
RMS_EPS = 1e-5
NEG_INF = -1e30
PROMPT_CHUNK = 16
MOE_TILE = 512
MOE_SUB = 256
TOKEN_TILE = 512
DISPATCH_TILE = 2048
COMBINE_TILE = 256


def _vmem_limit(nbytes):
    return int(min(nbytes + COMPILER_SCRATCH_BYTES, V7X_VMEM_BYTES - COMPILER_SCRATCH_BYTES))


def _rms(x, g):
    return x * lax.rsqrt(jnp.mean(x * x, axis=-1, keepdims=True) + RMS_EPS) * g


def _to_token_tiles(ref, val):
    rows, width = val.shape
    assert width == V7X_SUBLANES * V7X_LANES and ref.shape == (rows * V7X_SUBLANES, V7X_LANES)
    for j in range(V7X_SUBLANES):
        ref[pl.ds(j, rows, stride=V7X_SUBLANES), :] = val[:, j * V7X_LANES:(j + 1) * V7X_LANES]


def _from_token_tiles(ref):
    rows = ref.shape[0] // V7X_SUBLANES
    return jnp.concatenate([ref[pl.ds(j, rows, stride=V7X_SUBLANES), :] for j in range(V7X_SUBLANES)], axis=1)


def _proj_kernel(x_ref, g_ref, w_ref, q_ref, k_ref, v_ref, u_ref, *kv_t, attn_width, q_scale):
    h = _rms(x_ref[...], g_ref[...])
    z = jnp.dot(h, w_ref[...], preferred_element_type=F32)
    a = attn_width
    q_ref[...] = z[:, :a] * q_scale
    k_ref[...] = z[:, a:2 * a]
    v_ref[...] = z[:, 2 * a:3 * a]
    u_ref[...] = z[:, 3 * a:]
    if kv_t:
        kt_ref, vt_ref = kv_t
        kt_ref[...] = z[:, a:2 * a].T
        vt_ref[...] = z[:, 2 * a:3 * a].T


def _proj(x2d, norm_g, w_in, attn_width, tm, seq=None):
    n, d = x2d.shape
    mix = w_in.shape[1]
    ssm_width = mix - 3 * attn_width
    row = lambda i: (i, 0)
    const = lambda i: (0, 0)
    out_shape = (jax.ShapeDtypeStruct((n, attn_width), F32),) * 3 + (
        jax.ShapeDtypeStruct((n, ssm_width), F32),)
    out_specs = (pl.BlockSpec((tm, attn_width), row),) * 3 + (pl.BlockSpec((tm, ssm_width), row),)
    if seq is not None:
        assert seq % tm == 0 and tm % V7X_LANES == 0
        per_seq = seq // tm
        out_shape += (jax.ShapeDtypeStruct((n // seq, attn_width, seq), F32),) * 2
        out_specs += (pl.BlockSpec((None, attn_width, tm), lambda i: (i // per_seq, 0, i % per_seq)),) * 2
    vmem = 2 * 4 * (tm * d + d * mix + tm * mix + 2 * tm * attn_width) + 4 * tm * mix
    return pl.pallas_call(
        functools.partial(_proj_kernel, attn_width=attn_width, q_scale=HEAD_DIM ** -0.5),
        grid=(n // tm,),
        in_specs=[pl.BlockSpec((tm, d), row), pl.BlockSpec((1, d), const),
                  pl.BlockSpec((d, mix), const)],
        out_specs=out_specs,
        out_shape=out_shape,
        compiler_params=pltpu.CompilerParams(dimension_semantics=("arbitrary",),
                                             vmem_limit_bytes=_vmem_limit(vmem)),
        name="proj",
    )(x2d, norm_g.reshape(1, d), w_in)


def _attn_prompt_kernel(q_ref, k_ref, v_ref, o_ref, *scratch, seq, unroll):
    nb = KEYS_BACK
    n_groups = len(DILATED_GROUPS)
    m_sc, l_sc, acc_sc = scratch[:n_groups], scratch[n_groups:2 * n_groups], scratch[2 * n_groups:]
    lane = lax.broadcasted_iota(I32, (1, V7X_LANES), 1)
    head_a = lane < HEAD_DIM
    qi = lax.broadcasted_iota(I32, (nb, 2 * nb), 0)
    kj = lax.broadcasted_iota(I32, (nb, 2 * nb), 1)
    rel = nb + qi - kj
    band = (rel >= 0) & (rel <= nb)

    def rows(ref, start, d):
        if d == 1:
            return ref[pl.ds(pl.multiple_of(start, nb), nb), :]
        return ref[pl.ds(start, nb, stride=d), :]

    def put(ref, start, d, val):
        if d == 1:
            ref[pl.ds(pl.multiple_of(start, nb), nb), :] = val
        else:
            ref[pl.ds(start, nb, stride=d), :] = val

    for gi, (w, d) in enumerate(DILATED_GROUPS):
        assert w // d == nb
        nblk = seq // (d * nb)

        def block(t, gi=gi, d=d, nblk=nblk):
            r = t // nblk
            n = t % nblk
            cur = r + d * nb * n
            prev = r + d * nb * jnp.maximum(n - 1, 0)
            q = rows(q_ref, cur, d)
            kk = jnp.concatenate([rows(k_ref, prev, d), rows(k_ref, cur, d)], axis=0)
            vv = jnp.concatenate([rows(v_ref, prev, d), rows(v_ref, cur, d)], axis=0)
            qs = jnp.concatenate([jnp.where(head_a, q, 0.0), jnp.where(head_a, 0.0, q)], axis=0)
            s = lax.dot_general(qs, kk, (((1,), (1,)), ((), ())), preferred_element_type=F32)
            valid = band & (kj >= jnp.where(n > 0, 0, nb))
            s = jnp.where(jnp.concatenate([valid, valid], axis=0), s, NEG_INF)
            m = jnp.max(s, axis=-1, keepdims=True)
            p = jnp.exp(s - m)
            l = jnp.sum(p, axis=-1, keepdims=True)
            o = jnp.dot(p, vv, preferred_element_type=F32)
            o = jnp.where(head_a, o[:nb], o[nb:])
            put(m_sc[gi], cur, d, jnp.where(head_a, m[:nb], m[nb:]))
            put(l_sc[gi], cur, d, jnp.where(head_a, l[:nb], l[nb:]))
            put(acc_sc[gi], cur, d, o)

        def blocks(tt, carry, block=block):
            for u in range(unroll):
                block(tt * unroll + u)
            return carry

        lax.fori_loop(0, seq // nb // unroll, blocks, 0)

    def merge(n, carry):
        sl = pl.ds(pl.multiple_of(n * nb, nb), nb)
        ms = [m[sl, :] for m in m_sc]
        m_all = functools.reduce(jnp.maximum, ms)
        cs = [jnp.exp(m - m_all) for m in ms]
        den = functools.reduce(jnp.add, [c * l[sl, :] for c, l in zip(cs, l_sc)])
        num = functools.reduce(jnp.add, [c * acc[sl, :] for c, acc in zip(cs, acc_sc)])
        o_ref[sl, :] = num / den
        return carry

    lax.fori_loop(0, seq // nb, merge, 0, unroll=2)


def _attn_prompt(q, k, v, unroll=16):
    b, s, a = q.shape
    n_scratch = 3 * len(DILATED_GROUPS)
    spec = pl.BlockSpec((None, s, V7X_LANES), lambda i, j: (i, 0, j))
    vmem = 4 * s * V7X_LANES * (2 * 4 + n_scratch)
    return pl.pallas_call(
        functools.partial(_attn_prompt_kernel, seq=s, unroll=unroll),
        grid=(b, a // V7X_LANES),
        in_specs=[spec, spec, spec],
        out_specs=spec,
        out_shape=jax.ShapeDtypeStruct((b, s, a), F32),
        scratch_shapes=[pltpu.VMEM((s, V7X_LANES), F32)] * n_scratch,
        compiler_params=pltpu.CompilerParams(dimension_semantics=("arbitrary", "arbitrary"),
                                             vmem_limit_bytes=_vmem_limit(vmem)),
        name="attn_prompt",
    )(q, k, v)


def _sample_key_counts(n_buf, n_new):
    cnt = np.zeros((n_new, n_buf + n_new), np.float32)
    for w, d in DILATED_GROUPS:
        for i in range(n_new):
            for j in range(w // d + 1):
                idx = n_buf + i - d * j
                if idx >= 0:
                    cnt[i, idx] += 1.0
    return cnt[:, :n_buf], cnt[:, n_buf:]


def _attn_sample_kernel(q_ref, kn_ref, vn_ref, kt_ref, vt_ref, cnt_ref, o_ref, *, n_heads, cnt_new):
    n_new, a = q_ref.shape
    rows = n_new * n_heads
    lane_head = lax.broadcasted_iota(I32, (rows, a), 1) // HEAD_DIM
    row_head = lax.broadcasted_iota(I32, (rows, a), 0) % n_heads
    row_query = lax.broadcasted_iota(I32, (rows, 1), 0) // n_heads
    own = lane_head == row_head
    q = q_ref[...]
    qb = jnp.where(own, jnp.broadcast_to(q[:, None, :], (n_new, n_heads, a)).reshape(rows, a), 0.0)
    s_buf = jnp.dot(qb, kt_ref[...], preferred_element_type=F32)
    cnt = cnt_ref[...]
    cnt = jnp.broadcast_to(cnt[:, None, :], (n_new, n_heads, cnt.shape[-1])).reshape(rows, -1)
    valid = cnt > 0.0
    m = jnp.max(jnp.where(valid, s_buf, NEG_INF), axis=-1, keepdims=True)
    s_new, c_new = [], []
    for j in range(n_new):
        c = functools.reduce(lambda acc, i: jnp.where(row_query == i, float(cnt_new[i, j]), acc),
                             range(n_new), jnp.zeros((rows, 1), F32))
        s = jnp.sum(qb * kn_ref[pl.ds(j, 1), :], axis=-1, keepdims=True)
        m = jnp.maximum(m, jnp.where(c > 0.0, s, NEG_INF))
        s_new.append(s)
        c_new.append(c)
    p = jnp.where(valid, cnt * jnp.exp(s_buf - m), 0.0)
    den = jnp.sum(p, axis=-1, keepdims=True)
    o = lax.dot_general(p, vt_ref[...], (((1,), (1,)), ((), ())), preferred_element_type=F32)
    for j in range(n_new):
        pj = jnp.where(c_new[j] > 0.0, c_new[j] * jnp.exp(s_new[j] - m), 0.0)
        den = den + pj
        o = o + pj * vn_ref[pl.ds(j, 1), :]
    o = jnp.where(own, o / den, 0.0)
    o_ref[...] = jnp.sum(o.reshape(n_new, n_heads, a), axis=1)


def _attn_sample(q, k_new, v_new, cache_kt, cache_vt):
    b, t, a = q.shape
    n_buf = cache_kt.shape[-1]
    n_heads = a // HEAD_DIM
    cnt_buf, cnt_new = _sample_key_counts(n_buf, t)
    new = pl.BlockSpec((None, t, a), lambda i: (i, 0, 0))
    buf = pl.BlockSpec((None, a, n_buf), lambda i: (i, 0, 0))
    vmem = 4 * (2 * 2 * n_buf * a + 8 * t * n_heads * n_buf)
    return pl.pallas_call(
        functools.partial(_attn_sample_kernel, n_heads=n_heads, cnt_new=cnt_new),
        grid=(b,),
        in_specs=[new, new, new, buf, buf, pl.BlockSpec((t, n_buf), lambda i: (0, 0))],
        out_specs=new,
        out_shape=jax.ShapeDtypeStruct((b, t, a), F32),
        compiler_params=pltpu.CompilerParams(dimension_semantics=("arbitrary",),
                                             vmem_limit_bytes=_vmem_limit(vmem)),
        name="attn_sample",
    )(q, k_new, v_new, cache_kt, cache_vt, jnp.asarray(cnt_buf))


def _block_toeplitz_kernel(br_ref, bi_ref, cr_ref, ci_ref, toep_ref, tt_ref, *, chunk):
    c = br_ref.shape[1]
    over_states = (((1,), (1,)), ((), ()))
    full = dict(preferred_element_type=F32, precision=lax.Precision.HIGHEST)
    for i in range(br_ref.shape[0]):
        k = (lax.dot_general(br_ref[i], cr_ref[i], over_states, **full)
             - lax.dot_general(bi_ref[i], ci_ref[i], over_states, **full))
        for s in range(chunk):
            shifted = k if s == 0 else jnp.concatenate(
                [jnp.zeros((c, s * c), F32), k[:, :(chunk - s) * c]], axis=1)
            toep_ref[i, pl.ds(s * c, c), :] = shifted
        tt_ref[i] = toep_ref[i].T


def _block_toeplitz(bbr, bbi, cpr, cpi, chunk):
    g, c, n = bbr.shape
    lc = cpr.shape[1]
    per_step = 4
    step = lambda i: (i, 0, 0)
    spec = pl.BlockSpec((per_step, lc, lc), step)
    return pl.pallas_call(
        functools.partial(_block_toeplitz_kernel, chunk=chunk),
        grid=(g // per_step,),
        in_specs=[pl.BlockSpec((per_step, c, n), step)] * 2 + [pl.BlockSpec((per_step, lc, n), step)] * 2,
        out_specs=(spec, spec),
        out_shape=(jax.ShapeDtypeStruct((g, lc, lc), F32),) * 2,
        compiler_params=pltpu.CompilerParams(dimension_semantics=("arbitrary",)),
        name="ssm_toeplitz",
    )(bbr, bbi, cpr, cpi)


def _pad_states(m, g):
    half = (np.arange(g) % 2)[:, None, None]
    z = jnp.zeros_like(m)
    return jnp.where(half == 0, jnp.concatenate([m, z], -1), jnp.concatenate([z, m], -1))


def _ssm_tables(a_re, a_im, log_dt, b_re, b_im, c_re, c_im, d_skip, chunk):
    g, n = a_re.shape
    c = b_re.shape[-1]
    lc = chunk * c
    dt = jnp.exp(log_dt)[:, None]
    x, y = a_re * dt, a_im * dt
    ex = jnp.exp(x)
    ar, ai = ex * jnp.cos(y), ex * jnp.sin(y)
    nr = jnp.expm1(x) * jnp.cos(y) - 2.0 * jnp.sin(0.5 * y) ** 2
    ni = ai
    den = a_re * a_re + a_im * a_im
    fr = (nr * a_re + ni * a_im) / den
    fi = (ni * a_re - nr * a_im) / den
    b_re_t, b_im_t = b_re.transpose(0, 2, 1), b_im.transpose(0, 2, 1)
    bbr = fr[:, None, :] * b_re_t - fi[:, None, :] * b_im_t
    bbi = fr[:, None, :] * b_im_t + fi[:, None, :] * b_re_t
    pr, pi = [jnp.ones_like(ar)], [jnp.zeros_like(ar)]
    for _ in range(chunk):
        pr.append(pr[-1] * ar - pi[-1] * ai)
        pi.append(pr[-2] * ai + pi[-1] * ar)
    pr, pi = jnp.stack(pr, 1), jnp.stack(pi, 1)
    cpr = c_re[:, None] * pr[:, :, None, :] - c_im[:, None] * pi[:, :, None, :]
    cpi = c_re[:, None] * pi[:, :, None, :] + c_im[:, None] * pr[:, :, None, :]
    toep, tt = _block_toeplitz(bbr, bbi, cpr[:, :chunk].reshape(g, lc, n), cpi[:, :chunk].reshape(g, lc, n), chunk)
    qr, qi = pr[:, chunk - 1::-1][:, :, None, :], pi[:, chunk - 1::-1][:, :, None, :]
    wr = (qr * bbr[:, None] - qi * bbi[:, None]).reshape(g, lc, n)
    wi = (qr * bbi[:, None] + qi * bbr[:, None]).reshape(g, lc, n)
    vr = cpr[:, 1:].reshape(g, lc, n)
    vi = cpi[:, 1:].reshape(g, lc, n)
    return dict(chunk=chunk, toep=toep, tt=tt, wr=wr, wi=wi, vr=vr, vi=vi, d=d_skip, pr=pr, pi=pi)


def _ssm_tables_lanes(tab):
    g, chunk = tab["d"].shape[0], tab["chunk"]
    tr = lambda m: m.transpose(0, 2, 1)
    return dict(tt=tab["tt"], wt_re=tr(tab["wr"]), wt_im=tr(tab["wi"]), v_re=tab["vr"], v_im=tab["vi"],
                dcol=jnp.tile(tab["d"], (1, chunk))[:, :, None],
                al_re=tab["pr"][:, chunk].reshape(1, -1), al_im=tab["pi"][:, chunk].reshape(1, -1))


def _ssm_tables_rows(tab, chunk):
    g, c = tab["d"].shape
    big, lc = tab["chunk"], chunk * c
    tr = lambda m: m.transpose(0, 2, 1)
    tail = slice((big - chunk) * c, big * c)
    return dict(toep=tab["toep"][:, :lc, :lc],
                wr=_pad_states(tab["wr"][:, tail], g), wi=_pad_states(tab["wi"][:, tail], g),
                vr=tr(_pad_states(tab["vr"][:, :lc], g)), vi=tr(_pad_states(-tab["vi"][:, :lc], g)),
                d=jnp.tile(tab["d"], (1, chunk))[:, None, :],
                al_re=tab["pr"][:, chunk].reshape(1, -1), al_im=tab["pi"][:, chunk].reshape(1, -1))


def _ssm_state_kernel(u_ref, wr_ref, wi_ref, er_ref, ei_ref):
    er = jnp.dot(u_ref[0], wr_ref[0], preferred_element_type=F32)
    ei = jnp.dot(u_ref[0], wi_ref[0], preferred_element_type=F32)
    er_ref[...] = er + jnp.dot(u_ref[1], wr_ref[1], preferred_element_type=F32)
    ei_ref[...] = ei + jnp.dot(u_ref[1], wi_ref[1], preferred_element_type=F32)


def _ssm_scan_kernel(er_ref, ei_ref, ar_ref, ai_ref, hr_ref, hi_ref, xr_ref, xi_ref, lr_ref, li_ref,
                     *, n_chunks, nb):
    shape = hr_ref.shape
    ar = jnp.broadcast_to(ar_ref[...], shape)
    ai = jnp.broadcast_to(ai_ref[...], shape)

    def step(j, carry):
        xr, xi = carry
        sl = pl.ds(pl.multiple_of(j * nb, nb), nb)
        xr_ref[sl, :] = xr
        xi_ref[sl, :] = xi
        return (ar * xr - ai * xi + er_ref[sl, :], ar * xi + ai * xr + ei_ref[sl, :])

    xr, xi = lax.fori_loop(0, n_chunks, step, (hr_ref[...], hi_ref[...]))
    lr_ref[...] = xr
    li_ref[...] = xi


def _ssm_out_kernel(u_ref, t_ref, d_ref, xr_ref, xi_ref, vr_ref, vi_ref, y_ref):
    xr = xr_ref[...]
    xi = xi_ref[...]
    for a in range(2):
        u = u_ref[a]
        y = jnp.dot(u, t_ref[a], preferred_element_type=F32) + d_ref[a] * u
        y = y + jnp.dot(xr, vr_ref[a], preferred_element_type=F32)
        y_ref[a] = y + jnp.dot(xi, vi_ref[a], preferred_element_type=F32)


def _ssm_scan(er, ei, tab, h0_re, h0_im, n_chunks):
    r, gn = er.shape
    b = r // n_chunks
    lanes = 4 * V7X_LANES
    col = lambda i: (0, i)
    return pl.pallas_call(
        functools.partial(_ssm_scan_kernel, n_chunks=n_chunks, nb=b),
        grid=(gn // lanes,),
        in_specs=[pl.BlockSpec((r, lanes), col), pl.BlockSpec((r, lanes), col),
                  pl.BlockSpec((1, lanes), col), pl.BlockSpec((1, lanes), col),
                  pl.BlockSpec((b, lanes), col), pl.BlockSpec((b, lanes), col)],
        out_specs=(pl.BlockSpec((r, lanes), col),) * 2 + (pl.BlockSpec((b, lanes), col),) * 2,
        out_shape=(jax.ShapeDtypeStruct((r, gn), F32),) * 2 + (jax.ShapeDtypeStruct((b, gn), F32),) * 2,
        compiler_params=pltpu.CompilerParams(dimension_semantics=("arbitrary",),
                                             vmem_limit_bytes=_vmem_limit(2 * 4 * 4 * r * lanes)),
        name="ssm_scan",
    )(er, ei, tab["al_re"], tab["al_im"], h0_re, h0_im)


def _ssm_state_t_kernel(a_ref, wr_ref, wi_ref, er_ref, ei_ref):
    def seq(b, carry):
        a0, a1 = a_ref[0, b], a_ref[1, b]
        er_ref[b] = jnp.concatenate([jnp.dot(wr_ref[0], a0, preferred_element_type=F32),
                                     jnp.dot(wr_ref[1], a1, preferred_element_type=F32)], axis=0)
        ei_ref[b] = jnp.concatenate([jnp.dot(wi_ref[0], a0, preferred_element_type=F32),
                                     jnp.dot(wi_ref[1], a1, preferred_element_type=F32)], axis=0)
        return carry

    lax.fori_loop(0, a_ref.shape[1], seq, 0, unroll=True)


def _ssm_out_t_kernel(a_ref, t_ref, d_ref, xr_ref, xi_ref, vr_ref, vi_ref, y_ref):
    def seq(b, carry):
        xr, xi = xr_ref[b], xi_ref[b]
        n = vr_ref.shape[-1]
        for a in range(2):
            at = a_ref[a, b]
            own = slice(a * n, (a + 1) * n)
            y = jnp.dot(t_ref[a], at, preferred_element_type=F32) + d_ref[a] * at
            y = y + jnp.dot(vr_ref[a], xr[own], preferred_element_type=F32)
            y_ref[a, b] = y - jnp.dot(vi_ref[a], xi[own], preferred_element_type=F32)
        return carry

    lax.fori_loop(0, a_ref.shape[1], seq, 0, unroll=True)


def _ssm_chunks_on_lanes(u, h0_re, h0_im, tab, chunk):
    b, s, width = u.shape
    g = width // SSM_GROUP
    n_chunks = s // chunk
    lc = chunk * SSM_GROUP
    assert n_chunks == V7X_LANES
    at = (u.reshape(b, n_chunks, chunk * width).transpose(0, 2, 1)
          .reshape(b, chunk, g, SSM_GROUP, n_chunks).transpose(2, 0, 1, 3, 4).reshape(g, b, lc, n_chunks))
    pair4 = lambda i: (i, 0, 0, 0)
    pair3 = lambda i: (i, 0, 0)
    cp = lambda vmem: pltpu.CompilerParams(dimension_semantics=("arbitrary",),
                                           vmem_limit_bytes=_vmem_limit(vmem))
    a_spec = pl.BlockSpec((2, b, lc, n_chunks), pair4)
    st_spec = pl.BlockSpec((None, b, V7X_LANES, n_chunks), pair4)
    w_spec = pl.BlockSpec((2, SSM_STATE, lc), pair3)
    v_spec = pl.BlockSpec((2, lc, SSM_STATE), pair3)
    a_bytes = 2 * b * lc * n_chunks
    st_bytes = b * V7X_LANES * n_chunks
    et_re, et_im = pl.pallas_call(
        _ssm_state_t_kernel,
        grid=(g // 2,),
        in_specs=[a_spec, w_spec, w_spec],
        out_specs=(st_spec,) * 2,
        out_shape=(jax.ShapeDtypeStruct((g // 2, b, V7X_LANES, n_chunks), F32),) * 2,
        compiler_params=cp(2 * 4 * (a_bytes + 2 * st_bytes + 4 * SSM_STATE * lc)),
        name="ssm_state_t",
    )(at, tab["wt_re"], tab["wt_im"])
    rows = lambda m: m.transpose(3, 1, 0, 2).reshape(n_chunks * b, g * SSM_STATE)
    xr, xi, lr, li = _ssm_scan(rows(et_re), rows(et_im), tab, h0_re, h0_im, n_chunks)
    cols = lambda m: m.reshape(n_chunks, b, g // 2, V7X_LANES).transpose(2, 1, 3, 0)
    yt = pl.pallas_call(
        _ssm_out_t_kernel,
        grid=(g // 2,),
        in_specs=[a_spec, pl.BlockSpec((2, lc, lc), pair3), pl.BlockSpec((2, lc, 1), pair3),
                  st_spec, st_spec, v_spec, v_spec],
        out_specs=a_spec,
        out_shape=jax.ShapeDtypeStruct((g, b, lc, n_chunks), F32),
        compiler_params=cp(2 * 4 * (2 * a_bytes + 2 * st_bytes + 2 * lc * lc + 6 * SSM_STATE * lc)),
        name="ssm_out_t",
    )(at, tab["tt"], tab["dcol"], cols(xr), cols(xi), tab["v_re"], tab["v_im"])
    y = (yt.reshape(g, b, chunk, SSM_GROUP, n_chunks).transpose(1, 2, 0, 3, 4)
         .reshape(b, chunk * width, n_chunks).transpose(0, 2, 1).reshape(b, s, width))
    return y, lr, li


def _ssm(u, h0_re, h0_im, tab, chunk):
    b, s, width = u.shape
    g = width // SSM_GROUP
    n_chunks = s // chunk
    lc = chunk * SSM_GROUP
    r = n_chunks * b
    gn = g * SSM_STATE
    ut = u.reshape(b, n_chunks, chunk, g, SSM_GROUP).transpose(3, 1, 0, 2, 4).reshape(g, r, lc)
    pair = lambda i: (i, 0, 0)
    cp = lambda vmem: pltpu.CompilerParams(dimension_semantics=("arbitrary",),
                                           vmem_limit_bytes=_vmem_limit(vmem))
    er, ei = pl.pallas_call(
        _ssm_state_kernel,
        grid=(g // 2,),
        in_specs=[pl.BlockSpec((2, r, lc), pair), pl.BlockSpec((2, lc, V7X_LANES), pair),
                  pl.BlockSpec((2, lc, V7X_LANES), pair)],
        out_specs=(pl.BlockSpec((r, V7X_LANES), lambda i: (0, i)),) * 2,
        out_shape=(jax.ShapeDtypeStruct((r, gn), F32),) * 2,
        compiler_params=cp(2 * 4 * (2 * r * lc + 4 * lc * V7X_LANES + 2 * r * V7X_LANES)),
        name="ssm_state",
    )(ut, tab["wr"], tab["wi"])
    xr, xi, lr, li = _ssm_scan(er, ei, tab, h0_re, h0_im, n_chunks)
    y = pl.pallas_call(
        _ssm_out_kernel,
        grid=(g // 2,),
        in_specs=[pl.BlockSpec((2, r, lc), pair), pl.BlockSpec((2, lc, lc), pair),
                  pl.BlockSpec((2, 1, lc), pair),
                  pl.BlockSpec((r, V7X_LANES), lambda i: (0, i)),
                  pl.BlockSpec((r, V7X_LANES), lambda i: (0, i)),
                  pl.BlockSpec((2, V7X_LANES, lc), pair), pl.BlockSpec((2, V7X_LANES, lc), pair)],
        out_specs=pl.BlockSpec((2, r, lc), pair),
        out_shape=jax.ShapeDtypeStruct((g, r, lc), F32),
        compiler_params=cp(2 * 4 * (4 * r * lc + 2 * lc * lc + 2 * r * V7X_LANES + 4 * V7X_LANES * lc)),
        name="ssm_out",
    )(ut, tab["toep"], tab["d"], xr, xi, tab["vr"], tab["vi"])
    y = y.reshape(g, n_chunks, b, chunk, SSM_GROUP).transpose(2, 1, 3, 0, 4).reshape(b, s, width)
    return y, lr, li


def _finish_kernel(x_ref, attn_ref, ssm_ref, ga_ref, wg_ref, bg_ref, gs_ref, wo_ref, gf_ref,
                   wr_ref, br_ref, base_ref, x1_ref, hf_ref, ti_ref, tg_ref, cnt_ref, seen):
    a = attn_ref.shape[-1]
    tm = x_ref.shape[0]

    @pl.when(pl.program_id(0) == 0)
    def _():
        seen[...] = base_ref[...]

    g = jax.nn.gelu(ssm_ref[...])
    gate = jax.nn.sigmoid(jnp.dot(g, wg_ref[...], preferred_element_type=F32) + bg_ref[...])
    mixed_a = _rms(attn_ref[...], ga_ref[...])
    mixed_s = _rms(g * gate, gs_ref[...])
    x1 = (x_ref[...] + jnp.dot(mixed_a, wo_ref[pl.ds(0, a), :], preferred_element_type=F32)
          + jnp.dot(mixed_s, wo_ref[pl.ds(a, wo_ref.shape[0] - a), :], preferred_element_type=F32))
    x1_ref[...] = x1
    hf = _rms(x1, gf_ref[...])
    _to_token_tiles(hf_ref, hf)
    hf_hi = hf.astype(jnp.bfloat16).astype(F32)
    hf_lo = hf - hf_hi
    small = (jnp.dot(hf_lo, wr_ref[1], preferred_element_type=F32)
             + jnp.dot(hf_hi, wr_ref[1], preferred_element_type=F32)
             + jnp.dot(hf_lo, wr_ref[0], preferred_element_type=F32))
    logits = jnp.dot(hf_hi, wr_ref[0], preferred_element_type=F32) + small + br_ref[...]
    lane = lax.broadcasted_iota(I32, logits.shape, 1)
    work = logits
    vals, idxs = [], []
    for _ in range(TOP_K):
        m = jnp.max(work, axis=-1, keepdims=True)
        idx = jnp.min(jnp.where(work == m, lane, V7X_LANES), axis=-1, keepdims=True)
        vals.append(m)
        idxs.append(idx)
        work = jnp.where(lane == idx, -jnp.inf, work)
    exps = [jnp.exp(v - vals[0]) for v in vals]
    den = exps[0]
    for e in exps[1:]:
        den = den + e
    picked = jnp.zeros(logits.shape, F32)
    for k in range(TOP_K):
        picked = picked + jnp.where(lane == idxs[k], 1.0, 0.0)
    tri = jnp.where(lax.broadcasted_iota(I32, (tm, tm), 0) > lax.broadcasted_iota(I32, (tm, tm), 1), 1.0, 0.0)
    before = jnp.dot(tri, picked, preferred_element_type=F32) + seen[...]
    ti = jnp.zeros(logits.shape, I32)
    tg = jnp.zeros(logits.shape, F32)
    for k in range(TOP_K):
        rank = jnp.sum(jnp.where(lane == idxs[k], before, 0.0), axis=-1, keepdims=True)
        ti = jnp.where(lane == k, idxs[k], ti)
        ti = jnp.where(lane == TOP_K + k, rank.astype(I32), ti)
        tg = jnp.where(lane == k, exps[k] / den, tg)
    ti_ref[...] = ti
    tg_ref[...] = tg
    seen[...] = seen[...] + jnp.sum(picked, axis=0, keepdims=True)
    cnt_ref[...] = seen[...]


def _finish(x2d, attn, ssm_y, p, seen, tm):
    n, d = x2d.shape
    a = attn.shape[1]
    sw = ssm_y.shape[1]
    ne = p["w_router"].shape[1]
    w_router = jnp.pad(p["w_router"], ((0, 0), (0, V7X_LANES - ne)))
    w_hi = w_router.astype(jnp.bfloat16).astype(F32)
    w_router = jnp.stack([w_hi, w_router - w_hi])
    b_router = jnp.pad(p["b_router"], (0, V7X_LANES - ne), constant_values=-jnp.inf).reshape(1, V7X_LANES)
    row = lambda i: (i, 0)
    full = lambda arr: pl.BlockSpec(arr.shape, lambda i: (0,) * arr.ndim)
    ins = [x2d, attn, ssm_y, p["attn_out_norm_g"].reshape(1, a), p["w_glu"], p["b_glu"].reshape(1, sw),
           p["ssm_out_norm_g"].reshape(1, sw), p["w_out"], p["norm_ffn_g"].reshape(1, d),
           w_router, b_router, seen]
    in_specs = [pl.BlockSpec((tm, d), row), pl.BlockSpec((tm, a), row), pl.BlockSpec((tm, sw), row)]
    in_specs += [full(arr) for arr in ins[3:]]
    vmem = 2 * 4 * (tm * (3 * d + a + sw + 2 * V7X_LANES) + sw * sw + d * d + 2 * d * V7X_LANES) + 16 * tm * d
    return pl.pallas_call(
        _finish_kernel,
        grid=(n // tm,),
        in_specs=in_specs,
        out_specs=(pl.BlockSpec((tm, d), row), pl.BlockSpec((tm * V7X_SUBLANES, V7X_LANES), row),
                   pl.BlockSpec((tm, V7X_LANES), row), pl.BlockSpec((tm, V7X_LANES), row),
                   pl.BlockSpec((1, V7X_LANES), lambda i: (0, 0))),
        out_shape=(jax.ShapeDtypeStruct((n, d), F32), jax.ShapeDtypeStruct((n * V7X_SUBLANES, V7X_LANES), F32),
                   jax.ShapeDtypeStruct((n, V7X_LANES), I32), jax.ShapeDtypeStruct((n, V7X_LANES), F32),
                   jax.ShapeDtypeStruct((1, V7X_LANES), F32)),
        scratch_shapes=[pltpu.VMEM((1, V7X_LANES), F32)],
        compiler_params=pltpu.CompilerParams(dimension_semantics=("arbitrary",),
                                             vmem_limit_bytes=_vmem_limit(vmem)),
        name="finish",
    )(*ins)


def _tile_copy(src_ref, src_row, dst_ref, dst_row, sem):
    return pltpu.make_async_copy(src_ref.at[src_row], dst_ref.at[dst_row], sem)


def _dispatch_kernel(dest_ref, hp_ref, hs_ref, xs_ref, sem, *, n_tiles_p):
    def scatter(hf_ref):
        tm = hf_ref.shape[0]

        def tokens(i, carry):
            for u in range(V7X_SUBLANES):
                r = i * V7X_SUBLANES + u
                for k in range(TOP_K):
                    _tile_copy(hf_ref, r, xs_ref, dest_ref[0, 0, r * TOP_K + k], sem).start(priority=k % 2)
            return carry

        lax.fori_loop(0, tm // V7X_SUBLANES, tokens, 0)
        rows = pl.ds(0, tm * TOP_K)
        pltpu.make_async_copy(xs_ref.at[rows], xs_ref.at[rows], sem).wait()

    @pl.when(pl.program_id(0) < n_tiles_p)
    def _():
        scatter(hp_ref)

    @pl.when(pl.program_id(0) == n_tiles_p)
    def _():
        scatter(hs_ref)


def _dispatch(hf_p, dest_p, hf_s, dest_s, tm):
    tile = (V7X_SUBLANES, V7X_LANES)
    hf_p, hf_s = hf_p.reshape(-1, *tile), hf_s.reshape(-1, *tile)
    n_p, n_s = hf_p.shape[0], hf_s.shape[0]
    assert n_p % tm == 0 and n_s <= tm
    n_tiles_p = n_p // tm
    dest3 = jnp.concatenate([dest_p.reshape(n_tiles_p, tm * TOP_K),
                             jnp.pad(dest_s.reshape(1, n_s * TOP_K), ((0, 0), (0, (tm - n_s) * TOP_K)))])
    dest3 = dest3.reshape(n_tiles_p + 1, 1, tm * TOP_K)
    return pl.pallas_call(
        functools.partial(_dispatch_kernel, n_tiles_p=n_tiles_p),
        grid=(n_tiles_p + 1,),
        in_specs=[pl.BlockSpec((1, 1, tm * TOP_K), lambda i: (i, 0, 0), memory_space=pltpu.SMEM),
                  pl.BlockSpec((tm, *tile), lambda i: (jnp.minimum(i, n_tiles_p - 1), 0, 0)),
                  pl.BlockSpec((n_s, *tile), lambda i: (0, 0, 0))],
        out_specs=pl.BlockSpec(memory_space=pl.ANY),
        out_shape=jax.ShapeDtypeStruct(((n_p + n_s) * TOP_K, *tile), F32),
        scratch_shapes=[pltpu.SemaphoreType.DMA(())],
        compiler_params=pltpu.CompilerParams(dimension_semantics=("arbitrary",)),
        name="dispatch",
    )(dest3, hf_p, hf_s)


def _moe_kernel(vt_ref, ve_ref, nx_ref, lo_ref, hi_ref, x_ref, w1_ref, b1_ref, w2_ref, b2_ref, y_ref,
                w1s, w2s, w1b, w2b, sem, *, d_ff):
    v = pl.program_id(0)
    lo, hi = lo_ref[v], hi_ref[v]
    prev = jnp.maximum(v - 1, 0)
    first = jnp.logical_or(v == 0, vt_ref[v] != vt_ref[prev])

    def fetch(e):
        return (pltpu.make_async_copy(w1_ref.at[e], w1s, sem.at[0]),
                pltpu.make_async_copy(w2_ref.at[e], w2s, sem.at[1]))

    @pl.when(v == 0)
    def _():
        for cp in fetch(ve_ref[0]):
            cp.start()

    @pl.when(jnp.logical_or(v == 0, ve_ref[v] != ve_ref[prev]))
    def _():
        for cp in fetch(ve_ref[v]):
            cp.wait()
        w1b[...] = w1s[...].astype(jnp.bfloat16)
        w2b[...] = w2s[...].astype(jnp.bfloat16)

        @pl.when(nx_ref[v] >= 0)
        def _():
            for cp in fetch(nx_ref[v]):
                cp.start()

    def swiglu(x_tiles):
        x = _from_token_tiles(x_tiles).astype(jnp.bfloat16)
        a = jnp.dot(x, w1b[...], preferred_element_type=F32) + b1_ref[0]
        x_glu = jnp.minimum(a[:, :d_ff], SWIGLU_LIMIT)
        x_lin = jnp.clip(a[:, d_ff:], -SWIGLU_LIMIT, SWIGLU_LIMIT)
        mid = x_glu * jax.nn.sigmoid(SWIGLU_ALPHA * x_glu) * (x_lin + 1.0)
        return jnp.dot(mid.astype(jnp.bfloat16), w2b[...], preferred_element_type=F32) + b2_ref[0]

    whole = jnp.logical_and(lo == 0, hi == MOE_TILE)

    @pl.when(whole)
    def _():
        _to_token_tiles(y_ref, swiglu(x_ref))

    for r0 in range(0, MOE_TILE, MOE_SUB):
        sub = pl.ds(r0 * V7X_SUBLANES, MOE_SUB * V7X_SUBLANES)
        x_sub, y_sub = x_ref.at[sub, :], y_ref.at[sub, :]
        touched = jnp.logical_and(jnp.logical_not(whole), jnp.logical_and(lo < r0 + MOE_SUB, hi > r0))

        @pl.when(touched)
        def _(r0=r0, x_sub=x_sub, y_sub=y_sub):
            y = swiglu(x_sub)
            rows = r0 + lax.broadcasted_iota(I32, (MOE_SUB, 1), 0)
            mine = (rows >= lo) & (rows < hi)

            @pl.when(first)
            def _():
                _to_token_tiles(y_sub, jnp.where(mine, y, 0.0))

            @pl.when(jnp.logical_not(first))
            def _():
                _to_token_tiles(y_sub, jnp.where(mine, y, _from_token_tiles(y_sub)))

        @pl.when(jnp.logical_and(first, jnp.logical_not(jnp.logical_or(touched, whole))))
        def _(y_sub=y_sub):
            y_sub[...] = jnp.zeros(y_sub.shape, F32)


def _moe_experts(xs, visits, w1, b1, w2, b2):
    ne, d, ff2 = w1.shape
    d_ff = ff2 // 2
    n_visits = visits[0].shape[0]
    row = lambda v, vt, ve, nx, lo, hi: (vt[v], 0)
    exp = lambda v, vt, ve, nx, lo, hi: (ve[v], 0, 0)
    vmem = (4 * 2 * 2 * MOE_TILE * d + (4 + 2) * (d * ff2 + d_ff * d) + 2 * 4 * (ff2 + d)
            + 4 * 4 * MOE_TILE * ff2)
    tiles = pl.BlockSpec((MOE_TILE * V7X_SUBLANES, V7X_LANES), row)
    grid_spec = pltpu.PrefetchScalarGridSpec(
        num_scalar_prefetch=5,
        grid=(n_visits,),
        in_specs=[tiles,
                  pl.BlockSpec(memory_space=pl.ANY), pl.BlockSpec((1, 1, ff2), exp),
                  pl.BlockSpec(memory_space=pl.ANY), pl.BlockSpec((1, 1, d), exp)],
        out_specs=tiles,
        scratch_shapes=[pltpu.VMEM((d, ff2), F32), pltpu.VMEM((d_ff, d), F32),
                        pltpu.VMEM((d, ff2), jnp.bfloat16), pltpu.VMEM((d_ff, d), jnp.bfloat16),
                        pltpu.SemaphoreType.DMA((2,))],
    )
    return pl.pallas_call(
        functools.partial(_moe_kernel, d_ff=d_ff),
        grid_spec=grid_spec,
        out_shape=jax.ShapeDtypeStruct(xs.shape, F32),
        compiler_params=pltpu.CompilerParams(dimension_semantics=("arbitrary",),
                                             vmem_limit_bytes=_vmem_limit(vmem)),
        name="moe",
    )(*visits, xs, w1, b1.reshape(ne, 1, ff2), w2, b2.reshape(ne, 1, d))


def _moe_visits(sizes, n_rows):
    n_tiles = n_rows // MOE_TILE
    n_visits = n_tiles + N_EXPERTS - 1
    ends = jnp.cumsum(sizes)
    starts = ends - sizes
    first_tile = starts // MOE_TILE
    n_vis = jnp.where(sizes > 0, (ends - 1) // MOE_TILE - first_tile + 1, 0)
    vis_end = jnp.cumsum(n_vis)
    total = vis_end[-1]
    v = jnp.clip(jnp.arange(n_visits, dtype=I32), 0, jnp.maximum(total - 1, 0))
    e = jnp.sum((vis_end[None, :] <= v[:, None]).astype(I32), axis=1)
    pick = lambda tab: jnp.sum(jnp.where(e[:, None] == jnp.arange(N_EXPERTS)[None, :], tab[None, :], 0), axis=1)
    tile = pick(first_tile) + v - pick(vis_end - n_vis)
    lo = jnp.maximum(pick(starts), tile * MOE_TILE) - tile * MOE_TILE
    hi = jnp.minimum(pick(ends), (tile + 1) * MOE_TILE) - tile * MOE_TILE
    live = jnp.arange(n_visits) < total
    ids = jnp.arange(N_EXPERTS, dtype=I32)
    later = jnp.where((ids[None, :] > ids[:, None]) & (sizes[None, :] > 0), ids[None, :], N_EXPERTS)
    nxt = jnp.min(later, axis=1)
    nxt = pick(jnp.where(nxt < N_EXPERTS, nxt, -1))
    return (tile.astype(I32), e.astype(I32), nxt.astype(I32), jnp.where(live, lo, 0).astype(I32),
            jnp.where(live, hi, 0).astype(I32))


def _final_kernel(dest_ref, nxt_ref, x_ref, tg_ref, g_ref, ys_ref, y_ref, buf, sem, *, tm):
    i = pl.program_id(0)
    n = pl.num_programs(0)

    def gather(d_ref, slot):
        def tokens(i8, carry):
            for u in range(V7X_SUBLANES):
                r = i8 * V7X_SUBLANES + u
                tile = pl.ds(pl.multiple_of(r * V7X_SUBLANES, V7X_SUBLANES), V7X_SUBLANES)
                for k in range(TOP_K):
                    pltpu.make_async_copy(ys_ref.at[d_ref[0, 0, r * TOP_K + k]], buf.at[slot, k, tile, :],
                                          sem.at[slot]).start(priority=k % 2)
            return carry
        lax.fori_loop(0, tm // V7X_SUBLANES, tokens, 0)

    @pl.when(i == 0)
    def _():
        gather(dest_ref, 0)

    for slot in range(2):
        @pl.when(jnp.logical_and(i + 1 < n, (i + 1) % 2 == slot))
        def _(slot=slot):
            gather(nxt_ref, slot)

    for slot in range(2):
        @pl.when(i % 2 == slot)
        def _(slot=slot):
            pltpu.make_async_copy(buf.at[slot], buf.at[slot], sem.at[slot]).wait()
            tg = tg_ref[...]
            parts = []
            for j in range(V7X_SUBLANES):
                f = None
                for k in range(TOP_K):
                    rows = buf[slot, k, pl.ds(j, tm, stride=V7X_SUBLANES), :]
                    f = tg[:, k:k + 1] * rows if f is None else f + tg[:, k:k + 1] * rows
                parts.append(f)
            y_ref[...] = _rms(x_ref[...] + jnp.concatenate(parts, axis=1), g_ref[...])


def _final(x1, tg, dest, ys, g, tm):
    n, d = x1.shape
    n_tiles = n // tm
    tile = (V7X_SUBLANES, V7X_LANES)
    dest3 = dest.reshape(n_tiles, 1, tm * TOP_K)
    row = lambda i: (i, 0)
    smem = lambda index_map: pl.BlockSpec((1, 1, tm * TOP_K), index_map, memory_space=pltpu.SMEM)
    vmem = 4 * (2 * TOP_K * tm * d + 2 * 2 * tm * d + 2 * tm * V7X_LANES + 4 * tm * d)
    return pl.pallas_call(
        functools.partial(_final_kernel, tm=tm),
        grid=(n_tiles,),
        in_specs=[smem(lambda i: (i, 0, 0)), smem(lambda i: (jnp.minimum(i + 1, n_tiles - 1), 0, 0)),
                  pl.BlockSpec((tm, d), row), pl.BlockSpec((tm, V7X_LANES), row),
                  pl.BlockSpec((1, d), lambda i: (0, 0)), pl.BlockSpec(memory_space=pl.ANY)],
        out_specs=pl.BlockSpec((tm, d), row),
        out_shape=jax.ShapeDtypeStruct((n, d), F32),
        scratch_shapes=[pltpu.VMEM((2, TOP_K, tm * V7X_SUBLANES, V7X_LANES), F32), pltpu.SemaphoreType.DMA((2,))],
        compiler_params=pltpu.CompilerParams(dimension_semantics=("arbitrary",),
                                             vmem_limit_bytes=_vmem_limit(vmem)),
        name="final",
    )(dest3, dest3, x1, tg, g.reshape(1, d), ys.reshape(-1, *tile))


def kernel(x_prompt, x_sample, cache_k, cache_v, state_ssm_re, state_ssm_im, norm_mix_g, w_in, attn_out_norm_g, ssm_a_re, ssm_a_im, ssm_log_dt, ssm_b_re, ssm_b_im, ssm_c_re, ssm_c_im, ssm_d, w_glu, b_glu, ssm_out_norm_g, w_out, norm_ffn_g, w_router, b_router, w_moe1, b_moe1, w_moe2, b_moe2, norm_final_g):
    depth = w_in.shape[0]
    assert depth == 1, "single-layer trunk"
    bp, s, d = x_prompt.shape
    bs, t, _ = x_sample.shape
    n_buf = cache_k.shape[2]
    n_heads, head_dim = cache_k.shape[3], cache_k.shape[4]
    assert head_dim == HEAD_DIM
    a = n_heads * head_dim
    g = ssm_a_re.shape[1]
    gn = g * SSM_STATE
    assert s % (max(dd for _, dd in DILATED_GROUPS) * KEYS_BACK) == 0
    np_, ns = bp * s, bs * t
    tm_p, tm_s = TOKEN_TILE, ns
    l = 0
    ssm_p = (ssm_a_re[l], ssm_a_im[l], ssm_log_dt[l], ssm_b_re[l], ssm_b_im[l], ssm_c_re[l], ssm_c_im[l],
             ssm_d[l])
    tail = dict(attn_out_norm_g=attn_out_norm_g[l], w_glu=w_glu[l], b_glu=b_glu[l],
                ssm_out_norm_g=ssm_out_norm_g[l], w_out=w_out[l], norm_ffn_g=norm_ffn_g[l],
                w_router=w_router[l], b_router=b_router[l])

    xp = x_prompt.reshape(np_, d)
    qp, kp, vp, up, ktp, vtp = _proj(xp, norm_mix_g[l], w_in[l], a, tm_p, seq=s)
    attn_p = _attn_prompt(qp.reshape(bp, s, a), kp.reshape(bp, s, a), vp.reshape(bp, s, a))
    zeros = jnp.zeros((bp, gn), F32)
    tables = _ssm_tables(*ssm_p, PROMPT_CHUNK)
    y_p, hp_re, hp_im = _ssm_chunks_on_lanes(up.reshape(bp, s, -1), zeros, zeros, _ssm_tables_lanes(tables),
                                             PROMPT_CHUNK)
    seen = jnp.zeros((1, V7X_LANES), F32)
    x1_p, hf_p, ti_p, tg_p, seen = _finish(xp, attn_p.reshape(np_, a), y_p.reshape(np_, -1), tail, seen, tm_p)

    xs = x_sample.reshape(ns, d)
    qs, ks, vs, us = _proj(xs, norm_mix_g[l], w_in[l], a, tm_s)
    by_dim = lambda c: c.transpose(0, 2, 3, 1).reshape(bs, a, n_buf)
    attn_s = _attn_sample(qs.reshape(bs, t, a), ks.reshape(bs, t, a), vs.reshape(bs, t, a),
                          by_dim(cache_k[l]), by_dim(cache_v[l]))
    y_s, hs_re, hs_im = _ssm(us.reshape(bs, t, -1), state_ssm_re[l].reshape(bs, gn),
                             state_ssm_im[l].reshape(bs, gn), _ssm_tables_rows(tables, t), t)
    x1_s, hf_s, ti_s, tg_s, seen = _finish(xs, attn_s.reshape(ns, a), y_s.reshape(ns, -1), tail, seen, tm_s)

    sizes = seen[0, :N_EXPERTS].astype(I32)
    starts = jnp.cumsum(sizes) - sizes
    experts = jnp.arange(N_EXPERTS, dtype=I32)[None, None, :]

    def sorted_row(ti):
        hit = ti[:, :TOP_K, None] == experts
        return jnp.sum(jnp.where(hit, starts[None, None, :], 0), axis=-1) + ti[:, TOP_K:2 * TOP_K]

    dest_p, dest_s = sorted_row(ti_p), sorted_row(ti_s)
    n_rows = (np_ + ns) * TOP_K
    assert n_rows % MOE_TILE == 0
    xs_sorted = _dispatch(hf_p, dest_p, hf_s, dest_s, DISPATCH_TILE).reshape(-1, V7X_LANES)
    ys = _moe_experts(xs_sorted, _moe_visits(sizes, n_rows), w_moe1[l], b_moe1[l], w_moe2[l], b_moe2[l])
    y_prompt = _final(x1_p, tg_p, dest_p, ys, norm_final_g, COMBINE_TILE).reshape(bp, s, d)
    y_sample = _final(x1_s, tg_s, dest_s, ys, norm_final_g, tm_s).reshape(bs, t, d)

    keep = min(max(w for w, _ in DILATED_GROUPS), s)
    k5 = lambda z, b_, s_: z.reshape(1, b_, s_, n_heads, head_dim)
    by_pos = lambda zt: zt.reshape(bp, n_heads, head_dim, s).transpose(0, 3, 1, 2)[None]
    st = lambda z, b_: z.reshape(1, b_, g, SSM_STATE)
    return (y_prompt, y_sample,
            by_pos(ktp)[:, :, s - keep:], by_pos(vtp)[:, :, s - keep:], st(hp_re, bp), st(hp_im, bp),
            k5(ks, bs, t), k5(vs, bs, t), st(hs_re, bs), st(hs_im, bs))
```

```python
import functools

import numpy as np
import jax
import jax.numpy as jnp
from jax import lax
from jax.experimental import pallas as pl
from jax.experimental.pallas import tpu as pltpu

F32 = jnp.float32
I32 = jnp.int32

V7X_LANES = 128
V7X_SUBLANES = 8
V7X_VMEM_BYTES = 64 * 1024 * 1024
COMPILER_SCRATCH_BYTES = 8 * 1024 * 1024

HEAD_DIM = 64
DILATED_GROUPS = ((128, 1), (512, 4), (2048, 16))
KEYS_BACK = 128
SSM_GROUP = 16
SSM_STATE = 64
N_EXPERTS = 32
TOP_K = 4
SWIGLU_LIMIT = 7.0
SWIGLU_ALPHA = 1.702
RMS_EPS = 1e-5
NEG_INF = -1e30
PROMPT_CHUNK = 16
MOE_TILE = 512
MOE_SUB = 256
TOKEN_TILE = 512
DISPATCH_TILE = 4096
COMBINE_TILE = 256


def _vmem_limit(nbytes):
    return int(min(nbytes + COMPILER_SCRATCH_BYTES, V7X_VMEM_BYTES - COMPILER_SCRATCH_BYTES))


def _rms(x, g):
    return x * lax.rsqrt(jnp.mean(x * x, axis=-1, keepdims=True) + RMS_EPS) * g


def _to_token_tiles(ref, val):
    rows, width = val.shape
    assert width == V7X_SUBLANES * V7X_LANES and ref.shape == (rows * V7X_SUBLANES, V7X_LANES)
    for j in range(V7X_SUBLANES):
        ref[pl.ds(j, rows, stride=V7X_SUBLANES), :] = val[:, j * V7X_LANES:(j + 1) * V7X_LANES]


def _from_token_tiles(ref):
    rows = ref.shape[0] // V7X_SUBLANES
    return jnp.concatenate([ref[pl.ds(j, rows, stride=V7X_SUBLANES), :] for j in range(V7X_SUBLANES)], axis=1)


def _proj_kernel(x_ref, g_ref, w_ref, q_ref, k_ref, v_ref, u_ref, *kv_t, attn_width, q_scale):
    h = _rms(x_ref[...], g_ref[...])
    z = jnp.dot(h, w_ref[...], preferred_element_type=F32)
    a = attn_width
    q_ref[...] = z[:, :a] * q_scale
    k_ref[...] = z[:, a:2 * a]
    v_ref[...] = z[:, 2 * a:3 * a]
    u_ref[...] = z[:, 3 * a:]
    if kv_t:
        kt_ref, vt_ref = kv_t
        kt_ref[...] = z[:, a:2 * a].T
        vt_ref[...] = z[:, 2 * a:3 * a].T


def _proj(x2d, norm_g, w_in, attn_width, tm, seq=None):
    n, d = x2d.shape
    mix = w_in.shape[1]
    ssm_width = mix - 3 * attn_width
    row = lambda i: (i, 0)
    const = lambda i: (0, 0)
    out_shape = (jax.ShapeDtypeStruct((n, attn_width), F32),) * 3 + (
        jax.ShapeDtypeStruct((n, ssm_width), F32),)
    out_specs = (pl.BlockSpec((tm, attn_width), row),) * 3 + (pl.BlockSpec((tm, ssm_width), row),)
    if seq is not None:
        assert seq % tm == 0 and tm % V7X_LANES == 0
        per_seq = seq // tm
        out_shape += (jax.ShapeDtypeStruct((n // seq, attn_width, seq), F32),) * 2
        out_specs += (pl.BlockSpec((None, attn_width, tm), lambda i: (i // per_seq, 0, i % per_seq)),) * 2
    vmem = 2 * 4 * (tm * d + d * mix + tm * mix + 2 * tm * attn_width) + 4 * tm * mix
    return pl.pallas_call(
        functools.partial(_proj_kernel, attn_width=attn_width, q_scale=HEAD_DIM ** -0.5),
        grid=(n // tm,),
        in_specs=[pl.BlockSpec((tm, d), row), pl.BlockSpec((1, d), const),
                  pl.BlockSpec((d, mix), const)],
        out_specs=out_specs,
        out_shape=out_shape,
        compiler_params=pltpu.CompilerParams(dimension_semantics=("arbitrary",),
                                             vmem_limit_bytes=_vmem_limit(vmem)),
        name="proj",
    )(x2d, norm_g.reshape(1, d), w_in)


def _attn_prompt_kernel(q_ref, k_ref, v_ref, o_ref, *scratch, seq, unroll):
    nb = KEYS_BACK
    n_groups = len(DILATED_GROUPS)
    m_sc, l_sc, acc_sc = scratch[:n_groups], scratch[n_groups:2 * n_groups], scratch[2 * n_groups:]
    lane = lax.broadcasted_iota(I32, (1, V7X_LANES), 1)
    head_a = lane < HEAD_DIM
    qi = lax.broadcasted_iota(I32, (nb, 2 * nb), 0)
    kj = lax.broadcasted_iota(I32, (nb, 2 * nb), 1)
    rel = nb + qi - kj
    band = (rel >= 0) & (rel <= nb)

    def rows(ref, start, d):
        if d == 1:
            return ref[pl.ds(pl.multiple_of(start, nb), nb), :]
        return ref[pl.ds(start, nb, stride=d), :]

    def put(ref, start, d, val):
        if d == 1:
            ref[pl.ds(pl.multiple_of(start, nb), nb), :] = val
        else:
            ref[pl.ds(start, nb, stride=d), :] = val

    for gi, (w, d) in enumerate(DILATED_GROUPS):
        assert w // d == nb
        nblk = seq // (d * nb)

        def block(t, gi=gi, d=d, nblk=nblk):
            r = t // nblk
            n = t % nblk
            cur = r + d * nb * n
            prev = r + d * nb * jnp.maximum(n - 1, 0)
            q = rows(q_ref, cur, d)
            kk = jnp.concatenate([rows(k_ref, prev, d), rows(k_ref, cur, d)], axis=0)
            vv = jnp.concatenate([rows(v_ref, prev, d), rows(v_ref, cur, d)], axis=0)
            qs = jnp.concatenate([jnp.where(head_a, q, 0.0), jnp.where(head_a, 0.0, q)], axis=0)
            s = lax.dot_general(qs, kk, (((1,), (1,)), ((), ())), preferred_element_type=F32)
            valid = band & (kj >= jnp.where(n > 0, 0, nb))
            s = jnp.where(jnp.concatenate([valid, valid], axis=0), s, NEG_INF)
            m = jnp.max(s, axis=-1, keepdims=True)
            p = jnp.exp(s - m)
            l = jnp.sum(p, axis=-1, keepdims=True)
            o = jnp.dot(p, vv, preferred_element_type=F32)
            o = jnp.where(head_a, o[:nb], o[nb:])
            put(m_sc[gi], cur, d, jnp.where(head_a, m[:nb], m[nb:]))
            put(l_sc[gi], cur, d, jnp.where(head_a, l[:nb], l[nb:]))
            put(acc_sc[gi], cur, d, o)

        def blocks(tt, carry, block=block):
            for u in range(unroll):
                block(tt * unroll + u)
            return carry

        lax.fori_loop(0, seq // nb // unroll, blocks, 0)

    def merge(n, carry):
        sl = pl.ds(pl.multiple_of(n * nb, nb), nb)
        ms = [m[sl, :] for m in m_sc]
        m_all = functools.reduce(jnp.maximum, ms)
        cs = [jnp.exp(m - m_all) for m in ms]
        den = functools.reduce(jnp.add, [c * l[sl, :] for c, l in zip(cs, l_sc)])
        num = functools.reduce(jnp.add, [c * acc[sl, :] for c, acc in zip(cs, acc_sc)])
        o_ref[sl, :] = num / den
        return carry

    lax.fori_loop(0, seq // nb, merge, 0, unroll=2)


def _attn_prompt(q, k, v, unroll=16):
    b, s, a = q.shape
    n_scratch = 3 * len(DILATED_GROUPS)
    spec = pl.BlockSpec((None, s, V7X_LANES), lambda i, j: (i, 0, j))
    vmem = 4 * s * V7X_LANES * (2 * 4 + n_scratch)
    return pl.pallas_call(
        functools.partial(_attn_prompt_kernel, seq=s, unroll=unroll),
        grid=(b, a // V7X_LANES),
        in_specs=[spec, spec, spec],
        out_specs=spec,
        out_shape=jax.ShapeDtypeStruct((b, s, a), F32),
        scratch_shapes=[pltpu.VMEM((s, V7X_LANES), F32)] * n_scratch,
        compiler_params=pltpu.CompilerParams(dimension_semantics=("arbitrary", "arbitrary"),
                                             vmem_limit_bytes=_vmem_limit(vmem)),
        name="attn_prompt",
    )(q, k, v)


def _sample_key_counts(n_buf, n_new):
    cnt = np.zeros((n_new, n_buf + n_new), np.float32)
    for w, d in DILATED_GROUPS:
        for i in range(n_new):
            for j in range(w // d + 1):
                idx = n_buf + i - d * j
                if idx >= 0:
                    cnt[i, idx] += 1.0
    return cnt[:, :n_buf], cnt[:, n_buf:]


def _attn_sample_kernel(q_ref, kn_ref, vn_ref, kt_ref, vt_ref, cnt_ref, o_ref, *, n_heads, cnt_new):
    n_new, a = q_ref.shape
    rows = n_new * n_heads
    lane_head = lax.broadcasted_iota(I32, (rows, a), 1) // HEAD_DIM
    row_head = lax.broadcasted_iota(I32, (rows, a), 0) % n_heads
    row_query = lax.broadcasted_iota(I32, (rows, 1), 0) // n_heads
    own = lane_head == row_head
    q = q_ref[...]
    qb = jnp.where(own, jnp.broadcast_to(q[:, None, :], (n_new, n_heads, a)).reshape(rows, a), 0.0)
    s_buf = jnp.dot(qb, kt_ref[...], preferred_element_type=F32)
    cnt = cnt_ref[...]
    cnt = jnp.broadcast_to(cnt[:, None, :], (n_new, n_heads, cnt.shape[-1])).reshape(rows, -1)
    valid = cnt > 0.0
    m = jnp.max(jnp.where(valid, s_buf, NEG_INF), axis=-1, keepdims=True)
    s_new, c_new = [], []
    for j in range(n_new):
        c = functools.reduce(lambda acc, i: jnp.where(row_query == i, float(cnt_new[i, j]), acc),
                             range(n_new), jnp.zeros((rows, 1), F32))
        s = jnp.sum(qb * kn_ref[pl.ds(j, 1), :], axis=-1, keepdims=True)
        m = jnp.maximum(m, jnp.where(c > 0.0, s, NEG_INF))
        s_new.append(s)
        c_new.append(c)
    p = jnp.where(valid, cnt * jnp.exp(s_buf - m), 0.0)
    den = jnp.sum(p, axis=-1, keepdims=True)
    o = lax.dot_general(p, vt_ref[...], (((1,), (1,)), ((), ())), preferred_element_type=F32)
    for j in range(n_new):
        pj = jnp.where(c_new[j] > 0.0, c_new[j] * jnp.exp(s_new[j] - m), 0.0)
        den = den + pj
        o = o + pj * vn_ref[pl.ds(j, 1), :]
    o = jnp.where(own, o / den, 0.0)
    o_ref[...] = jnp.sum(o.reshape(n_new, n_heads, a), axis=1)


def _attn_sample(q, k_new, v_new, cache_kt, cache_vt):
    b, t, a = q.shape
    n_buf = cache_kt.shape[-1]
    n_heads = a // HEAD_DIM
    cnt_buf, cnt_new = _sample_key_counts(n_buf, t)
    new = pl.BlockSpec((None, t, a), lambda i: (i, 0, 0))
    buf = pl.BlockSpec((None, a, n_buf), lambda i: (i, 0, 0))
    vmem = 4 * (2 * 2 * n_buf * a + 8 * t * n_heads * n_buf)
    return pl.pallas_call(
        functools.partial(_attn_sample_kernel, n_heads=n_heads, cnt_new=cnt_new),
        grid=(b,),
        in_specs=[new, new, new, buf, buf, pl.BlockSpec((t, n_buf), lambda i: (0, 0))],
        out_specs=new,
        out_shape=jax.ShapeDtypeStruct((b, t, a), F32),
        compiler_params=pltpu.CompilerParams(dimension_semantics=("arbitrary",),
                                             vmem_limit_bytes=_vmem_limit(vmem)),
        name="attn_sample",
    )(q, k_new, v_new, cache_kt, cache_vt, jnp.asarray(cnt_buf))


def _block_toeplitz_kernel(br_ref, bi_ref, cr_ref, ci_ref, toep_ref, tt_ref, *, chunk):
    c = br_ref.shape[1]
    over_states = (((1,), (1,)), ((), ()))
    full = dict(preferred_element_type=F32, precision=lax.Precision.HIGHEST)
    for i in range(br_ref.shape[0]):
        k = (lax.dot_general(br_ref[i], cr_ref[i], over_states, **full)
             - lax.dot_general(bi_ref[i], ci_ref[i], over_states, **full))
        for s in range(chunk):
            shifted = k if s == 0 else jnp.concatenate(
                [jnp.zeros((c, s * c), F32), k[:, :(chunk - s) * c]], axis=1)
            toep_ref[i, pl.ds(s * c, c), :] = shifted
        tt_ref[i] = toep_ref[i].T


def _block_toeplitz(bbr, bbi, cpr, cpi, chunk):
    g, c, n = bbr.shape
    lc = cpr.shape[1]
    per_step = 4
    step = lambda i: (i, 0, 0)
    spec = pl.BlockSpec((per_step, lc, lc), step)
    return pl.pallas_call(
        functools.partial(_block_toeplitz_kernel, chunk=chunk),
        grid=(g // per_step,),
        in_specs=[pl.BlockSpec((per_step, c, n), step)] * 2 + [pl.BlockSpec((per_step, lc, n), step)] * 2,
        out_specs=(spec, spec),
        out_shape=(jax.ShapeDtypeStruct((g, lc, lc), F32),) * 2,
        compiler_params=pltpu.CompilerParams(dimension_semantics=("arbitrary",)),
        name="ssm_toeplitz",
    )(bbr, bbi, cpr, cpi)


def _pad_states(m, g):
    half = (np.arange(g) % 2)[:, None, None]
    z = jnp.zeros_like(m)
    return jnp.where(half == 0, jnp.concatenate([m, z], -1), jnp.concatenate([z, m], -1))


def _ssm_tables(a_re, a_im, log_dt, b_re, b_im, c_re, c_im, d_skip, chunk):
    g, n = a_re.shape
    c = b_re.shape[-1]
    lc = chunk * c
    dt = jnp.exp(log_dt)[:, None]
    x, y = a_re * dt, a_im * dt
    ex = jnp.exp(x)
    ar, ai = ex * jnp.cos(y), ex * jnp.sin(y)
    nr = jnp.expm1(x) * jnp.cos(y) - 2.0 * jnp.sin(0.5 * y) ** 2
    ni = ai
    den = a_re * a_re + a_im * a_im
    fr = (nr * a_re + ni * a_im) / den
    fi = (ni * a_re - nr * a_im) / den
    b_re_t, b_im_t = b_re.transpose(0, 2, 1), b_im.transpose(0, 2, 1)
    bbr = fr[:, None, :] * b_re_t - fi[:, None, :] * b_im_t
    bbi = fr[:, None, :] * b_im_t + fi[:, None, :] * b_re_t
    pr, pi = [jnp.ones_like(ar)], [jnp.zeros_like(ar)]
    for _ in range(chunk):
        pr.append(pr[-1] * ar - pi[-1] * ai)
        pi.append(pr[-2] * ai + pi[-1] * ar)
    pr, pi = jnp.stack(pr, 1), jnp.stack(pi, 1)
    cpr = c_re[:, None] * pr[:, :, None, :] - c_im[:, None] * pi[:, :, None, :]
    cpi = c_re[:, None] * pi[:, :, None, :] + c_im[:, None] * pr[:, :, None, :]
    toep, tt = _block_toeplitz(bbr, bbi, cpr[:, :chunk].reshape(g, lc, n), cpi[:, :chunk].reshape(g, lc, n), chunk)
    qr, qi = pr[:, chunk - 1::-1][:, :, None, :], pi[:, chunk - 1::-1][:, :, None, :]
    wr = (qr * bbr[:, None] - qi * bbi[:, None]).reshape(g, lc, n)
    wi = (qr * bbi[:, None] + qi * bbr[:, None]).reshape(g, lc, n)
    vr = cpr[:, 1:].reshape(g, lc, n)
    vi = cpi[:, 1:].reshape(g, lc, n)
    return dict(chunk=chunk, toep=toep, tt=tt, wr=wr, wi=wi, vr=vr, vi=vi, d=d_skip, pr=pr, pi=pi)


def _ssm_tables_lanes(tab):
    g, chunk = tab["d"].shape[0], tab["chunk"]
    tr = lambda m: m.transpose(0, 2, 1)
    return dict(tt=tab["tt"], wt_re=tr(tab["wr"]), wt_im=tr(tab["wi"]), v_re=tab["vr"], v_im=tab["vi"],
                dcol=jnp.tile(tab["d"], (1, chunk))[:, :, None],
                al_re=tab["pr"][:, chunk].reshape(1, -1), al_im=tab["pi"][:, chunk].reshape(1, -1))


def _ssm_tables_rows(tab, chunk):
    g, c = tab["d"].shape
    big, lc = tab["chunk"], chunk * c
    tr = lambda m: m.transpose(0, 2, 1)
    tail = slice((big - chunk) * c, big * c)
    return dict(toep=tab["toep"][:, :lc, :lc],
                wr=_pad_states(tab["wr"][:, tail], g), wi=_pad_states(tab["wi"][:, tail], g),
                vr=tr(_pad_states(tab["vr"][:, :lc], g)), vi=tr(_pad_states(-tab["vi"][:, :lc], g)),
                d=jnp.tile(tab["d"], (1, chunk))[:, None, :],
                al_re=tab["pr"][:, chunk].reshape(1, -1), al_im=tab["pi"][:, chunk].reshape(1, -1))


def _ssm_state_kernel(u_ref, wr_ref, wi_ref, er_ref, ei_ref):
    er = jnp.dot(u_ref[0], wr_ref[0], preferred_element_type=F32)
    ei = jnp.dot(u_ref[0], wi_ref[0], preferred_element_type=F32)
    er_ref[...] = er + jnp.dot(u_ref[1], wr_ref[1], preferred_element_type=F32)
    ei_ref[...] = ei + jnp.dot(u_ref[1], wi_ref[1], preferred_element_type=F32)


def _ssm_scan_kernel(er_ref, ei_ref, ar_ref, ai_ref, hr_ref, hi_ref, xr_ref, xi_ref, lr_ref, li_ref,
                     *, n_chunks, nb):
    shape = hr_ref.shape
    ar = jnp.broadcast_to(ar_ref[...], shape)
    ai = jnp.broadcast_to(ai_ref[...], shape)

    def step(j, carry):
        xr, xi = carry
        sl = pl.ds(pl.multiple_of(j * nb, nb), nb)
        xr_ref[sl, :] = xr
        xi_ref[sl, :] = xi
        return (ar * xr - ai * xi + er_ref[sl, :], ar * xi + ai * xr + ei_ref[sl, :])

    xr, xi = lax.fori_loop(0, n_chunks, step, (hr_ref[...], hi_ref[...]))
    lr_ref[...] = xr
    li_ref[...] = xi


def _ssm_out_kernel(u_ref, t_ref, d_ref, xr_ref, xi_ref, vr_ref, vi_ref, y_ref):
    xr = xr_ref[...]
    xi = xi_ref[...]
    for a in range(2):
        u = u_ref[a]
        y = jnp.dot(u, t_ref[a], preferred_element_type=F32) + d_ref[a] * u
        y = y + jnp.dot(xr, vr_ref[a], preferred_element_type=F32)
        y_ref[a] = y + jnp.dot(xi, vi_ref[a], preferred_element_type=F32)


def _ssm_scan(er, ei, tab, h0_re, h0_im, n_chunks):
    r, gn = er.shape
    b = r // n_chunks
    lanes = 4 * V7X_LANES
    col = lambda i: (0, i)
    return pl.pallas_call(
        functools.partial(_ssm_scan_kernel, n_chunks=n_chunks, nb=b),
        grid=(gn // lanes,),
        in_specs=[pl.BlockSpec((r, lanes), col), pl.BlockSpec((r, lanes), col),
                  pl.BlockSpec((1, lanes), col), pl.BlockSpec((1, lanes), col),
                  pl.BlockSpec((b, lanes), col), pl.BlockSpec((b, lanes), col)],
        out_specs=(pl.BlockSpec((r, lanes), col),) * 2 + (pl.BlockSpec((b, lanes), col),) * 2,
        out_shape=(jax.ShapeDtypeStruct((r, gn), F32),) * 2 + (jax.ShapeDtypeStruct((b, gn), F32),) * 2,
        compiler_params=pltpu.CompilerParams(dimension_semantics=("arbitrary",),
                                             vmem_limit_bytes=_vmem_limit(2 * 4 * 4 * r * lanes)),
        name="ssm_scan",
    )(er, ei, tab["al_re"], tab["al_im"], h0_re, h0_im)


def _ssm_state_t_kernel(a_ref, wr_ref, wi_ref, er_ref, ei_ref):
    def seq(b, carry):
        a0, a1 = a_ref[0, b], a_ref[1, b]
        er_ref[b] = jnp.concatenate([jnp.dot(wr_ref[0], a0, preferred_element_type=F32),
                                     jnp.dot(wr_ref[1], a1, preferred_element_type=F32)], axis=0)
        ei_ref[b] = jnp.concatenate([jnp.dot(wi_ref[0], a0, preferred_element_type=F32),
                                     jnp.dot(wi_ref[1], a1, preferred_element_type=F32)], axis=0)
        return carry

    lax.fori_loop(0, a_ref.shape[1], seq, 0, unroll=True)


def _ssm_out_t_kernel(a_ref, t_ref, d_ref, xr_ref, xi_ref, vr_ref, vi_ref, y_ref):
    def seq(b, carry):
        xr, xi = xr_ref[b], xi_ref[b]
        n = vr_ref.shape[-1]
        for a in range(2):
            at = a_ref[a, b]
            own = slice(a * n, (a + 1) * n)
            y = jnp.dot(t_ref[a], at, preferred_element_type=F32) + d_ref[a] * at
            y = y + jnp.dot(vr_ref[a], xr[own], preferred_element_type=F32)
            y_ref[a, b] = y - jnp.dot(vi_ref[a], xi[own], preferred_element_type=F32)
        return carry

    lax.fori_loop(0, a_ref.shape[1], seq, 0, unroll=True)


def _ssm_chunks_on_lanes(u, h0_re, h0_im, tab, chunk):
    b, s, width = u.shape
    g = width // SSM_GROUP
    n_chunks = s // chunk
    lc = chunk * SSM_GROUP
    assert n_chunks == V7X_LANES
    at = (u.reshape(b, n_chunks, chunk * width).transpose(0, 2, 1)
          .reshape(b, chunk, g, SSM_GROUP, n_chunks).transpose(2, 0, 1, 3, 4).reshape(g, b, lc, n_chunks))
    pair4 = lambda i: (i, 0, 0, 0)
    pair3 = lambda i: (i, 0, 0)
    cp = lambda vmem: pltpu.CompilerParams(dimension_semantics=("arbitrary",),
                                           vmem_limit_bytes=_vmem_limit(vmem))
    a_spec = pl.BlockSpec((2, b, lc, n_chunks), pair4)
    st_spec = pl.BlockSpec((None, b, V7X_LANES, n_chunks), pair4)
    w_spec = pl.BlockSpec((2, SSM_STATE, lc), pair3)
    v_spec = pl.BlockSpec((2, lc, SSM_STATE), pair3)
    a_bytes = 2 * b * lc * n_chunks
    st_bytes = b * V7X_LANES * n_chunks
    et_re, et_im = pl.pallas_call(
        _ssm_state_t_kernel,
        grid=(g // 2,),
        in_specs=[a_spec, w_spec, w_spec],
        out_specs=(st_spec,) * 2,
        out_shape=(jax.ShapeDtypeStruct((g // 2, b, V7X_LANES, n_chunks), F32),) * 2,
        compiler_params=cp(2 * 4 * (a_bytes + 2 * st_bytes + 4 * SSM_STATE * lc)),
        name="ssm_state_t",
    )(at, tab["wt_re"], tab["wt_im"])
    rows = lambda m: m.transpose(3, 1, 0, 2).reshape(n_chunks * b, g * SSM_STATE)
    xr, xi, lr, li = _ssm_scan(rows(et_re), rows(et_im), tab, h0_re, h0_im, n_chunks)
    cols = lambda m: m.reshape(n_chunks, b, g // 2, V7X_LANES).transpose(2, 1, 3, 0)
    yt = pl.pallas_call(
        _ssm_out_t_kernel,
        grid=(g // 2,),
        in_specs=[a_spec, pl.BlockSpec((2, lc, lc), pair3), pl.BlockSpec((2, lc, 1), pair3),
                  st_spec, st_spec, v_spec, v_spec],
        out_specs=a_spec,
        out_shape=jax.ShapeDtypeStruct((g, b, lc, n_chunks), F32),
        compiler_params=cp(2 * 4 * (2 * a_bytes + 2 * st_bytes + 2 * lc * lc + 6 * SSM_STATE * lc)),
        name="ssm_out_t",
    )(at, tab["tt"], tab["dcol"], cols(xr), cols(xi), tab["v_re"], tab["v_im"])
    y = (yt.reshape(g, b, chunk, SSM_GROUP, n_chunks).transpose(1, 2, 0, 3, 4)
         .reshape(b, chunk * width, n_chunks).transpose(0, 2, 1).reshape(b, s, width))
    return y, lr, li


def _ssm(u, h0_re, h0_im, tab, chunk):
    b, s, width = u.shape
    g = width // SSM_GROUP
    n_chunks = s // chunk
    lc = chunk * SSM_GROUP
    r = n_chunks * b
    gn = g * SSM_STATE
    ut = u.reshape(b, n_chunks, chunk, g, SSM_GROUP).transpose(3, 1, 0, 2, 4).reshape(g, r, lc)
    pair = lambda i: (i, 0, 0)
    cp = lambda vmem: pltpu.CompilerParams(dimension_semantics=("arbitrary",),
                                           vmem_limit_bytes=_vmem_limit(vmem))
    er, ei = pl.pallas_call(
        _ssm_state_kernel,
        grid=(g // 2,),
        in_specs=[pl.BlockSpec((2, r, lc), pair), pl.BlockSpec((2, lc, V7X_LANES), pair),
                  pl.BlockSpec((2, lc, V7X_LANES), pair)],
        out_specs=(pl.BlockSpec((r, V7X_LANES), lambda i: (0, i)),) * 2,
        out_shape=(jax.ShapeDtypeStruct((r, gn), F32),) * 2,
        compiler_params=cp(2 * 4 * (2 * r * lc + 4 * lc * V7X_LANES + 2 * r * V7X_LANES)),
        name="ssm_state",
    )(ut, tab["wr"], tab["wi"])
    xr, xi, lr, li = _ssm_scan(er, ei, tab, h0_re, h0_im, n_chunks)
    y = pl.pallas_call(
        _ssm_out_kernel,
        grid=(g // 2,),
        in_specs=[pl.BlockSpec((2, r, lc), pair), pl.BlockSpec((2, lc, lc), pair),
                  pl.BlockSpec((2, 1, lc), pair),
                  pl.BlockSpec((r, V7X_LANES), lambda i: (0, i)),
                  pl.BlockSpec((r, V7X_LANES), lambda i: (0, i)),
                  pl.BlockSpec((2, V7X_LANES, lc), pair), pl.BlockSpec((2, V7X_LANES, lc), pair)],
        out_specs=pl.BlockSpec((2, r, lc), pair),
        out_shape=jax.ShapeDtypeStruct((g, r, lc), F32),
        compiler_params=cp(2 * 4 * (4 * r * lc + 2 * lc * lc + 2 * r * V7X_LANES + 4 * V7X_LANES * lc)),
        name="ssm_out",
    )(ut, tab["toep"], tab["d"], xr, xi, tab["vr"], tab["vi"])
    y = y.reshape(g, n_chunks, b, chunk, SSM_GROUP).transpose(2, 1, 3, 0, 4).reshape(b, s, width)
    return y, lr, li


def _finish_kernel(x_ref, attn_ref, ssm_ref, ga_ref, wg_ref, bg_ref, gs_ref, wo_ref, gf_ref,
                   wr_ref, br_ref, base_ref, x1_ref, hf_ref, ti_ref, tg_ref, cnt_ref, seen):
    a = attn_ref.shape[-1]
    tm = x_ref.shape[0]

    @pl.when(pl.program_id(0) == 0)
    def _():
        seen[...] = base_ref[...]

    g = jax.nn.gelu(ssm_ref[...])
    gate = jax.nn.sigmoid(jnp.dot(g, wg_ref[...], preferred_element_type=F32) + bg_ref[...])
    mixed_a = _rms(attn_ref[...], ga_ref[...])
    mixed_s = _rms(g * gate, gs_ref[...])
    x1 = (x_ref[...] + jnp.dot(mixed_a, wo_ref[pl.ds(0, a), :], preferred_element_type=F32)
          + jnp.dot(mixed_s, wo_ref[pl.ds(a, wo_ref.shape[0] - a), :], preferred_element_type=F32))
    x1_ref[...] = x1
    hf = _rms(x1, gf_ref[...])
    _to_token_tiles(hf_ref, hf)
    hf_hi = hf.astype(jnp.bfloat16).astype(F32)
    hf_lo = hf - hf_hi
    small = (jnp.dot(hf_lo, wr_ref[1], preferred_element_type=F32)
             + jnp.dot(hf_hi, wr_ref[1], preferred_element_type=F32)
             + jnp.dot(hf_lo, wr_ref[0], preferred_element_type=F32))
    logits = jnp.dot(hf_hi, wr_ref[0], preferred_element_type=F32) + small + br_ref[...]
    lane = lax.broadcasted_iota(I32, logits.shape, 1)
    work = logits
    vals, idxs = [], []
    for _ in range(TOP_K):
        m = jnp.max(work, axis=-1, keepdims=True)
        idx = jnp.min(jnp.where(work == m, lane, V7X_LANES), axis=-1, keepdims=True)
        vals.append(m)
        idxs.append(idx)
        work = jnp.where(lane == idx, -jnp.inf, work)
    exps = [jnp.exp(v - vals[0]) for v in vals]
    den = exps[0]
    for e in exps[1:]:
        den = den + e
    picked = jnp.zeros(logits.shape, F32)
    for k in range(TOP_K):
        picked = picked + jnp.where(lane == idxs[k], 1.0, 0.0)
    tri = jnp.where(lax.broadcasted_iota(I32, (tm, tm), 0) > lax.broadcasted_iota(I32, (tm, tm), 1), 1.0, 0.0)
    before = jnp.dot(tri, picked, preferred_element_type=F32) + seen[...]
    ti = jnp.zeros(logits.shape, I32)
    tg = jnp.zeros(logits.shape, F32)
    for k in range(TOP_K):
        rank = jnp.sum(jnp.where(lane == idxs[k], before, 0.0), axis=-1, keepdims=True)
        ti = jnp.where(lane == k, idxs[k], ti)
        ti = jnp.where(lane == TOP_K + k, rank.astype(I32), ti)
        tg = jnp.where(lane == k, exps[k] / den, tg)
    ti_ref[...] = ti
    tg_ref[...] = tg
    seen[...] = seen[...] + jnp.sum(picked, axis=0, keepdims=True)
    cnt_ref[...] = seen[...]


def _finish(x2d, attn, ssm_y, p, seen, tm):
    n, d = x2d.shape
    a = attn.shape[1]
    sw = ssm_y.shape[1]
    ne = p["w_router"].shape[1]
    w_router = jnp.pad(p["w_router"], ((0, 0), (0, V7X_LANES - ne)))
    w_hi = w_router.astype(jnp.bfloat16).astype(F32)
    w_router = jnp.stack([w_hi, w_router - w_hi])
    b_router = jnp.pad(p["b_router"], (0, V7X_LANES - ne), constant_values=-jnp.inf).reshape(1, V7X_LANES)
    row = lambda i: (i, 0)
    full = lambda arr: pl.BlockSpec(arr.shape, lambda i: (0,) * arr.ndim)
    ins = [x2d, attn, ssm_y, p["attn_out_norm_g"].reshape(1, a), p["w_glu"], p["b_glu"].reshape(1, sw),
           p["ssm_out_norm_g"].reshape(1, sw), p["w_out"], p["norm_ffn_g"].reshape(1, d),
           w_router, b_router, seen]
    in_specs = [pl.BlockSpec((tm, d), row), pl.BlockSpec((tm, a), row), pl.BlockSpec((tm, sw), row)]
    in_specs += [full(arr) for arr in ins[3:]]
    vmem = 2 * 4 * (tm * (3 * d + a + sw + 2 * V7X_LANES) + sw * sw + d * d + 2 * d * V7X_LANES) + 16 * tm * d
    return pl.pallas_call(
        _finish_kernel,
        grid=(n // tm,),
        in_specs=in_specs,
        out_specs=(pl.BlockSpec((tm, d), row), pl.BlockSpec((tm * V7X_SUBLANES, V7X_LANES), row),
                   pl.BlockSpec((tm, V7X_LANES), row), pl.BlockSpec((tm, V7X_LANES), row),
                   pl.BlockSpec((1, V7X_LANES), lambda i: (0, 0))),
        out_shape=(jax.ShapeDtypeStruct((n, d), F32), jax.ShapeDtypeStruct((n * V7X_SUBLANES, V7X_LANES), F32),
                   jax.ShapeDtypeStruct((n, V7X_LANES), I32), jax.ShapeDtypeStruct((n, V7X_LANES), F32),
                   jax.ShapeDtypeStruct((1, V7X_LANES), F32)),
        scratch_shapes=[pltpu.VMEM((1, V7X_LANES), F32)],
        compiler_params=pltpu.CompilerParams(dimension_semantics=("arbitrary",),
                                             vmem_limit_bytes=_vmem_limit(vmem)),
        name="finish",
    )(*ins)


def _tile_copy(src_ref, src_row, dst_ref, dst_row, sem):
    return pltpu.make_async_copy(src_ref.at[src_row], dst_ref.at[dst_row], sem)


def _dispatch_kernel(dest_ref, hp_ref, hs_ref, xs_ref, sem, *, n_tiles_p):
    def scatter(hf_ref):
        tm = hf_ref.shape[0]

        def tokens(i, carry):
            for u in range(V7X_SUBLANES):
                r = i * V7X_SUBLANES + u
                for k in range(TOP_K):
                    _tile_copy(hf_ref, r, xs_ref, dest_ref[0, 0, r * TOP_K + k], sem).start(priority=k % 2)
            return carry

        lax.fori_loop(0, tm // V7X_SUBLANES, tokens, 0)
        rows = pl.ds(0, tm * TOP_K)
        pltpu.make_async_copy(xs_ref.at[rows], xs_ref.at[rows], sem).wait()

    @pl.when(pl.program_id(0) < n_tiles_p)
    def _():
        scatter(hp_ref)

    @pl.when(pl.program_id(0) == n_tiles_p)
    def _():
        scatter(hs_ref)


def _dispatch(hf_p, dest_p, hf_s, dest_s, tm):
    tile = (V7X_SUBLANES, V7X_LANES)
    hf_p, hf_s = hf_p.reshape(-1, *tile), hf_s.reshape(-1, *tile)
    n_p, n_s = hf_p.shape[0], hf_s.shape[0]
    assert n_p % tm == 0 and n_s <= tm
    n_tiles_p = n_p // tm
    dest3 = jnp.concatenate([dest_p.reshape(n_tiles_p, tm * TOP_K),
                             jnp.pad(dest_s.reshape(1, n_s * TOP_K), ((0, 0), (0, (tm - n_s) * TOP_K)))])
    dest3 = dest3.reshape(n_tiles_p + 1, 1, tm * TOP_K)
    return pl.pallas_call(
        functools.partial(_dispatch_kernel, n_tiles_p=n_tiles_p),
        grid=(n_tiles_p + 1,),
        in_specs=[pl.BlockSpec((1, 1, tm * TOP_K), lambda i: (i, 0, 0), memory_space=pltpu.SMEM),
                  pl.BlockSpec((tm, *tile), lambda i: (jnp.minimum(i, n_tiles_p - 1), 0, 0)),
                  pl.BlockSpec((n_s, *tile), lambda i: (0, 0, 0))],
        out_specs=pl.BlockSpec(memory_space=pl.ANY),
        out_shape=jax.ShapeDtypeStruct(((n_p + n_s) * TOP_K, *tile), F32),
        scratch_shapes=[pltpu.SemaphoreType.DMA(())],
        compiler_params=pltpu.CompilerParams(dimension_semantics=("arbitrary",)),
        name="dispatch",
    )(dest3, hf_p, hf_s)


def _moe_kernel(vt_ref, ve_ref, nx_ref, lo_ref, hi_ref, x_ref, w1_ref, b1_ref, w2_ref, b2_ref, y_ref,
                w1s, w2s, w1b, w2b, sem, *, d_ff):
    v = pl.program_id(0)
    lo, hi = lo_ref[v], hi_ref[v]
    prev = jnp.maximum(v - 1, 0)
    first = jnp.logical_or(v == 0, vt_ref[v] != vt_ref[prev])

    def fetch(e):
        return (pltpu.make_async_copy(w1_ref.at[e], w1s, sem.at[0]),
                pltpu.make_async_copy(w2_ref.at[e], w2s, sem.at[1]))

    @pl.when(v == 0)
    def _():
        for cp in fetch(ve_ref[0]):
            cp.start()

    @pl.when(jnp.logical_or(v == 0, ve_ref[v] != ve_ref[prev]))
    def _():
        for cp in fetch(ve_ref[v]):
            cp.wait()
        w1b[...] = w1s[...].astype(jnp.bfloat16)
        w2b[...] = w2s[...].astype(jnp.bfloat16)

        @pl.when(nx_ref[v] >= 0)
        def _():
            for cp in fetch(nx_ref[v]):
                cp.start()

    def swiglu(x_tiles):
        x = _from_token_tiles(x_tiles).astype(jnp.bfloat16)
        a = jnp.dot(x, w1b[...], preferred_element_type=F32) + b1_ref[0]
        x_glu = jnp.minimum(a[:, :d_ff], SWIGLU_LIMIT)
        x_lin = jnp.clip(a[:, d_ff:], -SWIGLU_LIMIT, SWIGLU_LIMIT)
        mid = x_glu * jax.nn.sigmoid(SWIGLU_ALPHA * x_glu) * (x_lin + 1.0)
        return jnp.dot(mid.astype(jnp.bfloat16), w2b[...], preferred_element_type=F32) + b2_ref[0]

    whole = jnp.logical_and(lo == 0, hi == MOE_TILE)

    @pl.when(whole)
    def _():
        _to_token_tiles(y_ref, swiglu(x_ref))

    for r0 in range(0, MOE_TILE, MOE_SUB):
        sub = pl.ds(r0 * V7X_SUBLANES, MOE_SUB * V7X_SUBLANES)
        x_sub, y_sub = x_ref.at[sub, :], y_ref.at[sub, :]
        touched = jnp.logical_and(jnp.logical_not(whole), jnp.logical_and(lo < r0 + MOE_SUB, hi > r0))

        @pl.when(touched)
        def _(r0=r0, x_sub=x_sub, y_sub=y_sub):
            y = swiglu(x_sub)
            rows = r0 + lax.broadcasted_iota(I32, (MOE_SUB, 1), 0)
            mine = (rows >= lo) & (rows < hi)

            @pl.when(first)
            def _():
                _to_token_tiles(y_sub, jnp.where(mine, y, 0.0))

            @pl.when(jnp.logical_not(first))
            def _():
                _to_token_tiles(y_sub, jnp.where(mine, y, _from_token_tiles(y_sub)))

        @pl.when(jnp.logical_and(first, jnp.logical_not(jnp.logical_or(touched, whole))))
        def _(y_sub=y_sub):
            y_sub[...] = jnp.zeros(y_sub.shape, F32)


def _moe_experts(xs, visits, w1, b1, w2, b2):
    ne, d, ff2 = w1.shape
    d_ff = ff2 // 2
    n_visits = visits[0].shape[0]
    row = lambda v, vt, ve, nx, lo, hi: (vt[v], 0)
    exp = lambda v, vt, ve, nx, lo, hi: (ve[v], 0, 0)
    vmem = (4 * 2 * 2 * MOE_TILE * d + (4 + 2) * (d * ff2 + d_ff * d) + 2 * 4 * (ff2 + d)
            + 4 * 4 * MOE_TILE * ff2)
    tiles = pl.BlockSpec((MOE_TILE * V7X_SUBLANES, V7X_LANES), row)
    grid_spec = pltpu.PrefetchScalarGridSpec(
        num_scalar_prefetch=5,
        grid=(n_visits,),
        in_specs=[tiles,
                  pl.BlockSpec(memory_space=pl.ANY), pl.BlockSpec((1, 1, ff2), exp),
                  pl.BlockSpec(memory_space=pl.ANY), pl.BlockSpec((1, 1, d), exp)],
        out_specs=tiles,
        scratch_shapes=[pltpu.VMEM((d, ff2), F32), pltpu.VMEM((d_ff, d), F32),
                        pltpu.VMEM((d, ff2), jnp.bfloat16), pltpu.VMEM((d_ff, d), jnp.bfloat16),
                        pltpu.SemaphoreType.DMA((2,))],
    )
    return pl.pallas_call(
        functools.partial(_moe_kernel, d_ff=d_ff),
        grid_spec=grid_spec,
        out_shape=jax.ShapeDtypeStruct(xs.shape, F32),
        compiler_params=pltpu.CompilerParams(dimension_semantics=("arbitrary",),
                                             vmem_limit_bytes=_vmem_limit(vmem)),
        name="moe",
    )(*visits, xs, w1, b1.reshape(ne, 1, ff2), w2, b2.reshape(ne, 1, d))


def _moe_visits(sizes, n_rows):
    n_tiles = n_rows // MOE_TILE
    n_visits = n_tiles + N_EXPERTS - 1
    ends = jnp.cumsum(sizes)
    starts = ends - sizes
    first_tile = starts // MOE_TILE
    n_vis = jnp.where(sizes > 0, (ends - 1) // MOE_TILE - first_tile + 1, 0)
    vis_end = jnp.cumsum(n_vis)
    total = vis_end[-1]
    v = jnp.clip(jnp.arange(n_visits, dtype=I32), 0, jnp.maximum(total - 1, 0))
    e = jnp.sum((vis_end[None, :] <= v[:, None]).astype(I32), axis=1)
    pick = lambda tab: jnp.sum(jnp.where(e[:, None] == jnp.arange(N_EXPERTS)[None, :], tab[None, :], 0), axis=1)
    tile = pick(first_tile) + v - pick(vis_end - n_vis)
    lo = jnp.maximum(pick(starts), tile * MOE_TILE) - tile * MOE_TILE
    hi = jnp.minimum(pick(ends), (tile + 1) * MOE_TILE) - tile * MOE_TILE
    live = jnp.arange(n_visits) < total
    ids = jnp.arange(N_EXPERTS, dtype=I32)
    later = jnp.where((ids[None, :] > ids[:, None]) & (sizes[None, :] > 0), ids[None, :], N_EXPERTS)
    nxt = jnp.min(later, axis=1)
    nxt = pick(jnp.where(nxt < N_EXPERTS, nxt, -1))
    return (tile.astype(I32), e.astype(I32), nxt.astype(I32), jnp.where(live, lo, 0).astype(I32),
            jnp.where(live, hi, 0).astype(I32))


def _final_kernel(dest_ref, nxt_ref, x_ref, tg_ref, g_ref, ys_ref, y_ref, buf, sem, *, tm):
    i = pl.program_id(0)
    n = pl.num_programs(0)

    def gather(d_ref, slot):
        def tokens(i8, carry):
            for u in range(V7X_SUBLANES):
                r = i8 * V7X_SUBLANES + u
                tile = pl.ds(pl.multiple_of(r * V7X_SUBLANES, V7X_SUBLANES), V7X_SUBLANES)
                for k in range(TOP_K):
                    pltpu.make_async_copy(ys_ref.at[d_ref[0, 0, r * TOP_K + k]], buf.at[slot, k, tile, :],
                                          sem.at[slot]).start(priority=k % 2)
            return carry
        lax.fori_loop(0, tm // V7X_SUBLANES, tokens, 0)

    @pl.when(i == 0)
    def _():
        gather(dest_ref, 0)

    for slot in range(2):
        @pl.when(jnp.logical_and(i + 1 < n, (i + 1) % 2 == slot))
        def _(slot=slot):
            gather(nxt_ref, slot)

    for slot in range(2):
        @pl.when(i % 2 == slot)
        def _(slot=slot):
            pltpu.make_async_copy(buf.at[slot], buf.at[slot], sem.at[slot]).wait()
            tg = tg_ref[...]
            parts = []
            for j in range(V7X_SUBLANES):
                f = None
                for k in range(TOP_K):
                    rows = buf[slot, k, pl.ds(j, tm, stride=V7X_SUBLANES), :]
                    f = tg[:, k:k + 1] * rows if f is None else f + tg[:, k:k + 1] * rows
                parts.append(f)
            y_ref[...] = _rms(x_ref[...] + jnp.concatenate(parts, axis=1), g_ref[...])


def _final(x1, tg, dest, ys, g, tm):
    n, d = x1.shape
    n_tiles = n // tm
    tile = (V7X_SUBLANES, V7X_LANES)
    dest3 = dest.reshape(n_tiles, 1, tm * TOP_K)
    row = lambda i: (i, 0)
    smem = lambda index_map: pl.BlockSpec((1, 1, tm * TOP_K), index_map, memory_space=pltpu.SMEM)
    vmem = 4 * (2 * TOP_K * tm * d + 2 * 2 * tm * d + 2 * tm * V7X_LANES + 4 * tm * d)
    return pl.pallas_call(
        functools.partial(_final_kernel, tm=tm),
        grid=(n_tiles,),
        in_specs=[smem(lambda i: (i, 0, 0)), smem(lambda i: (jnp.minimum(i + 1, n_tiles - 1), 0, 0)),
                  pl.BlockSpec((tm, d), row), pl.BlockSpec((tm, V7X_LANES), row),
                  pl.BlockSpec((1, d), lambda i: (0, 0)), pl.BlockSpec(memory_space=pl.ANY)],
        out_specs=pl.BlockSpec((tm, d), row),
        out_shape=jax.ShapeDtypeStruct((n, d), F32),
        scratch_shapes=[pltpu.VMEM((2, TOP_K, tm * V7X_SUBLANES, V7X_LANES), F32), pltpu.SemaphoreType.DMA((2,))],
        compiler_params=pltpu.CompilerParams(dimension_semantics=("arbitrary",),
                                             vmem_limit_bytes=_vmem_limit(vmem)),
        name="final",
    )(dest3, dest3, x1, tg, g.reshape(1, d), ys.reshape(-1, *tile))


def kernel(x_prompt, x_sample, cache_k, cache_v, state_ssm_re, state_ssm_im, norm_mix_g, w_in, attn_out_norm_g, ssm_a_re, ssm_a_im, ssm_log_dt, ssm_b_re, ssm_b_im, ssm_c_re, ssm_c_im, ssm_d, w_glu, b_glu, ssm_out_norm_g, w_out, norm_ffn_g, w_router, b_router, w_moe1, b_moe1, w_moe2, b_moe2, norm_final_g):
    depth = w_in.shape[0]
    assert depth == 1, "single-layer trunk"
    bp, s, d = x_prompt.shape
    bs, t, _ = x_sample.shape
    n_buf = cache_k.shape[2]
    n_heads, head_dim = cache_k.shape[3], cache_k.shape[4]
    assert head_dim == HEAD_DIM
    a = n_heads * head_dim
    g = ssm_a_re.shape[1]
    gn = g * SSM_STATE
    assert s % (max(dd for _, dd in DILATED_GROUPS) * KEYS_BACK) == 0
    np_, ns = bp * s, bs * t
    tm_p, tm_s = TOKEN_TILE, ns
    l = 0
    ssm_p = (ssm_a_re[l], ssm_a_im[l], ssm_log_dt[l], ssm_b_re[l], ssm_b_im[l], ssm_c_re[l], ssm_c_im[l],
             ssm_d[l])
    tail = dict(attn_out_norm_g=attn_out_norm_g[l], w_glu=w_glu[l], b_glu=b_glu[l],
                ssm_out_norm_g=ssm_out_norm_g[l], w_out=w_out[l], norm_ffn_g=norm_ffn_g[l],
                w_router=w_router[l], b_router=b_router[l])

    xp = x_prompt.reshape(np_, d)
    qp, kp, vp, up, ktp, vtp = _proj(xp, norm_mix_g[l], w_in[l], a, tm_p, seq=s)
    attn_p = _attn_prompt(qp.reshape(bp, s, a), kp.reshape(bp, s, a), vp.reshape(bp, s, a))
    zeros = jnp.zeros((bp, gn), F32)
    tables = _ssm_tables(*ssm_p, PROMPT_CHUNK)
    y_p, hp_re, hp_im = _ssm_chunks_on_lanes(up.reshape(bp, s, -1), zeros, zeros, _ssm_tables_lanes(tables),
                                             PROMPT_CHUNK)
    seen = jnp.zeros((1, V7X_LANES), F32)
    x1_p, hf_p, ti_p, tg_p, seen = _finish(xp, attn_p.reshape(np_, a), y_p.reshape(np_, -1), tail, seen, tm_p)

    xs = x_sample.reshape(ns, d)
    qs, ks, vs, us = _proj(xs, norm_mix_g[l], w_in[l], a, tm_s)
    by_dim = lambda c: c.transpose(0, 2, 3, 1).reshape(bs, a, n_buf)
    attn_s = _attn_sample(qs.reshape(bs, t, a), ks.reshape(bs, t, a), vs.reshape(bs, t, a),
                          by_dim(cache_k[l]), by_dim(cache_v[l]))
    y_s, hs_re, hs_im = _ssm(us.reshape(bs, t, -1), state_ssm_re[l].reshape(bs, gn),
                             state_ssm_im[l].reshape(bs, gn), _ssm_tables_rows(tables, t), t)
    x1_s, hf_s, ti_s, tg_s, seen = _finish(xs, attn_s.reshape(ns, a), y_s.reshape(ns, -1), tail, seen, tm_s)

    sizes = seen[0, :N_EXPERTS].astype(I32)
    starts = jnp.cumsum(sizes) - sizes
    experts = jnp.arange(N_EXPERTS, dtype=I32)[None, None, :]

    def sorted_row(ti):
        hit = ti[:, :TOP_K, None] == experts
        return jnp.sum(jnp.where(hit, starts[None, None, :], 0), axis=-1) + ti[:, TOP_K:2 * TOP_K]

    dest_p, dest_s = sorted_row(ti_p), sorted_row(ti_s)
    n_rows = (np_ + ns) * TOP_K
    assert n_rows % MOE_TILE == 0
    xs_sorted = _dispatch(hf_p, dest_p, hf_s, dest_s, DISPATCH_TILE).reshape(-1, V7X_LANES)
    ys = _moe_experts(xs_sorted, _moe_visits(sizes, n_rows), w_moe1[l], b_moe1[l], w_moe2[l], b_moe2[l])
    y_prompt = _final(x1_p, tg_p, dest_p, ys, norm_final_g, COMBINE_TILE).reshape(bp, s, d)
    y_sample = _final(x1_s, tg_s, dest_s, ys, norm_final_g, tm_s).reshape(bs, t, d)

    keep = min(max(w for w, _ in DILATED_GROUPS), s)
    k5 = lambda z, b_, s_: z.reshape(1, b_, s_, n_heads, head_dim)
    by_pos = lambda zt: zt.reshape(bp, n_heads, head_dim, s).transpose(0, 3, 1, 2)[None]
    st = lambda z, b_: z.reshape(1, b_, g, SSM_STATE)
    return (y_prompt, y_sample,
            by_pos(ktp)[:, :, s - keep:], by_pos(vtp)[:, :, s - keep:], st(hp_re, bp), st(hp_im, bp),
            k5(ks, bs, t), k5(vs, bs, t), st(hs_re, bs), st(hs_im, bs))
```
